```python
import math
import jax, jax.numpy as jnp
from jax import lax
import numpy as np

D_MODEL = 1024
BATCH = 8
SEQ = 2048
DEPTH = 1
DEC_BATCH = 128
DEC_SEQ = 8
PAST_LEN = 16384
PAGE_SIZE = 128

HG_HEADS = 8
HG_DK = 128
HG_DV = 128
HG_KW = HG_HEADS * HG_DK
HG_VW = HG_HEADS * HG_DV
HG_CHUNK = 64
HG_SCALE = HG_DK ** -0.5
S5_WIDTH = 512
S5_GROUP = 16
S5_GROUPS = S5_WIDTH // S5_GROUP
S5_STATE = 64
S5_DT_MIN = 0.001
S5_DT_MAX = 0.1
MOE_GROUPS = 4
MOE_EXPERTS_PER_GROUP = 8
N_EXPERTS = MOE_GROUPS * MOE_EXPERTS_PER_GROUP
MOE_TOP_K = 2
D_EXPERT = 512
RMS_EPS = 1e-6
IN_WIDTH = 2 * HG_KW + 2 * HG_VW + S5_WIDTH + 2 * D_MODEL
IN_SPLITS = (HG_KW, 2 * HG_KW, 2 * HG_KW + HG_VW, 2 * HG_KW + 2 * HG_VW,
             2 * HG_KW + 2 * HG_VW + S5_WIDTH, 2 * HG_KW + 2 * HG_VW + S5_WIDTH + D_MODEL)

kernel_name = "hgrn2_s5_hmoe_hybrid_step"


def rmsnorm(x, g):
    xf = x.astype(jnp.float32)
    return xf * lax.rsqrt(jnp.mean(xf * xf, axis=-1, keepdims=True) + RMS_EPS) * g.astype(jnp.float32)


def hgrn2_recurrence(q, k, v, log_f, s0):
    B, L, H, DK = q.shape
    DV = v.shape[-1]
    c = L if L <= HG_CHUNK else math.gcd(L, HG_CHUNK)
    n = L // c

    def to_chunks(t):
        return t.reshape(B, n, c, H, t.shape[-1]).transpose(1, 0, 3, 2, 4)

    qc, kc, vc, gc = to_chunks(q), to_chunks(k), to_chunks(v), to_chunks(log_f)
    causal = jnp.tril(jnp.ones((c, c), dtype=bool))

    def step(s, inp):
        qb, kb, vb, gb = inp
        b = jnp.cumsum(gb, axis=2)
        b_last = b[:, :, -1:, :]
        q_dec = qb * jnp.exp(b)
        k_dec = kb * jnp.exp(-b)
        scores = jnp.where(causal, jnp.einsum('bhtk,bhsk->bhts', q_dec, k_dec), 0.0)
        o = jnp.einsum('bhtk,bhkv->bhtv', q_dec, s) + jnp.einsum('bhts,bhsv->bhtv', scores, vb)
        s_new = (jnp.exp(b_last[:, :, 0, :])[..., None] * s
                 + jnp.einsum('bhsk,bhsv->bhkv', kb * jnp.exp(b_last - b), vb))
        return s_new, o

    s_final, o = lax.scan(step, s0, (qc, kc, vc, gc))
    o = o.transpose(1, 0, 3, 2, 4).reshape(B, L, H, DV)
    return o, s_final


def s5_branch(u, h0_re, h0_im, lam_re, lam_im, log_dt, b_re, b_im, c_re, c_im, d_skip):
    lam_re = lam_re.astype(jnp.float32)
    lam_im = lam_im.astype(jnp.float32)
    dt = jnp.exp(log_dt.astype(jnp.float32))[:, None]
    mag = jnp.exp(lam_re * dt)
    ab_re = mag * jnp.cos(lam_im * dt)
    ab_im = mag * jnp.sin(lam_im * dt)
    den = lam_re * lam_re + lam_im * lam_im
    nr = ab_re - 1.0
    coef_re = (nr * lam_re + ab_im * lam_im) / den
    coef_im = (ab_im * lam_re - nr * lam_im) / den
    bb_re = coef_re[..., None] * b_re - coef_im[..., None] * b_im
    bb_im = coef_re[..., None] * b_im + coef_im[..., None] * b_re
    bu_re = jnp.einsum('gpc,blgc->lbgp', bb_re, u)
    bu_im = jnp.einsum('gpc,blgc->lbgp', bb_im, u)
    bu_re = bu_re.at[0].add(ab_re * h0_re - ab_im * h0_im)
    bu_im = bu_im.at[0].add(ab_re * h0_im + ab_im * h0_re)
    a_re = jnp.broadcast_to(ab_re, bu_re.shape)
    a_im = jnp.broadcast_to(ab_im, bu_im.shape)

    def combine(e1, e2):
        a1r, a1i, b1r, b1i = e1
        a2r, a2i, b2r, b2i = e2
        return (a2r * a1r - a2i * a1i, a2r * a1i + a2i * a1r,
                a2r * b1r - a2i * b1i + b2r, a2r * b1i + a2i * b1r + b2i)

    _, _, h_re, h_im = lax.associative_scan(combine, (a_re, a_im, bu_re, bu_im), axis=0)
    y = (jnp.einsum('gcp,lbgp->blgc', c_re, h_re) - jnp.einsum('gcp,lbgp->blgc', c_im, h_im)
         + d_skip.reshape(S5_GROUPS, S5_GROUP) * u)
    return y, h_re[-1], h_im[-1]


def token_mixer(x, s_hg, s_re, s_im, lb, norm_g, w_in, onorm_g, w_branch_a, lam_re, lam_im, log_dt,
                b_re, b_im, c_re, c_im, d_skip, w_glu, b_glu, w_out):
    B, L, _ = x.shape
    h = rmsnorm(x, norm_g)
    proj = h @ w_in
    q, f_raw, i_in, o_gate, u, gate_a, gate_b = jnp.split(proj, IN_SPLITS, axis=-1)
    f = lb + (1.0 - lb) * jax.nn.sigmoid(f_raw)
    log_f = jnp.log(f)
    k = 1.0 - f
    o, s_hg_new = hgrn2_recurrence(
        (q * HG_SCALE).reshape(B, L, HG_HEADS, HG_DK), k.reshape(B, L, HG_HEADS, HG_DK),
        i_in.reshape(B, L, HG_HEADS, HG_DV), log_f.reshape(B, L, HG_HEADS, HG_DK),
        s_hg.astype(jnp.float32))
    o = o * lax.rsqrt(jnp.mean(o * o, axis=-1, keepdims=True) + RMS_EPS) * onorm_g
    o = o.reshape(B, L, HG_VW) * jax.nn.silu(o_gate)
    y_a = o @ w_branch_a
    y_s5, s_re_new, s_im_new = s5_branch(u.reshape(B, L, S5_GROUPS, S5_GROUP), s_re.astype(jnp.float32),
                                         s_im.astype(jnp.float32), lam_re, lam_im, log_dt,
                                         b_re, b_im, c_re, c_im, d_skip)
    z = jax.nn.gelu(y_s5.reshape(B, L, S5_WIDTH)) @ w_glu + b_glu
    za, zg = jnp.split(z, 2, axis=-1)
    y_b = za * jax.nn.sigmoid(zg)
    merged = jax.nn.sigmoid(gate_a) * y_a + jax.nn.sigmoid(gate_b) * y_b
    return merged @ w_out, s_hg_new, s_re_new, s_im_new


def hier_moe(h, w_rg, b_rg, w_re, b_re, w_gate, w_up, w_down):
    B, L, D = h.shape
    t = h.reshape(B * L, D)
    g_logits = (t @ w_rg + b_rg).astype(jnp.float32)
    g_prob = jax.nn.softmax(g_logits, axis=-1)
    g_idx = jnp.argmax(g_logits, axis=-1)
    g_w = jnp.take_along_axis(g_prob, g_idx[:, None], axis=-1)[:, 0]
    e_all = (jnp.einsum('td,gde->tge', t, w_re) + b_re).astype(jnp.float32)
    e_logits = jnp.take_along_axis(e_all, g_idx[:, None, None], axis=1)[:, 0]
    top_v, top_i = lax.top_k(e_logits, MOE_TOP_K)
    top_w = jax.nn.softmax(top_v, axis=-1) * g_w[:, None]
    expert_id = g_idx[:, None] * MOE_EXPERTS_PER_GROUP + top_i
    combine = jnp.einsum('tk,tke->te', top_w, jax.nn.one_hot(expert_id, N_EXPERTS, dtype=jnp.float32))
    y = jnp.zeros((B * L, D), jnp.float32)
    for e in range(N_EXPERTS):
        hid = jax.nn.silu(t @ w_gate[e]) * (t @ w_up[e])
        y = y + combine[:, e:e + 1] * (hid @ w_down[e])
    return y.reshape(B, L, D)


def setup_inputs(seed: int = 0) -> dict:
    key = jax.random.key(seed)
    ks = jax.random.split(key, 32)
    f32 = jnp.float32
    nrm = lambda k, shape, s: jax.random.normal(k, shape, f32) * s
    n_idx = jnp.arange(S5_STATE, dtype=f32)
    return {
        'x_prompt': nrm(ks[0], (BATCH, SEQ, D_MODEL), 1.0),
        'x_sample': nrm(ks[1], (DEC_BATCH, DEC_SEQ, D_MODEL), 1.0),
        'state_hgrn': nrm(ks[2], (DEPTH, DEC_BATCH, HG_HEADS, HG_DK, HG_DV), 0.5),
        'state_s5_re': nrm(ks[3], (DEPTH, DEC_BATCH, S5_GROUPS, S5_STATE), 0.5),
        'state_s5_im': nrm(ks[4], (DEPTH, DEC_BATCH, S5_GROUPS, S5_STATE), 0.5),
        'norm_mix_g': 1.0 + nrm(ks[5], (DEPTH, D_MODEL), 0.02),
        'w_in': nrm(ks[6], (DEPTH, D_MODEL, IN_WIDTH), D_MODEL ** -0.5),
        'hgrn_lb_raw': nrm(ks[7], (DEPTH + 1, HG_KW), 0.1),
        'hgrn_onorm_g': 1.0 + nrm(ks[8], (DEPTH, HG_HEADS, HG_DV), 0.02),
        'w_branch_a': nrm(ks[9], (DEPTH, HG_VW, D_MODEL), HG_VW ** -0.5),
        's5_lambda_re': -0.5 + nrm(ks[10], (DEPTH, S5_GROUPS, S5_STATE), 0.02),
        's5_lambda_im': math.pi * n_idx + nrm(ks[11], (DEPTH, S5_GROUPS, S5_STATE), 0.02),
        's5_log_dt': jax.random.uniform(ks[12], (DEPTH, S5_GROUPS), f32, math.log(S5_DT_MIN), math.log(S5_DT_MAX)),
        's5_b_re': nrm(ks[13], (DEPTH, S5_GROUPS, S5_STATE, S5_GROUP), (2 * S5_GROUP) ** -0.5),
        's5_b_im': nrm(ks[14], (DEPTH, S5_GROUPS, S5_STATE, S5_GROUP), (2 * S5_GROUP) ** -0.5),
        's5_c_re': nrm(ks[15], (DEPTH, S5_GROUPS, S5_GROUP, S5_STATE), (2 * S5_STATE) ** -0.5),
        's5_c_im': nrm(ks[16], (DEPTH, S5_GROUPS, S5_GROUP, S5_STATE), (2 * S5_STATE) ** -0.5),
        's5_d': nrm(ks[17], (DEPTH, S5_WIDTH), 1.0),
        'w_glu': nrm(ks[18], (DEPTH, S5_WIDTH, 2 * D_MODEL), S5_WIDTH ** -0.5),
        'b_glu': nrm(ks[19], (DEPTH, 2 * D_MODEL), 0.01),
        'w_out': nrm(ks[20], (DEPTH, D_MODEL, D_MODEL), D_MODEL ** -0.5),
        'norm_ffn_g': 1.0 + nrm(ks[21], (DEPTH, D_MODEL), 0.02),
        'w_router_group': nrm(ks[22], (DEPTH, D_MODEL, MOE_GROUPS), D_MODEL ** -0.5),
        'b_router_group': nrm(ks[23], (DEPTH, MOE_GROUPS), 0.01),
        'w_router_expert': nrm(ks[24], (DEPTH, MOE_GROUPS, D_MODEL, MOE_EXPERTS_PER_GROUP), D_MODEL ** -0.5),
        'b_router_expert': nrm(ks[25], (DEPTH, MOE_GROUPS, MOE_EXPERTS_PER_GROUP), 0.01),
        'w_exp_gate': nrm(ks[26], (DEPTH, N_EXPERTS, D_MODEL, D_EXPERT), D_MODEL ** -0.5),
        'w_exp_up': nrm(ks[27], (DEPTH, N_EXPERTS, D_MODEL, D_EXPERT), D_MODEL ** -0.5),
        'w_exp_down': nrm(ks[28], (DEPTH, N_EXPERTS, D_EXPERT, D_MODEL), D_EXPERT ** -0.5),
        'norm_final_g': 1.0 + nrm(ks[29], (D_MODEL,), 0.02),
    }


def reference(x_prompt, x_sample, state_hgrn, state_s5_re, state_s5_im, norm_mix_g, w_in, hgrn_lb_raw,
              hgrn_onorm_g, w_branch_a, s5_lambda_re, s5_lambda_im, s5_log_dt, s5_b_re, s5_b_im, s5_c_re,
              s5_c_im, s5_d, w_glu, b_glu, w_out, norm_ffn_g, w_router_group, b_router_group,
              w_router_expert, b_router_expert, w_exp_gate, w_exp_up, w_exp_down, norm_final_g):
    lb_all = jnp.cumsum(jax.nn.softmax(hgrn_lb_raw.astype(jnp.float32), axis=0), axis=0)

    def trunk(x, s_hg_all, s_re_all, s_im_all):
        h = x
        new_hg, new_re, new_im = [], [], []
        for l in range(DEPTH):
            mix, s_hg, s_re, s_im = token_mixer(
                h, s_hg_all[l], s_re_all[l], s_im_all[l], lb_all[l], norm_mix_g[l], w_in[l], hgrn_onorm_g[l],
                w_branch_a[l], s5_lambda_re[l], s5_lambda_im[l], s5_log_dt[l], s5_b_re[l], s5_b_im[l],
                s5_c_re[l], s5_c_im[l], s5_d[l], w_glu[l], b_glu[l], w_out[l])
            h = h + mix
            h = h + hier_moe(rmsnorm(h, norm_ffn_g[l]), w_router_group[l], b_router_group[l],
                             w_router_expert[l], b_router_expert[l], w_exp_gate[l], w_exp_up[l], w_exp_down[l])
            new_hg.append(s_hg)
            new_re.append(s_re)
            new_im.append(s_im)
        y = rmsnorm(h, norm_final_g).astype(x.dtype)
        return y, jnp.stack(new_hg), jnp.stack(new_re), jnp.stack(new_im)

    bp = x_prompt.shape[0]
    zero_hg = jnp.zeros((DEPTH, bp, HG_HEADS, HG_DK, HG_DV), jnp.float32)
    zero_s5 = jnp.zeros((DEPTH, bp, S5_GROUPS, S5_STATE), jnp.float32)
    y_prompt, hg_p, re_p, im_p = trunk(x_prompt, zero_hg, zero_s5, zero_s5)
    y_sample, hg_s, re_s, im_s = trunk(x_sample, state_hgrn, state_s5_re, state_s5_im)
    return (y_prompt, y_sample, hg_p, re_p, im_p, hg_s, re_s, im_s)
```

```python
import functools

import jax
import jax.numpy as jnp
from jax import lax
from jax.experimental import pallas as pl
from jax.experimental.pallas import tpu as pltpu

F32 = jnp.float32
BF16 = jnp.bfloat16
I32 = jnp.int32

RMS_EPS = 1e-6
HG_CHUNK = 64
MOE_TOP_K = 2

V7X_VMEM_BYTES = 64 * 1024 * 1024
VMEM_LIMIT_BYTES = V7X_VMEM_BYTES - 8 * 1024 * 1024
SUBLANES = 8
LANES = 128

TOKEN_TILE = 512
EXPERT_TILE = 256
S5_TIME_TILE = 32
HGRN_TIME_TILE = 256
HGRN_SEQ_TILE = 8
PROJ_COL_TILE = 512


def _cparams(sem):
    return pltpu.CompilerParams(dimension_semantics=sem, vmem_limit_bytes=VMEM_LIMIT_BYTES)


def _resident(shape):
    nd = len(shape)
    return pl.BlockSpec(shape, lambda *_: (0,) * nd, pipeline_mode=pl.Buffered(1))


def _rmsnorm(x, g):
    return x * lax.rsqrt(jnp.mean(x * x, axis=-1, keepdims=True) + RMS_EPS) * g


def _inproj_body(x_ref, g_ref, w_ref, *rest):
    o_ref = rest[-1]
    xb = _rmsnorm(x_ref[...], g_ref[...]).astype(BF16)
    for j in range(0, w_ref.shape[1], PROJ_COL_TILE):
        o_ref[:, j:j + PROJ_COL_TILE] = jnp.dot(xb, w_ref[:, j:j + PROJ_COL_TILE], preferred_element_type=F32)


def _in_proj(x2d, g, w, prev, total_rows, row_off):
    rows, d = x2d.shape
    n = w.shape[1]
    tm = TOKEN_TILE
    off = row_off // tm
    in_specs = [pl.BlockSpec((tm, d), lambda i: (i, 0)), _resident((1, d)), _resident((d, n))]
    args = [x2d, g, w]
    aliases = {}
    if prev is not None:
        in_specs.append(pl.BlockSpec(memory_space=pl.ANY))
        args.append(prev)
        aliases = {3: 0}
    return pl.pallas_call(
        _inproj_body, grid=(rows // tm,), in_specs=in_specs,
        out_specs=pl.BlockSpec((tm, n), lambda i: (i + off, 0)),
        out_shape=jax.ShapeDtypeStruct((total_rows, n), F32),
        input_output_aliases=aliases, compiler_params=_cparams(("parallel",)), name="in_proj")(*args)


def _s5_body(u_ref, h0r_ref, h0i_ref, ar_ref, ai_ref, bb_ref, cc_ref, d_ref, wg_ref, bg_ref,
             y_ref, hr_out, hi_out, hr_scr, hi_scr, bu_scr, *, tt, nstate):
    j = pl.program_id(1)
    rows = tt * SUBLANES
    half = nstate // 2
    kw = u_ref.shape[-1] // 2

    @pl.when(j == 0)
    def _():
        hr_scr[...] = h0r_ref[0]
        hi_scr[...] = h0i_ref[0]

    u = u_ref[0].reshape(rows, u_ref.shape[-1])
    ub = u.astype(BF16)
    for kt in range(2):
        ukt = ub[:, kt * kw:(kt + 1) * kw]
        bu_scr[:, kt * half:(kt + 1) * half] = jnp.dot(ukt, bb_ref[kt, :, :half], preferred_element_type=F32)
        bu_scr[:, nstate + kt * half:nstate + (kt + 1) * half] = jnp.dot(
            ukt, bb_ref[kt, :, half:], preferred_element_type=F32)

    lane_chunk = 512
    for lc in range(nstate // lane_chunk):
        lo = lc * lane_chunk
        re_sl = slice(lo, lo + lane_chunk)
        im_sl = slice(nstate + lo, nstate + lo + lane_chunk)
        ar = jnp.broadcast_to(ar_ref[:, re_sl], (SUBLANES, lane_chunk))
        ai = jnp.broadcast_to(ai_ref[:, re_sl], (SUBLANES, lane_chunk))

        def step(t, carry, re_sl=re_sl, im_sl=im_sl, ar=ar, ai=ai):
            hr, hi = carry
            r0 = pl.multiple_of(t * SUBLANES, SUBLANES)
            nhr = ar * hr - ai * hi + bu_scr[pl.ds(r0, SUBLANES), re_sl]
            nhi = ar * hi + ai * hr + bu_scr[pl.ds(r0, SUBLANES), im_sl]
            bu_scr[pl.ds(r0, SUBLANES), re_sl] = nhr
            bu_scr[pl.ds(r0, SUBLANES), im_sl] = nhi
            return nhr, nhi

        hr, hi = lax.fori_loop(0, tt, step, (hr_scr[:, re_sl], hi_scr[:, re_sl]), unroll=4)
        hr_scr[:, re_sl] = hr
        hi_scr[:, re_sl] = hi

    ys = []
    for n in range(2):
        h_re = bu_scr[:, n * half:(n + 1) * half].astype(BF16)
        h_im = bu_scr[:, nstate + n * half:nstate + (n + 1) * half].astype(BF16)
        ys.append(jnp.dot(h_re, cc_ref[n, :half, :], preferred_element_type=F32)
                  + jnp.dot(h_im, cc_ref[n, half:, :], preferred_element_type=F32))
    y = jnp.concatenate(ys, axis=-1) + d_ref[...] * u
    z = jnp.dot(jax.nn.gelu(y).astype(BF16), wg_ref[...], preferred_element_type=F32) + bg_ref[...]
    dm = z.shape[-1] // 2
    y_ref[0] = (z[:, :dm] * jax.nn.sigmoid(z[:, dm:])).reshape(tt, SUBLANES, dm)

    @pl.when(j == pl.num_programs(1) - 1)
    def _():
        hr_out[0] = hr_scr[...]
        hi_out[0] = hi_scr[...]


def _s5_branch(u_tb, h0r, h0i, ar, ai, bb, cc, d_skip, w_glu, b_glu):
    nbb, seq, _, w = u_tb.shape
    nstate = ar.shape[-1]
    dm = w_glu.shape[1] // 2
    tt = min(S5_TIME_TILE, seq)
    body = functools.partial(_s5_body, tt=tt, nstate=nstate)
    state_spec = pl.BlockSpec((1, SUBLANES, nstate), lambda b, j: (b, 0, 0))
    return pl.pallas_call(
        body, grid=(nbb, seq // tt),
        in_specs=[pl.BlockSpec((1, tt, SUBLANES, w), lambda b, j: (b, j, 0, 0)), state_spec, state_spec,
                  _resident(ar.shape), _resident(ai.shape), _resident(bb.shape), _resident(cc.shape),
                  _resident(d_skip.shape), _resident(w_glu.shape), _resident(b_glu.shape)],
        out_specs=[pl.BlockSpec((1, tt, SUBLANES, dm), lambda b, j: (b, j, 0, 0)), state_spec, state_spec],
        out_shape=[jax.ShapeDtypeStruct((nbb, seq, SUBLANES, dm), F32),
                   jax.ShapeDtypeStruct((nbb, SUBLANES, nstate), F32),
                   jax.ShapeDtypeStruct((nbb, SUBLANES, nstate), F32)],
        scratch_shapes=[pltpu.VMEM((SUBLANES, nstate), F32), pltpu.VMEM((SUBLANES, nstate), F32),
                        pltpu.VMEM((tt * SUBLANES, 2 * nstate), F32)],
        compiler_params=_cparams(("parallel", "arbitrary")), name="s5_branch")(
            u_tb, h0r, h0i, ar, ai, bb, cc, d_skip, w_glu, b_glu)


def _cumsum_rows(x):
    c = x.shape[0]
    row = lax.broadcasted_iota(I32, x.shape, 0)
    s = 1
    while s < c:
        x = x + jnp.where(row >= s, pltpu.roll(x, s, axis=0), 0.0)
        s *= 2
    return x


def _hgrn_gates(q, fr, lb, scale):
    f = lb + (1.0 - lb) * jax.nn.sigmoid(fr)
    k = 1.0 - f
    b = _cumsum_rows(jnp.log(f))
    b_last = b[-1:, :]
    q_dec = (q * scale) * jnp.exp(b)
    k_dec = k * jnp.exp(-b)
    k_end = k * jnp.exp(b_last - b)
    return q_dec.astype(BF16), k_dec.astype(BF16), k_end.astype(BF16), b_last


def _causal_scores(q_dec, k_dec):
    c = q_dec.shape[0]
    s = lax.dot_general(q_dec, k_dec, (((1,), (1,)), ((), ())), preferred_element_type=F32)
    keep = lax.broadcasted_iota(I32, (c, c), 0) >= lax.broadcasted_iota(I32, (c, c), 1)
    return jnp.where(keep, s, 0.0).astype(BF16)


def _gated_out(o, gn, og):
    o = o * lax.rsqrt(jnp.mean(o * o, axis=-1, keepdims=True) + RMS_EPS) * gn
    return (o * jax.nn.silu(og)).astype(BF16)


def _hgrn_long_body(q_ref, f_ref, v_ref, og_ref, lb_ref, gn_ref, *rest, c, heads, dk, scale):
    o_ref, sfin_ref, st_scr = rest[-3:]
    j = pl.program_id(1)

    @pl.when(j == 0)
    def _():
        st_scr[...] = jnp.zeros_like(st_scr)

    def chunk(ci, carry):
        r0 = pl.multiple_of(ci * c, c)
        rs = pl.ds(r0, c)
        for h in range(heads):
            hs = slice(h * dk, (h + 1) * dk)
            q_dec, k_dec, k_end, b_last = _hgrn_gates(q_ref[rs, hs], f_ref[rs, hs], lb_ref[:, hs], scale)
            v = v_ref[rs, hs].astype(BF16)
            scores = _causal_scores(q_dec, k_dec)
            st = st_scr[h]
            o = (lax.dot_general(q_dec, st.astype(BF16), (((1,), (1,)), ((), ())), preferred_element_type=F32)
                 + jnp.dot(scores, v, preferred_element_type=F32))
            st_scr[h] = jnp.exp(b_last) * st + lax.dot_general(
                v, k_end, (((0,), (0,)), ((), ())), preferred_element_type=F32)
            o_ref[rs, hs] = _gated_out(o, gn_ref[:, hs], og_ref[rs, hs])
        return carry

    lax.fori_loop(0, q_ref.shape[0] // c, chunk, 0)

    @pl.when(j == pl.num_programs(1) - 1)
    def _():
        for h in range(heads):
            sfin_ref[0, h] = st_scr[h].T


def _hgrn_long(proj, lb, gn, prev, batch, seq, heads, dk, row_off):
    total_rows = proj.shape[0]
    width = heads * dk
    tb = min(HGRN_TIME_TILE, seq)
    nj = seq // tb
    off = row_off // tb
    body = functools.partial(_hgrn_long_body, c=HG_CHUNK, heads=heads, dk=dk, scale=dk ** -0.5)

    def col(k):
        return pl.BlockSpec((tb, width), lambda b, j, k=k: (off + b * nj + j, k))

    in_specs = [col(0), col(1), col(2), col(3), _resident(lb.shape), _resident(gn.shape)]
    args = [proj, proj, proj, proj, lb, gn]
    aliases = {}
    if prev is not None:
        in_specs.append(pl.BlockSpec(memory_space=pl.ANY))
        args.append(prev)
        aliases = {6: 0}
    return pl.pallas_call(
        body, grid=(batch, nj), in_specs=in_specs,
        out_specs=[pl.BlockSpec((tb, width), lambda b, j: (off + b * nj + j, 0)),
                   pl.BlockSpec((1, heads, dk, dk), lambda b, j: (b, 0, 0, 0))],
        out_shape=[jax.ShapeDtypeStruct((total_rows, width), BF16),
                   jax.ShapeDtypeStruct((batch, heads, dk, dk), F32)],
        scratch_shapes=[pltpu.VMEM((heads, dk, dk), F32)],
        input_output_aliases=aliases,
        compiler_params=_cparams(("parallel", "arbitrary")), name="hgrn_long")(*args)


def _hgrn_short_body(q_ref, f_ref, v_ref, og_ref, lb_ref, gn_ref, s0_ref, *rest, c, heads, dk, scale):
    o_ref, snew_ref = rest[-2:]

    def one_seq(sq, carry):
        r0 = pl.multiple_of(sq * c, c)
        rs = pl.ds(r0, c)
        for h in range(heads):
            hs = slice(h * dk, (h + 1) * dk)
            q_dec, k_dec, k_end, b_last = _hgrn_gates(q_ref[rs, hs], f_ref[rs, hs], lb_ref[:, hs], scale)
            v = v_ref[rs, hs].astype(BF16)
            scores = _causal_scores(q_dec, k_dec)
            s0 = s0_ref[sq, h]
            o = (jnp.dot(q_dec, s0.astype(BF16), preferred_element_type=F32)
                 + jnp.dot(scores, v, preferred_element_type=F32))
            decay_col = jnp.broadcast_to(jnp.exp(b_last), (dk, dk)).T
            snew_ref[sq, h] = decay_col * s0 + lax.dot_general(
                k_end, v, (((0,), (0,)), ((), ())), preferred_element_type=F32)
            o_ref[rs, hs] = _gated_out(o, gn_ref[:, hs], og_ref[rs, hs])
        return carry

    lax.fori_loop(0, s0_ref.shape[0], one_seq, 0)


def _hgrn_short(proj, lb, gn, s0, prev, seq, row_off):
    batch, heads, dk, _ = s0.shape
    total_rows = proj.shape[0]
    width = heads * dk
    nb = HGRN_SEQ_TILE
    rows = nb * seq
    off = row_off // rows
    body = functools.partial(_hgrn_short_body, c=seq, heads=heads, dk=dk, scale=dk ** -0.5)

    def col(k):
        return pl.BlockSpec((rows, width), lambda i, k=k: (off + i, k))

    state_spec = pl.BlockSpec((nb, heads, dk, dk), lambda i: (i, 0, 0, 0))
    in_specs = [col(0), col(1), col(2), col(3), _resident(lb.shape), _resident(gn.shape), state_spec]
    args = [proj, proj, proj, proj, lb, gn, s0]
    aliases = {}
    if prev is not None:
        in_specs.append(pl.BlockSpec(memory_space=pl.ANY))
        args.append(prev)
        aliases = {7: 0}
    return pl.pallas_call(
        body, grid=(batch // nb,), in_specs=in_specs,
        out_specs=[pl.BlockSpec((rows, width), lambda i: (off + i, 0)), state_spec],
        out_shape=[jax.ShapeDtypeStruct((total_rows, width), BF16),
                   jax.ShapeDtypeStruct(s0.shape, F32)],
        input_output_aliases=aliases,
        compiler_params=_cparams(("parallel",)), name="hgrn_short")(*args)


def _merge_body(o_ref, yb_ref, ga_ref, gb_ref, x_ref, wa_ref, wo_ref, gf_ref, wr_ref, br_ref, *rest):
    h_ref, xn_ref, lg_ref = rest[-3:]
    y_a = jnp.dot(o_ref[...], wa_ref[...], preferred_element_type=F32)
    merged = jax.nn.sigmoid(ga_ref[...]) * y_a + jax.nn.sigmoid(gb_ref[...]) * yb_ref[...]
    h = x_ref[...] + jnp.dot(merged.astype(BF16), wo_ref[...], preferred_element_type=F32)
    h_ref[...] = h
    xn = _rmsnorm(h, gf_ref[...])
    xn_ref[...] = xn
    lg_ref[...] = lax.dot_general(wr_ref[...], xn.astype(BF16), (((1,), (1,)), ((), ())),
                                  preferred_element_type=F32) + br_ref[...]


def _merge(o_all, yb_all, proj, x2d, wa, wo, gf, wr, br, prevs, row_off):
    rows, d = x2d.shape
    total_rows = proj.shape[0]
    nr = wr.shape[0]
    tm = TOKEN_TILE
    off = row_off // tm

    def row(width, k=0):
        return pl.BlockSpec((tm, width), lambda i, k=k: (off + i, k))

    in_specs = [row(d), row(d), row(d, 4), row(d, 5), pl.BlockSpec((tm, d), lambda i: (i, 0)),
                _resident(wa.shape), _resident(wo.shape), _resident(gf.shape), _resident(wr.shape),
                _resident(br.shape)]
    args = [o_all, yb_all, proj, proj, x2d, wa, wo, gf, wr, br]
    aliases = {}
    if prevs is not None:
        for k, p in enumerate(prevs):
            in_specs.append(pl.BlockSpec(memory_space=pl.ANY))
            args.append(p)
            aliases[10 + k] = k
    return pl.pallas_call(
        _merge_body, grid=(rows // tm,), in_specs=in_specs,
        out_specs=[row(d), row(d), pl.BlockSpec((nr, tm), lambda i: (0, off + i))],
        out_shape=[jax.ShapeDtypeStruct((total_rows, d), F32), jax.ShapeDtypeStruct((total_rows, d), F32),
                   jax.ShapeDtypeStruct((nr, total_rows), F32)],
        input_output_aliases=aliases, compiler_params=_cparams(("parallel",)), name="merge_out")(*args)


def _first_index_of_max(vals):
    m = vals[0]
    for v in vals[1:]:
        m = jnp.maximum(m, v)
    idx = jnp.full(m.shape, len(vals), I32)
    for e in range(len(vals) - 1, -1, -1):
        idx = jnp.where(vals[e] == m, e, idx)
    return m, idx


def _route_body(lg_ref, ids_ref, w_ref, rk_ref, cnt_ref, carry_scr, *, groups, experts):
    i = pl.program_id(0)
    tile = lg_ref.shape[1]
    n_exp = groups * experts

    @pl.when(i == 0)
    def _():
        carry_scr[...] = jnp.zeros_like(carry_scr)

    gl = [lg_ref[g:g + 1, :] for g in range(groups)]
    gmax, gidx = _first_index_of_max(gl)
    denom = jnp.exp(gl[0] - gmax)
    for g in range(1, groups):
        denom = denom + jnp.exp(gl[g] - gmax)
    g_w = 1.0 / denom

    el = []
    for e in range(experts):
        v = lg_ref[groups + e:groups + e + 1, :]
        for g in range(1, groups):
            r = groups + g * experts + e
            v = jnp.where(gidx == g, lg_ref[r:r + 1, :], v)
        el.append(v)
    v1, i1 = _first_index_of_max(el)
    rest = [jnp.where(i1 == e, -jnp.inf, el[e]) for e in range(experts)]
    v2, i2 = _first_index_of_max(rest)
    t = jnp.exp(v2 - v1)
    inv = 1.0 / (1.0 + t)
    e1 = gidx * experts + i1
    e2 = gidx * experts + i2

    erow = lax.broadcasted_iota(I32, (n_exp, tile), 0)
    oh1 = (erow == e1).astype(F32)
    oh2 = (erow == e2).astype(F32)
    oh = oh1 + oh2
    before = (lax.broadcasted_iota(I32, (tile, tile), 0) < lax.broadcasted_iota(I32, (tile, tile), 1))
    cnt = jnp.dot(oh.astype(BF16), before.astype(BF16), preferred_element_type=F32) + carry_scr[:, 0:1]
    ids_ref[0:1, :] = e1
    ids_ref[1:2, :] = e2
    w_ref[0:1, :] = inv * g_w
    w_ref[1:2, :] = (t * inv) * g_w
    rk_ref[0:1, :] = jnp.sum(oh1 * cnt, axis=0, keepdims=True).astype(I32)
    rk_ref[1:2, :] = jnp.sum(oh2 * cnt, axis=0, keepdims=True).astype(I32)
    carry_scr[...] = carry_scr[...] + jnp.sum(oh, axis=1, keepdims=True)

    @pl.when(i == pl.num_programs(0) - 1)
    def _():
        cnt_ref[...] = carry_scr[...]


def _route(logits_t, groups, experts):
    nr, total = logits_t.shape
    tile = TOKEN_TILE
    n_exp = groups * experts
    body = functools.partial(_route_body, groups=groups, experts=experts)
    pair = pl.BlockSpec((MOE_TOP_K, tile), lambda i: (0, i))
    return pl.pallas_call(
        body, grid=(total // tile,),
        in_specs=[pl.BlockSpec((nr, tile), lambda i: (0, i))],
        out_specs=[pair, pair, pair, pl.BlockSpec((n_exp, LANES), lambda i: (0, 0))],
        out_shape=[jax.ShapeDtypeStruct((MOE_TOP_K, total), I32), jax.ShapeDtypeStruct((MOE_TOP_K, total), F32),
                   jax.ShapeDtypeStruct((MOE_TOP_K, total), I32), jax.ShapeDtypeStruct((n_exp, LANES), F32)],
        scratch_shapes=[pltpu.VMEM((n_exp, LANES), F32)],
        compiler_params=_cparams(("arbitrary",)), name="route")(logits_t)


def _row_copy(src, dst, sem):
    return pltpu.make_async_copy(src, dst, sem)


def _dispatch_body(pos_ref, x_ref, xs_hbm, o_hbm, sem):
    del xs_hbm
    tile = x_ref.shape[0]

    def issue(r, carry):
        for k in range(MOE_TOP_K):
            p = pos_ref[0, k, r]
            _row_copy(x_ref.at[pl.ds(r, 1)], o_hbm.at[pl.ds(p, 1)], sem.at[k]).start()
        return carry

    lax.fori_loop(0, tile, issue, 0, unroll=8)
    for k in range(MOE_TOP_K):
        _row_copy(x_ref, o_hbm.at[pl.ds(0, tile)], sem.at[k]).wait()


def _dispatch(pos3, xn, n_sorted):
    total, d = xn.shape
    tile = TOKEN_TILE
    return pl.pallas_call(
        _dispatch_body, grid=(total // tile,),
        in_specs=[pl.BlockSpec((1, MOE_TOP_K, tile), lambda i: (i, 0, 0), memory_space=pltpu.SMEM),
                  pl.BlockSpec((tile, d), lambda i: (i, 0)),
                  pl.BlockSpec(memory_space=pl.ANY)],
        out_specs=pl.BlockSpec(memory_space=pl.ANY),
        out_shape=jax.ShapeDtypeStruct((n_sorted, d), F32),
        scratch_shapes=[pltpu.SemaphoreType.DMA((MOE_TOP_K,))],
        input_output_aliases={2: 0},
        compiler_params=_cparams(("arbitrary",)), name="dispatch")(
            pos3, xn, jnp.zeros((n_sorted, d), F32))


def _experts_body(it_tile, it_exp, it_lo, it_hi, n_items, x_ref, wg_ref, wu_ref, wd_ref, o_ref):
    j = pl.program_id(0)
    tm = x_ref.shape[0]
    first = jnp.logical_or(j == 0, it_tile[j] != it_tile[jnp.maximum(j - 1, 0)])

    @pl.when(jnp.logical_and(first, j < n_items[0]))
    def _():
        o_ref[...] = jnp.zeros_like(o_ref)

    @pl.when(j < n_items[0])
    def _():
        x = x_ref[...].astype(BF16)
        hg = jnp.dot(x, wg_ref[0], preferred_element_type=F32)
        hu = jnp.dot(x, wu_ref[0], preferred_element_type=F32)
        hid = (jax.nn.silu(hg) * hu).astype(BF16)
        out = jnp.dot(hid, wd_ref[0], preferred_element_type=F32)
        rows = lax.broadcasted_iota(I32, (tm, 1), 0)
        mine = jnp.logical_and(rows >= it_lo[j], rows < it_hi[j])
        o_ref[...] = jnp.where(mine, out, o_ref[...])


def _experts(items, xs, wg, wu, wd):
    n_sorted, d = xs.shape
    n_exp, _, de = wg.shape
    tm = EXPERT_TILE
    max_items = items[0].shape[0]
    grid_spec = pltpu.PrefetchScalarGridSpec(
        num_scalar_prefetch=5, grid=(max_items,),
        in_specs=[pl.BlockSpec((tm, d), lambda j, t, e, lo, hi, n: (t[j], 0)),
                  pl.BlockSpec((1, d, de), lambda j, t, e, lo, hi, n: (e[j], 0, 0)),
                  pl.BlockSpec((1, d, de), lambda j, t, e, lo, hi, n: (e[j], 0, 0)),
                  pl.BlockSpec((1, de, d), lambda j, t, e, lo, hi, n: (e[j], 0, 0))],
        out_specs=pl.BlockSpec((tm, d), lambda j, t, e, lo, hi, n: (t[j], 0)))
    return pl.pallas_call(
        _experts_body, grid_spec=grid_spec, out_shape=jax.ShapeDtypeStruct((n_sorted, d), F32),
        compiler_params=_cparams(("arbitrary",)), name="experts")(*items, xs, wg, wu, wd)


def _combine_body(pos_ref, h_ref, w_ref, g_ref, ys_hbm, y_ref, buf, sem, *, final_norm):
    tile = h_ref.shape[0]

    def issue(r, carry):
        for k in range(MOE_TOP_K):
            p = pos_ref[0, k, r]
            _row_copy(ys_hbm.at[pl.ds(p, 1)], buf.at[k, pl.ds(r, 1)], sem.at[k]).start()
        return carry

    lax.fori_loop(0, tile, issue, 0, unroll=8)
    for k in range(MOE_TOP_K):
        _row_copy(ys_hbm.at[pl.ds(0, tile)], buf.at[k], sem.at[k]).wait()
    h = h_ref[...] + (w_ref[:, 0:1] * buf[0] + w_ref[:, 1:2] * buf[1])
    y_ref[...] = _rmsnorm(h, g_ref[...]) if final_norm else h


def _combine(pos3, h_all, w_t, g, ys, rows, row_off, final_norm):
    d = h_all.shape[1]
    tile = TOKEN_TILE
    off = row_off // tile
    return pl.pallas_call(
        functools.partial(_combine_body, final_norm=final_norm), grid=(rows // tile,),
        in_specs=[pl.BlockSpec((1, MOE_TOP_K, tile), lambda i: (off + i, 0, 0), memory_space=pltpu.SMEM),
                  pl.BlockSpec((tile, d), lambda i: (off + i, 0)),
                  pl.BlockSpec((tile, MOE_TOP_K), lambda i: (off + i, 0)),
                  _resident(g.shape),
                  pl.BlockSpec(memory_space=pl.ANY)],
        out_specs=pl.BlockSpec((tile, d), lambda i: (i, 0)),
        out_shape=jax.ShapeDtypeStruct((rows, d), F32),
        scratch_shapes=[pltpu.VMEM((MOE_TOP_K, tile, d), F32), pltpu.SemaphoreType.DMA((MOE_TOP_K,))],
        compiler_params=_cparams(("arbitrary",)), name="combine")(pos3, h_all, w_t, g, ys)


def _work_items(counts, n_tiles):
    tm = EXPERT_TILE
    n_exp = counts.shape[0]
    max_items = n_tiles + n_exp - 1
    ends = jnp.cumsum(counts)
    starts = ends - counts
    first_tile = starts // tm
    n_e = jnp.where(counts > 0, (ends - 1) // tm - first_tile + 1, 0)
    item_end = jnp.cumsum(n_e)
    item_start = item_end - n_e
    n_items = item_end[-1]
    j = jnp.minimum(jnp.arange(max_items, dtype=I32), n_items - 1)
    e = jnp.searchsorted(item_end, j, side="right").astype(I32)
    t = first_tile[e] + (j - item_start[e])
    lo = jnp.maximum(starts[e], t * tm) - t * tm
    hi = jnp.minimum(ends[e], (t + 1) * tm) - t * tm
    return (t.astype(I32), e, lo.astype(I32), hi.astype(I32), n_items.reshape(1).astype(I32)), starts


def _s5_discretise(lam_re, lam_im, log_dt, b_re, b_im, c_re, c_im):
    g, p = lam_re.shape
    ch = b_re.shape[-1]
    lam_re = lam_re.astype(F32)
    lam_im = lam_im.astype(F32)
    dt = jnp.exp(log_dt.astype(F32))[:, None]
    mag = jnp.exp(lam_re * dt)
    ab_re = mag * jnp.cos(lam_im * dt)
    ab_im = mag * jnp.sin(lam_im * dt)
    den = lam_re * lam_re + lam_im * lam_im
    nr = ab_re - 1.0
    coef_re = (nr * lam_re + ab_im * lam_im) / den
    coef_im = (ab_im * lam_re - nr * lam_im) / den
    bb_re = coef_re[..., None] * b_re - coef_im[..., None] * b_im
    bb_im = coef_re[..., None] * b_im + coef_im[..., None] * b_re
    gh = g // 2
    eye = jnp.eye(gh, dtype=F32)

    def in_block(m):
        return jnp.einsum("gpc,gh->gchp", m, eye).reshape(gh * ch, gh * p)

    def out_block(m):
        return jnp.einsum("gcp,gh->gphc", m, eye).reshape(gh * p, gh * ch)

    bb = jnp.stack([jnp.concatenate([in_block(bb_re[k * gh:(k + 1) * gh]), in_block(bb_im[k * gh:(k + 1) * gh])],
                                    axis=1) for k in range(2)]).astype(BF16)
    cc = jnp.stack([jnp.concatenate([out_block(c_re[k * gh:(k + 1) * gh]), out_block(-c_im[k * gh:(k + 1) * gh])],
                                    axis=0) for k in range(2)]).astype(BF16)
    return ab_re.reshape(1, g * p), ab_im.reshape(1, g * p), bb, cc


def _to_time_major(x, batch, seq):
    w = x.shape[-1]
    return x.reshape(batch // SUBLANES, SUBLANES, seq, w).transpose(0, 2, 1, 3)


def _from_time_major(x):
    nbb, seq, _, w = x.shape
    return x.transpose(0, 2, 1, 3).reshape(nbb * SUBLANES * seq, w)


def kernel(x_prompt, x_sample, state_hgrn, state_s5_re, state_s5_im, norm_mix_g, w_in, hgrn_lb_raw, hgrn_onorm_g, w_branch_a, s5_lambda_re, s5_lambda_im, s5_log_dt, s5_b_re, s5_b_im, s5_c_re, s5_c_im, s5_d, w_glu, b_glu, w_out, norm_ffn_g, w_router_group, b_router_group, w_router_expert, b_router_expert, w_exp_gate, w_exp_up, w_exp_down, norm_final_g):
    depth = norm_mix_g.shape[0]
    bp, lp, d = x_prompt.shape
    bs, ls, _ = x_sample.shape
    heads, dk = state_hgrn.shape[2], state_hgrn.shape[3]
    kw = heads * dk
    s5_groups, s5_state = state_s5_re.shape[2], state_s5_re.shape[3]
    s5_width = s5_d.shape[-1]
    nstate = s5_groups * s5_state
    moe_groups, _, experts = w_router_expert.shape[1:]
    n_exp = moe_groups * experts
    rows_p, rows_s = bp * lp, bs * ls
    total = rows_p + rows_s
    n_sorted = total * MOE_TOP_K

    assert kw == d and state_hgrn.shape[4] == dk, "column blocks assume key width == value width == model width"
    assert s5_groups % 2 == 0 and bp % SUBLANES == 0 and bs % HGRN_SEQ_TILE == 0

    lb_all = jnp.cumsum(jax.nn.softmax(hgrn_lb_raw.astype(F32), axis=0), axis=0)

    hp = x_prompt.reshape(rows_p, d)
    hs = x_sample.reshape(rows_s, d)
    hg_p, re_p, im_p, hg_s, re_s, im_s = [], [], [], [], [], []
    zeros_state = jnp.zeros((bp // SUBLANES, SUBLANES, nstate), F32)

    for l in range(depth):
        w = w_in[l]
        w_cols = jnp.concatenate([w[:, :4 * kw], w[:, 4 * kw + s5_width:], w[:, 4 * kw:4 * kw + s5_width]],
                                 axis=1).astype(BF16)
        g_mix = norm_mix_g[l].reshape(1, d)
        proj = _in_proj(hp, g_mix, w_cols, None, total, 0)
        proj = _in_proj(hs, g_mix, w_cols, proj, total, rows_p)

        ar, ai, bb, cc = _s5_discretise(s5_lambda_re[l], s5_lambda_im[l], s5_log_dt[l], s5_b_re[l], s5_b_im[l],
                                        s5_c_re[l], s5_c_im[l])
        u_all = proj[:, 6 * kw:]
        s5_args = (ar, ai, bb, cc, s5_d[l].reshape(1, s5_width), w_glu[l].astype(BF16),
                   b_glu[l].reshape(1, -1))
        yb_p, fr_p, fi_p = _s5_branch(_to_time_major(u_all[:rows_p], bp, lp), zeros_state, zeros_state, *s5_args)
        yb_s, fr_s, fi_s = _s5_branch(
            _to_time_major(u_all[rows_p:], bs, ls),
            state_s5_re[l].reshape(bs // SUBLANES, SUBLANES, nstate),
            state_s5_im[l].reshape(bs // SUBLANES, SUBLANES, nstate), *s5_args)
        yb_all = jnp.concatenate([_from_time_major(yb_p), _from_time_major(yb_s)], axis=0)

        lb = lb_all[l].reshape(1, kw)
        gn = hgrn_onorm_g[l].reshape(1, kw)
        o_all, hgp = _hgrn_long(proj, lb, gn, None, bp, lp, heads, dk, 0)
        o_all, hgs = _hgrn_short(proj, lb, gn, state_hgrn[l].astype(F32), o_all, ls, rows_p)

        nr = -(-(moe_groups + n_exp) // SUBLANES) * SUBLANES
        wr = jnp.concatenate([w_router_group[l].T, w_router_expert[l].transpose(0, 2, 1).reshape(n_exp, d)], axis=0)
        wr = jnp.pad(wr, ((0, nr - wr.shape[0]), (0, 0))).astype(BF16)
        br = jnp.pad(jnp.concatenate([b_router_group[l], b_router_expert[l].reshape(n_exp)]),
                     (0, nr - moe_groups - n_exp)).reshape(nr, 1).astype(F32)
        merge_w = (w_branch_a[l].astype(BF16), w_out[l].astype(BF16), norm_ffn_g[l].reshape(1, d), wr, br)
        outs = _merge(o_all, yb_all, proj, hp, *merge_w, None, 0)
        h_all, xn_all, logits_t = _merge(o_all, yb_all, proj, hs, *merge_w, outs, rows_p)

        ids, wts, ranks, cnt = _route(logits_t, moe_groups, experts)
        counts = cnt[:, 0].astype(I32)
        items, starts = _work_items(counts, n_sorted // EXPERT_TILE)
        pos = starts[ids] + ranks
        nt = total // TOKEN_TILE
        pos3 = pos.reshape(MOE_TOP_K, nt, TOKEN_TILE).transpose(1, 0, 2)
        xs = _dispatch(pos3, xn_all, n_sorted)
        ys = _experts(items, xs, w_exp_gate[l].astype(BF16), w_exp_up[l].astype(BF16), w_exp_down[l].astype(BF16))

        last = l == depth - 1
        g_out = norm_final_g.reshape(1, d)
        hp = _combine(pos3, h_all, wts.T, g_out, ys, rows_p, 0, last)
        hs = _combine(pos3, h_all, wts.T, g_out, ys, rows_s, rows_p, last)

        hg_p.append(hgp)
        hg_s.append(hgs)
        re_p.append(fr_p.reshape(bp, s5_groups, s5_state))
        im_p.append(fi_p.reshape(bp, s5_groups, s5_state))
        re_s.append(fr_s.reshape(bs, s5_groups, s5_state))
        im_s.append(fi_s.reshape(bs, s5_groups, s5_state))

    y_prompt = hp.reshape(bp, lp, d).astype(x_prompt.dtype)
    y_sample = hs.reshape(bs, ls, d).astype(x_sample.dtype)
    return (y_prompt, y_sample, jnp.stack(hg_p), jnp.stack(re_p), jnp.stack(im_p),
            jnp.stack(hg_s), jnp.stack(re_s), jnp.stack(im_s))
```

```python
import functools

import jax
import jax.numpy as jnp
from jax import lax
from jax.experimental import pallas as pl
from jax.experimental.pallas import tpu as pltpu

F32 = jnp.float32
BF16 = jnp.bfloat16
I32 = jnp.int32

RMS_EPS = 1e-6
HG_CHUNK = 64
MOE_TOP_K = 2

V7X_VMEM_BYTES = 64 * 1024 * 1024
VMEM_LIMIT_BYTES = V7X_VMEM_BYTES - 8 * 1024 * 1024
SUBLANES = 8
LANES = 128

TOKEN_TILE = 512
EXPERT_TILE = 256
S5_TIME_TILE = 32
HGRN_TIME_TILE = 256
HGRN_SEQ_TILE = 8
PROJ_COL_TILE = 512


def _cparams(sem):
    return pltpu.CompilerParams(dimension_semantics=sem, vmem_limit_bytes=VMEM_LIMIT_BYTES)


def _resident(shape):
    nd = len(shape)
    return pl.BlockSpec(shape, lambda *_: (0,) * nd, pipeline_mode=pl.Buffered(1))


def _rmsnorm(x, g):
    return x * lax.rsqrt(jnp.mean(x * x, axis=-1, keepdims=True) + RMS_EPS) * g


def _two_source_specs(tm, width, n_first):
    return [pl.BlockSpec((tm, width), lambda i: (jnp.minimum(i, n_first - 1), 0)),
            pl.BlockSpec((tm, width), lambda i: (jnp.maximum(i - n_first, 0), 0))]


def _pick(first_ref, second_ref, n_first):
    return jnp.where(pl.program_id(0) < n_first, first_ref[...], second_ref[...])


def _inproj_body(xp_ref, xs_ref, g_ref, w_ref, o_ref, *, n_first):
    xb = _rmsnorm(_pick(xp_ref, xs_ref, n_first), g_ref[...]).astype(BF16)
    for j in range(0, w_ref.shape[1], PROJ_COL_TILE):
        o_ref[:, j:j + PROJ_COL_TILE] = jnp.dot(xb, w_ref[:, j:j + PROJ_COL_TILE], preferred_element_type=F32)


def _in_proj(xp, xs, g, w):
    d = xp.shape[1]
    n = w.shape[1]
    tm = TOKEN_TILE
    total = xp.shape[0] + xs.shape[0]
    n_first = xp.shape[0] // tm
    return pl.pallas_call(
        functools.partial(_inproj_body, n_first=n_first), grid=(total // tm,),
        in_specs=_two_source_specs(tm, d, n_first) + [_resident((1, d)), _resident((d, n))],
        out_specs=pl.BlockSpec((tm, n), lambda i: (i, 0)),
        out_shape=jax.ShapeDtypeStruct((total, n), F32),
        compiler_params=_cparams(("parallel",)), name="in_proj")(xp, xs, g, w)


def _s5_body(*refs, tt, nstate):
    u_refs = refs[:SUBLANES]
    (h0r_ref, h0i_ref, ar_ref, ai_ref, bb_ref, cc_ref, d_ref, wg_ref, bg_ref,
     y_ref, hr_out, hi_out, hr_scr, hi_scr, bu_scr, u_scr, y_scr) = refs[SUBLANES:]
    j = pl.program_id(1)
    half = nstate // 2
    w = u_refs[0].shape[-1]
    kw = w // 2

    @pl.when(j == 0)
    def _():
        hr_scr[...] = h0r_ref[0]
        hi_scr[...] = h0i_ref[0]

    for b in range(SUBLANES):
        ub = u_refs[b][...]
        for s in range(w // LANES):
            u_scr[s, pl.ds(b, tt, stride=SUBLANES), :] = ub[:, s * LANES:(s + 1) * LANES]
    u = jnp.concatenate([u_scr[s] for s in range(w // LANES)], axis=-1)
    ub16 = u.astype(BF16)
    for kt in range(2):
        ukt = ub16[:, kt * kw:(kt + 1) * kw]
        bu_scr[:, kt * half:(kt + 1) * half] = jnp.dot(ukt, bb_ref[kt, :, :half], preferred_element_type=F32)
        bu_scr[:, nstate + kt * half:nstate + (kt + 1) * half] = jnp.dot(
            ukt, bb_ref[kt, :, half:], preferred_element_type=F32)

    lane_chunk = 512
    for lc in range(nstate // lane_chunk):
        lo = lc * lane_chunk
        re_sl = slice(lo, lo + lane_chunk)
        im_sl = slice(nstate + lo, nstate + lo + lane_chunk)
        ar = jnp.broadcast_to(ar_ref[:, re_sl], (SUBLANES, lane_chunk))
        ai = jnp.broadcast_to(ai_ref[:, re_sl], (SUBLANES, lane_chunk))

        def step(t, carry, re_sl=re_sl, im_sl=im_sl, ar=ar, ai=ai):
            hr, hi = carry
            rs = pl.ds(pl.multiple_of(t * SUBLANES, SUBLANES), SUBLANES)
            nhr = ar * hr - ai * hi + bu_scr[rs, re_sl]
            nhi = ar * hi + ai * hr + bu_scr[rs, im_sl]
            bu_scr[rs, re_sl] = nhr
            bu_scr[rs, im_sl] = nhi
            return nhr, nhi

        hr, hi = lax.fori_loop(0, tt, step, (hr_scr[:, re_sl], hi_scr[:, re_sl]), unroll=4)
        hr_scr[:, re_sl] = hr
        hi_scr[:, re_sl] = hi

    ys = []
    for n in range(2):
        h_re = bu_scr[:, n * half:(n + 1) * half].astype(BF16)
        h_im = bu_scr[:, nstate + n * half:nstate + (n + 1) * half].astype(BF16)
        ys.append(jnp.dot(h_re, cc_ref[n, :half, :], preferred_element_type=F32)
                  + jnp.dot(h_im, cc_ref[n, half:, :], preferred_element_type=F32))
    y = jnp.concatenate(ys, axis=-1) + d_ref[...] * u
    z = jnp.dot(jax.nn.gelu(y).astype(BF16), wg_ref[...], preferred_element_type=F32) + bg_ref[...]
    dm = z.shape[-1] // 2
    yb = z[:, :dm] * jax.nn.sigmoid(z[:, dm:])
    for s in range(dm // LANES):
        y_scr[s] = yb[:, s * LANES:(s + 1) * LANES]
    for b in range(SUBLANES):
        for s in range(dm // LANES):
            y_ref[b, :, s * LANES:(s + 1) * LANES] = y_scr[s, pl.ds(b, tt, stride=SUBLANES), :]

    @pl.when(j == pl.num_programs(1) - 1)
    def _():
        hr_out[0] = hr_scr[...]
        hi_out[0] = hi_scr[...]


def _s5_branch(proj, u_col, width, batch, seq, row_off, h0r, h0i, ar, ai, bb, cc, d_skip, w_glu, b_glu):
    nstate = ar.shape[-1]
    dm = w_glu.shape[1] // 2
    tt = min(S5_TIME_TILE, seq)
    nj = seq // tt
    nbb = batch // SUBLANES
    body = functools.partial(_s5_body, tt=tt, nstate=nstate)

    def u_spec(b):
        return pl.BlockSpec((tt, width), lambda bb_, j, b=b: (row_off // tt + (bb_ * SUBLANES + b) * nj + j,
                                                               u_col // width))

    state_spec = pl.BlockSpec((1, SUBLANES, nstate), lambda bb_, j: (bb_, 0, 0))
    return pl.pallas_call(
        body, grid=(nbb, nj),
        in_specs=[u_spec(b) for b in range(SUBLANES)] + [
            state_spec, state_spec, _resident(ar.shape), _resident(ai.shape), _resident(bb.shape),
            _resident(cc.shape), _resident(d_skip.shape), _resident(w_glu.shape), _resident(b_glu.shape)],
        out_specs=[pl.BlockSpec((SUBLANES, tt, dm), lambda bb_, j: (bb_, j, 0)), state_spec, state_spec],
        out_shape=[jax.ShapeDtypeStruct((batch, seq, dm), F32),
                   jax.ShapeDtypeStruct((nbb, SUBLANES, nstate), F32),
                   jax.ShapeDtypeStruct((nbb, SUBLANES, nstate), F32)],
        scratch_shapes=[pltpu.VMEM((SUBLANES, nstate), F32), pltpu.VMEM((SUBLANES, nstate), F32),
                        pltpu.VMEM((tt * SUBLANES, 2 * nstate), F32),
                        pltpu.VMEM((width // LANES, tt * SUBLANES, LANES), F32),
                        pltpu.VMEM((dm // LANES, tt * SUBLANES, LANES), F32)],
        compiler_params=_cparams(("parallel", "arbitrary")), name="s5_branch")(
            *([proj] * SUBLANES), h0r, h0i, ar, ai, bb, cc, d_skip, w_glu, b_glu)


def _cumsum_rows(x):
    c = x.shape[0]
    row = lax.broadcasted_iota(I32, x.shape, 0)
    s = 1
    while s < c:
        x = x + jnp.where(row >= s, pltpu.roll(x, s, axis=0), 0.0)
        s *= 2
    return x


def _hgrn_gates(q, fr, lb, scale):
    f = lb + (1.0 - lb) * jax.nn.sigmoid(fr)
    k = 1.0 - f
    b = _cumsum_rows(jnp.log(f))
    b_last = b[-1:, :]
    q_dec = (q * scale) * jnp.exp(b)
    k_dec = k * jnp.exp(-b)
    k_end = k * jnp.exp(b_last - b)
    return q_dec.astype(BF16), k_dec.astype(BF16), k_end.astype(BF16), b_last


def _causal_scores(q_dec, k_dec):
    c = q_dec.shape[0]
    s = lax.dot_general(q_dec, k_dec, (((1,), (1,)), ((), ())), preferred_element_type=F32)
    keep = lax.broadcasted_iota(I32, (c, c), 0) >= lax.broadcasted_iota(I32, (c, c), 1)
    return jnp.where(keep, s, 0.0).astype(BF16)


def _gated_out(o, gn, og):
    o = o * lax.rsqrt(jnp.mean(o * o, axis=-1, keepdims=True) + RMS_EPS) * gn
    return (o * jax.nn.silu(og)).astype(BF16)


def _hgrn_long_body(q_ref, f_ref, v_ref, og_ref, lb_ref, gn_ref, o_ref, sfin_ref, st_scr, *, c, heads, dk, scale):
    j = pl.program_id(1)

    @pl.when(j == 0)
    def _():
        st_scr[...] = jnp.zeros_like(st_scr)

    def chunk(ci, carry):
        rs = pl.ds(pl.multiple_of(ci * c, c), c)
        for h in range(heads):
            hs = slice(h * dk, (h + 1) * dk)
            q_dec, k_dec, k_end, b_last = _hgrn_gates(q_ref[rs, hs], f_ref[rs, hs], lb_ref[:, hs], scale)
            v = v_ref[rs, hs].astype(BF16)
            scores = _causal_scores(q_dec, k_dec)
            st = st_scr[h]
            o = (lax.dot_general(q_dec, st.astype(BF16), (((1,), (1,)), ((), ())), preferred_element_type=F32)
                 + jnp.dot(scores, v, preferred_element_type=F32))
            st_scr[h] = jnp.exp(b_last) * st + lax.dot_general(
                v, k_end, (((0,), (0,)), ((), ())), preferred_element_type=F32)
            o_ref[rs, hs] = _gated_out(o, gn_ref[:, hs], og_ref[rs, hs])
        return carry

    lax.fori_loop(0, q_ref.shape[0] // c, chunk, 0)

    @pl.when(j == pl.num_programs(1) - 1)
    def _():
        for h in range(heads):
            sfin_ref[0, h] = st_scr[h].T


def _hgrn_long(proj, lb, gn, batch, seq, heads, dk, row_off):
    width = heads * dk
    tb = min(HGRN_TIME_TILE, seq)
    nj = seq // tb
    off = row_off // tb
    body = functools.partial(_hgrn_long_body, c=min(HG_CHUNK, seq), heads=heads, dk=dk, scale=dk ** -0.5)

    def col(k):
        return pl.BlockSpec((tb, width), lambda b, j, k=k: (off + b * nj + j, k))

    return pl.pallas_call(
        body, grid=(batch, nj),
        in_specs=[col(0), col(1), col(2), col(3), _resident(lb.shape), _resident(gn.shape)],
        out_specs=[pl.BlockSpec((tb, width), lambda b, j: (b * nj + j, 0)),
                   pl.BlockSpec((1, heads, dk, dk), lambda b, j: (b, 0, 0, 0))],
        out_shape=[jax.ShapeDtypeStruct((batch * seq, width), BF16),
                   jax.ShapeDtypeStruct((batch, heads, dk, dk), F32)],
        scratch_shapes=[pltpu.VMEM((heads, dk, dk), F32)],
        compiler_params=_cparams(("parallel", "arbitrary")), name="hgrn_long")(proj, proj, proj, proj, lb, gn)


def _hgrn_short_body(q_ref, f_ref, v_ref, og_ref, lb_ref, gn_ref, s0_ref, o_ref, snew_ref, *, c, heads, dk, scale):
    def one_seq(sq, carry):
        rs = pl.ds(pl.multiple_of(sq * c, c), c)
        for h in range(heads):
            hs = slice(h * dk, (h + 1) * dk)
            q_dec, k_dec, k_end, b_last = _hgrn_gates(q_ref[rs, hs], f_ref[rs, hs], lb_ref[:, hs], scale)
            v = v_ref[rs, hs].astype(BF16)
            scores = _causal_scores(q_dec, k_dec)
            s0 = s0_ref[sq, h]
            o = (jnp.dot(q_dec, s0.astype(BF16), preferred_element_type=F32)
                 + jnp.dot(scores, v, preferred_element_type=F32))
            decay_col = jnp.broadcast_to(jnp.exp(b_last), (dk, dk)).T
            snew_ref[sq, h] = decay_col * s0 + lax.dot_general(
                k_end, v, (((0,), (0,)), ((), ())), preferred_element_type=F32)
            o_ref[rs, hs] = _gated_out(o, gn_ref[:, hs], og_ref[rs, hs])
        return carry

    lax.fori_loop(0, s0_ref.shape[0], one_seq, 0)


def _hgrn_short(proj, lb, gn, s0, seq, row_off):
    batch, heads, dk, _ = s0.shape
    width = heads * dk
    nb = HGRN_SEQ_TILE
    rows = nb * seq
    off = row_off // rows
    body = functools.partial(_hgrn_short_body, c=seq, heads=heads, dk=dk, scale=dk ** -0.5)

    def col(k):
        return pl.BlockSpec((rows, width), lambda i, k=k: (off + i, k))

    state_spec = pl.BlockSpec((nb, heads, dk, dk), lambda i: (i, 0, 0, 0))
    return pl.pallas_call(
        body, grid=(batch // nb,),
        in_specs=[col(0), col(1), col(2), col(3), _resident(lb.shape), _resident(gn.shape), state_spec],
        out_specs=[pl.BlockSpec((rows, width), lambda i: (i, 0)), state_spec],
        out_shape=[jax.ShapeDtypeStruct((batch * seq, width), BF16), jax.ShapeDtypeStruct(s0.shape, F32)],
        compiler_params=_cparams(("parallel",)), name="hgrn_short")(proj, proj, proj, proj, lb, gn, s0)


def _merge_body(op_ref, os_ref, ybp_ref, ybs_ref, ga_ref, gb_ref, xp_ref, xs_ref, wa_ref, wo_ref, gf_ref, wr_ref,
                br_ref, h_ref, xn_ref, lg_ref, *, n_first):
    y_a = jnp.dot(_pick(op_ref, os_ref, n_first), wa_ref[...], preferred_element_type=F32)
    merged = jax.nn.sigmoid(ga_ref[...]) * y_a + jax.nn.sigmoid(gb_ref[...]) * _pick(ybp_ref, ybs_ref, n_first)
    h = _pick(xp_ref, xs_ref, n_first) + jnp.dot(merged.astype(BF16), wo_ref[...], preferred_element_type=F32)
    h_ref[...] = h
    xn = _rmsnorm(h, gf_ref[...])
    xn_ref[...] = xn
    lg_ref[...] = lax.dot_general(wr_ref[...], xn.astype(BF16), (((1,), (1,)), ((), ())),
                                  preferred_element_type=F32) + br_ref[...]


def _merge(o_p, o_s, yb_p, yb_s, proj, xp, xs, wa, wo, gf, wr, br):
    d = xp.shape[1]
    total = proj.shape[0]
    nr = wr.shape[0]
    tm = TOKEN_TILE
    n_first = xp.shape[0] // tm
    pair = _two_source_specs(tm, d, n_first)

    def row(k=0):
        return pl.BlockSpec((tm, d), lambda i, k=k: (i, k))

    return pl.pallas_call(
        functools.partial(_merge_body, n_first=n_first), grid=(total // tm,),
        in_specs=pair + pair + [row(4), row(5)] + pair + [
            _resident(wa.shape), _resident(wo.shape), _resident(gf.shape), _resident(wr.shape), _resident(br.shape)],
        out_specs=[row(), row(), pl.BlockSpec((nr, tm), lambda i: (0, i))],
        out_shape=[jax.ShapeDtypeStruct((total, d), F32), jax.ShapeDtypeStruct((total, d), F32),
                   jax.ShapeDtypeStruct((nr, total), F32)],
        compiler_params=_cparams(("parallel",)), name="merge_out")(
            o_p, o_s, yb_p, yb_s, proj, proj, xp, xs, wa, wo, gf, wr, br)


def _first_index_of_max(vals):
    m = vals[0]
    for v in vals[1:]:
        m = jnp.maximum(m, v)
    idx = jnp.full(m.shape, len(vals), I32)
    for e in range(len(vals) - 1, -1, -1):
        idx = jnp.where(vals[e] == m, e, idx)
    return m, idx


def _route_body(lg_ref, ids_ref, w_ref, rk_ref, cnt_ref, carry_scr, *, groups, experts):
    i = pl.program_id(0)
    tile = lg_ref.shape[1]
    n_exp = groups * experts

    @pl.when(i == 0)
    def _():
        carry_scr[...] = jnp.zeros_like(carry_scr)

    gl = [lg_ref[g:g + 1, :] for g in range(groups)]
    gmax, gidx = _first_index_of_max(gl)
    denom = jnp.exp(gl[0] - gmax)
    for g in range(1, groups):
        denom = denom + jnp.exp(gl[g] - gmax)
    g_w = 1.0 / denom

    el = []
    for e in range(experts):
        v = lg_ref[groups + e:groups + e + 1, :]
        for g in range(1, groups):
            r = groups + g * experts + e
            v = jnp.where(gidx == g, lg_ref[r:r + 1, :], v)
        el.append(v)
    v1, i1 = _first_index_of_max(el)
    rest = [jnp.where(i1 == e, -jnp.inf, el[e]) for e in range(experts)]
    v2, i2 = _first_index_of_max(rest)
    t = jnp.exp(v2 - v1)
    inv = 1.0 / (1.0 + t)
    e1 = gidx * experts + i1
    e2 = gidx * experts + i2

    erow = lax.broadcasted_iota(I32, (n_exp, tile), 0)
    oh1 = (erow == e1).astype(F32)
    oh2 = (erow == e2).astype(F32)
    oh = oh1 + oh2
    before = (lax.broadcasted_iota(I32, (tile, tile), 0) < lax.broadcasted_iota(I32, (tile, tile), 1))
    cnt = jnp.dot(oh.astype(BF16), before.astype(BF16), preferred_element_type=F32) + carry_scr[:, 0:1]
    ids_ref[0:1, :] = e1
    ids_ref[1:2, :] = e2
    w_ref[0:1, :] = inv * g_w
    w_ref[1:2, :] = (t * inv) * g_w
    rk_ref[0:1, :] = jnp.sum(oh1 * cnt, axis=0, keepdims=True).astype(I32)
    rk_ref[1:2, :] = jnp.sum(oh2 * cnt, axis=0, keepdims=True).astype(I32)
    carry_scr[...] = carry_scr[...] + jnp.sum(oh, axis=1, keepdims=True)

    @pl.when(i == pl.num_programs(0) - 1)
    def _():
        cnt_ref[...] = carry_scr[...]


def _route(logits_t, groups, experts):
    nr, total = logits_t.shape
    tile = TOKEN_TILE
    n_exp = groups * experts
    body = functools.partial(_route_body, groups=groups, experts=experts)
    pair = pl.BlockSpec((MOE_TOP_K, tile), lambda i: (0, i))
    return pl.pallas_call(
        body, grid=(total // tile,),
        in_specs=[pl.BlockSpec((nr, tile), lambda i: (0, i))],
        out_specs=[pair, pair, pair, pl.BlockSpec((n_exp, LANES), lambda i: (0, 0))],
        out_shape=[jax.ShapeDtypeStruct((MOE_TOP_K, total), I32), jax.ShapeDtypeStruct((MOE_TOP_K, total), F32),
                   jax.ShapeDtypeStruct((MOE_TOP_K, total), I32), jax.ShapeDtypeStruct((n_exp, LANES), F32)],
        scratch_shapes=[pltpu.VMEM((n_exp, LANES), F32)],
        compiler_params=_cparams(("arbitrary",)), name="route")(logits_t)


def _row_copy(src, dst, sem):
    return pltpu.make_async_copy(src, dst, sem)


def _dispatch_body(pos_ref, x_ref, o_hbm, sem):
    tile = x_ref.shape[0]

    def issue(r, carry):
        for k in range(MOE_TOP_K):
            p = pos_ref[0, k, r]
            _row_copy(x_ref.at[pl.ds(r, 1)], o_hbm.at[pl.ds(p, 1)], sem.at[k]).start(priority=k)
        return carry

    lax.fori_loop(0, tile, issue, 0, unroll=8)
    for k in range(MOE_TOP_K):
        _row_copy(x_ref, o_hbm.at[pl.ds(0, tile)], sem.at[k]).wait()


def _dispatch(pos3, xn):
    total, d = xn.shape
    tile = TOKEN_TILE
    return pl.pallas_call(
        _dispatch_body, grid=(total // tile,),
        in_specs=[pl.BlockSpec((1, MOE_TOP_K, tile), lambda i: (i, 0, 0), memory_space=pltpu.SMEM),
                  pl.BlockSpec((tile, d), lambda i: (i, 0))],
        out_specs=pl.BlockSpec(memory_space=pl.ANY),
        out_shape=jax.ShapeDtypeStruct((total * MOE_TOP_K, d), F32),
        scratch_shapes=[pltpu.SemaphoreType.DMA((MOE_TOP_K,))],
        compiler_params=_cparams(("arbitrary",)), name="dispatch")(pos3, xn)


def _experts_body(it_tile, it_exp, it_lo, it_hi, n_items, x_ref, wg_ref, wu_ref, wd_ref, o_ref, wg_b, wu_b, wd_b):
    j = pl.program_id(0)
    tm = x_ref.shape[0]
    prev = jnp.maximum(j - 1, 0)
    active = j < n_items[0]
    new_tile = jnp.logical_or(j == 0, it_tile[j] != it_tile[prev])
    new_expert = jnp.logical_or(j == 0, it_exp[j] != it_exp[prev])

    @pl.when(jnp.logical_and(active, new_expert))
    def _():
        wg_b[...] = wg_ref[0].astype(BF16)
        wu_b[...] = wu_ref[0].astype(BF16)
        wd_b[...] = wd_ref[0].astype(BF16)

    @pl.when(jnp.logical_and(active, new_tile))
    def _():
        o_ref[...] = jnp.zeros_like(o_ref)

    @pl.when(active)
    def _():
        x = x_ref[...].astype(BF16)
        hg = jnp.dot(x, wg_b[...], preferred_element_type=F32)
        hu = jnp.dot(x, wu_b[...], preferred_element_type=F32)
        hid = (jax.nn.silu(hg) * hu).astype(BF16)
        out = jnp.dot(hid, wd_b[...], preferred_element_type=F32)
        rows = lax.broadcasted_iota(I32, (tm, 1), 0)
        mine = jnp.logical_and(rows >= it_lo[j], rows < it_hi[j])
        o_ref[...] = jnp.where(mine, out, o_ref[...])


def _experts(items, xs, wg, wu, wd):
    n_sorted, d = xs.shape
    de = wg.shape[2]
    tm = EXPERT_TILE
    max_items = items[0].shape[0]
    grid_spec = pltpu.PrefetchScalarGridSpec(
        num_scalar_prefetch=5, grid=(max_items,),
        in_specs=[pl.BlockSpec((tm, d), lambda j, t, e, lo, hi, n: (t[j], 0)),
                  pl.BlockSpec((1, d, de), lambda j, t, e, lo, hi, n: (e[j], 0, 0)),
                  pl.BlockSpec((1, d, de), lambda j, t, e, lo, hi, n: (e[j], 0, 0)),
                  pl.BlockSpec((1, de, d), lambda j, t, e, lo, hi, n: (e[j], 0, 0))],
        out_specs=pl.BlockSpec((tm, d), lambda j, t, e, lo, hi, n: (t[j], 0)),
        scratch_shapes=[pltpu.VMEM((d, de), BF16), pltpu.VMEM((d, de), BF16), pltpu.VMEM((de, d), BF16)])
    return pl.pallas_call(
        _experts_body, grid_spec=grid_spec, out_shape=jax.ShapeDtypeStruct((n_sorted, d), F32),
        compiler_params=_cparams(("arbitrary",)), name="experts")(*items, xs, wg, wu, wd)


def _combine_body(pos_ref, h_ref, w_ref, g_ref, ys_hbm, y_ref, buf, sem, *, final_norm):
    tile = h_ref.shape[0]

    def issue(r, carry):
        for k in range(MOE_TOP_K):
            p = pos_ref[0, k, r]
            _row_copy(ys_hbm.at[pl.ds(p, 1)], buf.at[k, pl.ds(r, 1)], sem.at[k]).start(priority=k)
        return carry

    lax.fori_loop(0, tile, issue, 0, unroll=8)
    for k in range(MOE_TOP_K):
        _row_copy(ys_hbm.at[pl.ds(0, tile)], buf.at[k], sem.at[k]).wait()
    h = h_ref[...] + (w_ref[:, 0:1] * buf[0] + w_ref[:, 1:2] * buf[1])
    y_ref[...] = _rmsnorm(h, g_ref[...]) if final_norm else h


def _combine(pos3, h_all, w_t, g, ys, rows, row_off, final_norm):
    d = h_all.shape[1]
    tile = TOKEN_TILE
    off = row_off // tile
    return pl.pallas_call(
        functools.partial(_combine_body, final_norm=final_norm), grid=(rows // tile,),
        in_specs=[pl.BlockSpec((1, MOE_TOP_K, tile), lambda i: (off + i, 0, 0), memory_space=pltpu.SMEM),
                  pl.BlockSpec((tile, d), lambda i: (off + i, 0)),
                  pl.BlockSpec((tile, MOE_TOP_K), lambda i: (off + i, 0)),
                  _resident(g.shape),
                  pl.BlockSpec(memory_space=pl.ANY)],
        out_specs=pl.BlockSpec((tile, d), lambda i: (i, 0)),
        out_shape=jax.ShapeDtypeStruct((rows, d), F32),
        scratch_shapes=[pltpu.VMEM((MOE_TOP_K, tile, d), F32), pltpu.SemaphoreType.DMA((MOE_TOP_K,))],
        compiler_params=_cparams(("arbitrary",)), name="combine")(pos3, h_all, w_t, g, ys)


def _lookup(table, idx):
    sel = idx[None] == jnp.arange(table.shape[0], dtype=I32).reshape((-1,) + (1,) * idx.ndim)
    return jnp.sum(jnp.where(sel, table.reshape(sel.shape[:1] + (1,) * idx.ndim), 0), axis=0)


def _work_items(counts, n_tiles):
    tm = EXPERT_TILE
    n_exp = counts.shape[0]
    max_items = n_tiles + n_exp - 1
    ends = jnp.cumsum(counts)
    starts = ends - counts
    first_tile = starts // tm
    n_e = jnp.where(counts > 0, (ends - 1) // tm - first_tile + 1, 0)
    item_end = jnp.cumsum(n_e)
    item_start = item_end - n_e
    n_items = item_end[-1]
    j = jnp.minimum(jnp.arange(max_items, dtype=I32), n_items - 1)
    e = jnp.sum((item_end[None, :] <= j[:, None]).astype(I32), axis=1)
    t = _lookup(first_tile, e) + (j - _lookup(item_start, e))
    lo = jnp.maximum(_lookup(starts, e), t * tm) - t * tm
    hi = jnp.minimum(_lookup(ends, e), (t + 1) * tm) - t * tm
    return (t.astype(I32), e, lo.astype(I32), hi.astype(I32), n_items.reshape(1).astype(I32)), starts


def _s5_discretise(lam_re, lam_im, log_dt, b_re, b_im, c_re, c_im):
    g, p = lam_re.shape
    ch = b_re.shape[-1]
    lam_re = lam_re.astype(F32)
    lam_im = lam_im.astype(F32)
    dt = jnp.exp(log_dt.astype(F32))[:, None]
    mag = jnp.exp(lam_re * dt)
    ab_re = mag * jnp.cos(lam_im * dt)
    ab_im = mag * jnp.sin(lam_im * dt)
    den = lam_re * lam_re + lam_im * lam_im
    nr = ab_re - 1.0
    coef_re = (nr * lam_re + ab_im * lam_im) / den
    coef_im = (ab_im * lam_re - nr * lam_im) / den
    bb_re = coef_re[..., None] * b_re - coef_im[..., None] * b_im
    bb_im = coef_re[..., None] * b_im + coef_im[..., None] * b_re
    gh = g // 2
    eye = jnp.eye(gh, dtype=F32)

    def in_block(m):
        return jnp.einsum("gpc,gh->gchp", m, eye).reshape(gh * ch, gh * p)

    def out_block(m):
        return jnp.einsum("gcp,gh->gphc", m, eye).reshape(gh * p, gh * ch)

    bb = jnp.stack([jnp.concatenate([in_block(bb_re[k * gh:(k + 1) * gh]), in_block(bb_im[k * gh:(k + 1) * gh])],
                                    axis=1) for k in range(2)]).astype(BF16)
    cc = jnp.stack([jnp.concatenate([out_block(c_re[k * gh:(k + 1) * gh]), out_block(-c_im[k * gh:(k + 1) * gh])],
                                    axis=0) for k in range(2)]).astype(BF16)
    return ab_re.reshape(1, g * p), ab_im.reshape(1, g * p), bb, cc


def kernel(x_prompt, x_sample, state_hgrn, state_s5_re, state_s5_im, norm_mix_g, w_in, hgrn_lb_raw, hgrn_onorm_g, w_branch_a, s5_lambda_re, s5_lambda_im, s5_log_dt, s5_b_re, s5_b_im, s5_c_re, s5_c_im, s5_d, w_glu, b_glu, w_out, norm_ffn_g, w_router_group, b_router_group, w_router_expert, b_router_expert, w_exp_gate, w_exp_up, w_exp_down, norm_final_g):
    depth = norm_mix_g.shape[0]
    bp, lp, d = x_prompt.shape
    bs, ls, _ = x_sample.shape
    heads, dk = state_hgrn.shape[2], state_hgrn.shape[3]
    kw = heads * dk
    s5_groups, s5_state = state_s5_re.shape[2], state_s5_re.shape[3]
    s5_width = s5_d.shape[-1]
    nstate = s5_groups * s5_state
    moe_groups, _, experts = w_router_expert.shape[1:]
    n_exp = moe_groups * experts
    rows_p, rows_s = bp * lp, bs * ls
    total = rows_p + rows_s
    n_sorted = total * MOE_TOP_K
    assert kw == d and state_hgrn.shape[4] == dk, "column blocks assume key width == value width == model width"
    assert s5_groups % 2 == 0 and bp % SUBLANES == 0 and bs % HGRN_SEQ_TILE == 0

    lb_all = jnp.cumsum(jax.nn.softmax(hgrn_lb_raw.astype(F32), axis=0), axis=0)

    hp = x_prompt.reshape(rows_p, d)
    hs = x_sample.reshape(rows_s, d)
    hg_p, re_p, im_p, hg_s, re_s, im_s = [], [], [], [], [], []
    zeros_state = jnp.zeros((bp // SUBLANES, SUBLANES, nstate), F32)

    for l in range(depth):
        w = w_in[l]
        w_cols = jnp.concatenate([w[:, :4 * kw], w[:, 4 * kw + s5_width:], w[:, 4 * kw:4 * kw + s5_width]],
                                 axis=1).astype(BF16)
        proj = _in_proj(hp, hs, norm_mix_g[l].reshape(1, d), w_cols)

        ar, ai, bb, cc = _s5_discretise(s5_lambda_re[l], s5_lambda_im[l], s5_log_dt[l], s5_b_re[l], s5_b_im[l],
                                        s5_c_re[l], s5_c_im[l])
        s5_args = (ar, ai, bb, cc, s5_d[l].reshape(1, s5_width), w_glu[l].astype(BF16), b_glu[l].reshape(1, -1))
        yb_p, fr_p, fi_p = _s5_branch(proj, 6 * kw, s5_width, bp, lp, 0, zeros_state, zeros_state, *s5_args)
        yb_s, fr_s, fi_s = _s5_branch(proj, 6 * kw, s5_width, bs, ls, rows_p,
                                      state_s5_re[l].reshape(bs // SUBLANES, SUBLANES, nstate),
                                      state_s5_im[l].reshape(bs // SUBLANES, SUBLANES, nstate), *s5_args)

        lb = lb_all[l].reshape(1, kw)
        gn = hgrn_onorm_g[l].reshape(1, kw)
        o_p, hgp = _hgrn_long(proj, lb, gn, bp, lp, heads, dk, 0)
        o_s, hgs = _hgrn_short(proj, lb, gn, state_hgrn[l].astype(F32), ls, rows_p)

        nr = -(-(moe_groups + n_exp) // SUBLANES) * SUBLANES
        wr = jnp.concatenate([w_router_group[l].T, w_router_expert[l].transpose(0, 2, 1).reshape(n_exp, d)], axis=0)
        wr = jnp.pad(wr, ((0, nr - wr.shape[0]), (0, 0))).astype(BF16)
        br = jnp.pad(jnp.concatenate([b_router_group[l], b_router_expert[l].reshape(n_exp)]),
                     (0, nr - moe_groups - n_exp)).reshape(nr, 1).astype(F32)
        h_all, xn_all, logits_t = _merge(
            o_p, o_s, yb_p.reshape(rows_p, d), yb_s.reshape(rows_s, d), proj, hp, hs,
            w_branch_a[l].astype(BF16), w_out[l].astype(BF16), norm_ffn_g[l].reshape(1, d), wr, br)

        ids, wts, ranks, cnt = _route(logits_t, moe_groups, experts)
        items, starts = _work_items(cnt[:, 0].astype(I32), n_sorted // EXPERT_TILE)
        pos = _lookup(starts, ids) + ranks
        pos3 = pos.reshape(MOE_TOP_K, total // TOKEN_TILE, TOKEN_TILE).transpose(1, 0, 2)
        xs = _dispatch(pos3, xn_all)
        ys = _experts(items, xs, w_exp_gate[l], w_exp_up[l], w_exp_down[l])

        last = l == depth - 1
        g_out = norm_final_g.reshape(1, d)
        hp = _combine(pos3, h_all, wts.T, g_out, ys, rows_p, 0, last)
        hs = _combine(pos3, h_all, wts.T, g_out, ys, rows_s, rows_p, last)

        hg_p.append(hgp)
        hg_s.append(hgs)
        re_p.append(fr_p.reshape(bp, s5_groups, s5_state))
        im_p.append(fi_p.reshape(bp, s5_groups, s5_state))
        re_s.append(fr_s.reshape(bs, s5_groups, s5_state))
        im_s.append(fi_s.reshape(bs, s5_groups, s5_state))

    y_prompt = hp.reshape(bp, lp, d).astype(x_prompt.dtype)
    y_sample = hs.reshape(bs, ls, d).astype(x_sample.dtype)
    return (y_prompt, y_sample, jnp.stack(hg_p), jnp.stack(re_p), jnp.stack(im_p),
            jnp.stack(hg_s), jnp.stack(re_s), jnp.stack(im_s))
```

```python
import functools

import jax
import jax.numpy as jnp
from jax import lax
from jax.experimental import pallas as pl
from jax.experimental.pallas import tpu as pltpu

F32 = jnp.float32
BF16 = jnp.bfloat16
I32 = jnp.int32

RMS_EPS = 1e-6
HG_CHUNK = 64
MOE_TOP_K = 2

V7X_VMEM_BYTES = 64 * 1024 * 1024
VMEM_LIMIT_BYTES = V7X_VMEM_BYTES - 8 * 1024 * 1024
SUBLANES = 8
LANES = 128

TOKEN_TILE = 512
EXPERT_TILE = 512
S5_TIME_TILE = 32
HGRN_TIME_TILE = 256
HGRN_SEQ_TILE = 8
PROJ_COL_TILE = 512


def _cparams(sem):
    return pltpu.CompilerParams(dimension_semantics=sem, vmem_limit_bytes=VMEM_LIMIT_BYTES)


def _resident(shape):
    nd = len(shape)
    return pl.BlockSpec(shape, lambda *_: (0,) * nd, pipeline_mode=pl.Buffered(1))


def _rmsnorm(x, g):
    return x * lax.rsqrt(jnp.mean(x * x, axis=-1, keepdims=True) + RMS_EPS) * g


def _two_source_specs(tm, width, n_first):
    return [pl.BlockSpec((tm, width), lambda i: (jnp.minimum(i, n_first - 1), 0)),
            pl.BlockSpec((tm, width), lambda i: (jnp.maximum(i - n_first, 0), 0))]


def _pick(first_ref, second_ref, n_first):
    return jnp.where(pl.program_id(0) < n_first, first_ref[...], second_ref[...])


def _store_token_tiles(ref, x, lead=()):
    rows = x.shape[0]
    for c in range(SUBLANES):
        ref[lead + (pl.ds(c, rows, stride=SUBLANES), slice(None))] = x[:, c * LANES:(c + 1) * LANES]


def _load_token_tiles(ref, rows, lead=()):
    return jnp.concatenate([ref[lead + (pl.ds(c, rows, stride=SUBLANES), slice(None))] for c in range(SUBLANES)],
                           axis=-1)


def _inproj_body(xp_ref, xs_ref, g_ref, w_ref, o_ref, *, n_first):
    xb = _rmsnorm(_pick(xp_ref, xs_ref, n_first), g_ref[...]).astype(BF16)
    for j in range(0, w_ref.shape[1], PROJ_COL_TILE):
        o_ref[:, j:j + PROJ_COL_TILE] = jnp.dot(xb, w_ref[:, j:j + PROJ_COL_TILE], preferred_element_type=F32)


def _in_proj(xp, xs, g, w):
    d = xp.shape[1]
    n = w.shape[1]
    tm = TOKEN_TILE
    total = xp.shape[0] + xs.shape[0]
    n_first = xp.shape[0] // tm
    return pl.pallas_call(
        functools.partial(_inproj_body, n_first=n_first), grid=(total // tm,),
        in_specs=_two_source_specs(tm, d, n_first) + [_resident((1, d)), _resident((d, n))],
        out_specs=pl.BlockSpec((tm, n), lambda i: (i, 0)),
        out_shape=jax.ShapeDtypeStruct((total, n), F32),
        compiler_params=_cparams(("parallel",)), name="in_proj")(xp, xs, g, w)


def _s5_body(*refs, tt, nstate):
    u_refs = refs[:SUBLANES]
    (h0r_ref, h0i_ref, ar_ref, ai_ref, bb_ref, cc_ref, d_ref, wg_ref, bg_ref,
     y_ref, hr_out, hi_out, hr_scr, hi_scr, bu_scr, u_scr, y_scr) = refs[SUBLANES:]
    j = pl.program_id(1)
    half = nstate // 2
    w = u_refs[0].shape[-1]
    kw = w // 2

    @pl.when(j == 0)
    def _():
        hr_scr[...] = h0r_ref[0]
        hi_scr[...] = h0i_ref[0]

    for b in range(SUBLANES):
        ub = u_refs[b][...]
        for s in range(w // LANES):
            u_scr[s, pl.ds(b, tt, stride=SUBLANES), :] = ub[:, s * LANES:(s + 1) * LANES]
    u = jnp.concatenate([u_scr[s] for s in range(w // LANES)], axis=-1)
    ub16 = u.astype(BF16)
    for kt in range(2):
        ukt = ub16[:, kt * kw:(kt + 1) * kw]
        bu_scr[:, kt * half:(kt + 1) * half] = jnp.dot(ukt, bb_ref[kt, :, :half], preferred_element_type=F32)
        bu_scr[:, nstate + kt * half:nstate + (kt + 1) * half] = jnp.dot(
            ukt, bb_ref[kt, :, half:], preferred_element_type=F32)

    lane_chunk = 512
    for lc in range(nstate // lane_chunk):
        lo = lc * lane_chunk
        re_sl = slice(lo, lo + lane_chunk)
        im_sl = slice(nstate + lo, nstate + lo + lane_chunk)
        ar = jnp.broadcast_to(ar_ref[:, re_sl], (SUBLANES, lane_chunk))
        ai = jnp.broadcast_to(ai_ref[:, re_sl], (SUBLANES, lane_chunk))

        def step(t, carry, re_sl=re_sl, im_sl=im_sl, ar=ar, ai=ai):
            hr, hi = carry
            rs = pl.ds(pl.multiple_of(t * SUBLANES, SUBLANES), SUBLANES)
            nhr = ar * hr - ai * hi + bu_scr[rs, re_sl]
            nhi = ar * hi + ai * hr + bu_scr[rs, im_sl]
            bu_scr[rs, re_sl] = nhr
            bu_scr[rs, im_sl] = nhi
            return nhr, nhi

        hr, hi = lax.fori_loop(0, tt, step, (hr_scr[:, re_sl], hi_scr[:, re_sl]), unroll=4)
        hr_scr[:, re_sl] = hr
        hi_scr[:, re_sl] = hi

    ys = []
    for n in range(2):
        h_re = bu_scr[:, n * half:(n + 1) * half].astype(BF16)
        h_im = bu_scr[:, nstate + n * half:nstate + (n + 1) * half].astype(BF16)
        ys.append(jnp.dot(h_re, cc_ref[n, :half, :], preferred_element_type=F32)
                  + jnp.dot(h_im, cc_ref[n, half:, :], preferred_element_type=F32))
    y = jnp.concatenate(ys, axis=-1) + d_ref[...] * u
    z = jnp.dot(jax.nn.gelu(y).astype(BF16), wg_ref[...], preferred_element_type=F32) + bg_ref[...]
    dm = z.shape[-1] // 2
    yb = z[:, :dm] * jax.nn.sigmoid(z[:, dm:])
    for s in range(dm // LANES):
        y_scr[s] = yb[:, s * LANES:(s + 1) * LANES]
    for b in range(SUBLANES):
        for s in range(dm // LANES):
            y_ref[b, :, s * LANES:(s + 1) * LANES] = y_scr[s, pl.ds(b, tt, stride=SUBLANES), :]

    @pl.when(j == pl.num_programs(1) - 1)
    def _():
        hr_out[0] = hr_scr[...]
        hi_out[0] = hi_scr[...]


def _s5_branch(proj, u_col, width, batch, seq, row_off, h0r, h0i, ar, ai, bb, cc, d_skip, w_glu, b_glu):
    nstate = ar.shape[-1]
    dm = w_glu.shape[1] // 2
    tt = min(S5_TIME_TILE, seq)
    nj = seq // tt
    nbb = batch // SUBLANES
    body = functools.partial(_s5_body, tt=tt, nstate=nstate)

    def u_spec(b):
        return pl.BlockSpec((tt, width), lambda bb_, j, b=b: (row_off // tt + (bb_ * SUBLANES + b) * nj + j,
                                                               u_col // width))

    state_spec = pl.BlockSpec((1, SUBLANES, nstate), lambda bb_, j: (bb_, 0, 0))
    return pl.pallas_call(
        body, grid=(nbb, nj),
        in_specs=[u_spec(b) for b in range(SUBLANES)] + [
            state_spec, state_spec, _resident(ar.shape), _resident(ai.shape), _resident(bb.shape),
            _resident(cc.shape), _resident(d_skip.shape), _resident(w_glu.shape), _resident(b_glu.shape)],
        out_specs=[pl.BlockSpec((SUBLANES, tt, dm), lambda bb_, j: (bb_, j, 0)), state_spec, state_spec],
        out_shape=[jax.ShapeDtypeStruct((batch, seq, dm), F32),
                   jax.ShapeDtypeStruct((nbb, SUBLANES, nstate), F32),
                   jax.ShapeDtypeStruct((nbb, SUBLANES, nstate), F32)],
        scratch_shapes=[pltpu.VMEM((SUBLANES, nstate), F32), pltpu.VMEM((SUBLANES, nstate), F32),
                        pltpu.VMEM((tt * SUBLANES, 2 * nstate), F32),
                        pltpu.VMEM((width // LANES, tt * SUBLANES, LANES), F32),
                        pltpu.VMEM((dm // LANES, tt * SUBLANES, LANES), F32)],
        compiler_params=_cparams(("parallel", "arbitrary")), name="s5_branch")(
            *([proj] * SUBLANES), h0r, h0i, ar, ai, bb, cc, d_skip, w_glu, b_glu)


def _cumsum_rows(x):
    c = x.shape[0]
    row = lax.broadcasted_iota(I32, x.shape, 0)
    s = 1
    while s < c:
        x = x + jnp.where(row >= s, pltpu.roll(x, s, axis=0), 0.0)
        s *= 2
    return x


def _hgrn_gates(q, fr, lb, scale):
    f = lb + (1.0 - lb) * jax.nn.sigmoid(fr)
    k = 1.0 - f
    b = _cumsum_rows(jnp.log(f))
    b_last = b[-1:, :]
    q_dec = (q * scale) * jnp.exp(b)
    k_dec = k * jnp.exp(-b)
    k_end = k * jnp.exp(b_last - b)
    return q_dec.astype(BF16), k_dec.astype(BF16), k_end.astype(BF16), b_last


def _causal_scores(q_dec, k_dec):
    c = q_dec.shape[0]
    s = lax.dot_general(q_dec, k_dec, (((1,), (1,)), ((), ())), preferred_element_type=F32)
    keep = lax.broadcasted_iota(I32, (c, c), 0) >= lax.broadcasted_iota(I32, (c, c), 1)
    return jnp.where(keep, s, 0.0).astype(BF16)


def _gated_out(o, gn, og):
    o = o * lax.rsqrt(jnp.mean(o * o, axis=-1, keepdims=True) + RMS_EPS) * gn
    return (o * jax.nn.silu(og)).astype(BF16)


def _hgrn_long_body(q_ref, f_ref, v_ref, og_ref, lb_ref, gn_ref, o_ref, sfin_ref, st_scr, *, c, heads, dk, scale):
    j = pl.program_id(1)

    @pl.when(j == 0)
    def _():
        st_scr[...] = jnp.zeros_like(st_scr)

    def chunk(ci, carry):
        rs = pl.ds(pl.multiple_of(ci * c, c), c)
        for h in range(heads):
            hs = slice(h * dk, (h + 1) * dk)
            q_dec, k_dec, k_end, b_last = _hgrn_gates(q_ref[rs, hs], f_ref[rs, hs], lb_ref[:, hs], scale)
            v = v_ref[rs, hs].astype(BF16)
            scores = _causal_scores(q_dec, k_dec)
            st = st_scr[h]
            o = (lax.dot_general(q_dec, st.astype(BF16), (((1,), (1,)), ((), ())), preferred_element_type=F32)
                 + jnp.dot(scores, v, preferred_element_type=F32))
            st_scr[h] = jnp.exp(b_last) * st + lax.dot_general(
                v, k_end, (((0,), (0,)), ((), ())), preferred_element_type=F32)
            o_ref[rs, hs] = _gated_out(o, gn_ref[:, hs], og_ref[rs, hs])
        return carry

    lax.fori_loop(0, q_ref.shape[0] // c, chunk, 0)

    @pl.when(j == pl.num_programs(1) - 1)
    def _():
        for h in range(heads):
            sfin_ref[0, h] = st_scr[h].T


def _hgrn_long(proj, lb, gn, batch, seq, heads, dk, row_off):
    width = heads * dk
    tb = min(HGRN_TIME_TILE, seq)
    nj = seq // tb
    off = row_off // tb
    body = functools.partial(_hgrn_long_body, c=min(HG_CHUNK, seq), heads=heads, dk=dk, scale=dk ** -0.5)

    def col(k):
        return pl.BlockSpec((tb, width), lambda b, j, k=k: (off + b * nj + j, k))

    return pl.pallas_call(
        body, grid=(batch, nj),
        in_specs=[col(0), col(1), col(2), col(3), _resident(lb.shape), _resident(gn.shape)],
        out_specs=[pl.BlockSpec((tb, width), lambda b, j: (b * nj + j, 0)),
                   pl.BlockSpec((1, heads, dk, dk), lambda b, j: (b, 0, 0, 0))],
        out_shape=[jax.ShapeDtypeStruct((batch * seq, width), BF16),
                   jax.ShapeDtypeStruct((batch, heads, dk, dk), F32)],
        scratch_shapes=[pltpu.VMEM((heads, dk, dk), F32)],
        compiler_params=_cparams(("parallel", "arbitrary")), name="hgrn_long")(proj, proj, proj, proj, lb, gn)


def _hgrn_short_body(q_ref, f_ref, v_ref, og_ref, lb_ref, gn_ref, s0_ref, o_ref, snew_ref, *, c, heads, dk, scale):
    def one_seq(sq, carry):
        rs = pl.ds(pl.multiple_of(sq * c, c), c)
        for h in range(heads):
            hs = slice(h * dk, (h + 1) * dk)
            q_dec, k_dec, k_end, b_last = _hgrn_gates(q_ref[rs, hs], f_ref[rs, hs], lb_ref[:, hs], scale)
            v = v_ref[rs, hs].astype(BF16)
            scores = _causal_scores(q_dec, k_dec)
            s0 = s0_ref[sq, h]
            o = (jnp.dot(q_dec, s0.astype(BF16), preferred_element_type=F32)
                 + jnp.dot(scores, v, preferred_element_type=F32))
            decay_col = jnp.broadcast_to(jnp.exp(b_last), (dk, dk)).T
            snew_ref[sq, h] = decay_col * s0 + lax.dot_general(
                k_end, v, (((0,), (0,)), ((), ())), preferred_element_type=F32)
            o_ref[rs, hs] = _gated_out(o, gn_ref[:, hs], og_ref[rs, hs])
        return carry

    lax.fori_loop(0, s0_ref.shape[0], one_seq, 0)


def _hgrn_short(proj, lb, gn, s0, seq, row_off):
    batch, heads, dk, _ = s0.shape
    width = heads * dk
    nb = HGRN_SEQ_TILE
    rows = nb * seq
    off = row_off // rows
    body = functools.partial(_hgrn_short_body, c=seq, heads=heads, dk=dk, scale=dk ** -0.5)

    def col(k):
        return pl.BlockSpec((rows, width), lambda i, k=k: (off + i, k))

    state_spec = pl.BlockSpec((nb, heads, dk, dk), lambda i: (i, 0, 0, 0))
    return pl.pallas_call(
        body, grid=(batch // nb,),
        in_specs=[col(0), col(1), col(2), col(3), _resident(lb.shape), _resident(gn.shape), state_spec],
        out_specs=[pl.BlockSpec((rows, width), lambda i: (i, 0)), state_spec],
        out_shape=[jax.ShapeDtypeStruct((batch * seq, width), BF16), jax.ShapeDtypeStruct(s0.shape, F32)],
        compiler_params=_cparams(("parallel",)), name="hgrn_short")(proj, proj, proj, proj, lb, gn, s0)


def _merge_body(op_ref, os_ref, ybp_ref, ybs_ref, ga_ref, gb_ref, xp_ref, xs_ref, wa_ref, wo_ref, gf_ref, wr_ref,
                br_ref, h_ref, xn_ref, lg_ref, *, n_first):
    y_a = jnp.dot(_pick(op_ref, os_ref, n_first), wa_ref[...], preferred_element_type=F32)
    merged = jax.nn.sigmoid(ga_ref[...]) * y_a + jax.nn.sigmoid(gb_ref[...]) * _pick(ybp_ref, ybs_ref, n_first)
    h = _pick(xp_ref, xs_ref, n_first) + jnp.dot(merged.astype(BF16), wo_ref[...], preferred_element_type=F32)
    h_ref[...] = h
    xn = _rmsnorm(h, gf_ref[...])
    _store_token_tiles(xn_ref, xn)
    lg_ref[...] = lax.dot_general(wr_ref[...], xn.astype(BF16), (((1,), (1,)), ((), ())),
                                  preferred_element_type=F32) + br_ref[...]


def _merge(o_p, o_s, yb_p, yb_s, proj, xp, xs, wa, wo, gf, wr, br):
    d = xp.shape[1]
    total = proj.shape[0]
    nr = wr.shape[0]
    tm = TOKEN_TILE
    n_first = xp.shape[0] // tm
    pair = _two_source_specs(tm, d, n_first)

    def row(k=0):
        return pl.BlockSpec((tm, d), lambda i, k=k: (i, k))

    return pl.pallas_call(
        functools.partial(_merge_body, n_first=n_first), grid=(total // tm,),
        in_specs=pair + pair + [row(4), row(5)] + pair + [
            _resident(wa.shape), _resident(wo.shape), _resident(gf.shape), _resident(wr.shape), _resident(br.shape)],
        out_specs=[row(), pl.BlockSpec((tm * SUBLANES, LANES), lambda i: (i, 0)),
                   pl.BlockSpec((nr, tm), lambda i: (0, i))],
        out_shape=[jax.ShapeDtypeStruct((total, d), F32), jax.ShapeDtypeStruct((total * SUBLANES, LANES), F32),
                   jax.ShapeDtypeStruct((nr, total), F32)],
        compiler_params=_cparams(("parallel",)), name="merge_out")(
            o_p, o_s, yb_p, yb_s, proj, proj, xp, xs, wa, wo, gf, wr, br)


def _first_index_of_max(vals):
    m = vals[0]
    for v in vals[1:]:
        m = jnp.maximum(m, v)
    idx = jnp.full(m.shape, len(vals), I32)
    for e in range(len(vals) - 1, -1, -1):
        idx = jnp.where(vals[e] == m, e, idx)
    return m, idx


def _route_body(lg_ref, ids_ref, w_ref, rk_ref, cnt_ref, carry_scr, *, groups, experts):
    i = pl.program_id(0)
    tile = lg_ref.shape[1]
    n_exp = groups * experts

    @pl.when(i == 0)
    def _():
        carry_scr[...] = jnp.zeros_like(carry_scr)

    gl = [lg_ref[g:g + 1, :] for g in range(groups)]
    gmax, gidx = _first_index_of_max(gl)
    denom = jnp.exp(gl[0] - gmax)
    for g in range(1, groups):
        denom = denom + jnp.exp(gl[g] - gmax)
    g_w = 1.0 / denom

    el = []
    for e in range(experts):
        v = lg_ref[groups + e:groups + e + 1, :]
        for g in range(1, groups):
            r = groups + g * experts + e
            v = jnp.where(gidx == g, lg_ref[r:r + 1, :], v)
        el.append(v)
    v1, i1 = _first_index_of_max(el)
    rest = [jnp.where(i1 == e, -jnp.inf, el[e]) for e in range(experts)]
    v2, i2 = _first_index_of_max(rest)
    t = jnp.exp(v2 - v1)
    inv = 1.0 / (1.0 + t)
    e1 = gidx * experts + i1
    e2 = gidx * experts + i2

    erow = lax.broadcasted_iota(I32, (n_exp, tile), 0)
    oh1 = (erow == e1).astype(F32)
    oh2 = (erow == e2).astype(F32)
    oh = oh1 + oh2
    before = (lax.broadcasted_iota(I32, (tile, tile), 0) < lax.broadcasted_iota(I32, (tile, tile), 1))
    cnt = jnp.dot(oh.astype(BF16), before.astype(BF16), preferred_element_type=F32) + carry_scr[:, 0:1]
    ids_ref[0:1, :] = e1
    ids_ref[1:2, :] = e2
    w_ref[0:1, :] = inv * g_w
    w_ref[1:2, :] = (t * inv) * g_w
    rk_ref[0:1, :] = jnp.sum(oh1 * cnt, axis=0, keepdims=True).astype(I32)
    rk_ref[1:2, :] = jnp.sum(oh2 * cnt, axis=0, keepdims=True).astype(I32)
    carry_scr[...] = carry_scr[...] + jnp.sum(oh, axis=1, keepdims=True)

    @pl.when(i == pl.num_programs(0) - 1)
    def _():
        cnt_ref[...] = carry_scr[...]


def _route(logits_t, groups, experts):
    nr, total = logits_t.shape
    tile = TOKEN_TILE
    n_exp = groups * experts
    body = functools.partial(_route_body, groups=groups, experts=experts)
    pair = pl.BlockSpec((MOE_TOP_K, tile), lambda i: (0, i))
    return pl.pallas_call(
        body, grid=(total // tile,),
        in_specs=[pl.BlockSpec((nr, tile), lambda i: (0, i))],
        out_specs=[pair, pair, pair, pl.BlockSpec((n_exp, LANES), lambda i: (0, 0))],
        out_shape=[jax.ShapeDtypeStruct((MOE_TOP_K, total), I32), jax.ShapeDtypeStruct((MOE_TOP_K, total), F32),
                   jax.ShapeDtypeStruct((MOE_TOP_K, total), I32), jax.ShapeDtypeStruct((n_exp, LANES), F32)],
        scratch_shapes=[pltpu.VMEM((n_exp, LANES), F32)],
        compiler_params=_cparams(("arbitrary",)), name="route")(logits_t)


def _row_copy(src, dst, sem):
    return pltpu.make_async_copy(src, dst, sem)


def _token_rows(r):
    return pl.ds(pl.multiple_of(r * SUBLANES, SUBLANES), SUBLANES)


def _dispatch_body(pos_ref, x_ref, o_hbm, zero_scr, sem, *, n_sorted):
    tile = x_ref.shape[0] // SUBLANES

    @pl.when(pl.program_id(0) == 0)
    def _():
        zero_scr[...] = jnp.zeros_like(zero_scr)
        pad = _row_copy(zero_scr, o_hbm.at[pl.ds(n_sorted * SUBLANES, zero_scr.shape[0])], sem.at[MOE_TOP_K])
        pad.start()
        pad.wait()

    def issue(r, carry):
        for k in range(MOE_TOP_K):
            p = pos_ref[0, k, r]
            _row_copy(x_ref.at[_token_rows(r)], o_hbm.at[_token_rows(p)], sem.at[k]).start(priority=k)
        return carry

    lax.fori_loop(0, tile, issue, 0, unroll=8)
    for k in range(MOE_TOP_K):
        _row_copy(x_ref, o_hbm.at[pl.ds(0, tile * SUBLANES)], sem.at[k]).wait()


def _dispatch(pos3, xn_tiles):
    total = xn_tiles.shape[0] // SUBLANES
    tile = TOKEN_TILE
    n_sorted = total * MOE_TOP_K
    return pl.pallas_call(
        functools.partial(_dispatch_body, n_sorted=n_sorted), grid=(total // tile,),
        in_specs=[pl.BlockSpec((1, MOE_TOP_K, tile), lambda i: (i, 0, 0), memory_space=pltpu.SMEM),
                  pl.BlockSpec((tile * SUBLANES, LANES), lambda i: (i, 0))],
        out_specs=pl.BlockSpec(memory_space=pl.ANY),
        out_shape=jax.ShapeDtypeStruct(((n_sorted + EXPERT_TILE) * SUBLANES, LANES), F32),
        scratch_shapes=[pltpu.VMEM((EXPERT_TILE * SUBLANES, LANES), F32),
                        pltpu.SemaphoreType.DMA((MOE_TOP_K + 1,))],
        compiler_params=_cparams(("arbitrary",)), name="dispatch")(pos3, xn_tiles)


def _experts_body(it_exp, it_row, n_items, xs_hbm, wg_ref, wu_ref, wd_ref, ys_hbm, xbuf, ybuf, wg_b, wu_b, wd_b,
                  sem_in, sem_out):
    j = pl.program_id(0)
    n = n_items[0]
    tm = EXPERT_TILE
    slot = lax.rem(j, 2)

    def window(item):
        return pl.ds(pl.multiple_of(it_row[item] * SUBLANES, SUBLANES), tm * SUBLANES)

    def in_copy(item, s):
        return pltpu.make_async_copy(xs_hbm.at[window(item)], xbuf.at[s], sem_in.at[s])

    def out_copy(item, s):
        return pltpu.make_async_copy(ybuf.at[s], ys_hbm.at[window(item)], sem_out.at[s])

    @pl.when(j < n)
    def _():
        @pl.when(j == 0)
        def _():
            in_copy(0, 0).start()
            ybuf[1] = jnp.zeros(ybuf.shape[1:], F32)
            tail = pltpu.make_async_copy(
                ybuf.at[1], ys_hbm.at[pl.ds(ys_hbm.shape[0] - tm * SUBLANES, tm * SUBLANES)], sem_out.at[1])
            tail.start()
            tail.wait()

        @pl.when(j + 1 < n)
        def _():
            in_copy(j + 1, 1 - slot).start()

        in_copy(j, slot).wait()

        @pl.when(jnp.logical_or(j == 0, it_exp[j] != it_exp[jnp.maximum(j - 1, 0)]))
        def _():
            wg_b[...] = wg_ref[0].astype(BF16)
            wu_b[...] = wu_ref[0].astype(BF16)
            wd_b[...] = wd_ref[0].astype(BF16)

        x = _load_token_tiles(xbuf, tm, (slot,)).astype(BF16)
        hg = jnp.dot(x, wg_b[...], preferred_element_type=F32)
        hu = jnp.dot(x, wu_b[...], preferred_element_type=F32)
        hid = (jax.nn.silu(hg) * hu).astype(BF16)
        _store_token_tiles(ybuf, jnp.dot(hid, wd_b[...], preferred_element_type=F32), (slot,))

        @pl.when(j > 0)
        def _():
            out_copy(j - 1, 1 - slot).wait()

        out_copy(j, slot).start()

        @pl.when(j == n - 1)
        def _():
            out_copy(j, slot).wait()


def _experts(items, xs, wg, wu, wd):
    d, de = wg.shape[1], wg.shape[2]
    tm = EXPERT_TILE
    max_items = items[0].shape[0]
    grid_spec = pltpu.PrefetchScalarGridSpec(
        num_scalar_prefetch=3, grid=(max_items,),
        in_specs=[pl.BlockSpec(memory_space=pl.ANY),
                  pl.BlockSpec((1, d, de), lambda j, e, r, n: (e[j], 0, 0)),
                  pl.BlockSpec((1, d, de), lambda j, e, r, n: (e[j], 0, 0)),
                  pl.BlockSpec((1, de, d), lambda j, e, r, n: (e[j], 0, 0))],
        out_specs=pl.BlockSpec(memory_space=pl.ANY),
        scratch_shapes=[pltpu.VMEM((2, tm * SUBLANES, LANES), F32), pltpu.VMEM((2, tm * SUBLANES, LANES), F32),
                        pltpu.VMEM((d, de), BF16), pltpu.VMEM((d, de), BF16), pltpu.VMEM((de, d), BF16),
                        pltpu.SemaphoreType.DMA((2,)), pltpu.SemaphoreType.DMA((2,))])
    return pl.pallas_call(
        _experts_body, grid_spec=grid_spec, out_shape=jax.ShapeDtypeStruct(xs.shape, F32),
        compiler_params=_cparams(("arbitrary",)), name="experts")(*items, xs, wg, wu, wd)


def _combine_body(pos_ref, h_ref, w_ref, g_ref, ys_hbm, y_ref, buf, sem, *, final_norm):
    tile = h_ref.shape[0]

    def issue(r, carry):
        for k in range(MOE_TOP_K):
            p = pos_ref[0, k, r]
            _row_copy(ys_hbm.at[_token_rows(p)], buf.at[k, _token_rows(r)], sem.at[k]).start(priority=k)
        return carry

    lax.fori_loop(0, tile, issue, 0, unroll=8)
    for k in range(MOE_TOP_K):
        _row_copy(ys_hbm.at[pl.ds(0, tile * SUBLANES)], buf.at[k], sem.at[k]).wait()
    h = h_ref[...] + (w_ref[:, 0:1] * _load_token_tiles(buf, tile, (0,))
                      + w_ref[:, 1:2] * _load_token_tiles(buf, tile, (1,)))
    y_ref[...] = _rmsnorm(h, g_ref[...]) if final_norm else h


def _combine(pos3, h_all, w_t, g, ys, rows, row_off, final_norm):
    d = h_all.shape[1]
    tile = TOKEN_TILE
    off = row_off // tile
    return pl.pallas_call(
        functools.partial(_combine_body, final_norm=final_norm), grid=(rows // tile,),
        in_specs=[pl.BlockSpec((1, MOE_TOP_K, tile), lambda i: (off + i, 0, 0), memory_space=pltpu.SMEM),
                  pl.BlockSpec((tile, d), lambda i: (off + i, 0)),
                  pl.BlockSpec((tile, MOE_TOP_K), lambda i: (off + i, 0)),
                  _resident(g.shape),
                  pl.BlockSpec(memory_space=pl.ANY)],
        out_specs=pl.BlockSpec((tile, d), lambda i: (i, 0)),
        out_shape=jax.ShapeDtypeStruct((rows, d), F32),
        scratch_shapes=[pltpu.VMEM((MOE_TOP_K, tile * SUBLANES, LANES), F32),
                        pltpu.SemaphoreType.DMA((MOE_TOP_K,))],
        compiler_params=_cparams(("arbitrary",)), name="combine")(pos3, h_all, w_t, g, ys)


def _lookup(table, idx):
    sel = idx[None] == jnp.arange(table.shape[0], dtype=I32).reshape((-1,) + (1,) * idx.ndim)
    return jnp.sum(jnp.where(sel, table.reshape(sel.shape[:1] + (1,) * idx.ndim), 0), axis=0)


def _work_items(counts, n_sorted):
    tm = EXPERT_TILE
    n_exp = counts.shape[0]
    max_items = n_sorted // tm + n_exp
    ends = jnp.cumsum(counts)
    starts = ends - counts
    n_e = (counts + tm - 1) // tm
    item_end = jnp.cumsum(n_e)
    item_start = item_end - n_e
    n_items = item_end[-1]
    j = jnp.minimum(jnp.arange(max_items, dtype=I32), n_items - 1)
    e = jnp.sum((item_end[None, :] <= j[:, None]).astype(I32), axis=1)
    row = _lookup(starts, e) + (j - _lookup(item_start, e)) * tm
    return (e, row.astype(I32), n_items.reshape(1).astype(I32)), starts


def _s5_discretise(lam_re, lam_im, log_dt, b_re, b_im, c_re, c_im):
    g, p = lam_re.shape
    ch = b_re.shape[-1]
    lam_re = lam_re.astype(F32)
    lam_im = lam_im.astype(F32)
    dt = jnp.exp(log_dt.astype(F32))[:, None]
    mag = jnp.exp(lam_re * dt)
    ab_re = mag * jnp.cos(lam_im * dt)
    ab_im = mag * jnp.sin(lam_im * dt)
    den = lam_re * lam_re + lam_im * lam_im
    nr = ab_re - 1.0
    coef_re = (nr * lam_re + ab_im * lam_im) / den
    coef_im = (ab_im * lam_re - nr * lam_im) / den
    bb_re = coef_re[..., None] * b_re - coef_im[..., None] * b_im
    bb_im = coef_re[..., None] * b_im + coef_im[..., None] * b_re
    gh = g // 2
    eye = jnp.eye(gh, dtype=F32)

    def in_block(m):
        return jnp.einsum("gpc,gh->gchp", m, eye).reshape(gh * ch, gh * p)

    def out_block(m):
        return jnp.einsum("gcp,gh->gphc", m, eye).reshape(gh * p, gh * ch)

    bb = jnp.stack([jnp.concatenate([in_block(bb_re[k * gh:(k + 1) * gh]), in_block(bb_im[k * gh:(k + 1) * gh])],
                                    axis=1) for k in range(2)]).astype(BF16)
    cc = jnp.stack([jnp.concatenate([out_block(c_re[k * gh:(k + 1) * gh]), out_block(-c_im[k * gh:(k + 1) * gh])],
                                    axis=0) for k in range(2)]).astype(BF16)
    return ab_re.reshape(1, g * p), ab_im.reshape(1, g * p), bb, cc


def kernel(x_prompt, x_sample, state_hgrn, state_s5_re, state_s5_im, norm_mix_g, w_in, hgrn_lb_raw, hgrn_onorm_g, w_branch_a, s5_lambda_re, s5_lambda_im, s5_log_dt, s5_b_re, s5_b_im, s5_c_re, s5_c_im, s5_d, w_glu, b_glu, w_out, norm_ffn_g, w_router_group, b_router_group, w_router_expert, b_router_expert, w_exp_gate, w_exp_up, w_exp_down, norm_final_g):
    depth = norm_mix_g.shape[0]
    bp, lp, d = x_prompt.shape
    bs, ls, _ = x_sample.shape
    heads, dk = state_hgrn.shape[2], state_hgrn.shape[3]
    kw = heads * dk
    s5_groups, s5_state = state_s5_re.shape[2], state_s5_re.shape[3]
    s5_width = s5_d.shape[-1]
    nstate = s5_groups * s5_state
    moe_groups, _, experts = w_router_expert.shape[1:]
    n_exp = moe_groups * experts
    rows_p, rows_s = bp * lp, bs * ls
    total = rows_p + rows_s
    n_sorted = total * MOE_TOP_K
    assert kw == d and state_hgrn.shape[4] == dk, "column blocks assume key width == value width == model width"
    assert d == SUBLANES * LANES, "token-tile layout holds one token per (8, 128) tile"
    assert s5_groups % 2 == 0 and bp % SUBLANES == 0 and bs % HGRN_SEQ_TILE == 0

    lb_all = jnp.cumsum(jax.nn.softmax(hgrn_lb_raw.astype(F32), axis=0), axis=0)

    hp = x_prompt.reshape(rows_p, d)
    hs = x_sample.reshape(rows_s, d)
    hg_p, re_p, im_p, hg_s, re_s, im_s = [], [], [], [], [], []
    zeros_state = jnp.zeros((bp // SUBLANES, SUBLANES, nstate), F32)

    for l in range(depth):
        w = w_in[l]
        w_cols = jnp.concatenate([w[:, :4 * kw], w[:, 4 * kw + s5_width:], w[:, 4 * kw:4 * kw + s5_width]],
                                 axis=1).astype(BF16)
        proj = _in_proj(hp, hs, norm_mix_g[l].reshape(1, d), w_cols)

        ar, ai, bb, cc = _s5_discretise(s5_lambda_re[l], s5_lambda_im[l], s5_log_dt[l], s5_b_re[l], s5_b_im[l],
                                        s5_c_re[l], s5_c_im[l])
        s5_args = (ar, ai, bb, cc, s5_d[l].reshape(1, s5_width), w_glu[l].astype(BF16), b_glu[l].reshape(1, -1))
        yb_p, fr_p, fi_p = _s5_branch(proj, 6 * kw, s5_width, bp, lp, 0, zeros_state, zeros_state, *s5_args)
        yb_s, fr_s, fi_s = _s5_branch(proj, 6 * kw, s5_width, bs, ls, rows_p,
                                      state_s5_re[l].reshape(bs // SUBLANES, SUBLANES, nstate),
                                      state_s5_im[l].reshape(bs // SUBLANES, SUBLANES, nstate), *s5_args)

        lb = lb_all[l].reshape(1, kw)
        gn = hgrn_onorm_g[l].reshape(1, kw)
        o_p, hgp = _hgrn_long(proj, lb, gn, bp, lp, heads, dk, 0)
        o_s, hgs = _hgrn_short(proj, lb, gn, state_hgrn[l].astype(F32), ls, rows_p)

        nr = -(-(moe_groups + n_exp) // SUBLANES) * SUBLANES
        wr = jnp.concatenate([w_router_group[l].T, w_router_expert[l].transpose(0, 2, 1).reshape(n_exp, d)], axis=0)
        wr = jnp.pad(wr, ((0, nr - wr.shape[0]), (0, 0))).astype(BF16)
        br = jnp.pad(jnp.concatenate([b_router_group[l], b_router_expert[l].reshape(n_exp)]),
                     (0, nr - moe_groups - n_exp)).reshape(nr, 1).astype(F32)
        h_all, xn_all, logits_t = _merge(
            o_p, o_s, yb_p.reshape(rows_p, d), yb_s.reshape(rows_s, d), proj, hp, hs,
            w_branch_a[l].astype(BF16), w_out[l].astype(BF16), norm_ffn_g[l].reshape(1, d), wr, br)

        ids, wts, ranks, cnt = _route(logits_t, moe_groups, experts)
        items, starts = _work_items(cnt[:, 0].astype(I32), n_sorted)
        pos = _lookup(starts, ids) + ranks
        pos3 = pos.reshape(MOE_TOP_K, total // TOKEN_TILE, TOKEN_TILE).transpose(1, 0, 2)
        xs = _dispatch(pos3, xn_all)
        ys = _experts(items, xs, w_exp_gate[l], w_exp_up[l], w_exp_down[l])

        last = l == depth - 1
        g_out = norm_final_g.reshape(1, d)
        hp = _combine(pos3, h_all, wts.T, g_out, ys, rows_p, 0, last)
        hs = _combine(pos3, h_all, wts.T, g_out, ys, rows_s, rows_p, last)

        hg_p.append(hgp)
        hg_s.append(hgs)
        re_p.append(fr_p.reshape(bp, s5_groups, s5_state))
        im_p.append(fi_p.reshape(bp, s5_groups, s5_state))
        re_s.append(fr_s.reshape(bs, s5_groups, s5_state))
        im_s.append(fi_s.reshape(bs, s5_groups, s5_state))

    y_prompt = hp.reshape(bp, lp, d).astype(x_prompt.dtype)
    y_sample = hs.reshape(bs, ls, d).astype(x_sample.dtype)
    return (y_prompt, y_sample, jnp.stack(hg_p), jnp.stack(re_p), jnp.stack(im_p),
            jnp.stack(hg_s), jnp.stack(re_s), jnp.stack(im_s))
```

```python
import functools

import jax
import jax.numpy as jnp
from jax import lax
from jax.experimental import pallas as pl
from jax.experimental.pallas import tpu as pltpu

F32 = jnp.float32
BF16 = jnp.bfloat16
I32 = jnp.int32

RMS_EPS = 1e-6
HG_CHUNK = 64
MOE_TOP_K = 2

V7X_VMEM_BYTES = 64 * 1024 * 1024
VMEM_LIMIT_BYTES = V7X_VMEM_BYTES - 8 * 1024 * 1024
SUBLANES = 8
LANES = 128

TOKEN_TILE = 512
EXPERT_TILE = 512
EXPERT_WINDOWS = (EXPERT_TILE, EXPERT_TILE // 2, EXPERT_TILE // 4)
S5_TIME_TILE = 32
HGRN_TIME_TILE = 256
HGRN_SEQ_TILE = 8
PROJ_COL_TILE = 512


def _cparams(sem):
    return pltpu.CompilerParams(dimension_semantics=sem, vmem_limit_bytes=VMEM_LIMIT_BYTES)


def _resident(shape):
    nd = len(shape)
    return pl.BlockSpec(shape, lambda *_: (0,) * nd, pipeline_mode=pl.Buffered(1))


def _rmsnorm(x, g):
    return x * lax.rsqrt(jnp.mean(x * x, axis=-1, keepdims=True) + RMS_EPS) * g


def _two_source_specs(tm, width, n_first):
    return [pl.BlockSpec((tm, width), lambda i: (jnp.minimum(i, n_first - 1), 0)),
            pl.BlockSpec((tm, width), lambda i: (jnp.maximum(i - n_first, 0), 0))]


def _pick(first_ref, second_ref, n_first):
    return jnp.where(pl.program_id(0) < n_first, first_ref[...], second_ref[...])


def _store_token_tiles(ref, x, lead=()):
    rows = x.shape[0]
    for c in range(SUBLANES):
        ref[lead + (pl.ds(c, rows, stride=SUBLANES), slice(None))] = x[:, c * LANES:(c + 1) * LANES]


def _load_token_tiles(ref, rows, lead=()):
    return jnp.concatenate([ref[lead + (pl.ds(c, rows, stride=SUBLANES), slice(None))] for c in range(SUBLANES)],
                           axis=-1)


def _inproj_body(xp_ref, xs_ref, g_ref, w_ref, o_ref, *, n_first):
    xb = _rmsnorm(_pick(xp_ref, xs_ref, n_first), g_ref[...]).astype(BF16)
    for j in range(0, w_ref.shape[1], PROJ_COL_TILE):
        o_ref[:, j:j + PROJ_COL_TILE] = jnp.dot(xb, w_ref[:, j:j + PROJ_COL_TILE], preferred_element_type=F32)


def _in_proj(xp, xs, g, w):
    d = xp.shape[1]
    n = w.shape[1]
    tm = TOKEN_TILE
    total = xp.shape[0] + xs.shape[0]
    n_first = xp.shape[0] // tm
    return pl.pallas_call(
        functools.partial(_inproj_body, n_first=n_first), grid=(total // tm,),
        in_specs=_two_source_specs(tm, d, n_first) + [_resident((1, d)), _resident((d, n))],
        out_specs=pl.BlockSpec((tm, n), lambda i: (i, 0)),
        out_shape=jax.ShapeDtypeStruct((total, n), F32),
        compiler_params=_cparams(("parallel",)), name="in_proj")(xp, xs, g, w)


def _s5_body(*refs, tt, nstate):
    u_refs = refs[:SUBLANES]
    (h0r_ref, h0i_ref, ar_ref, ai_ref, bb_ref, cc_ref, d_ref, wg_ref, bg_ref,
     y_ref, hr_out, hi_out, hr_scr, hi_scr, bu_scr, u_scr, y_scr) = refs[SUBLANES:]
    j = pl.program_id(1)
    half = nstate // 2
    w = u_refs[0].shape[-1]
    kw = w // 2

    @pl.when(j == 0)
    def _():
        hr_scr[...] = h0r_ref[0]
        hi_scr[...] = h0i_ref[0]

    for b in range(SUBLANES):
        ub = u_refs[b][...]
        for s in range(w // LANES):
            u_scr[s, pl.ds(b, tt, stride=SUBLANES), :] = ub[:, s * LANES:(s + 1) * LANES]
    u = jnp.concatenate([u_scr[s] for s in range(w // LANES)], axis=-1)
    ub16 = u.astype(BF16)
    for kt in range(2):
        ukt = ub16[:, kt * kw:(kt + 1) * kw]
        bu_scr[:, kt * half:(kt + 1) * half] = jnp.dot(ukt, bb_ref[kt, :, :half], preferred_element_type=F32)
        bu_scr[:, nstate + kt * half:nstate + (kt + 1) * half] = jnp.dot(
            ukt, bb_ref[kt, :, half:], preferred_element_type=F32)

    lane_chunk = 512
    for lc in range(nstate // lane_chunk):
        lo = lc * lane_chunk
        re_sl = slice(lo, lo + lane_chunk)
        im_sl = slice(nstate + lo, nstate + lo + lane_chunk)
        ar = jnp.broadcast_to(ar_ref[:, re_sl], (SUBLANES, lane_chunk))
        ai = jnp.broadcast_to(ai_ref[:, re_sl], (SUBLANES, lane_chunk))

        def step(t, carry, re_sl=re_sl, im_sl=im_sl, ar=ar, ai=ai):
            hr, hi = carry
            rs = pl.ds(pl.multiple_of(t * SUBLANES, SUBLANES), SUBLANES)
            nhr = ar * hr - ai * hi + bu_scr[rs, re_sl]
            nhi = ar * hi + ai * hr + bu_scr[rs, im_sl]
            bu_scr[rs, re_sl] = nhr
            bu_scr[rs, im_sl] = nhi
            return nhr, nhi

        hr, hi = lax.fori_loop(0, tt, step, (hr_scr[:, re_sl], hi_scr[:, re_sl]), unroll=4)
        hr_scr[:, re_sl] = hr
        hi_scr[:, re_sl] = hi

    ys = []
    for n in range(2):
        h_re = bu_scr[:, n * half:(n + 1) * half].astype(BF16)
        h_im = bu_scr[:, nstate + n * half:nstate + (n + 1) * half].astype(BF16)
        ys.append(jnp.dot(h_re, cc_ref[n, :half, :], preferred_element_type=F32)
                  + jnp.dot(h_im, cc_ref[n, half:, :], preferred_element_type=F32))
    y = jnp.concatenate(ys, axis=-1) + d_ref[...] * u
    z = jnp.dot(jax.nn.gelu(y).astype(BF16), wg_ref[...], preferred_element_type=F32) + bg_ref[...]
    dm = z.shape[-1] // 2
    yb = z[:, :dm] * jax.nn.sigmoid(z[:, dm:])
    for s in range(dm // LANES):
        y_scr[s] = yb[:, s * LANES:(s + 1) * LANES]
    for b in range(SUBLANES):
        for s in range(dm // LANES):
            y_ref[b, :, s * LANES:(s + 1) * LANES] = y_scr[s, pl.ds(b, tt, stride=SUBLANES), :]

    @pl.when(j == pl.num_programs(1) - 1)
    def _():
        hr_out[0] = hr_scr[...]
        hi_out[0] = hi_scr[...]


def _s5_branch(proj, u_col, width, batch, seq, row_off, h0r, h0i, ar, ai, bb, cc, d_skip, w_glu, b_glu):
    nstate = ar.shape[-1]
    dm = w_glu.shape[1] // 2
    tt = min(S5_TIME_TILE, seq)
    nj = seq // tt
    nbb = batch // SUBLANES
    body = functools.partial(_s5_body, tt=tt, nstate=nstate)

    def u_spec(b):
        return pl.BlockSpec((tt, width), lambda bb_, j, b=b: (row_off // tt + (bb_ * SUBLANES + b) * nj + j,
                                                               u_col // width))

    state_spec = pl.BlockSpec((1, SUBLANES, nstate), lambda bb_, j: (bb_, 0, 0))
    return pl.pallas_call(
        body, grid=(nbb, nj),
        in_specs=[u_spec(b) for b in range(SUBLANES)] + [
            state_spec, state_spec, _resident(ar.shape), _resident(ai.shape), _resident(bb.shape),
            _resident(cc.shape), _resident(d_skip.shape), _resident(w_glu.shape), _resident(b_glu.shape)],
        out_specs=[pl.BlockSpec((SUBLANES, tt, dm), lambda bb_, j: (bb_, j, 0)), state_spec, state_spec],
        out_shape=[jax.ShapeDtypeStruct((batch, seq, dm), F32),
                   jax.ShapeDtypeStruct((nbb, SUBLANES, nstate), F32),
                   jax.ShapeDtypeStruct((nbb, SUBLANES, nstate), F32)],
        scratch_shapes=[pltpu.VMEM((SUBLANES, nstate), F32), pltpu.VMEM((SUBLANES, nstate), F32),
                        pltpu.VMEM((tt * SUBLANES, 2 * nstate), F32),
                        pltpu.VMEM((width // LANES, tt * SUBLANES, LANES), F32),
                        pltpu.VMEM((dm // LANES, tt * SUBLANES, LANES), F32)],
        compiler_params=_cparams(("parallel", "arbitrary")), name="s5_branch")(
            *([proj] * SUBLANES), h0r, h0i, ar, ai, bb, cc, d_skip, w_glu, b_glu)


def _cumsum_rows(x):
    c = x.shape[0]
    row = lax.broadcasted_iota(I32, x.shape, 0)
    s = 1
    while s < c:
        x = x + jnp.where(row >= s, pltpu.roll(x, s, axis=0), 0.0)
        s *= 2
    return x


def _hgrn_gates(q, fr, lb, scale):
    f = lb + (1.0 - lb) * jax.nn.sigmoid(fr)
    k = 1.0 - f
    b = _cumsum_rows(jnp.log(f))
    b_last = b[-1:, :]
    q_dec = (q * scale) * jnp.exp(b)
    k_dec = k * jnp.exp(-b)
    k_end = k * jnp.exp(b_last - b)
    return q_dec.astype(BF16), k_dec.astype(BF16), k_end.astype(BF16), b_last


def _causal_scores(q_dec, k_dec):
    c = q_dec.shape[0]
    s = lax.dot_general(q_dec, k_dec, (((1,), (1,)), ((), ())), preferred_element_type=F32)
    keep = lax.broadcasted_iota(I32, (c, c), 0) >= lax.broadcasted_iota(I32, (c, c), 1)
    return jnp.where(keep, s, 0.0).astype(BF16)


def _gated_out(o, gn, og):
    o = o * lax.rsqrt(jnp.mean(o * o, axis=-1, keepdims=True) + RMS_EPS) * gn
    return (o * jax.nn.silu(og)).astype(BF16)


def _hgrn_long_body(q_ref, f_ref, v_ref, og_ref, lb_ref, gn_ref, o_ref, sfin_ref, st_scr, *, c, heads, dk, scale):
    j = pl.program_id(1)

    @pl.when(j == 0)
    def _():
        st_scr[...] = jnp.zeros_like(st_scr)

    def chunk(ci, carry):
        rs = pl.ds(pl.multiple_of(ci * c, c), c)
        for h in range(heads):
            hs = slice(h * dk, (h + 1) * dk)
            q_dec, k_dec, k_end, b_last = _hgrn_gates(q_ref[rs, hs], f_ref[rs, hs], lb_ref[:, hs], scale)
            v = v_ref[rs, hs].astype(BF16)
            scores = _causal_scores(q_dec, k_dec)
            st = st_scr[h]
            o = (lax.dot_general(q_dec, st.astype(BF16), (((1,), (1,)), ((), ())), preferred_element_type=F32)
                 + jnp.dot(scores, v, preferred_element_type=F32))
            st_scr[h] = jnp.exp(b_last) * st + lax.dot_general(
                v, k_end, (((0,), (0,)), ((), ())), preferred_element_type=F32)
            o_ref[rs, hs] = _gated_out(o, gn_ref[:, hs], og_ref[rs, hs])
        return carry

    lax.fori_loop(0, q_ref.shape[0] // c, chunk, 0)

    @pl.when(j == pl.num_programs(1) - 1)
    def _():
        for h in range(heads):
            sfin_ref[0, h] = st_scr[h].T


def _hgrn_long(proj, lb, gn, batch, seq, heads, dk, row_off):
    width = heads * dk
    tb = min(HGRN_TIME_TILE, seq)
    nj = seq // tb
    off = row_off // tb
    body = functools.partial(_hgrn_long_body, c=min(HG_CHUNK, seq), heads=heads, dk=dk, scale=dk ** -0.5)

    def col(k):
        return pl.BlockSpec((tb, width), lambda b, j, k=k: (off + b * nj + j, k))

    return pl.pallas_call(
        body, grid=(batch, nj),
        in_specs=[col(0), col(1), col(2), col(3), _resident(lb.shape), _resident(gn.shape)],
        out_specs=[pl.BlockSpec((tb, width), lambda b, j: (b * nj + j, 0)),
                   pl.BlockSpec((1, heads, dk, dk), lambda b, j: (b, 0, 0, 0))],
        out_shape=[jax.ShapeDtypeStruct((batch * seq, width), BF16),
                   jax.ShapeDtypeStruct((batch, heads, dk, dk), F32)],
        scratch_shapes=[pltpu.VMEM((heads, dk, dk), F32)],
        compiler_params=_cparams(("parallel", "arbitrary")), name="hgrn_long")(proj, proj, proj, proj, lb, gn)


def _hgrn_short_body(q_ref, f_ref, v_ref, og_ref, lb_ref, gn_ref, s0_ref, o_ref, snew_ref, *, c, heads, dk, scale):
    def one_seq(sq, carry):
        rs = pl.ds(pl.multiple_of(sq * c, c), c)
        for h in range(heads):
            hs = slice(h * dk, (h + 1) * dk)
            q_dec, k_dec, k_end, b_last = _hgrn_gates(q_ref[rs, hs], f_ref[rs, hs], lb_ref[:, hs], scale)
            v = v_ref[rs, hs].astype(BF16)
            scores = _causal_scores(q_dec, k_dec)
            s0 = s0_ref[sq, h]
            o = (jnp.dot(q_dec, s0.astype(BF16), preferred_element_type=F32)
                 + jnp.dot(scores, v, preferred_element_type=F32))
            decay_col = jnp.broadcast_to(jnp.exp(b_last), (dk, dk)).T
            snew_ref[sq, h] = decay_col * s0 + lax.dot_general(
                k_end, v, (((0,), (0,)), ((), ())), preferred_element_type=F32)
            o_ref[rs, hs] = _gated_out(o, gn_ref[:, hs], og_ref[rs, hs])
        return carry

    lax.fori_loop(0, s0_ref.shape[0], one_seq, 0)


def _hgrn_short(proj, lb, gn, s0, seq, row_off):
    batch, heads, dk, _ = s0.shape
    width = heads * dk
    nb = HGRN_SEQ_TILE
    rows = nb * seq
    off = row_off // rows
    body = functools.partial(_hgrn_short_body, c=seq, heads=heads, dk=dk, scale=dk ** -0.5)

    def col(k):
        return pl.BlockSpec((rows, width), lambda i, k=k: (off + i, k))

    state_spec = pl.BlockSpec((nb, heads, dk, dk), lambda i: (i, 0, 0, 0))
    return pl.pallas_call(
        body, grid=(batch // nb,),
        in_specs=[col(0), col(1), col(2), col(3), _resident(lb.shape), _resident(gn.shape), state_spec],
        out_specs=[pl.BlockSpec((rows, width), lambda i: (i, 0)), state_spec],
        out_shape=[jax.ShapeDtypeStruct((batch * seq, width), BF16), jax.ShapeDtypeStruct(s0.shape, F32)],
        compiler_params=_cparams(("parallel",)), name="hgrn_short")(proj, proj, proj, proj, lb, gn, s0)


def _merge_body(op_ref, os_ref, ybp_ref, ybs_ref, ga_ref, gb_ref, xp_ref, xs_ref, wa_ref, wo_ref, gf_ref, wr_ref,
                br_ref, h_ref, xn_ref, lg_ref, *, n_first):
    y_a = jnp.dot(_pick(op_ref, os_ref, n_first), wa_ref[...], preferred_element_type=F32)
    merged = jax.nn.sigmoid(ga_ref[...]) * y_a + jax.nn.sigmoid(gb_ref[...]) * _pick(ybp_ref, ybs_ref, n_first)
    h = _pick(xp_ref, xs_ref, n_first) + jnp.dot(merged.astype(BF16), wo_ref[...], preferred_element_type=F32)
    h_ref[...] = h
    xn = _rmsnorm(h, gf_ref[...])
    _store_token_tiles(xn_ref, xn)
    lg_ref[...] = lax.dot_general(wr_ref[...], xn.astype(BF16), (((1,), (1,)), ((), ())),
                                  preferred_element_type=F32) + br_ref[...]


def _merge(o_p, o_s, yb_p, yb_s, proj, xp, xs, wa, wo, gf, wr, br):
    d = xp.shape[1]
    total = proj.shape[0]
    nr = wr.shape[0]
    tm = TOKEN_TILE
    n_first = xp.shape[0] // tm
    pair = _two_source_specs(tm, d, n_first)

    def row(k=0):
        return pl.BlockSpec((tm, d), lambda i, k=k: (i, k))

    return pl.pallas_call(
        functools.partial(_merge_body, n_first=n_first), grid=(total // tm,),
        in_specs=pair + pair + [row(4), row(5)] + pair + [
            _resident(wa.shape), _resident(wo.shape), _resident(gf.shape), _resident(wr.shape), _resident(br.shape)],
        out_specs=[row(), pl.BlockSpec((tm * SUBLANES, LANES), lambda i: (i, 0)),
                   pl.BlockSpec((nr, tm), lambda i: (0, i))],
        out_shape=[jax.ShapeDtypeStruct((total, d), F32), jax.ShapeDtypeStruct((total * SUBLANES, LANES), F32),
                   jax.ShapeDtypeStruct((nr, total), F32)],
        compiler_params=_cparams(("parallel",)), name="merge_out")(
            o_p, o_s, yb_p, yb_s, proj, proj, xp, xs, wa, wo, gf, wr, br)


def _first_index_of_max(vals):
    m = vals[0]
    for v in vals[1:]:
        m = jnp.maximum(m, v)
    idx = jnp.full(m.shape, len(vals), I32)
    for e in range(len(vals) - 1, -1, -1):
        idx = jnp.where(vals[e] == m, e, idx)
    return m, idx


def _route_body(lg_ref, ids_ref, w_ref, rk_ref, cnt_ref, carry_scr, *, groups, experts):
    i = pl.program_id(0)
    tile = lg_ref.shape[1]
    n_exp = groups * experts

    @pl.when(i == 0)
    def _():
        carry_scr[...] = jnp.zeros_like(carry_scr)

    gl = [lg_ref[g:g + 1, :] for g in range(groups)]
    gmax, gidx = _first_index_of_max(gl)
    denom = jnp.exp(gl[0] - gmax)
    for g in range(1, groups):
        denom = denom + jnp.exp(gl[g] - gmax)
    g_w = 1.0 / denom

    el = []
    for e in range(experts):
        v = lg_ref[groups + e:groups + e + 1, :]
        for g in range(1, groups):
            r = groups + g * experts + e
            v = jnp.where(gidx == g, lg_ref[r:r + 1, :], v)
        el.append(v)
    v1, i1 = _first_index_of_max(el)
    rest = [jnp.where(i1 == e, -jnp.inf, el[e]) for e in range(experts)]
    v2, i2 = _first_index_of_max(rest)
    t = jnp.exp(v2 - v1)
    inv = 1.0 / (1.0 + t)
    e1 = gidx * experts + i1
    e2 = gidx * experts + i2

    erow = lax.broadcasted_iota(I32, (n_exp, tile), 0)
    oh1 = (erow == e1).astype(F32)
    oh2 = (erow == e2).astype(F32)
    oh = oh1 + oh2
    before = (lax.broadcasted_iota(I32, (tile, tile), 0) < lax.broadcasted_iota(I32, (tile, tile), 1))
    cnt = jnp.dot(oh.astype(BF16), before.astype(BF16), preferred_element_type=F32) + carry_scr[:, 0:1]
    ids_ref[0:1, :] = e1
    ids_ref[1:2, :] = e2
    w_ref[0:1, :] = inv * g_w
    w_ref[1:2, :] = (t * inv) * g_w
    rk_ref[0:1, :] = jnp.sum(oh1 * cnt, axis=0, keepdims=True).astype(I32)
    rk_ref[1:2, :] = jnp.sum(oh2 * cnt, axis=0, keepdims=True).astype(I32)
    carry_scr[...] = carry_scr[...] + jnp.sum(oh, axis=1, keepdims=True)

    @pl.when(i == pl.num_programs(0) - 1)
    def _():
        cnt_ref[...] = carry_scr[...]


def _route(logits_t, groups, experts):
    nr, total = logits_t.shape
    tile = TOKEN_TILE
    n_exp = groups * experts
    body = functools.partial(_route_body, groups=groups, experts=experts)
    pair = pl.BlockSpec((MOE_TOP_K, tile), lambda i: (0, i))
    return pl.pallas_call(
        body, grid=(total // tile,),
        in_specs=[pl.BlockSpec((nr, tile), lambda i: (0, i))],
        out_specs=[pair, pair, pair, pl.BlockSpec((n_exp, LANES), lambda i: (0, 0))],
        out_shape=[jax.ShapeDtypeStruct((MOE_TOP_K, total), I32), jax.ShapeDtypeStruct((MOE_TOP_K, total), F32),
                   jax.ShapeDtypeStruct((MOE_TOP_K, total), I32), jax.ShapeDtypeStruct((n_exp, LANES), F32)],
        scratch_shapes=[pltpu.VMEM((n_exp, LANES), F32)],
        compiler_params=_cparams(("arbitrary",)), name="route")(logits_t)


def _row_copy(src, dst, sem):
    return pltpu.make_async_copy(src, dst, sem)


def _token_rows(r):
    return pl.ds(pl.multiple_of(r * SUBLANES, SUBLANES), SUBLANES)


def _dispatch_body(pos_ref, x_ref, o_hbm, zero_scr, sem, *, n_sorted):
    tile = x_ref.shape[0] // SUBLANES

    @pl.when(pl.program_id(0) == 0)
    def _():
        zero_scr[...] = jnp.zeros_like(zero_scr)
        pad = _row_copy(zero_scr, o_hbm.at[pl.ds(n_sorted * SUBLANES, zero_scr.shape[0])], sem.at[MOE_TOP_K])
        pad.start()
        pad.wait()

    def issue(r, carry):
        for k in range(MOE_TOP_K):
            p = pos_ref[MOE_TOP_K * r + k]
            _row_copy(x_ref.at[_token_rows(r)], o_hbm.at[_token_rows(p)], sem.at[k]).start(priority=k)
        return carry

    lax.fori_loop(0, tile, issue, 0, unroll=8)
    for k in range(MOE_TOP_K):
        _row_copy(x_ref, o_hbm.at[pl.ds(0, tile * SUBLANES)], sem.at[k]).wait()


def _dispatch(pos3, xn_tiles):
    total = xn_tiles.shape[0] // SUBLANES
    tile = TOKEN_TILE
    n_sorted = total * MOE_TOP_K
    return pl.pallas_call(
        functools.partial(_dispatch_body, n_sorted=n_sorted), grid=(total // tile,),
        in_specs=[pl.BlockSpec((MOE_TOP_K * tile,), lambda i: (i,), memory_space=pltpu.SMEM),
                  pl.BlockSpec((tile * SUBLANES, LANES), lambda i: (i, 0))],
        out_specs=pl.BlockSpec(memory_space=pl.ANY),
        out_shape=jax.ShapeDtypeStruct(((n_sorted + EXPERT_TILE) * SUBLANES, LANES), F32),
        scratch_shapes=[pltpu.VMEM((EXPERT_TILE * SUBLANES, LANES), F32),
                        pltpu.SemaphoreType.DMA((MOE_TOP_K + 1,))],
        compiler_params=_cparams(("arbitrary",)), name="dispatch")(pos3, xn_tiles)


def _experts_body(it_exp, it_row, it_cls, n_items, xs_hbm, wg_ref, wu_ref, wd_ref, ys_hbm, xbuf, ybuf, wg_b, wu_b,
                  wd_b, sem_in, sem_out):
    j = pl.program_id(0)
    n = n_items[0]
    tm = EXPERT_TILE
    slot = lax.rem(j, 2)

    def by_size(item, fn):
        for ci, m in enumerate(EXPERT_WINDOWS):
            pl.when(it_cls[item] == ci)(functools.partial(fn, m))

    def window(item, m):
        return pl.ds(pl.multiple_of(it_row[item] * SUBLANES, SUBLANES), m * SUBLANES)

    def in_copy(item, s, m):
        return pltpu.make_async_copy(xs_hbm.at[window(item, m)], xbuf.at[s, pl.ds(0, m * SUBLANES)], sem_in.at[s])

    def out_copy(item, s, m):
        return pltpu.make_async_copy(ybuf.at[s, pl.ds(0, m * SUBLANES)], ys_hbm.at[window(item, m)], sem_out.at[s])

    def compute(m):
        x = _load_token_tiles(xbuf, m, (slot,)).astype(BF16)
        hg = jnp.dot(x, wg_b[...], preferred_element_type=F32)
        hu = jnp.dot(x, wu_b[...], preferred_element_type=F32)
        hid = (jax.nn.silu(hg) * hu).astype(BF16)
        _store_token_tiles(ybuf, jnp.dot(hid, wd_b[...], preferred_element_type=F32), (slot,))

    @pl.when(j < n)
    def _():
        @pl.when(j == 0)
        def _():
            by_size(0, lambda m: in_copy(0, 0, m).start())
            ybuf[1] = jnp.zeros(ybuf.shape[1:], F32)
            tail = pltpu.make_async_copy(
                ybuf.at[1], ys_hbm.at[pl.ds(ys_hbm.shape[0] - tm * SUBLANES, tm * SUBLANES)], sem_out.at[1])
            tail.start()
            tail.wait()

        @pl.when(j + 1 < n)
        def _():
            by_size(j + 1, lambda m: in_copy(j + 1, 1 - slot, m).start())

        by_size(j, lambda m: in_copy(j, slot, m).wait())

        @pl.when(jnp.logical_or(j == 0, it_exp[j] != it_exp[jnp.maximum(j - 1, 0)]))
        def _():
            wg_b[...] = wg_ref[0].astype(BF16)
            wu_b[...] = wu_ref[0].astype(BF16)
            wd_b[...] = wd_ref[0].astype(BF16)

        by_size(j, compute)

        @pl.when(j > 0)
        def _():
            by_size(j - 1, lambda m: out_copy(j - 1, 1 - slot, m).wait())

        by_size(j, lambda m: out_copy(j, slot, m).start())

        @pl.when(j == n - 1)
        def _():
            by_size(j, lambda m: out_copy(j, slot, m).wait())


def _experts(items, xs, wg, wu, wd):
    d, de = wg.shape[1], wg.shape[2]
    tm = EXPERT_TILE
    max_items = items[0].shape[0]
    grid_spec = pltpu.PrefetchScalarGridSpec(
        num_scalar_prefetch=4, grid=(max_items,),
        in_specs=[pl.BlockSpec(memory_space=pl.ANY),
                  pl.BlockSpec((1, d, de), lambda j, e, r, c, n: (e[j], 0, 0)),
                  pl.BlockSpec((1, d, de), lambda j, e, r, c, n: (e[j], 0, 0)),
                  pl.BlockSpec((1, de, d), lambda j, e, r, c, n: (e[j], 0, 0))],
        out_specs=pl.BlockSpec(memory_space=pl.ANY),
        scratch_shapes=[pltpu.VMEM((2, tm * SUBLANES, LANES), F32), pltpu.VMEM((2, tm * SUBLANES, LANES), F32),
                        pltpu.VMEM((d, de), BF16), pltpu.VMEM((d, de), BF16), pltpu.VMEM((de, d), BF16),
                        pltpu.SemaphoreType.DMA((2,)), pltpu.SemaphoreType.DMA((2,))])
    return pl.pallas_call(
        _experts_body, grid_spec=grid_spec, out_shape=jax.ShapeDtypeStruct(xs.shape, F32),
        compiler_params=_cparams(("arbitrary",)), name="experts")(*items, xs, wg, wu, wd)


def _combine_body(pos_ref, h_ref, w_ref, g_ref, ys_hbm, y_ref, buf, sem, *, final_norm):
    tile = h_ref.shape[0]

    def issue(r, carry):
        for k in range(MOE_TOP_K):
            p = pos_ref[MOE_TOP_K * r + k]
            _row_copy(ys_hbm.at[_token_rows(p)], buf.at[k, _token_rows(r)], sem.at[k]).start(priority=k)
        return carry

    lax.fori_loop(0, tile, issue, 0, unroll=8)
    for k in range(MOE_TOP_K):
        _row_copy(ys_hbm.at[pl.ds(0, tile * SUBLANES)], buf.at[k], sem.at[k]).wait()
    h = h_ref[...] + (w_ref[:, 0:1] * _load_token_tiles(buf, tile, (0,))
                      + w_ref[:, 1:2] * _load_token_tiles(buf, tile, (1,)))
    y_ref[...] = _rmsnorm(h, g_ref[...]) if final_norm else h


def _combine(pos3, h_all, w_t, g, ys, rows, row_off, final_norm):
    d = h_all.shape[1]
    tile = TOKEN_TILE
    off = row_off // tile
    return pl.pallas_call(
        functools.partial(_combine_body, final_norm=final_norm), grid=(rows // tile,),
        in_specs=[pl.BlockSpec((MOE_TOP_K * tile,), lambda i: (off + i,), memory_space=pltpu.SMEM),
                  pl.BlockSpec((tile, d), lambda i: (off + i, 0)),
                  pl.BlockSpec((tile, MOE_TOP_K), lambda i: (off + i, 0)),
                  _resident(g.shape),
                  pl.BlockSpec(memory_space=pl.ANY)],
        out_specs=pl.BlockSpec((tile, d), lambda i: (i, 0)),
        out_shape=jax.ShapeDtypeStruct((rows, d), F32),
        scratch_shapes=[pltpu.VMEM((MOE_TOP_K, tile * SUBLANES, LANES), F32),
                        pltpu.SemaphoreType.DMA((MOE_TOP_K,))],
        compiler_params=_cparams(("arbitrary",)), name="combine")(pos3, h_all, w_t, g, ys)


def _lookup(table, idx):
    sel = idx[None] == jnp.arange(table.shape[0], dtype=I32).reshape((-1,) + (1,) * idx.ndim)
    return jnp.sum(jnp.where(sel, table.reshape(sel.shape[:1] + (1,) * idx.ndim), 0), axis=0)


def _work_items(counts, n_sorted):
    big, mid, small = EXPERT_WINDOWS
    n_exp = counts.shape[0]
    max_items = n_sorted // big + 2 * n_exp
    ends = jnp.cumsum(counts)
    starts = ends - counts
    units = (counts % big + small - 1) // small
    n_big = counts // big + (units == big // small)
    units = jnp.where(units == big // small, 0, units)
    n_mid = units // (mid // small)
    n_e = n_big + n_mid + units % (mid // small)
    item_end = jnp.cumsum(n_e)
    item_start = item_end - n_e
    n_items = item_end[-1]
    j = jnp.minimum(jnp.arange(max_items, dtype=I32), n_items - 1)
    e = jnp.sum((item_end[None, :] <= j[:, None]).astype(I32), axis=1)
    k = j - _lookup(item_start, e)
    nb, nm = _lookup(n_big, e), _lookup(n_mid, e)
    cls = jnp.where(k < nb, 0, jnp.where(k < nb + nm, 1, 2))
    row = _lookup(starts, e) + jnp.where(cls == 0, k * big, nb * big + jnp.where(cls == 1, 0, nm * mid))
    return (e, row.astype(I32), cls.astype(I32), n_items.reshape(1).astype(I32)), starts


def _s5_discretise(lam_re, lam_im, log_dt, b_re, b_im, c_re, c_im):
    g, p = lam_re.shape
    ch = b_re.shape[-1]
    lam_re = lam_re.astype(F32)
    lam_im = lam_im.astype(F32)
    dt = jnp.exp(log_dt.astype(F32))[:, None]
    mag = jnp.exp(lam_re * dt)
    ab_re = mag * jnp.cos(lam_im * dt)
    ab_im = mag * jnp.sin(lam_im * dt)
    den = lam_re * lam_re + lam_im * lam_im
    nr = ab_re - 1.0
    coef_re = (nr * lam_re + ab_im * lam_im) / den
    coef_im = (ab_im * lam_re - nr * lam_im) / den
    bb_re = coef_re[..., None] * b_re - coef_im[..., None] * b_im
    bb_im = coef_re[..., None] * b_im + coef_im[..., None] * b_re
    gh = g // 2
    eye = jnp.eye(gh, dtype=F32)

    def in_block(m):
        return jnp.einsum("gpc,gh->gchp", m, eye).reshape(gh * ch, gh * p)

    def out_block(m):
        return jnp.einsum("gcp,gh->gphc", m, eye).reshape(gh * p, gh * ch)

    bb = jnp.stack([jnp.concatenate([in_block(bb_re[k * gh:(k + 1) * gh]), in_block(bb_im[k * gh:(k + 1) * gh])],
                                    axis=1) for k in range(2)]).astype(BF16)
    cc = jnp.stack([jnp.concatenate([out_block(c_re[k * gh:(k + 1) * gh]), out_block(-c_im[k * gh:(k + 1) * gh])],
                                    axis=0) for k in range(2)]).astype(BF16)
    return ab_re.reshape(1, g * p), ab_im.reshape(1, g * p), bb, cc


def kernel(x_prompt, x_sample, state_hgrn, state_s5_re, state_s5_im, norm_mix_g, w_in, hgrn_lb_raw, hgrn_onorm_g, w_branch_a, s5_lambda_re, s5_lambda_im, s5_log_dt, s5_b_re, s5_b_im, s5_c_re, s5_c_im, s5_d, w_glu, b_glu, w_out, norm_ffn_g, w_router_group, b_router_group, w_router_expert, b_router_expert, w_exp_gate, w_exp_up, w_exp_down, norm_final_g):
    depth = norm_mix_g.shape[0]
    bp, lp, d = x_prompt.shape
    bs, ls, _ = x_sample.shape
    heads, dk = state_hgrn.shape[2], state_hgrn.shape[3]
    kw = heads * dk
    s5_groups, s5_state = state_s5_re.shape[2], state_s5_re.shape[3]
    s5_width = s5_d.shape[-1]
    nstate = s5_groups * s5_state
    moe_groups, _, experts = w_router_expert.shape[1:]
    n_exp = moe_groups * experts
    rows_p, rows_s = bp * lp, bs * ls
    total = rows_p + rows_s
    n_sorted = total * MOE_TOP_K
    assert kw == d and state_hgrn.shape[4] == dk, "column blocks assume key width == value width == model width"
    assert d == SUBLANES * LANES, "token-tile layout holds one token per (8, 128) tile"
    assert s5_groups % 2 == 0 and bp % SUBLANES == 0 and bs % HGRN_SEQ_TILE == 0

    lb_all = jnp.cumsum(jax.nn.softmax(hgrn_lb_raw.astype(F32), axis=0), axis=0)

    hp = x_prompt.reshape(rows_p, d)
    hs = x_sample.reshape(rows_s, d)
    hg_p, re_p, im_p, hg_s, re_s, im_s = [], [], [], [], [], []
    zeros_state = jnp.zeros((bp // SUBLANES, SUBLANES, nstate), F32)

    for l in range(depth):
        w = w_in[l]
        w_cols = jnp.concatenate([w[:, :4 * kw], w[:, 4 * kw + s5_width:], w[:, 4 * kw:4 * kw + s5_width]],
                                 axis=1).astype(BF16)
        proj = _in_proj(hp, hs, norm_mix_g[l].reshape(1, d), w_cols)

        ar, ai, bb, cc = _s5_discretise(s5_lambda_re[l], s5_lambda_im[l], s5_log_dt[l], s5_b_re[l], s5_b_im[l],
                                        s5_c_re[l], s5_c_im[l])
        s5_args = (ar, ai, bb, cc, s5_d[l].reshape(1, s5_width), w_glu[l].astype(BF16), b_glu[l].reshape(1, -1))
        yb_p, fr_p, fi_p = _s5_branch(proj, 6 * kw, s5_width, bp, lp, 0, zeros_state, zeros_state, *s5_args)
        yb_s, fr_s, fi_s = _s5_branch(proj, 6 * kw, s5_width, bs, ls, rows_p,
                                      state_s5_re[l].reshape(bs // SUBLANES, SUBLANES, nstate),
                                      state_s5_im[l].reshape(bs // SUBLANES, SUBLANES, nstate), *s5_args)

        lb = lb_all[l].reshape(1, kw)
        gn = hgrn_onorm_g[l].reshape(1, kw)
        o_p, hgp = _hgrn_long(proj, lb, gn, bp, lp, heads, dk, 0)
        o_s, hgs = _hgrn_short(proj, lb, gn, state_hgrn[l].astype(F32), ls, rows_p)

        nr = -(-(moe_groups + n_exp) // SUBLANES) * SUBLANES
        wr = jnp.concatenate([w_router_group[l].T, w_router_expert[l].transpose(0, 2, 1).reshape(n_exp, d)], axis=0)
        wr = jnp.pad(wr, ((0, nr - wr.shape[0]), (0, 0))).astype(BF16)
        br = jnp.pad(jnp.concatenate([b_router_group[l], b_router_expert[l].reshape(n_exp)]),
                     (0, nr - moe_groups - n_exp)).reshape(nr, 1).astype(F32)
        h_all, xn_all, logits_t = _merge(
            o_p, o_s, yb_p.reshape(rows_p, d), yb_s.reshape(rows_s, d), proj, hp, hs,
            w_branch_a[l].astype(BF16), w_out[l].astype(BF16), norm_ffn_g[l].reshape(1, d), wr, br)

        ids, wts, ranks, cnt = _route(logits_t, moe_groups, experts)
        items, starts = _work_items(cnt[:, 0].astype(I32), n_sorted)
        pos = _lookup(starts, ids) + ranks
        pos3 = pos.T.reshape(-1)
        xs = _dispatch(pos3, xn_all)
        ys = _experts(items, xs, w_exp_gate[l], w_exp_up[l], w_exp_down[l])

        last = l == depth - 1
        g_out = norm_final_g.reshape(1, d)
        hp = _combine(pos3, h_all, wts.T, g_out, ys, rows_p, 0, last)
        hs = _combine(pos3, h_all, wts.T, g_out, ys, rows_s, rows_p, last)

        hg_p.append(hgp)
        hg_s.append(hgs)
        re_p.append(fr_p.reshape(bp, s5_groups, s5_state))
        im_p.append(fi_p.reshape(bp, s5_groups, s5_state))
        re_s.append(fr_s.reshape(bs, s5_groups, s5_state))
        im_s.append(fi_s.reshape(bs, s5_groups, s5_state))

    y_prompt = hp.reshape(bp, lp, d).astype(x_prompt.dtype)
    y_sample = hs.reshape(bs, ls, d).astype(x_sample.dtype)
    return (y_prompt, y_sample, jnp.stack(hg_p), jnp.stack(re_p), jnp.stack(im_p),
            jnp.stack(hg_s), jnp.stack(re_s), jnp.stack(im_s))
```

```python
import functools

import jax
import jax.numpy as jnp
from jax import lax
from jax.experimental import pallas as pl
from jax.experimental.pallas import tpu as pltpu

F32 = jnp.float32
BF16 = jnp.bfloat16
I32 = jnp.int32

RMS_EPS = 1e-6
HG_CHUNK = 64
MOE_TOP_K = 2

V7X_VMEM_BYTES = 64 * 1024 * 1024
VMEM_LIMIT_BYTES = V7X_VMEM_BYTES - 8 * 1024 * 1024
SUBLANES = 8
LANES = 128

TOKEN_TILE = 512
EXPERT_TILE = 512
EXPERT_WINDOWS = (EXPERT_TILE, EXPERT_TILE // 2, EXPERT_TILE // 4)
S5_TIME_TILE = 32
HGRN_TIME_TILE = 256
HGRN_SEQ_TILE = 8
PROJ_COL_TILE = 512


def _cparams(sem):
    return pltpu.CompilerParams(dimension_semantics=sem, vmem_limit_bytes=VMEM_LIMIT_BYTES)


def _resident(shape):
    nd = len(shape)
    return pl.BlockSpec(shape, lambda *_: (0,) * nd, pipeline_mode=pl.Buffered(1))


def _rmsnorm(x, g):
    return x * lax.rsqrt(jnp.mean(x * x, axis=-1, keepdims=True) + RMS_EPS) * g


def _two_source_specs(tm, width, n_first):
    return [pl.BlockSpec((tm, width), lambda i: (jnp.minimum(i, n_first - 1), 0)),
            pl.BlockSpec((tm, width), lambda i: (jnp.maximum(i - n_first, 0), 0))]


def _pick(first_ref, second_ref, n_first):
    return jnp.where(pl.program_id(0) < n_first, first_ref[...], second_ref[...])


def _store_token_tiles(ref, x, lead=()):
    rows = x.shape[0]
    for c in range(SUBLANES):
        ref[lead + (pl.ds(c, rows, stride=SUBLANES), slice(None))] = x[:, c * LANES:(c + 1) * LANES]


def _load_token_tiles(ref, rows, lead=()):
    return jnp.concatenate([ref[lead + (pl.ds(c, rows, stride=SUBLANES), slice(None))] for c in range(SUBLANES)],
                           axis=-1)


def _inproj_body(xp_ref, xs_ref, g_ref, w_ref, o_ref, *, n_first):
    xb = _rmsnorm(_pick(xp_ref, xs_ref, n_first), g_ref[...]).astype(BF16)
    for j in range(0, w_ref.shape[1], PROJ_COL_TILE):
        o_ref[:, j:j + PROJ_COL_TILE] = jnp.dot(xb, w_ref[:, j:j + PROJ_COL_TILE], preferred_element_type=F32)


def _in_proj(xp, xs, g, w):
    d = xp.shape[1]
    n = w.shape[1]
    tm = TOKEN_TILE
    total = xp.shape[0] + xs.shape[0]
    n_first = xp.shape[0] // tm
    return pl.pallas_call(
        functools.partial(_inproj_body, n_first=n_first), grid=(total // tm,),
        in_specs=_two_source_specs(tm, d, n_first) + [_resident((1, d)), _resident((d, n))],
        out_specs=pl.BlockSpec((tm, n), lambda i: (i, 0)),
        out_shape=jax.ShapeDtypeStruct((total, n), F32),
        compiler_params=_cparams(("parallel",)), name="in_proj")(xp, xs, g, w)


def _s5_body(*refs, tt, nstate):
    u_refs = refs[:SUBLANES]
    (h0r_ref, h0i_ref, ar_ref, ai_ref, bb_ref, cc_ref, d_ref, wg_ref, bg_ref,
     y_ref, hr_out, hi_out, hr_scr, hi_scr, bu_scr, u_scr, y_scr) = refs[SUBLANES:]
    j = pl.program_id(1)
    half = nstate // 2
    w = u_refs[0].shape[-1]
    kw = w // 2

    @pl.when(j == 0)
    def _():
        hr_scr[...] = h0r_ref[0]
        hi_scr[...] = h0i_ref[0]

    for b in range(SUBLANES):
        ub = u_refs[b][...]
        for s in range(w // LANES):
            u_scr[s, pl.ds(b, tt, stride=SUBLANES), :] = ub[:, s * LANES:(s + 1) * LANES]
    u = jnp.concatenate([u_scr[s] for s in range(w // LANES)], axis=-1)
    ub16 = u.astype(BF16)
    for kt in range(2):
        ukt = ub16[:, kt * kw:(kt + 1) * kw]
        bu_scr[:, kt * half:(kt + 1) * half] = jnp.dot(ukt, bb_ref[kt, :, :half], preferred_element_type=F32)
        bu_scr[:, nstate + kt * half:nstate + (kt + 1) * half] = jnp.dot(
            ukt, bb_ref[kt, :, half:], preferred_element_type=F32)

    lane_chunk = 512
    for lc in range(nstate // lane_chunk):
        lo = lc * lane_chunk
        re_sl = slice(lo, lo + lane_chunk)
        im_sl = slice(nstate + lo, nstate + lo + lane_chunk)
        ar = jnp.broadcast_to(ar_ref[:, re_sl], (SUBLANES, lane_chunk))
        ai = jnp.broadcast_to(ai_ref[:, re_sl], (SUBLANES, lane_chunk))

        def step(t, carry, re_sl=re_sl, im_sl=im_sl, ar=ar, ai=ai):
            hr, hi = carry
            rs = pl.ds(pl.multiple_of(t * SUBLANES, SUBLANES), SUBLANES)
            nhr = ar * hr - ai * hi + bu_scr[rs, re_sl]
            nhi = ar * hi + ai * hr + bu_scr[rs, im_sl]
            bu_scr[rs, re_sl] = nhr
            bu_scr[rs, im_sl] = nhi
            return nhr, nhi

        hr, hi = lax.fori_loop(0, tt, step, (hr_scr[:, re_sl], hi_scr[:, re_sl]), unroll=4)
        hr_scr[:, re_sl] = hr
        hi_scr[:, re_sl] = hi

    ys = []
    for n in range(2):
        h_re = bu_scr[:, n * half:(n + 1) * half].astype(BF16)
        h_im = bu_scr[:, nstate + n * half:nstate + (n + 1) * half].astype(BF16)
        ys.append(jnp.dot(h_re, cc_ref[n, :half, :], preferred_element_type=F32)
                  + jnp.dot(h_im, cc_ref[n, half:, :], preferred_element_type=F32))
    y = jnp.concatenate(ys, axis=-1) + d_ref[...] * u
    z = jnp.dot(jax.nn.gelu(y).astype(BF16), wg_ref[...], preferred_element_type=F32) + bg_ref[...]
    dm = z.shape[-1] // 2
    yb = z[:, :dm] * jax.nn.sigmoid(z[:, dm:])
    for s in range(dm // LANES):
        y_scr[s] = yb[:, s * LANES:(s + 1) * LANES]
    for b in range(SUBLANES):
        for s in range(dm // LANES):
            y_ref[b, :, s * LANES:(s + 1) * LANES] = y_scr[s, pl.ds(b, tt, stride=SUBLANES), :]

    @pl.when(j == pl.num_programs(1) - 1)
    def _():
        hr_out[0] = hr_scr[...]
        hi_out[0] = hi_scr[...]


def _s5_branch(proj, u_col, width, batch, seq, row_off, h0r, h0i, ar, ai, bb, cc, d_skip, w_glu, b_glu):
    nstate = ar.shape[-1]
    dm = w_glu.shape[1] // 2
    tt = min(S5_TIME_TILE, seq)
    nj = seq // tt
    nbb = batch // SUBLANES
    body = functools.partial(_s5_body, tt=tt, nstate=nstate)

    def u_spec(b):
        return pl.BlockSpec((tt, width), lambda bb_, j, b=b: (row_off // tt + (bb_ * SUBLANES + b) * nj + j,
                                                               u_col // width))

    state_spec = pl.BlockSpec((1, SUBLANES, nstate), lambda bb_, j: (bb_, 0, 0))
    return pl.pallas_call(
        body, grid=(nbb, nj),
        in_specs=[u_spec(b) for b in range(SUBLANES)] + [
            state_spec, state_spec, _resident(ar.shape), _resident(ai.shape), _resident(bb.shape),
            _resident(cc.shape), _resident(d_skip.shape), _resident(w_glu.shape), _resident(b_glu.shape)],
        out_specs=[pl.BlockSpec((SUBLANES, tt, dm), lambda bb_, j: (bb_, j, 0)), state_spec, state_spec],
        out_shape=[jax.ShapeDtypeStruct((batch, seq, dm), F32),
                   jax.ShapeDtypeStruct((nbb, SUBLANES, nstate), F32),
                   jax.ShapeDtypeStruct((nbb, SUBLANES, nstate), F32)],
        scratch_shapes=[pltpu.VMEM((SUBLANES, nstate), F32), pltpu.VMEM((SUBLANES, nstate), F32),
                        pltpu.VMEM((tt * SUBLANES, 2 * nstate), F32),
                        pltpu.VMEM((width // LANES, tt * SUBLANES, LANES), F32),
                        pltpu.VMEM((dm // LANES, tt * SUBLANES, LANES), F32)],
        compiler_params=_cparams(("parallel", "arbitrary")), name="s5_branch")(
            *([proj] * SUBLANES), h0r, h0i, ar, ai, bb, cc, d_skip, w_glu, b_glu)


def _cumsum_rows(x):
    c = x.shape[0]
    row = lax.broadcasted_iota(I32, x.shape, 0)
    s = 1
    while s < c:
        x = x + jnp.where(row >= s, pltpu.roll(x, s, axis=0), 0.0)
        s *= 2
    return x


def _hgrn_gates(q, fr, lb, scale):
    f = lb + (1.0 - lb) * jax.nn.sigmoid(fr)
    k = 1.0 - f
    b = _cumsum_rows(jnp.log(f))
    b_last = b[-1:, :]
    q_dec = (q * scale) * jnp.exp(b)
    k_dec = k * jnp.exp(-b)
    k_end = k * jnp.exp(b_last - b)
    return q_dec.astype(BF16), k_dec.astype(BF16), k_end.astype(BF16), b_last


def _causal_scores(q_dec, k_dec):
    c = q_dec.shape[0]
    s = lax.dot_general(q_dec, k_dec, (((1,), (1,)), ((), ())), preferred_element_type=F32)
    keep = lax.broadcasted_iota(I32, (c, c), 0) >= lax.broadcasted_iota(I32, (c, c), 1)
    return jnp.where(keep, s, 0.0).astype(BF16)


def _gated_out(o, gn, og):
    o = o * lax.rsqrt(jnp.mean(o * o, axis=-1, keepdims=True) + RMS_EPS) * gn
    return (o * jax.nn.silu(og)).astype(BF16)


def _hgrn_long_body(q_ref, f_ref, v_ref, og_ref, lb_ref, gn_ref, o_ref, sfin_ref, st_scr, *, c, heads, dk, scale):
    j = pl.program_id(1)

    @pl.when(j == 0)
    def _():
        st_scr[...] = jnp.zeros_like(st_scr)

    def chunk(ci, carry):
        rs = pl.ds(pl.multiple_of(ci * c, c), c)
        for h in range(heads):
            hs = slice(h * dk, (h + 1) * dk)
            q_dec, k_dec, k_end, b_last = _hgrn_gates(q_ref[rs, hs], f_ref[rs, hs], lb_ref[:, hs], scale)
            v = v_ref[rs, hs].astype(BF16)
            scores = _causal_scores(q_dec, k_dec)
            st = st_scr[h]
            o = (lax.dot_general(q_dec, st.astype(BF16), (((1,), (1,)), ((), ())), preferred_element_type=F32)
                 + jnp.dot(scores, v, preferred_element_type=F32))
            st_scr[h] = jnp.exp(b_last) * st + lax.dot_general(
                v, k_end, (((0,), (0,)), ((), ())), preferred_element_type=F32)
            o_ref[rs, hs] = _gated_out(o, gn_ref[:, hs], og_ref[rs, hs])
        return carry

    lax.fori_loop(0, q_ref.shape[0] // c, chunk, 0)

    @pl.when(j == pl.num_programs(1) - 1)
    def _():
        for h in range(heads):
            sfin_ref[0, h] = st_scr[h].T


def _hgrn_long(proj, lb, gn, batch, seq, heads, dk, row_off):
    width = heads * dk
    tb = min(HGRN_TIME_TILE, seq)
    nj = seq // tb
    off = row_off // tb
    body = functools.partial(_hgrn_long_body, c=min(HG_CHUNK, seq), heads=heads, dk=dk, scale=dk ** -0.5)

    def col(k):
        return pl.BlockSpec((tb, width), lambda b, j, k=k: (off + b * nj + j, k))

    return pl.pallas_call(
        body, grid=(batch, nj),
        in_specs=[col(0), col(1), col(2), col(3), _resident(lb.shape), _resident(gn.shape)],
        out_specs=[pl.BlockSpec((tb, width), lambda b, j: (b * nj + j, 0)),
                   pl.BlockSpec((1, heads, dk, dk), lambda b, j: (b, 0, 0, 0))],
        out_shape=[jax.ShapeDtypeStruct((batch * seq, width), BF16),
                   jax.ShapeDtypeStruct((batch, heads, dk, dk), F32)],
        scratch_shapes=[pltpu.VMEM((heads, dk, dk), F32)],
        compiler_params=_cparams(("parallel", "arbitrary")), name="hgrn_long")(proj, proj, proj, proj, lb, gn)


def _hgrn_short_body(q_ref, f_ref, v_ref, og_ref, lb_ref, gn_ref, s0_ref, o_ref, snew_ref, *, c, heads, dk, scale):
    def one_seq(sq, carry):
        rs = pl.ds(pl.multiple_of(sq * c, c), c)
        for h in range(heads):
            hs = slice(h * dk, (h + 1) * dk)
            q_dec, k_dec, k_end, b_last = _hgrn_gates(q_ref[rs, hs], f_ref[rs, hs], lb_ref[:, hs], scale)
            v = v_ref[rs, hs].astype(BF16)
            scores = _causal_scores(q_dec, k_dec)
            s0 = s0_ref[sq, h]
            o = (jnp.dot(q_dec, s0.astype(BF16), preferred_element_type=F32)
                 + jnp.dot(scores, v, preferred_element_type=F32))
            decay_col = jnp.broadcast_to(jnp.exp(b_last), (dk, dk)).T
            snew_ref[sq, h] = decay_col * s0 + lax.dot_general(
                k_end, v, (((0,), (0,)), ((), ())), preferred_element_type=F32)
            o_ref[rs, hs] = _gated_out(o, gn_ref[:, hs], og_ref[rs, hs])
        return carry

    lax.fori_loop(0, s0_ref.shape[0], one_seq, 0)


def _hgrn_short(proj, lb, gn, s0, seq, row_off):
    batch, heads, dk, _ = s0.shape
    width = heads * dk
    nb = HGRN_SEQ_TILE
    rows = nb * seq
    off = row_off // rows
    body = functools.partial(_hgrn_short_body, c=seq, heads=heads, dk=dk, scale=dk ** -0.5)

    def col(k):
        return pl.BlockSpec((rows, width), lambda i, k=k: (off + i, k))

    state_spec = pl.BlockSpec((nb, heads, dk, dk), lambda i: (i, 0, 0, 0))
    return pl.pallas_call(
        body, grid=(batch // nb,),
        in_specs=[col(0), col(1), col(2), col(3), _resident(lb.shape), _resident(gn.shape), state_spec],
        out_specs=[pl.BlockSpec((rows, width), lambda i: (i, 0)), state_spec],
        out_shape=[jax.ShapeDtypeStruct((batch * seq, width), BF16), jax.ShapeDtypeStruct(s0.shape, F32)],
        compiler_params=_cparams(("parallel",)), name="hgrn_short")(proj, proj, proj, proj, lb, gn, s0)


def _merge_body(op_ref, os_ref, ybp_ref, ybs_ref, ga_ref, gb_ref, xp_ref, xs_ref, wa_ref, wo_ref, gf_ref, wr_ref,
                br_ref, h_ref, xn_ref, lg_ref, *, n_first):
    y_a = jnp.dot(_pick(op_ref, os_ref, n_first), wa_ref[...], preferred_element_type=F32)
    merged = jax.nn.sigmoid(ga_ref[...]) * y_a + jax.nn.sigmoid(gb_ref[...]) * _pick(ybp_ref, ybs_ref, n_first)
    h = _pick(xp_ref, xs_ref, n_first) + jnp.dot(merged.astype(BF16), wo_ref[...], preferred_element_type=F32)
    h_ref[...] = h
    xn = _rmsnorm(h, gf_ref[...])
    _store_token_tiles(xn_ref, xn)
    lg_ref[...] = lax.dot_general(wr_ref[...], xn.astype(BF16), (((1,), (1,)), ((), ())),
                                  preferred_element_type=F32) + br_ref[...]


def _merge(o_p, o_s, yb_p, yb_s, proj, xp, xs, wa, wo, gf, wr, br):
    d = xp.shape[1]
    total = proj.shape[0]
    nr = wr.shape[0]
    tm = TOKEN_TILE
    n_first = xp.shape[0] // tm
    pair = _two_source_specs(tm, d, n_first)

    def row(k=0):
        return pl.BlockSpec((tm, d), lambda i, k=k: (i, k))

    return pl.pallas_call(
        functools.partial(_merge_body, n_first=n_first), grid=(total // tm,),
        in_specs=pair + pair + [row(4), row(5)] + pair + [
            _resident(wa.shape), _resident(wo.shape), _resident(gf.shape), _resident(wr.shape), _resident(br.shape)],
        out_specs=[row(), pl.BlockSpec((tm * SUBLANES, LANES), lambda i: (i, 0)),
                   pl.BlockSpec((nr, tm), lambda i: (0, i))],
        out_shape=[jax.ShapeDtypeStruct((total, d), F32), jax.ShapeDtypeStruct((total * SUBLANES, LANES), F32),
                   jax.ShapeDtypeStruct((nr, total), F32)],
        compiler_params=_cparams(("parallel",)), name="merge_out")(
            o_p, o_s, yb_p, yb_s, proj, proj, xp, xs, wa, wo, gf, wr, br)


def _first_index_of_max(vals):
    m = vals[0]
    for v in vals[1:]:
        m = jnp.maximum(m, v)
    idx = jnp.full(m.shape, len(vals), I32)
    for e in range(len(vals) - 1, -1, -1):
        idx = jnp.where(vals[e] == m, e, idx)
    return m, idx


def _route_body(lg_ref, ids_ref, w_ref, rk_ref, cnt_ref, carry_scr, *, groups, experts):
    i = pl.program_id(0)
    tile = lg_ref.shape[1]
    n_exp = groups * experts

    @pl.when(i == 0)
    def _():
        carry_scr[...] = jnp.zeros_like(carry_scr)

    gl = [lg_ref[g:g + 1, :] for g in range(groups)]
    gmax, gidx = _first_index_of_max(gl)
    denom = jnp.exp(gl[0] - gmax)
    for g in range(1, groups):
        denom = denom + jnp.exp(gl[g] - gmax)
    g_w = 1.0 / denom

    el = []
    for e in range(experts):
        v = lg_ref[groups + e:groups + e + 1, :]
        for g in range(1, groups):
            r = groups + g * experts + e
            v = jnp.where(gidx == g, lg_ref[r:r + 1, :], v)
        el.append(v)
    v1, i1 = _first_index_of_max(el)
    rest = [jnp.where(i1 == e, -jnp.inf, el[e]) for e in range(experts)]
    v2, i2 = _first_index_of_max(rest)
    t = jnp.exp(v2 - v1)
    inv = 1.0 / (1.0 + t)
    e1 = gidx * experts + i1
    e2 = gidx * experts + i2

    erow = lax.broadcasted_iota(I32, (n_exp, tile), 0)
    oh1 = (erow == e1).astype(F32)
    oh2 = (erow == e2).astype(F32)
    oh = oh1 + oh2
    before = (lax.broadcasted_iota(I32, (tile, tile), 0) < lax.broadcasted_iota(I32, (tile, tile), 1))
    cnt = jnp.dot(oh.astype(BF16), before.astype(BF16), preferred_element_type=F32) + carry_scr[:, 0:1]
    ids_ref[0:1, :] = e1
    ids_ref[1:2, :] = e2
    w_ref[0:1, :] = inv * g_w
    w_ref[1:2, :] = (t * inv) * g_w
    rk_ref[0:1, :] = jnp.sum(oh1 * cnt, axis=0, keepdims=True).astype(I32)
    rk_ref[1:2, :] = jnp.sum(oh2 * cnt, axis=0, keepdims=True).astype(I32)
    carry_scr[...] = carry_scr[...] + jnp.sum(oh, axis=1, keepdims=True)

    @pl.when(i == pl.num_programs(0) - 1)
    def _():
        cnt_ref[...] = carry_scr[...]


def _route(logits_t, groups, experts):
    nr, total = logits_t.shape
    tile = TOKEN_TILE
    n_exp = groups * experts
    body = functools.partial(_route_body, groups=groups, experts=experts)
    pair = pl.BlockSpec((MOE_TOP_K, tile), lambda i: (0, i))
    return pl.pallas_call(
        body, grid=(total // tile,),
        in_specs=[pl.BlockSpec((nr, tile), lambda i: (0, i))],
        out_specs=[pair, pair, pair, pl.BlockSpec((n_exp, LANES), lambda i: (0, 0))],
        out_shape=[jax.ShapeDtypeStruct((MOE_TOP_K, total), I32), jax.ShapeDtypeStruct((MOE_TOP_K, total), F32),
                   jax.ShapeDtypeStruct((MOE_TOP_K, total), I32), jax.ShapeDtypeStruct((n_exp, LANES), F32)],
        scratch_shapes=[pltpu.VMEM((n_exp, LANES), F32)],
        compiler_params=_cparams(("arbitrary",)), name="route")(logits_t)


def _row_copy(src, dst, sem):
    return pltpu.make_async_copy(src, dst, sem)


def _token_rows(r):
    return pl.ds(pl.multiple_of(r * SUBLANES, SUBLANES), SUBLANES)


def _dispatch_body(pos_ref, x_ref, o_hbm, ring, zero_scr, sem, pad_sem, *, n_sorted):
    i = pl.program_id(0)
    tile = x_ref.shape[0] // SUBLANES
    par = lax.rem(i, 2)

    @pl.when(i == 0)
    def _():
        zero_scr[...] = jnp.zeros_like(zero_scr)
        pad = _row_copy(zero_scr, o_hbm.at[pl.ds(n_sorted * SUBLANES, zero_scr.shape[0])], pad_sem.at[0])
        pad.start()
        pad.wait()

    ring[par] = x_ref[...]

    def issue(r, carry):
        for k in range(MOE_TOP_K):
            p = pos_ref[MOE_TOP_K * r + k]
            _row_copy(ring.at[par, _token_rows(r)], o_hbm.at[_token_rows(p)], sem.at[par, k]).start(priority=k)
        return carry

    lax.fori_loop(0, tile, issue, 0, unroll=8)

    def drain(slot):
        for k in range(MOE_TOP_K):
            _row_copy(ring.at[slot], o_hbm.at[pl.ds(0, tile * SUBLANES)], sem.at[slot, k]).wait()

    @pl.when(i > 0)
    def _():
        drain(1 - par)

    @pl.when(i == pl.num_programs(0) - 1)
    def _():
        drain(par)


def _dispatch(pos3, xn_tiles):
    total = xn_tiles.shape[0] // SUBLANES
    tile = TOKEN_TILE
    n_sorted = total * MOE_TOP_K
    pad = EXPERT_WINDOWS[-1]
    return pl.pallas_call(
        functools.partial(_dispatch_body, n_sorted=n_sorted), grid=(total // tile,),
        in_specs=[pl.BlockSpec((MOE_TOP_K * tile,), lambda i: (i,), memory_space=pltpu.SMEM),
                  pl.BlockSpec((tile * SUBLANES, LANES), lambda i: (i, 0))],
        out_specs=pl.BlockSpec(memory_space=pl.ANY),
        out_shape=jax.ShapeDtypeStruct(((n_sorted + pad) * SUBLANES, LANES), F32),
        scratch_shapes=[pltpu.VMEM((2, tile * SUBLANES, LANES), F32), pltpu.VMEM((pad * SUBLANES, LANES), F32),
                        pltpu.SemaphoreType.DMA((2, MOE_TOP_K)), pltpu.SemaphoreType.DMA((1,))],
        compiler_params=_cparams(("arbitrary",)), name="dispatch")(pos3, xn_tiles)


def _experts_body(it_exp, it_row, it_cls, it_first, it_next, n_items, xs_hbm, wg_hbm, wu_hbm, wd_hbm, ys_hbm,
                  xbuf, ybuf, wg_s, wu_s, wd_s, wg_b, wu_b, wd_b, sem_in, sem_out, sem_w):
    j = pl.program_id(0)
    n = n_items[0]
    pad = EXPERT_WINDOWS[-1]
    slot = lax.rem(j, 2)

    def weight_copies(e, s):
        return [pltpu.make_async_copy(hbm.at[e], stage.at[s], sem_w.at[s, t])
                for t, (hbm, stage) in enumerate(((wg_hbm, wg_s), (wu_hbm, wu_s), (wd_hbm, wd_s)))]

    def by_size(item, fn):
        for ci, m in enumerate(EXPERT_WINDOWS):
            pl.when(it_cls[item] == ci)(functools.partial(fn, m))

    def window(item, m):
        return pl.ds(pl.multiple_of(it_row[item] * SUBLANES, SUBLANES), m * SUBLANES)

    def in_copy(item, s, m):
        return pltpu.make_async_copy(xs_hbm.at[window(item, m)], xbuf.at[s, pl.ds(0, m * SUBLANES)], sem_in.at[s])

    def out_copy(item, s, m):
        return pltpu.make_async_copy(ybuf.at[s, pl.ds(0, m * SUBLANES)], ys_hbm.at[window(item, m)], sem_out.at[s])

    def compute(m):
        x = _load_token_tiles(xbuf, m, (slot,)).astype(BF16)
        hg = jnp.dot(x, wg_b[...], preferred_element_type=F32)
        hu = jnp.dot(x, wu_b[...], preferred_element_type=F32)
        hid = (jax.nn.silu(hg) * hu).astype(BF16)
        _store_token_tiles(ybuf, jnp.dot(hid, wd_b[...], preferred_element_type=F32), (slot,))

    @pl.when(j < n)
    def _():
        @pl.when(j == 0)
        def _():
            by_size(0, lambda m: in_copy(0, 0, m).start())
            for c in weight_copies(it_exp[0], it_first[0] - 1):
                c.start()
            tail_rows = pl.ds(0, pad * SUBLANES)
            ybuf[1, tail_rows, :] = jnp.zeros((pad * SUBLANES, LANES), F32)
            tail = pltpu.make_async_copy(
                ybuf.at[1, tail_rows], ys_hbm.at[pl.ds(ys_hbm.shape[0] - pad * SUBLANES, pad * SUBLANES)],
                sem_out.at[1])
            tail.start()
            tail.wait()

        @pl.when(j + 1 < n)
        def _():
            by_size(j + 1, lambda m: in_copy(j + 1, 1 - slot, m).start())

        @pl.when(it_first[j] > 0)
        def _():
            s = it_first[j] - 1
            for c in weight_copies(it_exp[j], s):
                c.wait()
            wg_b[...] = wg_s[s].astype(BF16)
            wu_b[...] = wu_s[s].astype(BF16)
            wd_b[...] = wd_s[s].astype(BF16)

            @pl.when(it_next[j] >= 0)
            def _():
                for c in weight_copies(it_next[j], 1 - s):
                    c.start()

        by_size(j, lambda m: in_copy(j, slot, m).wait())
        by_size(j, compute)

        @pl.when(j > 0)
        def _():
            by_size(j - 1, lambda m: out_copy(j - 1, 1 - slot, m).wait())

        by_size(j, lambda m: out_copy(j, slot, m).start())

        @pl.when(j == n - 1)
        def _():
            by_size(j, lambda m: out_copy(j, slot, m).wait())


def _experts(items, xs, wg, wu, wd):
    d, de = wg.shape[1], wg.shape[2]
    tm = EXPERT_TILE
    max_items = items[0].shape[0]
    grid_spec = pltpu.PrefetchScalarGridSpec(
        num_scalar_prefetch=6, grid=(max_items,),
        in_specs=[pl.BlockSpec(memory_space=pl.ANY)] * 4,
        out_specs=pl.BlockSpec(memory_space=pl.ANY),
        scratch_shapes=[pltpu.VMEM((2, tm * SUBLANES, LANES), F32), pltpu.VMEM((2, tm * SUBLANES, LANES), F32),
                        pltpu.VMEM((2, d, de), F32), pltpu.VMEM((2, d, de), F32), pltpu.VMEM((2, de, d), F32),
                        pltpu.VMEM((d, de), BF16), pltpu.VMEM((d, de), BF16), pltpu.VMEM((de, d), BF16),
                        pltpu.SemaphoreType.DMA((2,)), pltpu.SemaphoreType.DMA((2,)),
                        pltpu.SemaphoreType.DMA((2, 3))])
    return pl.pallas_call(
        _experts_body, grid_spec=grid_spec, out_shape=jax.ShapeDtypeStruct(xs.shape, F32),
        compiler_params=_cparams(("arbitrary",)), name="experts")(*items, xs, wg, wu, wd)


def _combine_body(pos_ref, pos_next_ref, h_ref, w_ref, g_ref, ys_hbm, y_ref, buf, sem, *, final_norm):
    i = pl.program_id(0)
    tile = h_ref.shape[0]
    par = lax.rem(i, 2)

    def gather(table, slot):
        def issue(r, carry):
            for k in range(MOE_TOP_K):
                p = table[MOE_TOP_K * r + k]
                _row_copy(ys_hbm.at[_token_rows(p)], buf.at[slot, k, _token_rows(r)],
                          sem.at[slot, k]).start(priority=k)
            return carry

        lax.fori_loop(0, tile, issue, 0, unroll=8)

    @pl.when(i == 0)
    def _():
        gather(pos_ref, 0)

    @pl.when(i + 1 < pl.num_programs(0))
    def _():
        gather(pos_next_ref, 1 - par)

    for k in range(MOE_TOP_K):
        _row_copy(ys_hbm.at[pl.ds(0, tile * SUBLANES)], buf.at[par, k], sem.at[par, k]).wait()
    h = h_ref[...] + (w_ref[:, 0:1] * _load_token_tiles(buf, tile, (par, 0))
                      + w_ref[:, 1:2] * _load_token_tiles(buf, tile, (par, 1)))
    y_ref[...] = _rmsnorm(h, g_ref[...]) if final_norm else h


def _combine(pos3, h_all, w_t, g, ys, rows, row_off, final_norm):
    d = h_all.shape[1]
    tile = TOKEN_TILE
    off = row_off // tile
    last_block = h_all.shape[0] // tile - 1
    return pl.pallas_call(
        functools.partial(_combine_body, final_norm=final_norm), grid=(rows // tile,),
        in_specs=[pl.BlockSpec((MOE_TOP_K * tile,), lambda i: (off + i,), memory_space=pltpu.SMEM),
                  pl.BlockSpec((MOE_TOP_K * tile,), lambda i: (jnp.minimum(off + i + 1, last_block),),
                               memory_space=pltpu.SMEM),
                  pl.BlockSpec((tile, d), lambda i: (off + i, 0)),
                  pl.BlockSpec((tile, MOE_TOP_K), lambda i: (off + i, 0)),
                  _resident(g.shape),
                  pl.BlockSpec(memory_space=pl.ANY)],
        out_specs=pl.BlockSpec((tile, d), lambda i: (i, 0)),
        out_shape=jax.ShapeDtypeStruct((rows, d), F32),
        scratch_shapes=[pltpu.VMEM((2, MOE_TOP_K, tile * SUBLANES, LANES), F32),
                        pltpu.SemaphoreType.DMA((2, MOE_TOP_K))],
        compiler_params=_cparams(("arbitrary",)), name="combine")(pos3, pos3, h_all, w_t, g, ys)


def _lookup(table, idx):
    sel = idx[None] == jnp.arange(table.shape[0], dtype=I32).reshape((-1,) + (1,) * idx.ndim)
    return jnp.sum(jnp.where(sel, table.reshape(sel.shape[:1] + (1,) * idx.ndim), 0), axis=0)


def _work_items(counts, n_sorted):
    big, mid, small = EXPERT_WINDOWS
    n_exp = counts.shape[0]
    max_items = n_sorted // big + 2 * n_exp
    ends = jnp.cumsum(counts)
    starts = ends - counts
    units = (counts % big + small - 1) // small
    n_big = counts // big + (units == big // small)
    units = jnp.where(units == big // small, 0, units)
    n_mid = units // (mid // small)
    n_e = n_big + n_mid + units % (mid // small)
    item_end = jnp.cumsum(n_e)
    item_start = item_end - n_e
    n_items = item_end[-1]
    j = jnp.minimum(jnp.arange(max_items, dtype=I32), n_items - 1)
    e = jnp.sum((item_end[None, :] <= j[:, None]).astype(I32), axis=1)
    k = j - _lookup(item_start, e)
    nb, nm = _lookup(n_big, e), _lookup(n_mid, e)
    cls = jnp.where(k < nb, 0, jnp.where(k < nb + nm, 1, 2))
    row = _lookup(starts, e) + jnp.where(cls == 0, k * big, nb * big + jnp.where(cls == 1, 0, nm * mid))
    ordinal = jnp.cumsum((n_e > 0).astype(I32)) - 1
    first = jnp.where(k == 0, 1 + _lookup(ordinal, e) % 2, 0)
    nxt_item = _lookup(item_end, e)
    nxt = jnp.where(nxt_item < n_items, jnp.sum((item_end[None, :] <= nxt_item[:, None]).astype(I32), axis=1), -1)
    return (e, row.astype(I32), cls.astype(I32), first.astype(I32), nxt.astype(I32),
            n_items.reshape(1).astype(I32)), starts


def _s5_discretise(lam_re, lam_im, log_dt, b_re, b_im, c_re, c_im):
    g, p = lam_re.shape
    ch = b_re.shape[-1]
    lam_re = lam_re.astype(F32)
    lam_im = lam_im.astype(F32)
    dt = jnp.exp(log_dt.astype(F32))[:, None]
    mag = jnp.exp(lam_re * dt)
    ab_re = mag * jnp.cos(lam_im * dt)
    ab_im = mag * jnp.sin(lam_im * dt)
    den = lam_re * lam_re + lam_im * lam_im
    nr = ab_re - 1.0
    coef_re = (nr * lam_re + ab_im * lam_im) / den
    coef_im = (ab_im * lam_re - nr * lam_im) / den
    bb_re = coef_re[..., None] * b_re - coef_im[..., None] * b_im
    bb_im = coef_re[..., None] * b_im + coef_im[..., None] * b_re
    gh = g // 2
    eye = jnp.eye(gh, dtype=F32)

    def in_block(m):
        return jnp.einsum("gpc,gh->gchp", m, eye).reshape(gh * ch, gh * p)

    def out_block(m):
        return jnp.einsum("gcp,gh->gphc", m, eye).reshape(gh * p, gh * ch)

    bb = jnp.stack([jnp.concatenate([in_block(bb_re[k * gh:(k + 1) * gh]), in_block(bb_im[k * gh:(k + 1) * gh])],
                                    axis=1) for k in range(2)]).astype(BF16)
    cc = jnp.stack([jnp.concatenate([out_block(c_re[k * gh:(k + 1) * gh]), out_block(-c_im[k * gh:(k + 1) * gh])],
                                    axis=0) for k in range(2)]).astype(BF16)
    return ab_re.reshape(1, g * p), ab_im.reshape(1, g * p), bb, cc


def kernel(x_prompt, x_sample, state_hgrn, state_s5_re, state_s5_im, norm_mix_g, w_in, hgrn_lb_raw, hgrn_onorm_g, w_branch_a, s5_lambda_re, s5_lambda_im, s5_log_dt, s5_b_re, s5_b_im, s5_c_re, s5_c_im, s5_d, w_glu, b_glu, w_out, norm_ffn_g, w_router_group, b_router_group, w_router_expert, b_router_expert, w_exp_gate, w_exp_up, w_exp_down, norm_final_g):
    depth = norm_mix_g.shape[0]
    bp, lp, d = x_prompt.shape
    bs, ls, _ = x_sample.shape
    heads, dk = state_hgrn.shape[2], state_hgrn.shape[3]
    kw = heads * dk
    s5_groups, s5_state = state_s5_re.shape[2], state_s5_re.shape[3]
    s5_width = s5_d.shape[-1]
    nstate = s5_groups * s5_state
    moe_groups, _, experts = w_router_expert.shape[1:]
    n_exp = moe_groups * experts
    rows_p, rows_s = bp * lp, bs * ls
    total = rows_p + rows_s
    n_sorted = total * MOE_TOP_K
    assert kw == d and state_hgrn.shape[4] == dk, "column blocks assume key width == value width == model width"
    assert d == SUBLANES * LANES, "token-tile layout holds one token per (8, 128) tile"
    assert s5_groups % 2 == 0 and bp % SUBLANES == 0 and bs % HGRN_SEQ_TILE == 0

    lb_all = jnp.cumsum(jax.nn.softmax(hgrn_lb_raw.astype(F32), axis=0), axis=0)

    hp = x_prompt.reshape(rows_p, d)
    hs = x_sample.reshape(rows_s, d)
    hg_p, re_p, im_p, hg_s, re_s, im_s = [], [], [], [], [], []
    zeros_state = jnp.zeros((bp // SUBLANES, SUBLANES, nstate), F32)

    for l in range(depth):
        w = w_in[l]
        w_cols = jnp.concatenate([w[:, :4 * kw], w[:, 4 * kw + s5_width:], w[:, 4 * kw:4 * kw + s5_width]],
                                 axis=1).astype(BF16)
        proj = _in_proj(hp, hs, norm_mix_g[l].reshape(1, d), w_cols)

        ar, ai, bb, cc = _s5_discretise(s5_lambda_re[l], s5_lambda_im[l], s5_log_dt[l], s5_b_re[l], s5_b_im[l],
                                        s5_c_re[l], s5_c_im[l])
        s5_args = (ar, ai, bb, cc, s5_d[l].reshape(1, s5_width), w_glu[l].astype(BF16), b_glu[l].reshape(1, -1))
        yb_p, fr_p, fi_p = _s5_branch(proj, 6 * kw, s5_width, bp, lp, 0, zeros_state, zeros_state, *s5_args)
        yb_s, fr_s, fi_s = _s5_branch(proj, 6 * kw, s5_width, bs, ls, rows_p,
                                      state_s5_re[l].reshape(bs // SUBLANES, SUBLANES, nstate),
                                      state_s5_im[l].reshape(bs // SUBLANES, SUBLANES, nstate), *s5_args)

        lb = lb_all[l].reshape(1, kw)
        gn = hgrn_onorm_g[l].reshape(1, kw)
        o_p, hgp = _hgrn_long(proj, lb, gn, bp, lp, heads, dk, 0)
        o_s, hgs = _hgrn_short(proj, lb, gn, state_hgrn[l].astype(F32), ls, rows_p)

        nr = -(-(moe_groups + n_exp) // SUBLANES) * SUBLANES
        wr = jnp.concatenate([w_router_group[l].T, w_router_expert[l].transpose(0, 2, 1).reshape(n_exp, d)], axis=0)
        wr = jnp.pad(wr, ((0, nr - wr.shape[0]), (0, 0))).astype(BF16)
        br = jnp.pad(jnp.concatenate([b_router_group[l], b_router_expert[l].reshape(n_exp)]),
                     (0, nr - moe_groups - n_exp)).reshape(nr, 1).astype(F32)
        h_all, xn_all, logits_t = _merge(
            o_p, o_s, yb_p.reshape(rows_p, d), yb_s.reshape(rows_s, d), proj, hp, hs,
            w_branch_a[l].astype(BF16), w_out[l].astype(BF16), norm_ffn_g[l].reshape(1, d), wr, br)

        ids, wts, ranks, cnt = _route(logits_t, moe_groups, experts)
        items, starts = _work_items(cnt[:, 0].astype(I32), n_sorted)
        pos = _lookup(starts, ids) + ranks
        pos3 = pos.T.reshape(-1)
        xs = _dispatch(pos3, xn_all)
        ys = _experts(items, xs, w_exp_gate[l], w_exp_up[l], w_exp_down[l])

        last = l == depth - 1
        g_out = norm_final_g.reshape(1, d)
        hp = _combine(pos3, h_all, wts.T, g_out, ys, rows_p, 0, last)
        hs = _combine(pos3, h_all, wts.T, g_out, ys, rows_s, rows_p, last)

        hg_p.append(hgp)
        hg_s.append(hgs)
        re_p.append(fr_p.reshape(bp, s5_groups, s5_state))
        im_p.append(fi_p.reshape(bp, s5_groups, s5_state))
        re_s.append(fr_s.reshape(bs, s5_groups, s5_state))
        im_s.append(fi_s.reshape(bs, s5_groups, s5_state))

    y_prompt = hp.reshape(bp, lp, d).astype(x_prompt.dtype)
    y_sample = hs.reshape(bs, ls, d).astype(x_sample.dtype)
    return (y_prompt, y_sample, jnp.stack(hg_p), jnp.stack(re_p), jnp.stack(im_p),
            jnp.stack(hg_s), jnp.stack(re_s), jnp.stack(im_s))
```

```python
import functools

import jax
import jax.numpy as jnp
from jax import lax
from jax.experimental import pallas as pl
from jax.experimental.pallas import tpu as pltpu

F32 = jnp.float32
BF16 = jnp.bfloat16
I32 = jnp.int32

RMS_EPS = 1e-6
HG_CHUNK = 64
MOE_TOP_K = 2

V7X_VMEM_BYTES = 64 * 1024 * 1024
VMEM_LIMIT_BYTES = V7X_VMEM_BYTES - 8 * 1024 * 1024
SUBLANES = 8
LANES = 128

TOKEN_TILE = 512
EXPERT_TILE = 512
EXPERT_WINDOWS = (EXPERT_TILE, EXPERT_TILE // 2, EXPERT_TILE // 4)
S5_TIME_TILE = 64
HGRN_TIME_TILE = 256
HGRN_SEQ_TILE = 8
PROJ_COL_TILE = 512


def _cparams(sem):
    return pltpu.CompilerParams(dimension_semantics=sem, vmem_limit_bytes=VMEM_LIMIT_BYTES)


def _resident(shape):
    nd = len(shape)
    return pl.BlockSpec(shape, lambda *_: (0,) * nd, pipeline_mode=pl.Buffered(1))


def _rmsnorm(x, g):
    return x * lax.rsqrt(jnp.mean(x * x, axis=-1, keepdims=True) + RMS_EPS) * g


def _two_source_specs(tm, width, n_first):
    return [pl.BlockSpec((tm, width), lambda i: (jnp.minimum(i, n_first - 1), 0)),
            pl.BlockSpec((tm, width), lambda i: (jnp.maximum(i - n_first, 0), 0))]


def _pick(first_ref, second_ref, n_first):
    return jnp.where(pl.program_id(0) < n_first, first_ref[...], second_ref[...])


def _store_token_tiles(ref, x, lead=()):
    rows = x.shape[0]
    for c in range(SUBLANES):
        ref[lead + (pl.ds(c, rows, stride=SUBLANES), slice(None))] = x[:, c * LANES:(c + 1) * LANES]


def _load_token_tiles(ref, rows, lead=()):
    return jnp.concatenate([ref[lead + (pl.ds(c, rows, stride=SUBLANES), slice(None))] for c in range(SUBLANES)],
                           axis=-1)


def _inproj_body(xp_ref, xs_ref, g_ref, w_ref, o_ref, *, n_first):
    xb = _rmsnorm(_pick(xp_ref, xs_ref, n_first), g_ref[...]).astype(BF16)
    for j in range(0, w_ref.shape[1], PROJ_COL_TILE):
        o_ref[:, j:j + PROJ_COL_TILE] = jnp.dot(xb, w_ref[:, j:j + PROJ_COL_TILE], preferred_element_type=F32)


def _in_proj(xp, xs, g, w):
    d = xp.shape[1]
    n = w.shape[1]
    tm = TOKEN_TILE
    total = xp.shape[0] + xs.shape[0]
    n_first = xp.shape[0] // tm
    return pl.pallas_call(
        functools.partial(_inproj_body, n_first=n_first), grid=(total // tm,),
        in_specs=_two_source_specs(tm, d, n_first) + [_resident((1, d)), _resident((d, n))],
        out_specs=pl.BlockSpec((tm, n), lambda i: (i, 0)),
        out_shape=jax.ShapeDtypeStruct((total, n), F32),
        compiler_params=_cparams(("parallel",)), name="in_proj")(xp, xs, g, w)


def _s5_body(*refs, tt, nstate):
    u_refs = refs[:SUBLANES]
    (h0r_ref, h0i_ref, ar_ref, ai_ref, bb_ref, cc_ref, d_ref, wg_ref, bg_ref,
     y_ref, hr_out, hi_out, hr_scr, hi_scr, bu_scr, u_scr, y_scr) = refs[SUBLANES:]
    j = pl.program_id(1)
    half = nstate // 2
    w = u_refs[0].shape[-1]
    kw = w // 2

    @pl.when(j == 0)
    def _():
        hr_scr[...] = h0r_ref[0]
        hi_scr[...] = h0i_ref[0]

    for b in range(SUBLANES):
        ub = u_refs[b][...]
        for s in range(w // LANES):
            u_scr[s, pl.ds(b, tt, stride=SUBLANES), :] = ub[:, s * LANES:(s + 1) * LANES]
    u = jnp.concatenate([u_scr[s] for s in range(w // LANES)], axis=-1)
    ub16 = u.astype(BF16)
    for kt in range(2):
        ukt = ub16[:, kt * kw:(kt + 1) * kw]
        bu_scr[:, kt * half:(kt + 1) * half] = jnp.dot(ukt, bb_ref[kt, :, :half], preferred_element_type=F32)
        bu_scr[:, nstate + kt * half:nstate + (kt + 1) * half] = jnp.dot(
            ukt, bb_ref[kt, :, half:], preferred_element_type=F32)

    lane_chunk = 512
    for lc in range(nstate // lane_chunk):
        lo = lc * lane_chunk
        re_sl = slice(lo, lo + lane_chunk)
        im_sl = slice(nstate + lo, nstate + lo + lane_chunk)
        ar = jnp.broadcast_to(ar_ref[:, re_sl], (SUBLANES, lane_chunk))
        ai = jnp.broadcast_to(ai_ref[:, re_sl], (SUBLANES, lane_chunk))

        hr, hi = hr_scr[:, re_sl], hi_scr[:, re_sl]
        for t in range(tt):
            rs = slice(t * SUBLANES, (t + 1) * SUBLANES)
            hr, hi = (ar * hr - ai * hi + bu_scr[rs, re_sl], ar * hi + ai * hr + bu_scr[rs, im_sl])
            bu_scr[rs, re_sl] = hr
            bu_scr[rs, im_sl] = hi
        hr_scr[:, re_sl] = hr
        hi_scr[:, re_sl] = hi

    ys = []
    for n in range(2):
        h_re = bu_scr[:, n * half:(n + 1) * half].astype(BF16)
        h_im = bu_scr[:, nstate + n * half:nstate + (n + 1) * half].astype(BF16)
        ys.append(jnp.dot(h_re, cc_ref[n, :half, :], preferred_element_type=F32)
                  + jnp.dot(h_im, cc_ref[n, half:, :], preferred_element_type=F32))
    y = jnp.concatenate(ys, axis=-1) + d_ref[...] * u
    z = jnp.dot(jax.nn.gelu(y).astype(BF16), wg_ref[...], preferred_element_type=F32) + bg_ref[...]
    dm = z.shape[-1] // 2
    yb = z[:, :dm] * jax.nn.sigmoid(z[:, dm:])
    for s in range(dm // LANES):
        y_scr[s] = yb[:, s * LANES:(s + 1) * LANES]
    for b in range(SUBLANES):
        for s in range(dm // LANES):
            y_ref[b, :, s * LANES:(s + 1) * LANES] = y_scr[s, pl.ds(b, tt, stride=SUBLANES), :]

    @pl.when(j == pl.num_programs(1) - 1)
    def _():
        hr_out[0] = hr_scr[...]
        hi_out[0] = hi_scr[...]


def _s5_branch(proj, u_col, width, batch, seq, row_off, h0r, h0i, ar, ai, bb, cc, d_skip, w_glu, b_glu):
    nstate = ar.shape[-1]
    dm = w_glu.shape[1] // 2
    tt = min(S5_TIME_TILE, seq)
    nj = seq // tt
    nbb = batch // SUBLANES
    body = functools.partial(_s5_body, tt=tt, nstate=nstate)

    def u_spec(b):
        return pl.BlockSpec((tt, width), lambda bb_, j, b=b: (row_off // tt + (bb_ * SUBLANES + b) * nj + j,
                                                               u_col // width))

    state_spec = pl.BlockSpec((1, SUBLANES, nstate), lambda bb_, j: (bb_, 0, 0))
    return pl.pallas_call(
        body, grid=(nbb, nj),
        in_specs=[u_spec(b) for b in range(SUBLANES)] + [
            state_spec, state_spec, _resident(ar.shape), _resident(ai.shape), _resident(bb.shape),
            _resident(cc.shape), _resident(d_skip.shape), _resident(w_glu.shape), _resident(b_glu.shape)],
        out_specs=[pl.BlockSpec((SUBLANES, tt, dm), lambda bb_, j: (bb_, j, 0)), state_spec, state_spec],
        out_shape=[jax.ShapeDtypeStruct((batch, seq, dm), F32),
                   jax.ShapeDtypeStruct((nbb, SUBLANES, nstate), F32),
                   jax.ShapeDtypeStruct((nbb, SUBLANES, nstate), F32)],
        scratch_shapes=[pltpu.VMEM((SUBLANES, nstate), F32), pltpu.VMEM((SUBLANES, nstate), F32),
                        pltpu.VMEM((tt * SUBLANES, 2 * nstate), F32),
                        pltpu.VMEM((width // LANES, tt * SUBLANES, LANES), F32),
                        pltpu.VMEM((dm // LANES, tt * SUBLANES, LANES), F32)],
        compiler_params=_cparams(("parallel", "arbitrary")), name="s5_branch")(
            *([proj] * SUBLANES), h0r, h0i, ar, ai, bb, cc, d_skip, w_glu, b_glu)


def _cumsum_rows(x):
    c = x.shape[0]
    row = lax.broadcasted_iota(I32, x.shape, 0)
    s = 1
    while s < c:
        x = x + jnp.where(row >= s, pltpu.roll(x, s, axis=0), 0.0)
        s *= 2
    return x


def _hgrn_gates(q, fr, lb, scale):
    f = lb + (1.0 - lb) * jax.nn.sigmoid(fr)
    k = 1.0 - f
    b = _cumsum_rows(jnp.log(f))
    b_last = b[-1:, :]
    q_dec = (q * scale) * jnp.exp(b)
    k_dec = k * jnp.exp(-b)
    k_end = k * jnp.exp(b_last - b)
    return q_dec.astype(BF16), k_dec.astype(BF16), k_end.astype(BF16), b_last


def _causal_scores(q_dec, k_dec):
    c = q_dec.shape[0]
    s = lax.dot_general(q_dec, k_dec, (((1,), (1,)), ((), ())), preferred_element_type=F32)
    keep = lax.broadcasted_iota(I32, (c, c), 0) >= lax.broadcasted_iota(I32, (c, c), 1)
    return jnp.where(keep, s, 0.0).astype(BF16)


def _gated_out(o, gn, og):
    o = o * lax.rsqrt(jnp.mean(o * o, axis=-1, keepdims=True) + RMS_EPS) * gn
    return (o * jax.nn.silu(og)).astype(BF16)


def _hgrn_long_body(q_ref, f_ref, v_ref, og_ref, lb_ref, gn_ref, o_ref, sfin_ref, st_scr, *, c, heads, dk, scale):
    j = pl.program_id(1)

    @pl.when(j == 0)
    def _():
        st_scr[...] = jnp.zeros_like(st_scr)

    def chunk(ci, carry):
        rs = pl.ds(pl.multiple_of(ci * c, c), c)
        for h in range(heads):
            hs = slice(h * dk, (h + 1) * dk)
            q_dec, k_dec, k_end, b_last = _hgrn_gates(q_ref[rs, hs], f_ref[rs, hs], lb_ref[:, hs], scale)
            v = v_ref[rs, hs].astype(BF16)
            scores = _causal_scores(q_dec, k_dec)
            st = st_scr[h]
            o = (lax.dot_general(q_dec, st.astype(BF16), (((1,), (1,)), ((), ())), preferred_element_type=F32)
                 + jnp.dot(scores, v, preferred_element_type=F32))
            st_scr[h] = jnp.exp(b_last) * st + lax.dot_general(
                v, k_end, (((0,), (0,)), ((), ())), preferred_element_type=F32)
            o_ref[rs, hs] = _gated_out(o, gn_ref[:, hs], og_ref[rs, hs])
        return carry

    lax.fori_loop(0, q_ref.shape[0] // c, chunk, 0)

    @pl.when(j == pl.num_programs(1) - 1)
    def _():
        for h in range(heads):
            sfin_ref[0, h] = st_scr[h].T


def _hgrn_long(proj, lb, gn, batch, seq, heads, dk, row_off):
    width = heads * dk
    tb = min(HGRN_TIME_TILE, seq)
    nj = seq // tb
    off = row_off // tb
    body = functools.partial(_hgrn_long_body, c=min(HG_CHUNK, seq), heads=heads, dk=dk, scale=dk ** -0.5)

    def col(k):
        return pl.BlockSpec((tb, width), lambda b, j, k=k: (off + b * nj + j, k))

    return pl.pallas_call(
        body, grid=(batch, nj),
        in_specs=[col(0), col(1), col(2), col(3), _resident(lb.shape), _resident(gn.shape)],
        out_specs=[pl.BlockSpec((tb, width), lambda b, j: (b * nj + j, 0)),
                   pl.BlockSpec((1, heads, dk, dk), lambda b, j: (b, 0, 0, 0))],
        out_shape=[jax.ShapeDtypeStruct((batch * seq, width), BF16),
                   jax.ShapeDtypeStruct((batch, heads, dk, dk), F32)],
        scratch_shapes=[pltpu.VMEM((heads, dk, dk), F32)],
        compiler_params=_cparams(("parallel", "arbitrary")), name="hgrn_long")(proj, proj, proj, proj, lb, gn)


def _hgrn_short_body(q_ref, f_ref, v_ref, og_ref, lb_ref, gn_ref, s0_ref, o_ref, snew_ref, *, c, heads, dk, scale):
    def one_seq(sq, carry):
        rs = pl.ds(pl.multiple_of(sq * c, c), c)
        for h in range(heads):
            hs = slice(h * dk, (h + 1) * dk)
            q_dec, k_dec, k_end, b_last = _hgrn_gates(q_ref[rs, hs], f_ref[rs, hs], lb_ref[:, hs], scale)
            v = v_ref[rs, hs].astype(BF16)
            scores = _causal_scores(q_dec, k_dec)
            s0 = s0_ref[sq, h]
            o = (jnp.dot(q_dec, s0.astype(BF16), preferred_element_type=F32)
                 + jnp.dot(scores, v, preferred_element_type=F32))
            decay_col = jnp.broadcast_to(jnp.exp(b_last), (dk, dk)).T
            snew_ref[sq, h] = decay_col * s0 + lax.dot_general(
                k_end, v, (((0,), (0,)), ((), ())), preferred_element_type=F32)
            o_ref[rs, hs] = _gated_out(o, gn_ref[:, hs], og_ref[rs, hs])
        return carry

    lax.fori_loop(0, s0_ref.shape[0], one_seq, 0)


def _hgrn_short(proj, lb, gn, s0, seq, row_off):
    batch, heads, dk, _ = s0.shape
    width = heads * dk
    nb = HGRN_SEQ_TILE
    rows = nb * seq
    off = row_off // rows
    body = functools.partial(_hgrn_short_body, c=seq, heads=heads, dk=dk, scale=dk ** -0.5)

    def col(k):
        return pl.BlockSpec((rows, width), lambda i, k=k: (off + i, k))

    state_spec = pl.BlockSpec((nb, heads, dk, dk), lambda i: (i, 0, 0, 0))
    return pl.pallas_call(
        body, grid=(batch // nb,),
        in_specs=[col(0), col(1), col(2), col(3), _resident(lb.shape), _resident(gn.shape), state_spec],
        out_specs=[pl.BlockSpec((rows, width), lambda i: (i, 0)), state_spec],
        out_shape=[jax.ShapeDtypeStruct((batch * seq, width), BF16), jax.ShapeDtypeStruct(s0.shape, F32)],
        compiler_params=_cparams(("parallel",)), name="hgrn_short")(proj, proj, proj, proj, lb, gn, s0)


def _merge_body(op_ref, os_ref, ybp_ref, ybs_ref, ga_ref, gb_ref, xp_ref, xs_ref, wa_ref, wo_ref, gf_ref, wr_ref,
                br_ref, h_ref, xn_ref, lg_ref, *, n_first):
    y_a = jnp.dot(_pick(op_ref, os_ref, n_first), wa_ref[...], preferred_element_type=F32)
    merged = jax.nn.sigmoid(ga_ref[...]) * y_a + jax.nn.sigmoid(gb_ref[...]) * _pick(ybp_ref, ybs_ref, n_first)
    h = _pick(xp_ref, xs_ref, n_first) + jnp.dot(merged.astype(BF16), wo_ref[...], preferred_element_type=F32)
    h_ref[...] = h
    xn = _rmsnorm(h, gf_ref[...])
    _store_token_tiles(xn_ref, xn)
    lg_ref[...] = lax.dot_general(wr_ref[...], xn.astype(BF16), (((1,), (1,)), ((), ())),
                                  preferred_element_type=F32) + br_ref[...]


def _merge(o_p, o_s, yb_p, yb_s, proj, xp, xs, wa, wo, gf, wr, br):
    d = xp.shape[1]
    total = proj.shape[0]
    nr = wr.shape[0]
    tm = TOKEN_TILE
    n_first = xp.shape[0] // tm
    pair = _two_source_specs(tm, d, n_first)

    def row(k=0):
        return pl.BlockSpec((tm, d), lambda i, k=k: (i, k))

    return pl.pallas_call(
        functools.partial(_merge_body, n_first=n_first), grid=(total // tm,),
        in_specs=pair + pair + [row(4), row(5)] + pair + [
            _resident(wa.shape), _resident(wo.shape), _resident(gf.shape), _resident(wr.shape), _resident(br.shape)],
        out_specs=[row(), pl.BlockSpec((tm * SUBLANES, LANES), lambda i: (i, 0)),
                   pl.BlockSpec((nr, tm), lambda i: (0, i))],
        out_shape=[jax.ShapeDtypeStruct((total, d), F32), jax.ShapeDtypeStruct((total * SUBLANES, LANES), F32),
                   jax.ShapeDtypeStruct((nr, total), F32)],
        compiler_params=_cparams(("parallel",)), name="merge_out")(
            o_p, o_s, yb_p, yb_s, proj, proj, xp, xs, wa, wo, gf, wr, br)


def _first_index_of_max(vals):
    m = vals[0]
    for v in vals[1:]:
        m = jnp.maximum(m, v)
    idx = jnp.full(m.shape, len(vals), I32)
    for e in range(len(vals) - 1, -1, -1):
        idx = jnp.where(vals[e] == m, e, idx)
    return m, idx


def _route_body(lg_ref, ids_ref, w_ref, rk_ref, cnt_ref, carry_scr, *, groups, experts):
    i = pl.program_id(0)
    tile = lg_ref.shape[1]
    n_exp = groups * experts

    @pl.when(i == 0)
    def _():
        carry_scr[...] = jnp.zeros_like(carry_scr)

    gl = [lg_ref[g:g + 1, :] for g in range(groups)]
    gmax, gidx = _first_index_of_max(gl)
    denom = jnp.exp(gl[0] - gmax)
    for g in range(1, groups):
        denom = denom + jnp.exp(gl[g] - gmax)
    g_w = 1.0 / denom

    el = []
    for e in range(experts):
        v = lg_ref[groups + e:groups + e + 1, :]
        for g in range(1, groups):
            r = groups + g * experts + e
            v = jnp.where(gidx == g, lg_ref[r:r + 1, :], v)
        el.append(v)
    v1, i1 = _first_index_of_max(el)
    rest = [jnp.where(i1 == e, -jnp.inf, el[e]) for e in range(experts)]
    v2, i2 = _first_index_of_max(rest)
    t = jnp.exp(v2 - v1)
    inv = 1.0 / (1.0 + t)
    e1 = gidx * experts + i1
    e2 = gidx * experts + i2

    erow = lax.broadcasted_iota(I32, (n_exp, tile), 0)
    oh1 = (erow == e1).astype(F32)
    oh2 = (erow == e2).astype(F32)
    oh = oh1 + oh2
    before = (lax.broadcasted_iota(I32, (tile, tile), 0) < lax.broadcasted_iota(I32, (tile, tile), 1))
    cnt = jnp.dot(oh.astype(BF16), before.astype(BF16), preferred_element_type=F32) + carry_scr[:, 0:1]
    ids_ref[0:1, :] = e1
    ids_ref[1:2, :] = e2
    w_ref[0:1, :] = inv * g_w
    w_ref[1:2, :] = (t * inv) * g_w
    rk_ref[0:1, :] = jnp.sum(oh1 * cnt, axis=0, keepdims=True).astype(I32)
    rk_ref[1:2, :] = jnp.sum(oh2 * cnt, axis=0, keepdims=True).astype(I32)
    carry_scr[...] = carry_scr[...] + jnp.sum(oh, axis=1, keepdims=True)

    @pl.when(i == pl.num_programs(0) - 1)
    def _():
        cnt_ref[...] = carry_scr[...]


def _route(logits_t, groups, experts):
    nr, total = logits_t.shape
    tile = TOKEN_TILE
    n_exp = groups * experts
    body = functools.partial(_route_body, groups=groups, experts=experts)
    pair = pl.BlockSpec((MOE_TOP_K, tile), lambda i: (0, i))
    return pl.pallas_call(
        body, grid=(total // tile,),
        in_specs=[pl.BlockSpec((nr, tile), lambda i: (0, i))],
        out_specs=[pair, pair, pair, pl.BlockSpec((n_exp, LANES), lambda i: (0, 0))],
        out_shape=[jax.ShapeDtypeStruct((MOE_TOP_K, total), I32), jax.ShapeDtypeStruct((MOE_TOP_K, total), F32),
                   jax.ShapeDtypeStruct((MOE_TOP_K, total), I32), jax.ShapeDtypeStruct((n_exp, LANES), F32)],
        scratch_shapes=[pltpu.VMEM((n_exp, LANES), F32)],
        compiler_params=_cparams(("arbitrary",)), name="route")(logits_t)


def _row_copy(src, dst, sem):
    return pltpu.make_async_copy(src, dst, sem)


def _token_rows(r):
    return pl.ds(pl.multiple_of(r * SUBLANES, SUBLANES), SUBLANES)


def _dispatch_body(pos_ref, x_ref, o_hbm, ring, zero_scr, sem, pad_sem, *, n_sorted):
    i = pl.program_id(0)
    tile = x_ref.shape[0] // SUBLANES
    par = lax.rem(i, 2)

    @pl.when(i == 0)
    def _():
        zero_scr[...] = jnp.zeros_like(zero_scr)
        pad = _row_copy(zero_scr, o_hbm.at[pl.ds(n_sorted * SUBLANES, zero_scr.shape[0])], pad_sem.at[0])
        pad.start()
        pad.wait()

    ring[par] = x_ref[...]

    def issue(r, carry):
        for k in range(MOE_TOP_K):
            p = pos_ref[MOE_TOP_K * r + k]
            _row_copy(ring.at[par, _token_rows(r)], o_hbm.at[_token_rows(p)], sem.at[par, k]).start(priority=k)
        return carry

    lax.fori_loop(0, tile, issue, 0, unroll=8)

    def drain(slot):
        for k in range(MOE_TOP_K):
            _row_copy(ring.at[slot], o_hbm.at[pl.ds(0, tile * SUBLANES)], sem.at[slot, k]).wait()

    @pl.when(i > 0)
    def _():
        drain(1 - par)

    @pl.when(i == pl.num_programs(0) - 1)
    def _():
        drain(par)


def _dispatch(pos3, xn_tiles):
    total = xn_tiles.shape[0] // SUBLANES
    tile = TOKEN_TILE
    n_sorted = total * MOE_TOP_K
    pad = EXPERT_WINDOWS[-1]
    return pl.pallas_call(
        functools.partial(_dispatch_body, n_sorted=n_sorted), grid=(total // tile,),
        in_specs=[pl.BlockSpec((MOE_TOP_K * tile,), lambda i: (i,), memory_space=pltpu.SMEM),
                  pl.BlockSpec((tile * SUBLANES, LANES), lambda i: (i, 0))],
        out_specs=pl.BlockSpec(memory_space=pl.ANY),
        out_shape=jax.ShapeDtypeStruct(((n_sorted + pad) * SUBLANES, LANES), F32),
        scratch_shapes=[pltpu.VMEM((2, tile * SUBLANES, LANES), F32), pltpu.VMEM((pad * SUBLANES, LANES), F32),
                        pltpu.SemaphoreType.DMA((2, MOE_TOP_K)), pltpu.SemaphoreType.DMA((1,))],
        compiler_params=_cparams(("arbitrary",)), name="dispatch")(pos3, xn_tiles)


def _experts_body(it_exp, it_row, it_cls, it_first, it_next, n_items, xs_hbm, wg_hbm, wu_hbm, wd_hbm, ys_hbm,
                  xbuf, ybuf, wg_s, wu_s, wd_s, wg_b, wu_b, wd_b, sem_in, sem_out, sem_w):
    j = pl.program_id(0)
    n = n_items[0]
    pad = EXPERT_WINDOWS[-1]
    slot = lax.rem(j, 2)

    def weight_copies(e, s):
        return [pltpu.make_async_copy(hbm.at[e], stage.at[s], sem_w.at[s, t])
                for t, (hbm, stage) in enumerate(((wg_hbm, wg_s), (wu_hbm, wu_s), (wd_hbm, wd_s)))]

    def by_size(item, fn):
        for ci, m in enumerate(EXPERT_WINDOWS):
            pl.when(it_cls[item] == ci)(functools.partial(fn, m))

    def window(item, m):
        return pl.ds(pl.multiple_of(it_row[item] * SUBLANES, SUBLANES), m * SUBLANES)

    def in_copy(item, s, m):
        return pltpu.make_async_copy(xs_hbm.at[window(item, m)], xbuf.at[s, pl.ds(0, m * SUBLANES)], sem_in.at[s])

    def out_copy(item, s, m):
        return pltpu.make_async_copy(ybuf.at[s, pl.ds(0, m * SUBLANES)], ys_hbm.at[window(item, m)], sem_out.at[s])

    def compute(m):
        x = _load_token_tiles(xbuf, m, (slot,)).astype(BF16)
        hg = jnp.dot(x, wg_b[...], preferred_element_type=F32)
        hu = jnp.dot(x, wu_b[...], preferred_element_type=F32)
        hid = (jax.nn.silu(hg) * hu).astype(BF16)
        _store_token_tiles(ybuf, jnp.dot(hid, wd_b[...], preferred_element_type=F32), (slot,))

    @pl.when(j < n)
    def _():
        @pl.when(j == 0)
        def _():
            by_size(0, lambda m: in_copy(0, 0, m).start())
            for c in weight_copies(it_exp[0], it_first[0] - 1):
                c.start()
            tail_rows = pl.ds(0, pad * SUBLANES)
            ybuf[1, tail_rows, :] = jnp.zeros((pad * SUBLANES, LANES), F32)
            tail = pltpu.make_async_copy(
                ybuf.at[1, tail_rows], ys_hbm.at[pl.ds(ys_hbm.shape[0] - pad * SUBLANES, pad * SUBLANES)],
                sem_out.at[1])
            tail.start()
            tail.wait()

        @pl.when(j + 1 < n)
        def _():
            by_size(j + 1, lambda m: in_copy(j + 1, 1 - slot, m).start())

        @pl.when(it_first[j] > 0)
        def _():
            s = it_first[j] - 1
            for c in weight_copies(it_exp[j], s):
                c.wait()
            wg_b[...] = wg_s[s].astype(BF16)
            wu_b[...] = wu_s[s].astype(BF16)
            wd_b[...] = wd_s[s].astype(BF16)

            @pl.when(it_next[j] >= 0)
            def _():
                for c in weight_copies(it_next[j], 1 - s):
                    c.start()

        by_size(j, lambda m: in_copy(j, slot, m).wait())
        by_size(j, compute)

        @pl.when(j > 0)
        def _():
            by_size(j - 1, lambda m: out_copy(j - 1, 1 - slot, m).wait())

        by_size(j, lambda m: out_copy(j, slot, m).start())

        @pl.when(j == n - 1)
        def _():
            by_size(j, lambda m: out_copy(j, slot, m).wait())


def _experts(items, xs, wg, wu, wd):
    d, de = wg.shape[1], wg.shape[2]
    tm = EXPERT_TILE
    max_items = items[0].shape[0]
    grid_spec = pltpu.PrefetchScalarGridSpec(
        num_scalar_prefetch=6, grid=(max_items,),
        in_specs=[pl.BlockSpec(memory_space=pl.ANY)] * 4,
        out_specs=pl.BlockSpec(memory_space=pl.ANY),
        scratch_shapes=[pltpu.VMEM((2, tm * SUBLANES, LANES), F32), pltpu.VMEM((2, tm * SUBLANES, LANES), F32),
                        pltpu.VMEM((2, d, de), F32), pltpu.VMEM((2, d, de), F32), pltpu.VMEM((2, de, d), F32),
                        pltpu.VMEM((d, de), BF16), pltpu.VMEM((d, de), BF16), pltpu.VMEM((de, d), BF16),
                        pltpu.SemaphoreType.DMA((2,)), pltpu.SemaphoreType.DMA((2,)),
                        pltpu.SemaphoreType.DMA((2, 3))])
    return pl.pallas_call(
        _experts_body, grid_spec=grid_spec, out_shape=jax.ShapeDtypeStruct(xs.shape, F32),
        compiler_params=_cparams(("arbitrary",)), name="experts")(*items, xs, wg, wu, wd)


def _combine_body(pos_ref, pos_next_ref, h_ref, w_ref, g_ref, ys_hbm, y_ref, buf, sem, *, final_norm):
    i = pl.program_id(0)
    tile = h_ref.shape[0]
    par = lax.rem(i, 2)

    def gather(table, slot):
        def issue(r, carry):
            for k in range(MOE_TOP_K):
                p = table[MOE_TOP_K * r + k]
                _row_copy(ys_hbm.at[_token_rows(p)], buf.at[slot, k, _token_rows(r)],
                          sem.at[slot, k]).start(priority=k)
            return carry

        lax.fori_loop(0, tile, issue, 0, unroll=8)

    @pl.when(i == 0)
    def _():
        gather(pos_ref, 0)

    @pl.when(i + 1 < pl.num_programs(0))
    def _():
        gather(pos_next_ref, 1 - par)

    for k in range(MOE_TOP_K):
        _row_copy(ys_hbm.at[pl.ds(0, tile * SUBLANES)], buf.at[par, k], sem.at[par, k]).wait()
    h = h_ref[...] + (w_ref[:, 0:1] * _load_token_tiles(buf, tile, (par, 0))
                      + w_ref[:, 1:2] * _load_token_tiles(buf, tile, (par, 1)))
    y_ref[...] = _rmsnorm(h, g_ref[...]) if final_norm else h


def _combine(pos3, h_all, w_t, g, ys, rows, row_off, final_norm):
    d = h_all.shape[1]
    tile = TOKEN_TILE
    off = row_off // tile
    last_block = h_all.shape[0] // tile - 1
    return pl.pallas_call(
        functools.partial(_combine_body, final_norm=final_norm), grid=(rows // tile,),
        in_specs=[pl.BlockSpec((MOE_TOP_K * tile,), lambda i: (off + i,), memory_space=pltpu.SMEM),
                  pl.BlockSpec((MOE_TOP_K * tile,), lambda i: (jnp.minimum(off + i + 1, last_block),),
                               memory_space=pltpu.SMEM),
                  pl.BlockSpec((tile, d), lambda i: (off + i, 0)),
                  pl.BlockSpec((tile, MOE_TOP_K), lambda i: (off + i, 0)),
                  _resident(g.shape),
                  pl.BlockSpec(memory_space=pl.ANY)],
        out_specs=pl.BlockSpec((tile, d), lambda i: (i, 0)),
        out_shape=jax.ShapeDtypeStruct((rows, d), F32),
        scratch_shapes=[pltpu.VMEM((2, MOE_TOP_K, tile * SUBLANES, LANES), F32),
                        pltpu.SemaphoreType.DMA((2, MOE_TOP_K))],
        compiler_params=_cparams(("arbitrary",)), name="combine")(pos3, pos3, h_all, w_t, g, ys)


def _lookup(table, idx):
    sel = idx[None] == jnp.arange(table.shape[0], dtype=I32).reshape((-1,) + (1,) * idx.ndim)
    return jnp.sum(jnp.where(sel, table.reshape(sel.shape[:1] + (1,) * idx.ndim), 0), axis=0)


def _work_items(counts, n_sorted):
    big, mid, small = EXPERT_WINDOWS
    n_exp = counts.shape[0]
    max_items = n_sorted // big + 2 * n_exp
    ends = jnp.cumsum(counts)
    starts = ends - counts
    units = (counts % big + small - 1) // small
    n_big = counts // big + (units == big // small)
    units = jnp.where(units == big // small, 0, units)
    n_mid = units // (mid // small)
    n_e = n_big + n_mid + units % (mid // small)
    item_end = jnp.cumsum(n_e)
    item_start = item_end - n_e
    n_items = item_end[-1]
    j = jnp.minimum(jnp.arange(max_items, dtype=I32), n_items - 1)
    e = jnp.sum((item_end[None, :] <= j[:, None]).astype(I32), axis=1)
    k = j - _lookup(item_start, e)
    nb, nm = _lookup(n_big, e), _lookup(n_mid, e)
    cls = jnp.where(k < nb, 0, jnp.where(k < nb + nm, 1, 2))
    row = _lookup(starts, e) + jnp.where(cls == 0, k * big, nb * big + jnp.where(cls == 1, 0, nm * mid))
    ordinal = jnp.cumsum((n_e > 0).astype(I32)) - 1
    first = jnp.where(k == 0, 1 + _lookup(ordinal, e) % 2, 0)
    nxt_item = _lookup(item_end, e)
    nxt = jnp.where(nxt_item < n_items, jnp.sum((item_end[None, :] <= nxt_item[:, None]).astype(I32), axis=1), -1)
    return (e, row.astype(I32), cls.astype(I32), first.astype(I32), nxt.astype(I32),
            n_items.reshape(1).astype(I32)), starts


def _s5_discretise(lam_re, lam_im, log_dt, b_re, b_im, c_re, c_im):
    g, p = lam_re.shape
    ch = b_re.shape[-1]
    lam_re = lam_re.astype(F32)
    lam_im = lam_im.astype(F32)
    dt = jnp.exp(log_dt.astype(F32))[:, None]
    mag = jnp.exp(lam_re * dt)
    ab_re = mag * jnp.cos(lam_im * dt)
    ab_im = mag * jnp.sin(lam_im * dt)
    den = lam_re * lam_re + lam_im * lam_im
    nr = ab_re - 1.0
    coef_re = (nr * lam_re + ab_im * lam_im) / den
    coef_im = (ab_im * lam_re - nr * lam_im) / den
    bb_re = coef_re[..., None] * b_re - coef_im[..., None] * b_im
    bb_im = coef_re[..., None] * b_im + coef_im[..., None] * b_re
    gh = g // 2
    eye = jnp.eye(gh, dtype=F32)

    def in_block(m):
        return jnp.einsum("gpc,gh->gchp", m, eye).reshape(gh * ch, gh * p)

    def out_block(m):
        return jnp.einsum("gcp,gh->gphc", m, eye).reshape(gh * p, gh * ch)

    bb = jnp.stack([jnp.concatenate([in_block(bb_re[k * gh:(k + 1) * gh]), in_block(bb_im[k * gh:(k + 1) * gh])],
                                    axis=1) for k in range(2)]).astype(BF16)
    cc = jnp.stack([jnp.concatenate([out_block(c_re[k * gh:(k + 1) * gh]), out_block(-c_im[k * gh:(k + 1) * gh])],
                                    axis=0) for k in range(2)]).astype(BF16)
    return ab_re.reshape(1, g * p), ab_im.reshape(1, g * p), bb, cc


def kernel(x_prompt, x_sample, state_hgrn, state_s5_re, state_s5_im, norm_mix_g, w_in, hgrn_lb_raw, hgrn_onorm_g, w_branch_a, s5_lambda_re, s5_lambda_im, s5_log_dt, s5_b_re, s5_b_im, s5_c_re, s5_c_im, s5_d, w_glu, b_glu, w_out, norm_ffn_g, w_router_group, b_router_group, w_router_expert, b_router_expert, w_exp_gate, w_exp_up, w_exp_down, norm_final_g):
    depth = norm_mix_g.shape[0]
    bp, lp, d = x_prompt.shape
    bs, ls, _ = x_sample.shape
    heads, dk = state_hgrn.shape[2], state_hgrn.shape[3]
    kw = heads * dk
    s5_groups, s5_state = state_s5_re.shape[2], state_s5_re.shape[3]
    s5_width = s5_d.shape[-1]
    nstate = s5_groups * s5_state
    moe_groups, _, experts = w_router_expert.shape[1:]
    n_exp = moe_groups * experts
    rows_p, rows_s = bp * lp, bs * ls
    total = rows_p + rows_s
    n_sorted = total * MOE_TOP_K
    assert kw == d and state_hgrn.shape[4] == dk, "column blocks assume key width == value width == model width"
    assert d == SUBLANES * LANES, "token-tile layout holds one token per (8, 128) tile"
    assert s5_groups % 2 == 0 and bp % SUBLANES == 0 and bs % HGRN_SEQ_TILE == 0

    lb_all = jnp.cumsum(jax.nn.softmax(hgrn_lb_raw.astype(F32), axis=0), axis=0)

    hp = x_prompt.reshape(rows_p, d)
    hs = x_sample.reshape(rows_s, d)
    hg_p, re_p, im_p, hg_s, re_s, im_s = [], [], [], [], [], []
    zeros_state = jnp.zeros((bp // SUBLANES, SUBLANES, nstate), F32)

    for l in range(depth):
        w = w_in[l]
        w_cols = jnp.concatenate([w[:, :4 * kw], w[:, 4 * kw + s5_width:], w[:, 4 * kw:4 * kw + s5_width]],
                                 axis=1).astype(BF16)
        proj = _in_proj(hp, hs, norm_mix_g[l].reshape(1, d), w_cols)

        ar, ai, bb, cc = _s5_discretise(s5_lambda_re[l], s5_lambda_im[l], s5_log_dt[l], s5_b_re[l], s5_b_im[l],
                                        s5_c_re[l], s5_c_im[l])
        s5_args = (ar, ai, bb, cc, s5_d[l].reshape(1, s5_width), w_glu[l].astype(BF16), b_glu[l].reshape(1, -1))
        yb_p, fr_p, fi_p = _s5_branch(proj, 6 * kw, s5_width, bp, lp, 0, zeros_state, zeros_state, *s5_args)
        yb_s, fr_s, fi_s = _s5_branch(proj, 6 * kw, s5_width, bs, ls, rows_p,
                                      state_s5_re[l].reshape(bs // SUBLANES, SUBLANES, nstate),
                                      state_s5_im[l].reshape(bs // SUBLANES, SUBLANES, nstate), *s5_args)

        lb = lb_all[l].reshape(1, kw)
        gn = hgrn_onorm_g[l].reshape(1, kw)
        o_p, hgp = _hgrn_long(proj, lb, gn, bp, lp, heads, dk, 0)
        o_s, hgs = _hgrn_short(proj, lb, gn, state_hgrn[l].astype(F32), ls, rows_p)

        nr = -(-(moe_groups + n_exp) // SUBLANES) * SUBLANES
        wr = jnp.concatenate([w_router_group[l].T, w_router_expert[l].transpose(0, 2, 1).reshape(n_exp, d)], axis=0)
        wr = jnp.pad(wr, ((0, nr - wr.shape[0]), (0, 0))).astype(BF16)
        br = jnp.pad(jnp.concatenate([b_router_group[l], b_router_expert[l].reshape(n_exp)]),
                     (0, nr - moe_groups - n_exp)).reshape(nr, 1).astype(F32)
        h_all, xn_all, logits_t = _merge(
            o_p, o_s, yb_p.reshape(rows_p, d), yb_s.reshape(rows_s, d), proj, hp, hs,
            w_branch_a[l].astype(BF16), w_out[l].astype(BF16), norm_ffn_g[l].reshape(1, d), wr, br)

        ids, wts, ranks, cnt = _route(logits_t, moe_groups, experts)
        items, starts = _work_items(cnt[:, 0].astype(I32), n_sorted)
        pos = _lookup(starts, ids) + ranks
        pos3 = pos.T.reshape(-1)
        xs = _dispatch(pos3, xn_all)
        ys = _experts(items, xs, w_exp_gate[l], w_exp_up[l], w_exp_down[l])

        last = l == depth - 1
        g_out = norm_final_g.reshape(1, d)
        hp = _combine(pos3, h_all, wts.T, g_out, ys, rows_p, 0, last)
        hs = _combine(pos3, h_all, wts.T, g_out, ys, rows_s, rows_p, last)

        hg_p.append(hgp)
        hg_s.append(hgs)
        re_p.append(fr_p.reshape(bp, s5_groups, s5_state))
        im_p.append(fi_p.reshape(bp, s5_groups, s5_state))
        re_s.append(fr_s.reshape(bs, s5_groups, s5_state))
        im_s.append(fi_s.reshape(bs, s5_groups, s5_state))

    y_prompt = hp.reshape(bp, lp, d).astype(x_prompt.dtype)
    y_sample = hs.reshape(bs, ls, d).astype(x_sample.dtype)
    return (y_prompt, y_sample, jnp.stack(hg_p), jnp.stack(re_p), jnp.stack(im_p),
            jnp.stack(hg_s), jnp.stack(re_s), jnp.stack(im_s))
```

```python
import functools

import jax
import jax.numpy as jnp
from jax import lax
from jax.experimental import pallas as pl
from jax.experimental.pallas import tpu as pltpu

F32 = jnp.float32
BF16 = jnp.bfloat16
I32 = jnp.int32

RMS_EPS = 1e-6
HG_CHUNK = 64
MOE_TOP_K = 2

V7X_VMEM_BYTES = 64 * 1024 * 1024
VMEM_LIMIT_BYTES = V7X_VMEM_BYTES - 8 * 1024 * 1024
SUBLANES = 8
LANES = 128

TOKEN_TILE = 512
EXPERT_TILE = 512
EXPERT_WINDOWS = (EXPERT_TILE, EXPERT_TILE // 2, EXPERT_TILE // 4)
S5_TIME_TILE = 64
HGRN_TIME_TILE = 256
HGRN_SEQ_TILE = 8
HGRN_CHUNK_UNROLL = 4
HGRN_SEQ_UNROLL = 8
PROJ_COL_TILE = 512


def _cparams(sem):
    return pltpu.CompilerParams(dimension_semantics=sem, vmem_limit_bytes=VMEM_LIMIT_BYTES)


def _resident(shape):
    nd = len(shape)
    return pl.BlockSpec(shape, lambda *_: (0,) * nd, pipeline_mode=pl.Buffered(1))


def _rmsnorm(x, g):
    return x * lax.rsqrt(jnp.mean(x * x, axis=-1, keepdims=True) + RMS_EPS) * g


def _two_source_specs(tm, width, n_first):
    return [pl.BlockSpec((tm, width), lambda i: (jnp.minimum(i, n_first - 1), 0)),
            pl.BlockSpec((tm, width), lambda i: (jnp.maximum(i - n_first, 0), 0))]


def _pick(first_ref, second_ref, n_first):
    return jnp.where(pl.program_id(0) < n_first, first_ref[...], second_ref[...])


def _store_token_tiles(ref, x, lead=()):
    rows = x.shape[0]
    for c in range(SUBLANES):
        ref[lead + (pl.ds(c, rows, stride=SUBLANES), slice(None))] = x[:, c * LANES:(c + 1) * LANES]


def _load_token_tiles(ref, rows, lead=()):
    return jnp.concatenate([ref[lead + (pl.ds(c, rows, stride=SUBLANES), slice(None))] for c in range(SUBLANES)],
                           axis=-1)


def _inproj_body(xp_ref, xs_ref, g_ref, w_ref, o_ref, *, n_first):
    xb = _rmsnorm(_pick(xp_ref, xs_ref, n_first), g_ref[...]).astype(BF16)
    for j in range(0, w_ref.shape[1], PROJ_COL_TILE):
        o_ref[:, j:j + PROJ_COL_TILE] = jnp.dot(xb, w_ref[:, j:j + PROJ_COL_TILE], preferred_element_type=F32)


def _in_proj(xp, xs, g, w):
    d = xp.shape[1]
    n = w.shape[1]
    tm = TOKEN_TILE
    total = xp.shape[0] + xs.shape[0]
    n_first = xp.shape[0] // tm
    return pl.pallas_call(
        functools.partial(_inproj_body, n_first=n_first), grid=(total // tm,),
        in_specs=_two_source_specs(tm, d, n_first) + [_resident((1, d)), _resident((d, n))],
        out_specs=pl.BlockSpec((tm, n), lambda i: (i, 0)),
        out_shape=jax.ShapeDtypeStruct((total, n), F32),
        compiler_params=_cparams(("parallel",)), name="in_proj")(xp, xs, g, w)


def _s5_body(*refs, tt, nstate):
    u_refs = refs[:SUBLANES]
    (h0r_ref, h0i_ref, ar_ref, ai_ref, bb_ref, cc_ref, d_ref, wg_ref, bg_ref,
     y_ref, hr_out, hi_out, hr_scr, hi_scr, bu_scr, u_scr, y_scr) = refs[SUBLANES:]
    j = pl.program_id(1)
    half = nstate // 2
    w = u_refs[0].shape[-1]
    kw = w // 2

    @pl.when(j == 0)
    def _():
        hr_scr[...] = h0r_ref[0]
        hi_scr[...] = h0i_ref[0]

    for b in range(SUBLANES):
        ub = u_refs[b][...]
        for s in range(w // LANES):
            u_scr[s, pl.ds(b, tt, stride=SUBLANES), :] = ub[:, s * LANES:(s + 1) * LANES]
    u = jnp.concatenate([u_scr[s] for s in range(w // LANES)], axis=-1)
    ub16 = u.astype(BF16)
    for kt in range(2):
        ukt = ub16[:, kt * kw:(kt + 1) * kw]
        bu_scr[:, kt * half:(kt + 1) * half] = jnp.dot(ukt, bb_ref[kt, :, :half], preferred_element_type=F32)
        bu_scr[:, nstate + kt * half:nstate + (kt + 1) * half] = jnp.dot(
            ukt, bb_ref[kt, :, half:], preferred_element_type=F32)

    lane_chunk = 512
    for lc in range(nstate // lane_chunk):
        lo = lc * lane_chunk
        re_sl = slice(lo, lo + lane_chunk)
        im_sl = slice(nstate + lo, nstate + lo + lane_chunk)
        ar = jnp.broadcast_to(ar_ref[:, re_sl], (SUBLANES, lane_chunk))
        ai = jnp.broadcast_to(ai_ref[:, re_sl], (SUBLANES, lane_chunk))

        hr, hi = hr_scr[:, re_sl], hi_scr[:, re_sl]
        for t in range(tt):
            rs = slice(t * SUBLANES, (t + 1) * SUBLANES)
            hr, hi = (ar * hr - ai * hi + bu_scr[rs, re_sl], ar * hi + ai * hr + bu_scr[rs, im_sl])
            bu_scr[rs, re_sl] = hr
            bu_scr[rs, im_sl] = hi
        hr_scr[:, re_sl] = hr
        hi_scr[:, re_sl] = hi

    ys = []
    for n in range(2):
        h_re = bu_scr[:, n * half:(n + 1) * half].astype(BF16)
        h_im = bu_scr[:, nstate + n * half:nstate + (n + 1) * half].astype(BF16)
        ys.append(jnp.dot(h_re, cc_ref[n, :half, :], preferred_element_type=F32)
                  + jnp.dot(h_im, cc_ref[n, half:, :], preferred_element_type=F32))
    y = jnp.concatenate(ys, axis=-1) + d_ref[...] * u
    z = jnp.dot(jax.nn.gelu(y).astype(BF16), wg_ref[...], preferred_element_type=F32) + bg_ref[...]
    dm = z.shape[-1] // 2
    yb = z[:, :dm] * jax.nn.sigmoid(z[:, dm:])
    for s in range(dm // LANES):
        y_scr[s] = yb[:, s * LANES:(s + 1) * LANES]
    for b in range(SUBLANES):
        for s in range(dm // LANES):
            y_ref[b, :, s * LANES:(s + 1) * LANES] = y_scr[s, pl.ds(b, tt, stride=SUBLANES), :]

    @pl.when(j == pl.num_programs(1) - 1)
    def _():
        hr_out[0] = hr_scr[...]
        hi_out[0] = hi_scr[...]


def _s5_branch(proj, u_col, width, batch, seq, row_off, h0r, h0i, ar, ai, bb, cc, d_skip, w_glu, b_glu):
    nstate = ar.shape[-1]
    dm = w_glu.shape[1] // 2
    tt = min(S5_TIME_TILE, seq)
    nj = seq // tt
    nbb = batch // SUBLANES
    body = functools.partial(_s5_body, tt=tt, nstate=nstate)

    def u_spec(b):
        return pl.BlockSpec((tt, width), lambda bb_, j, b=b: (row_off // tt + (bb_ * SUBLANES + b) * nj + j,
                                                               u_col // width))

    state_spec = pl.BlockSpec((1, SUBLANES, nstate), lambda bb_, j: (bb_, 0, 0))
    return pl.pallas_call(
        body, grid=(nbb, nj),
        in_specs=[u_spec(b) for b in range(SUBLANES)] + [
            state_spec, state_spec, _resident(ar.shape), _resident(ai.shape), _resident(bb.shape),
            _resident(cc.shape), _resident(d_skip.shape), _resident(w_glu.shape), _resident(b_glu.shape)],
        out_specs=[pl.BlockSpec((SUBLANES, tt, dm), lambda bb_, j: (bb_, j, 0)), state_spec, state_spec],
        out_shape=[jax.ShapeDtypeStruct((batch, seq, dm), F32),
                   jax.ShapeDtypeStruct((nbb, SUBLANES, nstate), F32),
                   jax.ShapeDtypeStruct((nbb, SUBLANES, nstate), F32)],
        scratch_shapes=[pltpu.VMEM((SUBLANES, nstate), F32), pltpu.VMEM((SUBLANES, nstate), F32),
                        pltpu.VMEM((tt * SUBLANES, 2 * nstate), F32),
                        pltpu.VMEM((width // LANES, tt * SUBLANES, LANES), F32),
                        pltpu.VMEM((dm // LANES, tt * SUBLANES, LANES), F32)],
        compiler_params=_cparams(("parallel", "arbitrary")), name="s5_branch")(
            *([proj] * SUBLANES), h0r, h0i, ar, ai, bb, cc, d_skip, w_glu, b_glu)


def _cumsum_rows(x, c):
    row = lax.broadcasted_iota(I32, x.shape, 0) & (c - 1)
    s = 1
    while s < c:
        x = x + jnp.where(row >= s, pltpu.roll(x, s, axis=0), 0.0)
        s *= 2
    return x


def _hgrn_gates(q, fr, lb, scale, c):
    rows, n = q.shape
    f = lb + (1.0 - lb) * jax.nn.sigmoid(fr)
    k = 1.0 - f
    b = _cumsum_rows(jnp.log(f), c)
    b3 = b.reshape(rows // c, c, n)
    b_last = jnp.broadcast_to(b3[:, c - 1:c, :], b3.shape).reshape(rows, n)
    q_dec = (q * scale) * jnp.exp(b)
    k_dec = k * jnp.exp(-b)
    k_end = k * jnp.exp(b_last - b)
    return q_dec, k_dec, k_end, jnp.exp(b_last)


def _causal_scores(q_dec, k_dec):
    c = q_dec.shape[0]
    s = lax.dot_general(q_dec, k_dec, (((1,), (1,)), ((), ())), preferred_element_type=F32)
    keep = lax.broadcasted_iota(I32, (c, c), 0) >= lax.broadcasted_iota(I32, (c, c), 1)
    return jnp.where(keep, s, 0.0).astype(BF16)


def _gated_out(o, gn, og):
    o = o * lax.rsqrt(jnp.mean(o * o, axis=-1, keepdims=True) + RMS_EPS) * gn
    return (o * jax.nn.silu(og)).astype(BF16)


def _hgrn_long_body(q_ref, f_ref, v_ref, og_ref, lb_ref, gn_ref, o_ref, sfin_ref, st_scr, *, c, heads, dk, scale):
    j = pl.program_id(1)

    @pl.when(j == 0)
    def _():
        st_scr[...] = jnp.zeros_like(st_scr)

    def chunk(ci, carry):
        rs = pl.ds(pl.multiple_of(ci * c, c), c)
        for h in range(heads):
            hs = slice(h * dk, (h + 1) * dk)
            q_dec, k_dec, k_end, decay = _hgrn_gates(q_ref[rs, hs], f_ref[rs, hs], lb_ref[:, hs], scale, c)
            q_dec = q_dec.astype(BF16)
            v = v_ref[rs, hs].astype(BF16)
            scores = _causal_scores(q_dec, k_dec.astype(BF16))
            st = st_scr[h]
            o = (lax.dot_general(q_dec, st.astype(BF16), (((1,), (1,)), ((), ())), preferred_element_type=F32)
                 + jnp.dot(scores, v, preferred_element_type=F32))
            st_scr[h] = decay[:1] * st + lax.dot_general(
                v, k_end.astype(BF16), (((0,), (0,)), ((), ())), preferred_element_type=F32)
            o_ref[rs, hs] = _gated_out(o, gn_ref[:, hs], og_ref[rs, hs])
        return carry

    lax.fori_loop(0, q_ref.shape[0] // c, chunk, 0, unroll=HGRN_CHUNK_UNROLL)

    @pl.when(j == pl.num_programs(1) - 1)
    def _():
        for h in range(heads):
            sfin_ref[0, h] = st_scr[h].T


def _hgrn_long(proj, lb, gn, batch, seq, heads, dk, row_off):
    width = heads * dk
    tb = min(HGRN_TIME_TILE, seq)
    nj = seq // tb
    off = row_off // tb
    c = min(HG_CHUNK, seq)
    body = functools.partial(_hgrn_long_body, c=c, heads=heads, dk=dk, scale=dk ** -0.5)

    def col(k):
        return pl.BlockSpec((tb, width), lambda b, j, k=k: (off + b * nj + j, k))

    return pl.pallas_call(
        body, grid=(batch, nj),
        in_specs=[col(0), col(1), col(2), col(3), _resident(lb.shape), _resident(gn.shape)],
        out_specs=[pl.BlockSpec((tb, width), lambda b, j: (b * nj + j, 0)),
                   pl.BlockSpec((1, heads, dk, dk), lambda b, j: (b, 0, 0, 0))],
        out_shape=[jax.ShapeDtypeStruct((batch * seq, width), BF16),
                   jax.ShapeDtypeStruct((batch, heads, dk, dk), F32)],
        scratch_shapes=[pltpu.VMEM((heads, dk, dk), F32)],
        compiler_params=_cparams(("parallel", "arbitrary")), name="hgrn_long")(proj, proj, proj, proj, lb, gn)


def _hgrn_short_body(q_ref, f_ref, v_ref, og_ref, lb_ref, gn_ref, s0_ref, o_ref, snew_ref, *, c, heads, dk, scale):
    def one_seq(sq, carry):
        rs = pl.ds(pl.multiple_of(sq * c, c), c)
        for h in range(heads):
            hs = slice(h * dk, (h + 1) * dk)
            q_dec, k_dec, k_end, decay = _hgrn_gates(q_ref[rs, hs], f_ref[rs, hs], lb_ref[:, hs], scale, c)
            q_dec = q_dec.astype(BF16)
            v = v_ref[rs, hs].astype(BF16)
            scores = _causal_scores(q_dec, k_dec.astype(BF16))
            s0 = s0_ref[sq, h]
            o = (jnp.dot(q_dec, s0.astype(BF16), preferred_element_type=F32)
                 + jnp.dot(scores, v, preferred_element_type=F32))
            decay_col = jnp.broadcast_to(decay[:1], (dk, dk)).T
            snew_ref[sq, h] = decay_col * s0 + lax.dot_general(
                k_end.astype(BF16), v, (((0,), (0,)), ((), ())), preferred_element_type=F32)
            o_ref[rs, hs] = _gated_out(o, gn_ref[:, hs], og_ref[rs, hs])
        return carry

    lax.fori_loop(0, s0_ref.shape[0], one_seq, 0, unroll=HGRN_SEQ_UNROLL)


def _hgrn_short(proj, lb, gn, s0, seq, row_off):
    batch, heads, dk, _ = s0.shape
    width = heads * dk
    nb = HGRN_SEQ_TILE
    rows = nb * seq
    off = row_off // rows
    body = functools.partial(_hgrn_short_body, c=seq, heads=heads, dk=dk, scale=dk ** -0.5)

    def col(k):
        return pl.BlockSpec((rows, width), lambda i, k=k: (off + i, k))

    state_spec = pl.BlockSpec((nb, heads, dk, dk), lambda i: (i, 0, 0, 0))
    return pl.pallas_call(
        body, grid=(batch // nb,),
        in_specs=[col(0), col(1), col(2), col(3), _resident(lb.shape), _resident(gn.shape), state_spec],
        out_specs=[pl.BlockSpec((rows, width), lambda i: (i, 0)), state_spec],
        out_shape=[jax.ShapeDtypeStruct((batch * seq, width), BF16), jax.ShapeDtypeStruct(s0.shape, F32)],
        compiler_params=_cparams(("parallel",)), name="hgrn_short")(proj, proj, proj, proj, lb, gn, s0)


def _merge_body(op_ref, os_ref, ybp_ref, ybs_ref, ga_ref, gb_ref, xp_ref, xs_ref, wa_ref, wo_ref, gf_ref, wr_ref,
                br_ref, h_ref, xn_ref, lg_ref, *, n_first):
    y_a = jnp.dot(_pick(op_ref, os_ref, n_first), wa_ref[...], preferred_element_type=F32)
    merged = jax.nn.sigmoid(ga_ref[...]) * y_a + jax.nn.sigmoid(gb_ref[...]) * _pick(ybp_ref, ybs_ref, n_first)
    h = _pick(xp_ref, xs_ref, n_first) + jnp.dot(merged.astype(BF16), wo_ref[...], preferred_element_type=F32)
    h_ref[...] = h
    xn = _rmsnorm(h, gf_ref[...])
    _store_token_tiles(xn_ref, xn)
    lg_ref[...] = lax.dot_general(wr_ref[...], xn.astype(BF16), (((1,), (1,)), ((), ())),
                                  preferred_element_type=F32) + br_ref[...]


def _merge(o_p, o_s, yb_p, yb_s, proj, xp, xs, wa, wo, gf, wr, br):
    d = xp.shape[1]
    total = proj.shape[0]
    nr = wr.shape[0]
    tm = TOKEN_TILE
    n_first = xp.shape[0] // tm
    pair = _two_source_specs(tm, d, n_first)

    def row(k=0):
        return pl.BlockSpec((tm, d), lambda i, k=k: (i, k))

    return pl.pallas_call(
        functools.partial(_merge_body, n_first=n_first), grid=(total // tm,),
        in_specs=pair + pair + [row(4), row(5)] + pair + [
            _resident(wa.shape), _resident(wo.shape), _resident(gf.shape), _resident(wr.shape), _resident(br.shape)],
        out_specs=[row(), pl.BlockSpec((tm * SUBLANES, LANES), lambda i: (i, 0)),
                   pl.BlockSpec((nr, tm), lambda i: (0, i))],
        out_shape=[jax.ShapeDtypeStruct((total, d), F32), jax.ShapeDtypeStruct((total * SUBLANES, LANES), F32),
                   jax.ShapeDtypeStruct((nr, total), F32)],
        compiler_params=_cparams(("parallel",)), name="merge_out")(
            o_p, o_s, yb_p, yb_s, proj, proj, xp, xs, wa, wo, gf, wr, br)


def _first_index_of_max(vals):
    m = vals[0]
    for v in vals[1:]:
        m = jnp.maximum(m, v)
    idx = jnp.full(m.shape, len(vals), I32)
    for e in range(len(vals) - 1, -1, -1):
        idx = jnp.where(vals[e] == m, e, idx)
    return m, idx


def _route_body(lg_ref, ids_ref, w_ref, rk_ref, cnt_ref, carry_scr, *, groups, experts):
    i = pl.program_id(0)
    tile = lg_ref.shape[1]
    n_exp = groups * experts

    @pl.when(i == 0)
    def _():
        carry_scr[...] = jnp.zeros_like(carry_scr)

    gl = [lg_ref[g:g + 1, :] for g in range(groups)]
    gmax, gidx = _first_index_of_max(gl)
    denom = jnp.exp(gl[0] - gmax)
    for g in range(1, groups):
        denom = denom + jnp.exp(gl[g] - gmax)
    g_w = 1.0 / denom

    el = []
    for e in range(experts):
        v = lg_ref[groups + e:groups + e + 1, :]
        for g in range(1, groups):
            r = groups + g * experts + e
            v = jnp.where(gidx == g, lg_ref[r:r + 1, :], v)
        el.append(v)
    v1, i1 = _first_index_of_max(el)
    rest = [jnp.where(i1 == e, -jnp.inf, el[e]) for e in range(experts)]
    v2, i2 = _first_index_of_max(rest)
    t = jnp.exp(v2 - v1)
    inv = 1.0 / (1.0 + t)
    e1 = gidx * experts + i1
    e2 = gidx * experts + i2

    erow = lax.broadcasted_iota(I32, (n_exp, tile), 0)
    oh1 = (erow == e1).astype(F32)
    oh2 = (erow == e2).astype(F32)
    oh = oh1 + oh2
    before = (lax.broadcasted_iota(I32, (tile, tile), 0) < lax.broadcasted_iota(I32, (tile, tile), 1))
    cnt = jnp.dot(oh.astype(BF16), before.astype(BF16), preferred_element_type=F32) + carry_scr[:, 0:1]
    ids_ref[0:1, :] = e1
    ids_ref[1:2, :] = e2
    w_ref[0:1, :] = inv * g_w
    w_ref[1:2, :] = (t * inv) * g_w
    rk_ref[0:1, :] = jnp.sum(oh1 * cnt, axis=0, keepdims=True).astype(I32)
    rk_ref[1:2, :] = jnp.sum(oh2 * cnt, axis=0, keepdims=True).astype(I32)
    carry_scr[...] = carry_scr[...] + jnp.sum(oh, axis=1, keepdims=True)

    @pl.when(i == pl.num_programs(0) - 1)
    def _():
        cnt_ref[...] = carry_scr[...]


def _route(logits_t, groups, experts):
    nr, total = logits_t.shape
    tile = TOKEN_TILE
    n_exp = groups * experts
    body = functools.partial(_route_body, groups=groups, experts=experts)
    pair = pl.BlockSpec((MOE_TOP_K, tile), lambda i: (0, i))
    return pl.pallas_call(
        body, grid=(total // tile,),
        in_specs=[pl.BlockSpec((nr, tile), lambda i: (0, i))],
        out_specs=[pair, pair, pair, pl.BlockSpec((n_exp, LANES), lambda i: (0, 0))],
        out_shape=[jax.ShapeDtypeStruct((MOE_TOP_K, total), I32), jax.ShapeDtypeStruct((MOE_TOP_K, total), F32),
                   jax.ShapeDtypeStruct((MOE_TOP_K, total), I32), jax.ShapeDtypeStruct((n_exp, LANES), F32)],
        scratch_shapes=[pltpu.VMEM((n_exp, LANES), F32)],
        compiler_params=_cparams(("arbitrary",)), name="route")(logits_t)


def _row_copy(src, dst, sem):
    return pltpu.make_async_copy(src, dst, sem)


def _token_rows(r):
    return pl.ds(pl.multiple_of(r * SUBLANES, SUBLANES), SUBLANES)


def _dispatch_body(pos_ref, x_ref, o_hbm, ring, zero_scr, sem, pad_sem, *, n_sorted):
    i = pl.program_id(0)
    tile = x_ref.shape[0] // SUBLANES
    par = lax.rem(i, 2)

    @pl.when(i == 0)
    def _():
        zero_scr[...] = jnp.zeros_like(zero_scr)
        pad = _row_copy(zero_scr, o_hbm.at[pl.ds(n_sorted * SUBLANES, zero_scr.shape[0])], pad_sem.at[0])
        pad.start()
        pad.wait()

    ring[par] = x_ref[...]

    def issue(r, carry):
        for k in range(MOE_TOP_K):
            p = pos_ref[MOE_TOP_K * r + k]
            _row_copy(ring.at[par, _token_rows(r)], o_hbm.at[_token_rows(p)], sem.at[par, k]).start(priority=k)
        return carry

    lax.fori_loop(0, tile, issue, 0, unroll=8)

    def drain(slot):
        for k in range(MOE_TOP_K):
            _row_copy(ring.at[slot], o_hbm.at[pl.ds(0, tile * SUBLANES)], sem.at[slot, k]).wait()

    @pl.when(i > 0)
    def _():
        drain(1 - par)

    @pl.when(i == pl.num_programs(0) - 1)
    def _():
        drain(par)


def _dispatch(pos3, xn_tiles):
    total = xn_tiles.shape[0] // SUBLANES
    tile = TOKEN_TILE
    n_sorted = total * MOE_TOP_K
    pad = EXPERT_WINDOWS[-1]
    return pl.pallas_call(
        functools.partial(_dispatch_body, n_sorted=n_sorted), grid=(total // tile,),
        in_specs=[pl.BlockSpec((MOE_TOP_K * tile,), lambda i: (i,), memory_space=pltpu.SMEM),
                  pl.BlockSpec((tile * SUBLANES, LANES), lambda i: (i, 0))],
        out_specs=pl.BlockSpec(memory_space=pl.ANY),
        out_shape=jax.ShapeDtypeStruct(((n_sorted + pad) * SUBLANES, LANES), F32),
        scratch_shapes=[pltpu.VMEM((2, tile * SUBLANES, LANES), F32), pltpu.VMEM((pad * SUBLANES, LANES), F32),
                        pltpu.SemaphoreType.DMA((2, MOE_TOP_K)), pltpu.SemaphoreType.DMA((1,))],
        compiler_params=_cparams(("arbitrary",)), name="dispatch")(pos3, xn_tiles)


def _experts_body(it_exp, it_row, it_cls, it_first, it_next, n_items, xs_hbm, wg_hbm, wu_hbm, wd_hbm, ys_hbm,
                  xbuf, ybuf, wg_s, wu_s, wd_s, wg_b, wu_b, wd_b, sem_in, sem_out, sem_w):
    j = pl.program_id(0)
    n = n_items[0]
    pad = EXPERT_WINDOWS[-1]
    slot = lax.rem(j, 2)

    def weight_copies(e, s):
        return [pltpu.make_async_copy(hbm.at[e], stage.at[s], sem_w.at[s, t])
                for t, (hbm, stage) in enumerate(((wg_hbm, wg_s), (wu_hbm, wu_s), (wd_hbm, wd_s)))]

    def by_size(item, fn):
        for ci, m in enumerate(EXPERT_WINDOWS):
            pl.when(it_cls[item] == ci)(functools.partial(fn, m))

    def window(item, m):
        return pl.ds(pl.multiple_of(it_row[item] * SUBLANES, SUBLANES), m * SUBLANES)

    def in_copy(item, s, m):
        return pltpu.make_async_copy(xs_hbm.at[window(item, m)], xbuf.at[s, pl.ds(0, m * SUBLANES)], sem_in.at[s])

    def out_copy(item, s, m):
        return pltpu.make_async_copy(ybuf.at[s, pl.ds(0, m * SUBLANES)], ys_hbm.at[window(item, m)], sem_out.at[s])

    def compute(m):
        x = _load_token_tiles(xbuf, m, (slot,)).astype(BF16)
        hg = jnp.dot(x, wg_b[...], preferred_element_type=F32)
        hu = jnp.dot(x, wu_b[...], preferred_element_type=F32)
        hid = (jax.nn.silu(hg) * hu).astype(BF16)
        _store_token_tiles(ybuf, jnp.dot(hid, wd_b[...], preferred_element_type=F32), (slot,))

    @pl.when(j < n)
    def _():
        @pl.when(j == 0)
        def _():
            by_size(0, lambda m: in_copy(0, 0, m).start())
            for c in weight_copies(it_exp[0], it_first[0] - 1):
                c.start()
            tail_rows = pl.ds(0, pad * SUBLANES)
            ybuf[1, tail_rows, :] = jnp.zeros((pad * SUBLANES, LANES), F32)
            tail = pltpu.make_async_copy(
                ybuf.at[1, tail_rows], ys_hbm.at[pl.ds(ys_hbm.shape[0] - pad * SUBLANES, pad * SUBLANES)],
                sem_out.at[1])
            tail.start()
            tail.wait()

        @pl.when(j + 1 < n)
        def _():
            by_size(j + 1, lambda m: in_copy(j + 1, 1 - slot, m).start())

        @pl.when(it_first[j] > 0)
        def _():
            s = it_first[j] - 1
            for c in weight_copies(it_exp[j], s):
                c.wait()
            wg_b[...] = wg_s[s].astype(BF16)
            wu_b[...] = wu_s[s].astype(BF16)
            wd_b[...] = wd_s[s].astype(BF16)

            @pl.when(it_next[j] >= 0)
            def _():
                for c in weight_copies(it_next[j], 1 - s):
                    c.start()

        by_size(j, lambda m: in_copy(j, slot, m).wait())
        by_size(j, compute)

        @pl.when(j > 0)
        def _():
            by_size(j - 1, lambda m: out_copy(j - 1, 1 - slot, m).wait())

        by_size(j, lambda m: out_copy(j, slot, m).start())

        @pl.when(j == n - 1)
        def _():
            by_size(j, lambda m: out_copy(j, slot, m).wait())


def _experts(items, xs, wg, wu, wd):
    d, de = wg.shape[1], wg.shape[2]
    tm = EXPERT_TILE
    max_items = items[0].shape[0]
    grid_spec = pltpu.PrefetchScalarGridSpec(
        num_scalar_prefetch=6, grid=(max_items,),
        in_specs=[pl.BlockSpec(memory_space=pl.ANY)] * 4,
        out_specs=pl.BlockSpec(memory_space=pl.ANY),
        scratch_shapes=[pltpu.VMEM((2, tm * SUBLANES, LANES), F32), pltpu.VMEM((2, tm * SUBLANES, LANES), F32),
                        pltpu.VMEM((2, d, de), F32), pltpu.VMEM((2, d, de), F32), pltpu.VMEM((2, de, d), F32),
                        pltpu.VMEM((d, de), BF16), pltpu.VMEM((d, de), BF16), pltpu.VMEM((de, d), BF16),
                        pltpu.SemaphoreType.DMA((2,)), pltpu.SemaphoreType.DMA((2,)),
                        pltpu.SemaphoreType.DMA((2, 3))])
    return pl.pallas_call(
        _experts_body, grid_spec=grid_spec, out_shape=jax.ShapeDtypeStruct(xs.shape, F32),
        compiler_params=_cparams(("arbitrary",)), name="experts")(*items, xs, wg, wu, wd)


def _combine_body(pos_ref, pos_next_ref, h_ref, w_ref, g_ref, ys_hbm, y_ref, buf, sem, *, final_norm):
    i = pl.program_id(0)
    tile = h_ref.shape[0]
    par = lax.rem(i, 2)

    def gather(table, slot):
        def issue(r, carry):
            for k in range(MOE_TOP_K):
                p = table[MOE_TOP_K * r + k]
                _row_copy(ys_hbm.at[_token_rows(p)], buf.at[slot, k, _token_rows(r)],
                          sem.at[slot, k]).start(priority=k)
            return carry

        lax.fori_loop(0, tile, issue, 0, unroll=8)

    @pl.when(i == 0)
    def _():
        gather(pos_ref, 0)

    @pl.when(i + 1 < pl.num_programs(0))
    def _():
        gather(pos_next_ref, 1 - par)

    for k in range(MOE_TOP_K):
        _row_copy(ys_hbm.at[pl.ds(0, tile * SUBLANES)], buf.at[par, k], sem.at[par, k]).wait()
    h = h_ref[...] + (w_ref[:, 0:1] * _load_token_tiles(buf, tile, (par, 0))
                      + w_ref[:, 1:2] * _load_token_tiles(buf, tile, (par, 1)))
    y_ref[...] = _rmsnorm(h, g_ref[...]) if final_norm else h


def _combine(pos3, h_all, w_t, g, ys, rows, row_off, final_norm):
    d = h_all.shape[1]
    tile = TOKEN_TILE
    off = row_off // tile
    last_block = h_all.shape[0] // tile - 1
    return pl.pallas_call(
        functools.partial(_combine_body, final_norm=final_norm), grid=(rows // tile,),
        in_specs=[pl.BlockSpec((MOE_TOP_K * tile,), lambda i: (off + i,), memory_space=pltpu.SMEM),
                  pl.BlockSpec((MOE_TOP_K * tile,), lambda i: (jnp.minimum(off + i + 1, last_block),),
                               memory_space=pltpu.SMEM),
                  pl.BlockSpec((tile, d), lambda i: (off + i, 0)),
                  pl.BlockSpec((tile, MOE_TOP_K), lambda i: (off + i, 0)),
                  _resident(g.shape),
                  pl.BlockSpec(memory_space=pl.ANY)],
        out_specs=pl.BlockSpec((tile, d), lambda i: (i, 0)),
        out_shape=jax.ShapeDtypeStruct((rows, d), F32),
        scratch_shapes=[pltpu.VMEM((2, MOE_TOP_K, tile * SUBLANES, LANES), F32),
                        pltpu.SemaphoreType.DMA((2, MOE_TOP_K))],
        compiler_params=_cparams(("arbitrary",)), name="combine")(pos3, pos3, h_all, w_t, g, ys)


def _lookup(table, idx):
    sel = idx[None] == jnp.arange(table.shape[0], dtype=I32).reshape((-1,) + (1,) * idx.ndim)
    return jnp.sum(jnp.where(sel, table.reshape(sel.shape[:1] + (1,) * idx.ndim), 0), axis=0)


def _work_items(counts, n_sorted):
    big, mid, small = EXPERT_WINDOWS
    n_exp = counts.shape[0]
    max_items = n_sorted // big + 2 * n_exp
    ends = jnp.cumsum(counts)
    starts = ends - counts
    units = (counts % big + small - 1) // small
    n_big = counts // big + (units == big // small)
    units = jnp.where(units == big // small, 0, units)
    n_mid = units // (mid // small)
    n_e = n_big + n_mid + units % (mid // small)
    item_end = jnp.cumsum(n_e)
    item_start = item_end - n_e
    n_items = item_end[-1]
    j = jnp.minimum(jnp.arange(max_items, dtype=I32), n_items - 1)
    e = jnp.sum((item_end[None, :] <= j[:, None]).astype(I32), axis=1)
    k = j - _lookup(item_start, e)
    nb, nm = _lookup(n_big, e), _lookup(n_mid, e)
    cls = jnp.where(k < nb, 0, jnp.where(k < nb + nm, 1, 2))
    row = _lookup(starts, e) + jnp.where(cls == 0, k * big, nb * big + jnp.where(cls == 1, 0, nm * mid))
    ordinal = jnp.cumsum((n_e > 0).astype(I32)) - 1
    first = jnp.where(k == 0, 1 + _lookup(ordinal, e) % 2, 0)
    nxt_item = _lookup(item_end, e)
    nxt = jnp.where(nxt_item < n_items, jnp.sum((item_end[None, :] <= nxt_item[:, None]).astype(I32), axis=1), -1)
    return (e, row.astype(I32), cls.astype(I32), first.astype(I32), nxt.astype(I32),
            n_items.reshape(1).astype(I32)), starts


def _s5_discretise(lam_re, lam_im, log_dt, b_re, b_im, c_re, c_im):
    g, p = lam_re.shape
    ch = b_re.shape[-1]
    lam_re = lam_re.astype(F32)
    lam_im = lam_im.astype(F32)
    dt = jnp.exp(log_dt.astype(F32))[:, None]
    mag = jnp.exp(lam_re * dt)
    ab_re = mag * jnp.cos(lam_im * dt)
    ab_im = mag * jnp.sin(lam_im * dt)
    den = lam_re * lam_re + lam_im * lam_im
    nr = ab_re - 1.0
    coef_re = (nr * lam_re + ab_im * lam_im) / den
    coef_im = (ab_im * lam_re - nr * lam_im) / den
    bb_re = coef_re[..., None] * b_re - coef_im[..., None] * b_im
    bb_im = coef_re[..., None] * b_im + coef_im[..., None] * b_re
    gh = g // 2
    eye = jnp.eye(gh, dtype=F32)

    def in_block(m):
        return jnp.einsum("gpc,gh->gchp", m, eye).reshape(gh * ch, gh * p)

    def out_block(m):
        return jnp.einsum("gcp,gh->gphc", m, eye).reshape(gh * p, gh * ch)

    bb = jnp.stack([jnp.concatenate([in_block(bb_re[k * gh:(k + 1) * gh]), in_block(bb_im[k * gh:(k + 1) * gh])],
                                    axis=1) for k in range(2)]).astype(BF16)
    cc = jnp.stack([jnp.concatenate([out_block(c_re[k * gh:(k + 1) * gh]), out_block(-c_im[k * gh:(k + 1) * gh])],
                                    axis=0) for k in range(2)]).astype(BF16)
    return ab_re.reshape(1, g * p), ab_im.reshape(1, g * p), bb, cc


def kernel(x_prompt, x_sample, state_hgrn, state_s5_re, state_s5_im, norm_mix_g, w_in, hgrn_lb_raw, hgrn_onorm_g, w_branch_a, s5_lambda_re, s5_lambda_im, s5_log_dt, s5_b_re, s5_b_im, s5_c_re, s5_c_im, s5_d, w_glu, b_glu, w_out, norm_ffn_g, w_router_group, b_router_group, w_router_expert, b_router_expert, w_exp_gate, w_exp_up, w_exp_down, norm_final_g):
    depth = norm_mix_g.shape[0]
    bp, lp, d = x_prompt.shape
    bs, ls, _ = x_sample.shape
    heads, dk = state_hgrn.shape[2], state_hgrn.shape[3]
    kw = heads * dk
    s5_groups, s5_state = state_s5_re.shape[2], state_s5_re.shape[3]
    s5_width = s5_d.shape[-1]
    nstate = s5_groups * s5_state
    moe_groups, _, experts = w_router_expert.shape[1:]
    n_exp = moe_groups * experts
    rows_p, rows_s = bp * lp, bs * ls
    total = rows_p + rows_s
    n_sorted = total * MOE_TOP_K
    assert kw == d and state_hgrn.shape[4] == dk, "column blocks assume key width == value width == model width"
    assert d == SUBLANES * LANES, "token-tile layout holds one token per (8, 128) tile"
    assert s5_groups % 2 == 0 and bp % SUBLANES == 0 and bs % HGRN_SEQ_TILE == 0

    lb_all = jnp.cumsum(jax.nn.softmax(hgrn_lb_raw.astype(F32), axis=0), axis=0)

    hp = x_prompt.reshape(rows_p, d)
    hs = x_sample.reshape(rows_s, d)
    hg_p, re_p, im_p, hg_s, re_s, im_s = [], [], [], [], [], []
    zeros_state = jnp.zeros((bp // SUBLANES, SUBLANES, nstate), F32)

    for l in range(depth):
        w = w_in[l]
        w_cols = jnp.concatenate([w[:, :4 * kw], w[:, 4 * kw + s5_width:], w[:, 4 * kw:4 * kw + s5_width]],
                                 axis=1).astype(BF16)
        proj = _in_proj(hp, hs, norm_mix_g[l].reshape(1, d), w_cols)

        ar, ai, bb, cc = _s5_discretise(s5_lambda_re[l], s5_lambda_im[l], s5_log_dt[l], s5_b_re[l], s5_b_im[l],
                                        s5_c_re[l], s5_c_im[l])
        s5_args = (ar, ai, bb, cc, s5_d[l].reshape(1, s5_width), w_glu[l].astype(BF16), b_glu[l].reshape(1, -1))
        yb_p, fr_p, fi_p = _s5_branch(proj, 6 * kw, s5_width, bp, lp, 0, zeros_state, zeros_state, *s5_args)
        yb_s, fr_s, fi_s = _s5_branch(proj, 6 * kw, s5_width, bs, ls, rows_p,
                                      state_s5_re[l].reshape(bs // SUBLANES, SUBLANES, nstate),
                                      state_s5_im[l].reshape(bs // SUBLANES, SUBLANES, nstate), *s5_args)

        lb = lb_all[l].reshape(1, kw)
        gn = hgrn_onorm_g[l].reshape(1, kw)
        o_p, hgp = _hgrn_long(proj, lb, gn, bp, lp, heads, dk, 0)
        o_s, hgs = _hgrn_short(proj, lb, gn, state_hgrn[l].astype(F32), ls, rows_p)

        nr = -(-(moe_groups + n_exp) // SUBLANES) * SUBLANES
        wr = jnp.concatenate([w_router_group[l].T, w_router_expert[l].transpose(0, 2, 1).reshape(n_exp, d)], axis=0)
        wr = jnp.pad(wr, ((0, nr - wr.shape[0]), (0, 0))).astype(BF16)
        br = jnp.pad(jnp.concatenate([b_router_group[l], b_router_expert[l].reshape(n_exp)]),
                     (0, nr - moe_groups - n_exp)).reshape(nr, 1).astype(F32)
        h_all, xn_all, logits_t = _merge(
            o_p, o_s, yb_p.reshape(rows_p, d), yb_s.reshape(rows_s, d), proj, hp, hs,
            w_branch_a[l].astype(BF16), w_out[l].astype(BF16), norm_ffn_g[l].reshape(1, d), wr, br)

        ids, wts, ranks, cnt = _route(logits_t, moe_groups, experts)
        items, starts = _work_items(cnt[:, 0].astype(I32), n_sorted)
        pos = _lookup(starts, ids) + ranks
        pos3 = pos.T.reshape(-1)
        xs = _dispatch(pos3, xn_all)
        ys = _experts(items, xs, w_exp_gate[l], w_exp_up[l], w_exp_down[l])

        last = l == depth - 1
        g_out = norm_final_g.reshape(1, d)
        hp = _combine(pos3, h_all, wts.T, g_out, ys, rows_p, 0, last)
        hs = _combine(pos3, h_all, wts.T, g_out, ys, rows_s, rows_p, last)

        hg_p.append(hgp)
        hg_s.append(hgs)
        re_p.append(fr_p.reshape(bp, s5_groups, s5_state))
        im_p.append(fi_p.reshape(bp, s5_groups, s5_state))
        re_s.append(fr_s.reshape(bs, s5_groups, s5_state))
        im_s.append(fi_s.reshape(bs, s5_groups, s5_state))

    y_prompt = hp.reshape(bp, lp, d).astype(x_prompt.dtype)
    y_sample = hs.reshape(bs, ls, d).astype(x_sample.dtype)
    return (y_prompt, y_sample, jnp.stack(hg_p), jnp.stack(re_p), jnp.stack(im_p),
            jnp.stack(hg_s), jnp.stack(re_s), jnp.stack(im_s))
```

```python
import functools

import jax
import jax.numpy as jnp
from jax import lax
from jax.experimental import pallas as pl
from jax.experimental.pallas import tpu as pltpu

F32 = jnp.float32
BF16 = jnp.bfloat16
I32 = jnp.int32

RMS_EPS = 1e-6
HG_CHUNK = 64
MOE_TOP_K = 2

V7X_VMEM_BYTES = 64 * 1024 * 1024
VMEM_LIMIT_BYTES = V7X_VMEM_BYTES - 8 * 1024 * 1024
SUBLANES = 8
LANES = 128

TOKEN_TILE = 512
MOE_TILE = 1024
EXPERT_TILE = 512
EXPERT_WINDOWS = (EXPERT_TILE, EXPERT_TILE // 2, EXPERT_TILE // 4)
S5_TIME_TILE = 64
S5_ROW_BLOCK = 512
HGRN_TIME_TILE = 1024
HGRN_SEQ_TILE = 8
HGRN_CHUNK_UNROLL = 16
HGRN_SEQ_UNROLL = 8
PROJ_COL_TILE = 512


def _cparams(sem):
    return pltpu.CompilerParams(dimension_semantics=sem, vmem_limit_bytes=VMEM_LIMIT_BYTES)


def _resident(shape):
    nd = len(shape)
    return pl.BlockSpec(shape, lambda *_: (0,) * nd, pipeline_mode=pl.Buffered(1))


def _rmsnorm(x, g):
    return x * lax.rsqrt(jnp.mean(x * x, axis=-1, keepdims=True) + RMS_EPS) * g


def _two_source_specs(tm, width, n_first):
    return [pl.BlockSpec((tm, width), lambda i: (jnp.minimum(i, n_first - 1), 0)),
            pl.BlockSpec((tm, width), lambda i: (jnp.maximum(i - n_first, 0), 0))]


def _pick(first_ref, second_ref, n_first):
    return jnp.where(pl.program_id(0) < n_first, first_ref[...], second_ref[...])


def _store_token_tiles(ref, x, lead=()):
    rows = x.shape[0]
    for c in range(SUBLANES):
        ref[lead + (pl.ds(c, rows, stride=SUBLANES), slice(None))] = x[:, c * LANES:(c + 1) * LANES]


def _load_token_tiles(ref, rows, lead=()):
    return jnp.concatenate([ref[lead + (pl.ds(c, rows, stride=SUBLANES), slice(None))] for c in range(SUBLANES)],
                           axis=-1)


def _inproj_body(xp_ref, xs_ref, g_ref, w_ref, o_ref, *, n_first):
    xb = _rmsnorm(_pick(xp_ref, xs_ref, n_first), g_ref[...]).astype(BF16)
    for j in range(0, w_ref.shape[1], PROJ_COL_TILE):
        o_ref[:, j:j + PROJ_COL_TILE] = jnp.dot(xb, w_ref[:, j:j + PROJ_COL_TILE], preferred_element_type=F32)


def _in_proj(xp, xs, g, w):
    d = xp.shape[1]
    n = w.shape[1]
    tm = TOKEN_TILE
    total = xp.shape[0] + xs.shape[0]
    n_first = xp.shape[0] // tm
    return pl.pallas_call(
        functools.partial(_inproj_body, n_first=n_first), grid=(total // tm,),
        in_specs=_two_source_specs(tm, d, n_first) + [_resident((1, d)), _resident((d, n))],
        out_specs=pl.BlockSpec((tm, n), lambda i: (i, 0)),
        out_shape=jax.ShapeDtypeStruct((total, n), F32),
        compiler_params=_cparams(("parallel",)), name="in_proj")(xp, xs, g, w)


def _s5_body(*refs, tt, nstate):
    x_refs = refs[:SUBLANES]
    (gm_ref, wu_ref, h0r_ref, h0i_ref, ar_ref, ai_ref, bb_ref, cc_ref, d_ref, wg_ref, bg_ref,
     y_ref, hr_out, hi_out, hr_scr, hi_scr, bu_scr, x_scr, y_scr) = refs[SUBLANES:]
    j = pl.program_id(1)
    half = nstate // 2
    d = x_refs[0].shape[-1]
    kw = wu_ref.shape[-1] // 2

    @pl.when(j == 0)
    def _():
        hr_scr[...] = h0r_ref[0]
        hi_scr[...] = h0i_ref[0]

    for b in range(SUBLANES):
        xb = x_refs[b][...]
        for s in range(d // LANES):
            x_scr[s, pl.ds(b, tt, stride=SUBLANES), :] = xb[:, s * LANES:(s + 1) * LANES]
    rows = tt * SUBLANES
    row_blocks = [slice(r, min(r + S5_ROW_BLOCK, rows)) for r in range(0, rows, S5_ROW_BLOCK)]
    us = []
    for rb in row_blocks:
        x = jnp.concatenate([x_scr[s, rb, :] for s in range(d // LANES)], axis=-1)
        u = jnp.dot(_rmsnorm(x, gm_ref[...]).astype(BF16), wu_ref[...], preferred_element_type=F32)
        us.append(u)
        ub16 = u.astype(BF16)
        for kt in range(2):
            ukt = ub16[:, kt * kw:(kt + 1) * kw]
            bu_scr[rb, kt * half:(kt + 1) * half] = jnp.dot(ukt, bb_ref[kt, :, :half], preferred_element_type=F32)
            bu_scr[rb, nstate + kt * half:nstate + (kt + 1) * half] = jnp.dot(
                ukt, bb_ref[kt, :, half:], preferred_element_type=F32)

    lane_chunk = 512
    for lc in range(nstate // lane_chunk):
        lo = lc * lane_chunk
        re_sl = slice(lo, lo + lane_chunk)
        im_sl = slice(nstate + lo, nstate + lo + lane_chunk)
        ar = jnp.broadcast_to(ar_ref[:, re_sl], (SUBLANES, lane_chunk))
        ai = jnp.broadcast_to(ai_ref[:, re_sl], (SUBLANES, lane_chunk))

        hr, hi = hr_scr[:, re_sl], hi_scr[:, re_sl]
        for t in range(tt):
            rs = slice(t * SUBLANES, (t + 1) * SUBLANES)
            hr, hi = (ar * hr - ai * hi + bu_scr[rs, re_sl], ar * hi + ai * hr + bu_scr[rs, im_sl])
            bu_scr[rs, re_sl] = hr
            bu_scr[rs, im_sl] = hi
        hr_scr[:, re_sl] = hr
        hi_scr[:, re_sl] = hi

    dm = wg_ref.shape[-1] // 2
    for rb, u in zip(row_blocks, us):
        ys = []
        for n in range(2):
            h_re = bu_scr[rb, n * half:(n + 1) * half].astype(BF16)
            h_im = bu_scr[rb, nstate + n * half:nstate + (n + 1) * half].astype(BF16)
            ys.append(jnp.dot(h_re, cc_ref[n, :half, :], preferred_element_type=F32)
                      + jnp.dot(h_im, cc_ref[n, half:, :], preferred_element_type=F32))
        y = jnp.concatenate(ys, axis=-1) + d_ref[...] * u
        z = jnp.dot(jax.nn.gelu(y).astype(BF16), wg_ref[...], preferred_element_type=F32) + bg_ref[...]
        yb = z[:, :dm] * jax.nn.sigmoid(z[:, dm:])
        for s in range(dm // LANES):
            y_scr[s, rb, :] = yb[:, s * LANES:(s + 1) * LANES]
    for b in range(SUBLANES):
        for s in range(dm // LANES):
            y_ref[b, :, s * LANES:(s + 1) * LANES] = y_scr[s, pl.ds(b, tt, stride=SUBLANES), :]

    @pl.when(j == pl.num_programs(1) - 1)
    def _():
        hr_out[0] = hr_scr[...]
        hi_out[0] = hi_scr[...]


def _s5_branch(x2d, batch, seq, g_mix, w_u, h0r, h0i, ar, ai, bb, cc, d_skip, w_glu, b_glu):
    nstate = ar.shape[-1]
    d = x2d.shape[1]
    dm = w_glu.shape[1] // 2
    tt = min(S5_TIME_TILE, seq)
    nj = seq // tt
    nbb = batch // SUBLANES
    body = functools.partial(_s5_body, tt=tt, nstate=nstate)

    def x_spec(b):
        return pl.BlockSpec((tt, d), lambda bb_, j, b=b: ((bb_ * SUBLANES + b) * nj + j, 0))

    state_spec = pl.BlockSpec((1, SUBLANES, nstate), lambda bb_, j: (bb_, 0, 0))
    return pl.pallas_call(
        body, grid=(nbb, nj),
        in_specs=[x_spec(b) for b in range(SUBLANES)] + [
            _resident(g_mix.shape), _resident(w_u.shape),
            state_spec, state_spec, _resident(ar.shape), _resident(ai.shape), _resident(bb.shape),
            _resident(cc.shape), _resident(d_skip.shape), _resident(w_glu.shape), _resident(b_glu.shape)],
        out_specs=[pl.BlockSpec((SUBLANES, tt, dm), lambda bb_, j: (bb_, j, 0)), state_spec, state_spec],
        out_shape=[jax.ShapeDtypeStruct((batch, seq, dm), F32),
                   jax.ShapeDtypeStruct((nbb, SUBLANES, nstate), F32),
                   jax.ShapeDtypeStruct((nbb, SUBLANES, nstate), F32)],
        scratch_shapes=[pltpu.VMEM((SUBLANES, nstate), F32), pltpu.VMEM((SUBLANES, nstate), F32),
                        pltpu.VMEM((tt * SUBLANES, 2 * nstate), F32),
                        pltpu.VMEM((d // LANES, tt * SUBLANES, LANES), F32),
                        pltpu.VMEM((dm // LANES, tt * SUBLANES, LANES), F32)],
        compiler_params=_cparams(("parallel", "arbitrary")), name="s5_branch")(
            *([x2d] * SUBLANES), g_mix, w_u, h0r, h0i, ar, ai, bb, cc, d_skip, w_glu, b_glu)


def _cumsum_rows(x, c):
    row = lax.broadcasted_iota(I32, x.shape, 0) & (c - 1)
    s = 1
    while s < c:
        x = x + jnp.where(row >= s, pltpu.roll(x, s, axis=0), 0.0)
        s *= 2
    return x


def _hgrn_gates(q, fr, lb, scale, c):
    rows, n = q.shape
    f = lb + (1.0 - lb) * jax.nn.sigmoid(fr)
    k = 1.0 - f
    b = _cumsum_rows(jnp.log(f), c)
    b3 = b.reshape(rows // c, c, n)
    b_last = jnp.broadcast_to(b3[:, c - 1:c, :], b3.shape).reshape(rows, n)
    q_dec = (q * scale) * jnp.exp(b)
    k_dec = k * jnp.exp(-b)
    k_end = k * jnp.exp(b_last - b)
    return q_dec, k_dec, k_end, jnp.exp(b_last)


def _causal_scores(q_dec, k_dec):
    c = q_dec.shape[0]
    s = lax.dot_general(q_dec, k_dec, (((1,), (1,)), ((), ())), preferred_element_type=F32)
    keep = lax.broadcasted_iota(I32, (c, c), 0) >= lax.broadcasted_iota(I32, (c, c), 1)
    return jnp.where(keep, s, 0.0).astype(BF16)


def _gated_out(o, gn, og):
    o = o * lax.rsqrt(jnp.mean(o * o, axis=-1, keepdims=True) + RMS_EPS) * gn
    return (o * jax.nn.silu(og)).astype(BF16)


def _hgrn_long_body(q_ref, f_ref, v_ref, og_ref, lb_ref, gn_ref, o_ref, sfin_ref, st_scr, *, c, heads, dk, scale):
    j = pl.program_id(1)

    @pl.when(j == 0)
    def _():
        st_scr[...] = jnp.zeros_like(st_scr)

    def chunk(ci, carry):
        rs = pl.ds(pl.multiple_of(ci * c, c), c)
        for h in range(heads):
            hs = slice(h * dk, (h + 1) * dk)
            q_dec, k_dec, k_end, decay = _hgrn_gates(q_ref[rs, hs], f_ref[rs, hs], lb_ref[:, hs], scale, c)
            q_dec = q_dec.astype(BF16)
            v = v_ref[rs, hs].astype(BF16)
            scores = _causal_scores(q_dec, k_dec.astype(BF16))
            st = st_scr[h]
            o = (lax.dot_general(q_dec, st.astype(BF16), (((1,), (1,)), ((), ())), preferred_element_type=F32)
                 + jnp.dot(scores, v, preferred_element_type=F32))
            st_scr[h] = decay[:1] * st + lax.dot_general(
                v, k_end.astype(BF16), (((0,), (0,)), ((), ())), preferred_element_type=F32)
            o_ref[rs, hs] = _gated_out(o, gn_ref[:, hs], og_ref[rs, hs])
        return carry

    lax.fori_loop(0, q_ref.shape[0] // c, chunk, 0, unroll=HGRN_CHUNK_UNROLL)

    @pl.when(j == pl.num_programs(1) - 1)
    def _():
        for h in range(heads):
            sfin_ref[0, h] = st_scr[h].T


def _hgrn_long(proj, lb, gn, batch, seq, heads, dk, row_off):
    width = heads * dk
    tb = min(HGRN_TIME_TILE, seq)
    nj = seq // tb
    off = row_off // tb
    c = min(HG_CHUNK, seq)
    body = functools.partial(_hgrn_long_body, c=c, heads=heads, dk=dk, scale=dk ** -0.5)

    def col(k):
        return pl.BlockSpec((tb, width), lambda b, j, k=k: (off + b * nj + j, k))

    return pl.pallas_call(
        body, grid=(batch, nj),
        in_specs=[col(0), col(1), col(2), col(3), _resident(lb.shape), _resident(gn.shape)],
        out_specs=[pl.BlockSpec((tb, width), lambda b, j: (b * nj + j, 0)),
                   pl.BlockSpec((1, heads, dk, dk), lambda b, j: (b, 0, 0, 0))],
        out_shape=[jax.ShapeDtypeStruct((batch * seq, width), BF16),
                   jax.ShapeDtypeStruct((batch, heads, dk, dk), F32)],
        scratch_shapes=[pltpu.VMEM((heads, dk, dk), F32)],
        compiler_params=_cparams(("parallel", "arbitrary")), name="hgrn_long")(proj, proj, proj, proj, lb, gn)


def _hgrn_short_body(q_ref, f_ref, v_ref, og_ref, lb_ref, gn_ref, s0_ref, o_ref, snew_ref, *, c, heads, dk, scale):
    def one_seq(sq, carry):
        rs = pl.ds(pl.multiple_of(sq * c, c), c)
        for h in range(heads):
            hs = slice(h * dk, (h + 1) * dk)
            q_dec, k_dec, k_end, decay = _hgrn_gates(q_ref[rs, hs], f_ref[rs, hs], lb_ref[:, hs], scale, c)
            q_dec = q_dec.astype(BF16)
            v = v_ref[rs, hs].astype(BF16)
            scores = _causal_scores(q_dec, k_dec.astype(BF16))
            s0 = s0_ref[sq, h]
            o = (jnp.dot(q_dec, s0.astype(BF16), preferred_element_type=F32)
                 + jnp.dot(scores, v, preferred_element_type=F32))
            decay_col = jnp.broadcast_to(decay[:1], (dk, dk)).T
            snew_ref[sq, h] = decay_col * s0 + lax.dot_general(
                k_end.astype(BF16), v, (((0,), (0,)), ((), ())), preferred_element_type=F32)
            o_ref[rs, hs] = _gated_out(o, gn_ref[:, hs], og_ref[rs, hs])
        return carry

    lax.fori_loop(0, s0_ref.shape[0], one_seq, 0, unroll=HGRN_SEQ_UNROLL)


def _hgrn_short(proj, lb, gn, s0, seq, row_off):
    batch, heads, dk, _ = s0.shape
    width = heads * dk
    nb = HGRN_SEQ_TILE
    rows = nb * seq
    off = row_off // rows
    body = functools.partial(_hgrn_short_body, c=seq, heads=heads, dk=dk, scale=dk ** -0.5)

    def col(k):
        return pl.BlockSpec((rows, width), lambda i, k=k: (off + i, k))

    state_spec = pl.BlockSpec((nb, heads, dk, dk), lambda i: (i, 0, 0, 0))
    return pl.pallas_call(
        body, grid=(batch // nb,),
        in_specs=[col(0), col(1), col(2), col(3), _resident(lb.shape), _resident(gn.shape), state_spec],
        out_specs=[pl.BlockSpec((rows, width), lambda i: (i, 0)), state_spec],
        out_shape=[jax.ShapeDtypeStruct((batch * seq, width), BF16), jax.ShapeDtypeStruct(s0.shape, F32)],
        compiler_params=_cparams(("parallel",)), name="hgrn_short")(proj, proj, proj, proj, lb, gn, s0)


def _first_index_of_max(vals):
    m = vals[0]
    for v in vals[1:]:
        m = jnp.maximum(m, v)
    idx = jnp.full(m.shape, len(vals), I32)
    for e in range(len(vals) - 1, -1, -1):
        idx = jnp.where(vals[e] == m, e, idx)
    return m, idx


def _route_tile(lg, ids_ref, w_ref, rk_ref, cnt_ref, carry_scr, groups, experts):
    i = pl.program_id(0)
    tile = lg.shape[1]
    n_exp = groups * experts

    @pl.when(i == 0)
    def _():
        carry_scr[...] = jnp.zeros_like(carry_scr)

    gl = [lg[g:g + 1, :] for g in range(groups)]
    gmax, gidx = _first_index_of_max(gl)
    denom = jnp.exp(gl[0] - gmax)
    for g in range(1, groups):
        denom = denom + jnp.exp(gl[g] - gmax)
    g_w = 1.0 / denom

    el = []
    for e in range(experts):
        v = lg[groups + e:groups + e + 1, :]
        for g in range(1, groups):
            r = groups + g * experts + e
            v = jnp.where(gidx == g, lg[r:r + 1, :], v)
        el.append(v)
    v1, i1 = _first_index_of_max(el)
    rest = [jnp.where(i1 == e, -jnp.inf, el[e]) for e in range(experts)]
    v2, i2 = _first_index_of_max(rest)
    t = jnp.exp(v2 - v1)
    inv = 1.0 / (1.0 + t)
    e1 = gidx * experts + i1
    e2 = gidx * experts + i2

    erow = lax.broadcasted_iota(I32, (n_exp, tile), 0)
    oh1 = (erow == e1).astype(F32)
    oh2 = (erow == e2).astype(F32)
    oh = oh1 + oh2
    before = (lax.broadcasted_iota(I32, (tile, tile), 0) < lax.broadcasted_iota(I32, (tile, tile), 1))
    cnt = jnp.dot(oh.astype(BF16), before.astype(BF16), preferred_element_type=F32) + carry_scr[:, 0:1]
    ids_ref[0:1, :] = e1
    ids_ref[1:2, :] = e2
    w_ref[0:1, :] = inv * g_w
    w_ref[1:2, :] = (t * inv) * g_w
    rk_ref[0:1, :] = jnp.sum(oh1 * cnt, axis=0, keepdims=True).astype(I32)
    rk_ref[1:2, :] = jnp.sum(oh2 * cnt, axis=0, keepdims=True).astype(I32)
    carry_scr[...] = carry_scr[...] + jnp.sum(oh, axis=1, keepdims=True)

    @pl.when(i == pl.num_programs(0) - 1)
    def _():
        cnt_ref[...] = carry_scr[...]


def _merge_body(op_ref, os_ref, ybp_ref, ybs_ref, xp_ref, xs_ref, gm_ref, wgt_ref, wa_ref, wo_ref, gf_ref, wr_ref,
                br_ref, h_ref, xn_ref, ids_ref, w_ref, rk_ref, cnt_ref, carry_scr, *, n_first, groups, experts):
    x = _pick(xp_ref, xs_ref, n_first)
    d = x.shape[1]
    gates = jnp.dot(_rmsnorm(x, gm_ref[...]).astype(BF16), wgt_ref[...], preferred_element_type=F32)
    y_a = jnp.dot(_pick(op_ref, os_ref, n_first), wa_ref[...], preferred_element_type=F32)
    merged = (jax.nn.sigmoid(gates[:, :d]) * y_a
              + jax.nn.sigmoid(gates[:, d:]) * _pick(ybp_ref, ybs_ref, n_first))
    h = x + jnp.dot(merged.astype(BF16), wo_ref[...], preferred_element_type=F32)
    h_ref[...] = h
    xn = _rmsnorm(h, gf_ref[...])
    _store_token_tiles(xn_ref, xn)
    logits_t = lax.dot_general(wr_ref[...], xn.astype(BF16), (((1,), (1,)), ((), ())),
                               preferred_element_type=F32) + br_ref[...]
    _route_tile(logits_t, ids_ref, w_ref, rk_ref, cnt_ref, carry_scr, groups, experts)


def _merge(o_p, o_s, yb_p, yb_s, xp, xs, g_mix, w_gates, wa, wo, gf, wr, br, groups, experts):
    d = xp.shape[1]
    total = xp.shape[0] + xs.shape[0]
    tm = TOKEN_TILE
    n_first = xp.shape[0] // tm
    n_exp = groups * experts
    pair = _two_source_specs(tm, d, n_first)
    top = pl.BlockSpec((MOE_TOP_K, tm), lambda i: (0, i))
    weights = [g_mix, w_gates, wa, wo, gf, wr, br]
    return pl.pallas_call(
        functools.partial(_merge_body, n_first=n_first, groups=groups, experts=experts), grid=(total // tm,),
        in_specs=pair + pair + pair + [_resident(w.shape) for w in weights],
        out_specs=[pl.BlockSpec((tm, d), lambda i: (i, 0)), pl.BlockSpec((tm * SUBLANES, LANES), lambda i: (i, 0)),
                   top, top, top, pl.BlockSpec((n_exp, LANES), lambda i: (0, 0))],
        out_shape=[jax.ShapeDtypeStruct((total, d), F32), jax.ShapeDtypeStruct((total * SUBLANES, LANES), F32),
                   jax.ShapeDtypeStruct((MOE_TOP_K, total), I32), jax.ShapeDtypeStruct((MOE_TOP_K, total), F32),
                   jax.ShapeDtypeStruct((MOE_TOP_K, total), I32), jax.ShapeDtypeStruct((n_exp, LANES), F32)],
        scratch_shapes=[pltpu.VMEM((n_exp, LANES), F32)],
        compiler_params=_cparams(("arbitrary",)), name="merge_route")(o_p, o_s, yb_p, yb_s, xp, xs, *weights)


def _row_copy(src, dst, sem):
    return pltpu.make_async_copy(src, dst, sem)


def _token_rows(r):
    return pl.ds(pl.multiple_of(r * SUBLANES, SUBLANES), SUBLANES)


def _dispatch_body(pos_ref, x_ref, o_hbm, ring, zero_scr, sem, pad_sem, *, n_sorted):
    i = pl.program_id(0)
    tile = x_ref.shape[0] // SUBLANES
    par = lax.rem(i, 2)

    @pl.when(i == 0)
    def _():
        zero_scr[...] = jnp.zeros_like(zero_scr)
        pad = _row_copy(zero_scr, o_hbm.at[pl.ds(n_sorted * SUBLANES, zero_scr.shape[0])], pad_sem.at[0])
        pad.start()
        pad.wait()

    ring[par] = x_ref[...]

    def issue(r, carry):
        for k in range(MOE_TOP_K):
            p = pos_ref[MOE_TOP_K * r + k]
            _row_copy(ring.at[par, _token_rows(r)], o_hbm.at[_token_rows(p)], sem.at[par, k]).start(priority=k)
        return carry

    lax.fori_loop(0, tile, issue, 0, unroll=8)

    def drain(slot):
        for k in range(MOE_TOP_K):
            _row_copy(ring.at[slot], o_hbm.at[pl.ds(0, tile * SUBLANES)], sem.at[slot, k]).wait()

    @pl.when(i > 0)
    def _():
        drain(1 - par)

    @pl.when(i == pl.num_programs(0) - 1)
    def _():
        drain(par)


def _dispatch(pos3, xn_tiles):
    total = xn_tiles.shape[0] // SUBLANES
    tile = MOE_TILE
    n_sorted = total * MOE_TOP_K
    pad = EXPERT_WINDOWS[-1]
    return pl.pallas_call(
        functools.partial(_dispatch_body, n_sorted=n_sorted), grid=(total // tile,),
        in_specs=[pl.BlockSpec((MOE_TOP_K * tile,), lambda i: (i,), memory_space=pltpu.SMEM),
                  pl.BlockSpec((tile * SUBLANES, LANES), lambda i: (i, 0))],
        out_specs=pl.BlockSpec(memory_space=pl.ANY),
        out_shape=jax.ShapeDtypeStruct(((n_sorted + pad) * SUBLANES, LANES), F32),
        scratch_shapes=[pltpu.VMEM((2, tile * SUBLANES, LANES), F32), pltpu.VMEM((pad * SUBLANES, LANES), F32),
                        pltpu.SemaphoreType.DMA((2, MOE_TOP_K)), pltpu.SemaphoreType.DMA((1,))],
        compiler_params=_cparams(("arbitrary",)), name="dispatch")(pos3, xn_tiles)


def _experts_body(it_exp, it_row, it_cls, it_first, it_next, n_items, xs_hbm, wg_hbm, wu_hbm, wd_hbm, ys_hbm,
                  xbuf, ybuf, wg_s, wu_s, wd_s, wg_b, wu_b, wd_b, sem_in, sem_out, sem_w):
    j = pl.program_id(0)
    n = n_items[0]
    pad = EXPERT_WINDOWS[-1]
    slot = lax.rem(j, 2)

    def weight_copies(e, s):
        return [pltpu.make_async_copy(hbm.at[e], stage.at[s], sem_w.at[s, t])
                for t, (hbm, stage) in enumerate(((wg_hbm, wg_s), (wu_hbm, wu_s), (wd_hbm, wd_s)))]

    def by_size(item, fn):
        for ci, m in enumerate(EXPERT_WINDOWS):
            pl.when(it_cls[item] == ci)(functools.partial(fn, m))

    def window(item, m):
        return pl.ds(pl.multiple_of(it_row[item] * SUBLANES, SUBLANES), m * SUBLANES)

    def in_copy(item, s, m):
        return pltpu.make_async_copy(xs_hbm.at[window(item, m)], xbuf.at[s, pl.ds(0, m * SUBLANES)], sem_in.at[s])

    def out_copy(item, s, m):
        return pltpu.make_async_copy(ybuf.at[s, pl.ds(0, m * SUBLANES)], ys_hbm.at[window(item, m)], sem_out.at[s])

    def compute(m):
        x = _load_token_tiles(xbuf, m, (slot,)).astype(BF16)
        hg = jnp.dot(x, wg_b[...], preferred_element_type=F32)
        hu = jnp.dot(x, wu_b[...], preferred_element_type=F32)
        hid = (jax.nn.silu(hg) * hu).astype(BF16)
        _store_token_tiles(ybuf, jnp.dot(hid, wd_b[...], preferred_element_type=F32), (slot,))

    @pl.when(j < n)
    def _():
        @pl.when(j == 0)
        def _():
            by_size(0, lambda m: in_copy(0, 0, m).start())
            for c in weight_copies(it_exp[0], it_first[0] - 1):
                c.start()
            tail_rows = pl.ds(0, pad * SUBLANES)
            ybuf[1, tail_rows, :] = jnp.zeros((pad * SUBLANES, LANES), F32)
            tail = pltpu.make_async_copy(
                ybuf.at[1, tail_rows], ys_hbm.at[pl.ds(ys_hbm.shape[0] - pad * SUBLANES, pad * SUBLANES)],
                sem_out.at[1])
            tail.start()
            tail.wait()

        @pl.when(j + 1 < n)
        def _():
            by_size(j + 1, lambda m: in_copy(j + 1, 1 - slot, m).start())

        @pl.when(it_first[j] > 0)
        def _():
            s = it_first[j] - 1
            for c in weight_copies(it_exp[j], s):
                c.wait()
            wg_b[...] = wg_s[s].astype(BF16)
            wu_b[...] = wu_s[s].astype(BF16)
            wd_b[...] = wd_s[s].astype(BF16)

            @pl.when(it_next[j] >= 0)
            def _():
                for c in weight_copies(it_next[j], 1 - s):
                    c.start()

        by_size(j, lambda m: in_copy(j, slot, m).wait())
        by_size(j, compute)

        @pl.when(j > 0)
        def _():
            by_size(j - 1, lambda m: out_copy(j - 1, 1 - slot, m).wait())

        by_size(j, lambda m: out_copy(j, slot, m).start())

        @pl.when(j == n - 1)
        def _():
            by_size(j, lambda m: out_copy(j, slot, m).wait())


def _experts(items, xs, wg, wu, wd):
    d, de = wg.shape[1], wg.shape[2]
    tm = EXPERT_TILE
    max_items = items[0].shape[0]
    grid_spec = pltpu.PrefetchScalarGridSpec(
        num_scalar_prefetch=6, grid=(max_items,),
        in_specs=[pl.BlockSpec(memory_space=pl.ANY)] * 4,
        out_specs=pl.BlockSpec(memory_space=pl.ANY),
        scratch_shapes=[pltpu.VMEM((2, tm * SUBLANES, LANES), F32), pltpu.VMEM((2, tm * SUBLANES, LANES), F32),
                        pltpu.VMEM((2, d, de), F32), pltpu.VMEM((2, d, de), F32), pltpu.VMEM((2, de, d), F32),
                        pltpu.VMEM((d, de), BF16), pltpu.VMEM((d, de), BF16), pltpu.VMEM((de, d), BF16),
                        pltpu.SemaphoreType.DMA((2,)), pltpu.SemaphoreType.DMA((2,)),
                        pltpu.SemaphoreType.DMA((2, 3))])
    return pl.pallas_call(
        _experts_body, grid_spec=grid_spec, out_shape=jax.ShapeDtypeStruct(xs.shape, F32),
        compiler_params=_cparams(("arbitrary",)), name="experts")(*items, xs, wg, wu, wd)


def _combine_body(pos_ref, pos_next_ref, h_ref, w_ref, g_ref, ys_hbm, y_ref, buf, sem, *, final_norm):
    i = pl.program_id(0)
    tile = h_ref.shape[0]
    par = lax.rem(i, 2)

    def gather(table, slot):
        def issue(r, carry):
            for k in range(MOE_TOP_K):
                p = table[MOE_TOP_K * r + k]
                _row_copy(ys_hbm.at[_token_rows(p)], buf.at[slot, k, _token_rows(r)],
                          sem.at[slot, k]).start(priority=k)
            return carry

        lax.fori_loop(0, tile, issue, 0, unroll=8)

    @pl.when(i == 0)
    def _():
        gather(pos_ref, 0)

    @pl.when(i + 1 < pl.num_programs(0))
    def _():
        gather(pos_next_ref, 1 - par)

    for k in range(MOE_TOP_K):
        _row_copy(ys_hbm.at[pl.ds(0, tile * SUBLANES)], buf.at[par, k], sem.at[par, k]).wait()
    h = h_ref[...] + (w_ref[:, 0:1] * _load_token_tiles(buf, tile, (par, 0))
                      + w_ref[:, 1:2] * _load_token_tiles(buf, tile, (par, 1)))
    y_ref[...] = _rmsnorm(h, g_ref[...]) if final_norm else h


def _combine(pos3, h_all, w_t, g, ys, rows, row_off, final_norm):
    d = h_all.shape[1]
    tile = MOE_TILE
    off = row_off // tile
    last_block = h_all.shape[0] // tile - 1
    return pl.pallas_call(
        functools.partial(_combine_body, final_norm=final_norm), grid=(rows // tile,),
        in_specs=[pl.BlockSpec((MOE_TOP_K * tile,), lambda i: (off + i,), memory_space=pltpu.SMEM),
                  pl.BlockSpec((MOE_TOP_K * tile,), lambda i: (jnp.minimum(off + i + 1, last_block),),
                               memory_space=pltpu.SMEM),
                  pl.BlockSpec((tile, d), lambda i: (off + i, 0)),
                  pl.BlockSpec((tile, MOE_TOP_K), lambda i: (off + i, 0)),
                  _resident(g.shape),
                  pl.BlockSpec(memory_space=pl.ANY)],
        out_specs=pl.BlockSpec((tile, d), lambda i: (i, 0)),
        out_shape=jax.ShapeDtypeStruct((rows, d), F32),
        scratch_shapes=[pltpu.VMEM((2, MOE_TOP_K, tile * SUBLANES, LANES), F32),
                        pltpu.SemaphoreType.DMA((2, MOE_TOP_K))],
        compiler_params=_cparams(("arbitrary",)), name="combine")(pos3, pos3, h_all, w_t, g, ys)


def _lookup(table, idx):
    sel = idx[None] == jnp.arange(table.shape[0], dtype=I32).reshape((-1,) + (1,) * idx.ndim)
    return jnp.sum(jnp.where(sel, table.reshape(sel.shape[:1] + (1,) * idx.ndim), 0), axis=0)


def _work_items(counts, n_sorted):
    big, mid, small = EXPERT_WINDOWS
    n_exp = counts.shape[0]
    max_items = n_sorted // big + 2 * n_exp
    ends = jnp.cumsum(counts)
    starts = ends - counts
    units = (counts % big + small - 1) // small
    n_big = counts // big + (units == big // small)
    units = jnp.where(units == big // small, 0, units)
    n_mid = units // (mid // small)
    n_e = n_big + n_mid + units % (mid // small)
    item_end = jnp.cumsum(n_e)
    item_start = item_end - n_e
    n_items = item_end[-1]
    j = jnp.minimum(jnp.arange(max_items, dtype=I32), n_items - 1)
    e = jnp.sum((item_end[None, :] <= j[:, None]).astype(I32), axis=1)
    k = j - _lookup(item_start, e)
    nb, nm = _lookup(n_big, e), _lookup(n_mid, e)
    cls = jnp.where(k < nb, 0, jnp.where(k < nb + nm, 1, 2))
    row = _lookup(starts, e) + jnp.where(cls == 0, k * big, nb * big + jnp.where(cls == 1, 0, nm * mid))
    ordinal = jnp.cumsum((n_e > 0).astype(I32)) - 1
    first = jnp.where(k == 0, 1 + _lookup(ordinal, e) % 2, 0)
    nxt_item = _lookup(item_end, e)
    nxt = jnp.where(nxt_item < n_items, jnp.sum((item_end[None, :] <= nxt_item[:, None]).astype(I32), axis=1), -1)
    return (e, row.astype(I32), cls.astype(I32), first.astype(I32), nxt.astype(I32),
            n_items.reshape(1).astype(I32)), starts


def _s5_discretise(lam_re, lam_im, log_dt, b_re, b_im, c_re, c_im):
    g, p = lam_re.shape
    ch = b_re.shape[-1]
    lam_re = lam_re.astype(F32)
    lam_im = lam_im.astype(F32)
    dt = jnp.exp(log_dt.astype(F32))[:, None]
    mag = jnp.exp(lam_re * dt)
    ab_re = mag * jnp.cos(lam_im * dt)
    ab_im = mag * jnp.sin(lam_im * dt)
    den = lam_re * lam_re + lam_im * lam_im
    nr = ab_re - 1.0
    coef_re = (nr * lam_re + ab_im * lam_im) / den
    coef_im = (ab_im * lam_re - nr * lam_im) / den
    bb_re = coef_re[..., None] * b_re - coef_im[..., None] * b_im
    bb_im = coef_re[..., None] * b_im + coef_im[..., None] * b_re
    gh = g // 2
    eye = jnp.eye(gh, dtype=F32)

    def in_block(m):
        return jnp.einsum("gpc,gh->gchp", m, eye).reshape(gh * ch, gh * p)

    def out_block(m):
        return jnp.einsum("gcp,gh->gphc", m, eye).reshape(gh * p, gh * ch)

    bb = jnp.stack([jnp.concatenate([in_block(bb_re[k * gh:(k + 1) * gh]), in_block(bb_im[k * gh:(k + 1) * gh])],
                                    axis=1) for k in range(2)]).astype(BF16)
    cc = jnp.stack([jnp.concatenate([out_block(c_re[k * gh:(k + 1) * gh]), out_block(-c_im[k * gh:(k + 1) * gh])],
                                    axis=0) for k in range(2)]).astype(BF16)
    return ab_re.reshape(1, g * p), ab_im.reshape(1, g * p), bb, cc


def kernel(x_prompt, x_sample, state_hgrn, state_s5_re, state_s5_im, norm_mix_g, w_in, hgrn_lb_raw, hgrn_onorm_g, w_branch_a, s5_lambda_re, s5_lambda_im, s5_log_dt, s5_b_re, s5_b_im, s5_c_re, s5_c_im, s5_d, w_glu, b_glu, w_out, norm_ffn_g, w_router_group, b_router_group, w_router_expert, b_router_expert, w_exp_gate, w_exp_up, w_exp_down, norm_final_g):
    depth = norm_mix_g.shape[0]
    bp, lp, d = x_prompt.shape
    bs, ls, _ = x_sample.shape
    heads, dk = state_hgrn.shape[2], state_hgrn.shape[3]
    kw = heads * dk
    s5_groups, s5_state = state_s5_re.shape[2], state_s5_re.shape[3]
    s5_width = s5_d.shape[-1]
    nstate = s5_groups * s5_state
    moe_groups, _, experts = w_router_expert.shape[1:]
    n_exp = moe_groups * experts
    rows_p, rows_s = bp * lp, bs * ls
    total = rows_p + rows_s
    n_sorted = total * MOE_TOP_K
    assert kw == d and state_hgrn.shape[4] == dk, "column blocks assume key width == value width == model width"
    assert d == SUBLANES * LANES, "token-tile layout holds one token per (8, 128) tile"
    assert s5_groups % 2 == 0 and bp % SUBLANES == 0 and bs % HGRN_SEQ_TILE == 0

    lb_all = jnp.cumsum(jax.nn.softmax(hgrn_lb_raw.astype(F32), axis=0), axis=0)

    hp = x_prompt.reshape(rows_p, d)
    hs = x_sample.reshape(rows_s, d)
    hg_p, re_p, im_p, hg_s, re_s, im_s = [], [], [], [], [], []
    zeros_state = jnp.zeros((bp // SUBLANES, SUBLANES, nstate), F32)

    for l in range(depth):
        w = w_in[l]
        g_mix = norm_mix_g[l].reshape(1, d)
        w_u = w[:, 4 * kw:4 * kw + s5_width].astype(BF16)
        w_gates = w[:, 4 * kw + s5_width:].astype(BF16)
        proj = _in_proj(hp, hs, g_mix, w[:, :4 * kw].astype(BF16))

        ar, ai, bb, cc = _s5_discretise(s5_lambda_re[l], s5_lambda_im[l], s5_log_dt[l], s5_b_re[l], s5_b_im[l],
                                        s5_c_re[l], s5_c_im[l])
        s5_args = (ar, ai, bb, cc, s5_d[l].reshape(1, s5_width), w_glu[l].astype(BF16), b_glu[l].reshape(1, -1))
        yb_p, fr_p, fi_p = _s5_branch(hp, bp, lp, g_mix, w_u, zeros_state, zeros_state, *s5_args)
        yb_s, fr_s, fi_s = _s5_branch(hs, bs, ls, g_mix, w_u,
                                      state_s5_re[l].reshape(bs // SUBLANES, SUBLANES, nstate),
                                      state_s5_im[l].reshape(bs // SUBLANES, SUBLANES, nstate), *s5_args)

        lb = lb_all[l].reshape(1, kw)
        gn = hgrn_onorm_g[l].reshape(1, kw)
        o_p, hgp = _hgrn_long(proj, lb, gn, bp, lp, heads, dk, 0)
        o_s, hgs = _hgrn_short(proj, lb, gn, state_hgrn[l].astype(F32), ls, rows_p)

        nr = -(-(moe_groups + n_exp) // SUBLANES) * SUBLANES
        wr = jnp.concatenate([w_router_group[l].T, w_router_expert[l].transpose(0, 2, 1).reshape(n_exp, d)], axis=0)
        wr = jnp.pad(wr, ((0, nr - wr.shape[0]), (0, 0))).astype(BF16)
        br = jnp.pad(jnp.concatenate([b_router_group[l], b_router_expert[l].reshape(n_exp)]),
                     (0, nr - moe_groups - n_exp)).reshape(nr, 1).astype(F32)
        h_all, xn_all, ids, wts, ranks, cnt = _merge(
            o_p, o_s, yb_p.reshape(rows_p, d), yb_s.reshape(rows_s, d), hp, hs, g_mix, w_gates,
            w_branch_a[l].astype(BF16), w_out[l].astype(BF16), norm_ffn_g[l].reshape(1, d), wr, br,
            moe_groups, experts)
        items, starts = _work_items(cnt[:, 0].astype(I32), n_sorted)
        pos = _lookup(starts, ids) + ranks
        pos3 = pos.T.reshape(-1)
        xs = _dispatch(pos3, xn_all)
        ys = _experts(items, xs, w_exp_gate[l], w_exp_up[l], w_exp_down[l])

        last = l == depth - 1
        g_out = norm_final_g.reshape(1, d)
        hp = _combine(pos3, h_all, wts.T, g_out, ys, rows_p, 0, last)
        hs = _combine(pos3, h_all, wts.T, g_out, ys, rows_s, rows_p, last)

        hg_p.append(hgp)
        hg_s.append(hgs)
        re_p.append(fr_p.reshape(bp, s5_groups, s5_state))
        im_p.append(fi_p.reshape(bp, s5_groups, s5_state))
        re_s.append(fr_s.reshape(bs, s5_groups, s5_state))
        im_s.append(fi_s.reshape(bs, s5_groups, s5_state))

    y_prompt = hp.reshape(bp, lp, d).astype(x_prompt.dtype)
    y_sample = hs.reshape(bs, ls, d).astype(x_sample.dtype)
    return (y_prompt, y_sample, jnp.stack(hg_p), jnp.stack(re_p), jnp.stack(im_p),
            jnp.stack(hg_s), jnp.stack(re_s), jnp.stack(im_s))
```

```python
import functools

import jax
import jax.numpy as jnp
from jax import lax
from jax.experimental import pallas as pl
from jax.experimental.pallas import tpu as pltpu

F32 = jnp.float32
BF16 = jnp.bfloat16
I32 = jnp.int32

RMS_EPS = 1e-6
HG_CHUNK = 64
MOE_TOP_K = 2

V7X_VMEM_BYTES = 64 * 1024 * 1024
VMEM_LIMIT_BYTES = V7X_VMEM_BYTES - 8 * 1024 * 1024
SUBLANES = 8
LANES = 128

TOKEN_TILE = 512
MERGE_ROW_BLOCK = 512
DISPATCH_TILE = 1024
COMBINE_TILE = 512
EXPERT_TILE = 512
EXPERT_WINDOWS = (EXPERT_TILE, EXPERT_TILE // 2, EXPERT_TILE // 4)
S5_TIME_TILE = 64
S5_ROW_BLOCK = 512
HGRN_TIME_TILE = 1024
HGRN_SEQ_TILE = 8
HGRN_CHUNK_UNROLL = 16
HGRN_SEQ_UNROLL = 8
PROJ_COL_TILE = 512


def _cparams(sem):
    return pltpu.CompilerParams(dimension_semantics=sem, vmem_limit_bytes=VMEM_LIMIT_BYTES)


def _resident(shape):
    nd = len(shape)
    return pl.BlockSpec(shape, lambda *_: (0,) * nd, pipeline_mode=pl.Buffered(1))


def _rmsnorm(x, g):
    return x * lax.rsqrt(jnp.mean(x * x, axis=-1, keepdims=True) + RMS_EPS) * g


def _two_source_specs(tm, width, n_first):
    return [pl.BlockSpec((tm, width), lambda i: (jnp.minimum(i, n_first - 1), 0)),
            pl.BlockSpec((tm, width), lambda i: (jnp.maximum(i - n_first, 0), 0))]


def _pick(first_ref, second_ref, n_first):
    return jnp.where(pl.program_id(0) < n_first, first_ref[...], second_ref[...])


def _store_token_tiles(ref, x, lead=(), row0=0):
    rows = x.shape[0]
    for c in range(SUBLANES):
        ref[lead + (pl.ds(row0 * SUBLANES + c, rows, stride=SUBLANES), slice(None))] = x[:, c * LANES:(c + 1) * LANES]


def _load_token_tiles(ref, rows, lead=()):
    return jnp.concatenate([ref[lead + (pl.ds(c, rows, stride=SUBLANES), slice(None))] for c in range(SUBLANES)],
                           axis=-1)


def _inproj_body(xp_ref, xs_ref, g_ref, w_ref, o_ref, *, n_first):
    xb = _rmsnorm(_pick(xp_ref, xs_ref, n_first), g_ref[...]).astype(BF16)
    for j in range(0, w_ref.shape[1], PROJ_COL_TILE):
        o_ref[:, j:j + PROJ_COL_TILE] = jnp.dot(xb, w_ref[:, j:j + PROJ_COL_TILE], preferred_element_type=F32)


def _in_proj(xp, xs, g, w):
    d = xp.shape[1]
    n = w.shape[1]
    tm = TOKEN_TILE
    total = xp.shape[0] + xs.shape[0]
    n_first = xp.shape[0] // tm
    return pl.pallas_call(
        functools.partial(_inproj_body, n_first=n_first), grid=(total // tm,),
        in_specs=_two_source_specs(tm, d, n_first) + [_resident((1, d)), _resident((d, n))],
        out_specs=pl.BlockSpec((tm, n), lambda i: (i, 0)),
        out_shape=jax.ShapeDtypeStruct((total, n), F32),
        compiler_params=_cparams(("parallel",)), name="in_proj")(xp, xs, g, w)


def _s5_body(*refs, tt, groups, nstate, column_inputs):
    n_x = len(refs) - 19
    x_refs = refs[:n_x]
    (gm_ref, wu_ref, h0r_ref, h0i_ref, ar_ref, ai_ref, bb_ref, cc_ref, d_ref, wg_ref, bg_ref,
     y_ref, hr_out, hi_out, hr_scr, hi_scr, bu_scr, x_scr, y_scr) = refs[n_x:]
    j = pl.program_id(1)
    half = nstate // 2
    d = x_scr.shape[0] * LANES
    kw = wu_ref.shape[-1] // 2

    @pl.when(j == 0)
    def _():
        hr_scr[...] = h0r_ref[...]
        hi_scr[...] = h0i_ref[...]

    if column_inputs:
        for s in range(d // LANES):
            for g in range(groups):
                for t in range(tt):
                    r0 = (g * tt + t) * SUBLANES
                    x_scr[s, r0:r0 + SUBLANES, :] = x_refs[s][pl.ds(g * SUBLANES * tt + t, SUBLANES, stride=tt), :]
    else:
        for b in range(SUBLANES):
            xb = x_refs[b][...]
            for s in range(d // LANES):
                x_scr[s, pl.ds(b, tt, stride=SUBLANES), :] = xb[:, s * LANES:(s + 1) * LANES]
    rows = groups * tt * SUBLANES
    row_blocks = [slice(r, min(r + S5_ROW_BLOCK, rows)) for r in range(0, rows, S5_ROW_BLOCK)]
    us = []
    for rb in row_blocks:
        x = jnp.concatenate([x_scr[s, rb, :] for s in range(d // LANES)], axis=-1)
        u = jnp.dot(_rmsnorm(x, gm_ref[...]).astype(BF16), wu_ref[...], preferred_element_type=F32)
        us.append(u)
        ub16 = u.astype(BF16)
        for kt in range(2):
            ukt = ub16[:, kt * kw:(kt + 1) * kw]
            bu_scr[rb, kt * half:(kt + 1) * half] = jnp.dot(ukt, bb_ref[kt, :, :half], preferred_element_type=F32)
            bu_scr[rb, nstate + kt * half:nstate + (kt + 1) * half] = jnp.dot(
                ukt, bb_ref[kt, :, half:], preferred_element_type=F32)

    lane_chunk = 512
    for lc in range(nstate // lane_chunk):
        lo = lc * lane_chunk
        re_sl = slice(lo, lo + lane_chunk)
        im_sl = slice(nstate + lo, nstate + lo + lane_chunk)
        ar = jnp.broadcast_to(ar_ref[:, re_sl], (SUBLANES, lane_chunk))
        ai = jnp.broadcast_to(ai_ref[:, re_sl], (SUBLANES, lane_chunk))

        for g in range(groups):
            hr, hi = hr_scr[g, :, re_sl], hi_scr[g, :, re_sl]
            for t in range(tt):
                r0 = (g * tt + t) * SUBLANES
                rs = slice(r0, r0 + SUBLANES)
                hr, hi = (ar * hr - ai * hi + bu_scr[rs, re_sl], ar * hi + ai * hr + bu_scr[rs, im_sl])
                bu_scr[rs, re_sl] = hr
                bu_scr[rs, im_sl] = hi
            hr_scr[g, :, re_sl] = hr
            hi_scr[g, :, re_sl] = hi

    dm = wg_ref.shape[-1] // 2
    for rb, u in zip(row_blocks, us):
        ys = []
        for n in range(2):
            h_re = bu_scr[rb, n * half:(n + 1) * half].astype(BF16)
            h_im = bu_scr[rb, nstate + n * half:nstate + (n + 1) * half].astype(BF16)
            ys.append(jnp.dot(h_re, cc_ref[n, :half, :], preferred_element_type=F32)
                      + jnp.dot(h_im, cc_ref[n, half:, :], preferred_element_type=F32))
        y = jnp.concatenate(ys, axis=-1) + d_ref[...] * u
        z = jnp.dot(jax.nn.gelu(y).astype(BF16), wg_ref[...], preferred_element_type=F32) + bg_ref[...]
        yb = z[:, :dm] * jax.nn.sigmoid(z[:, dm:])
        for s in range(dm // LANES):
            y_scr[s, rb, :] = yb[:, s * LANES:(s + 1) * LANES]
    for g in range(groups):
        for b in range(SUBLANES):
            for s in range(dm // LANES):
                y_ref[g * SUBLANES + b, :, s * LANES:(s + 1) * LANES] = y_scr[
                    s, pl.ds(g * tt * SUBLANES + b, tt, stride=SUBLANES), :]

    @pl.when(j == pl.num_programs(1) - 1)
    def _():
        hr_out[...] = hr_scr[...]
        hi_out[...] = hi_scr[...]


def _s5_branch(x2d, batch, seq, g_mix, w_u, h0r, h0i, ar, ai, bb, cc, d_skip, w_glu, b_glu):
    nstate = ar.shape[-1]
    d = x2d.shape[1]
    dm = w_glu.shape[1] // 2
    tt = min(S5_TIME_TILE, seq)
    nj = seq // tt
    ngroups = batch // SUBLANES
    column_inputs = nj == 1
    groups = min(ngroups, max(1, S5_ROW_BLOCK // (tt * SUBLANES))) if column_inputs else 1
    rows = groups * tt * SUBLANES
    body = functools.partial(_s5_body, tt=tt, groups=groups, nstate=nstate, column_inputs=column_inputs)
    if column_inputs:
        x_specs = [pl.BlockSpec((rows, LANES), lambda i, j, s=s: (i, s)) for s in range(d // LANES)]
    else:
        x_specs = [pl.BlockSpec((tt, d), lambda i, j, b=b: ((i * SUBLANES + b) * nj + j, 0))
                   for b in range(SUBLANES)]
    state_spec = pl.BlockSpec((groups, SUBLANES, nstate), lambda i, j: (i, 0, 0))
    return pl.pallas_call(
        body, grid=(ngroups // groups, nj),
        in_specs=x_specs + [
            _resident(g_mix.shape), _resident(w_u.shape),
            state_spec, state_spec, _resident(ar.shape), _resident(ai.shape), _resident(bb.shape),
            _resident(cc.shape), _resident(d_skip.shape), _resident(w_glu.shape), _resident(b_glu.shape)],
        out_specs=[pl.BlockSpec((groups * SUBLANES, tt, dm), lambda i, j: (i, j, 0)), state_spec, state_spec],
        out_shape=[jax.ShapeDtypeStruct((batch, seq, dm), F32),
                   jax.ShapeDtypeStruct((ngroups, SUBLANES, nstate), F32),
                   jax.ShapeDtypeStruct((ngroups, SUBLANES, nstate), F32)],
        scratch_shapes=[pltpu.VMEM((groups, SUBLANES, nstate), F32), pltpu.VMEM((groups, SUBLANES, nstate), F32),
                        pltpu.VMEM((rows, 2 * nstate), F32),
                        pltpu.VMEM((d // LANES, rows, LANES), F32),
                        pltpu.VMEM((dm // LANES, rows, LANES), F32)],
        compiler_params=_cparams(("parallel", "arbitrary")), name="s5_branch")(
            *([x2d] * len(x_specs)), g_mix, w_u, h0r, h0i, ar, ai, bb, cc, d_skip, w_glu, b_glu)


def _cumsum_rows(x, c):
    row = lax.broadcasted_iota(I32, x.shape, 0) & (c - 1)
    s = 1
    while s < c:
        x = x + jnp.where(row >= s, pltpu.roll(x, s, axis=0), 0.0)
        s *= 2
    return x


def _hgrn_gates(q, fr, lb, scale, c):
    rows, n = q.shape
    f = lb + (1.0 - lb) * jax.nn.sigmoid(fr)
    k = 1.0 - f
    b = _cumsum_rows(jnp.log(f), c)
    b3 = b.reshape(rows // c, c, n)
    b_last = jnp.broadcast_to(b3[:, c - 1:c, :], b3.shape).reshape(rows, n)
    q_dec = (q * scale) * jnp.exp(b)
    k_dec = k * jnp.exp(-b)
    k_end = k * jnp.exp(b_last - b)
    return q_dec, k_dec, k_end, jnp.exp(b_last)


def _causal_scores(q_dec, k_dec):
    c = q_dec.shape[0]
    s = lax.dot_general(q_dec, k_dec, (((1,), (1,)), ((), ())), preferred_element_type=F32)
    keep = lax.broadcasted_iota(I32, (c, c), 0) >= lax.broadcasted_iota(I32, (c, c), 1)
    return jnp.where(keep, s, 0.0).astype(BF16)


def _gated_out(o, gn, og):
    o = o * lax.rsqrt(jnp.mean(o * o, axis=-1, keepdims=True) + RMS_EPS) * gn
    return (o * jax.nn.silu(og)).astype(BF16)


def _hgrn_long_body(q_ref, f_ref, v_ref, og_ref, lb_ref, gn_ref, o_ref, sfin_ref, st_scr, *, c, heads, dk, scale):
    j = pl.program_id(1)

    @pl.when(j == 0)
    def _():
        st_scr[...] = jnp.zeros_like(st_scr)

    def chunk(ci, carry):
        rs = pl.ds(pl.multiple_of(ci * c, c), c)
        for h in range(heads):
            hs = slice(h * dk, (h + 1) * dk)
            q_dec, k_dec, k_end, decay = _hgrn_gates(q_ref[rs, hs], f_ref[rs, hs], lb_ref[:, hs], scale, c)
            q_dec = q_dec.astype(BF16)
            v = v_ref[rs, hs].astype(BF16)
            scores = _causal_scores(q_dec, k_dec.astype(BF16))
            st = st_scr[h]
            o = (lax.dot_general(q_dec, st.astype(BF16), (((1,), (1,)), ((), ())), preferred_element_type=F32)
                 + jnp.dot(scores, v, preferred_element_type=F32))
            st_scr[h] = decay[:1] * st + lax.dot_general(
                v, k_end.astype(BF16), (((0,), (0,)), ((), ())), preferred_element_type=F32)
            o_ref[rs, hs] = _gated_out(o, gn_ref[:, hs], og_ref[rs, hs])
        return carry

    lax.fori_loop(0, q_ref.shape[0] // c, chunk, 0, unroll=HGRN_CHUNK_UNROLL)

    @pl.when(j == pl.num_programs(1) - 1)
    def _():
        for h in range(heads):
            sfin_ref[0, h] = st_scr[h].T


def _hgrn_long(proj, lb, gn, batch, seq, heads, dk, row_off):
    width = heads * dk
    tb = min(HGRN_TIME_TILE, seq)
    nj = seq // tb
    off = row_off // tb
    c = min(HG_CHUNK, seq)
    body = functools.partial(_hgrn_long_body, c=c, heads=heads, dk=dk, scale=dk ** -0.5)

    def col(k):
        return pl.BlockSpec((tb, width), lambda b, j, k=k: (off + b * nj + j, k))

    return pl.pallas_call(
        body, grid=(batch, nj),
        in_specs=[col(0), col(1), col(2), col(3), _resident(lb.shape), _resident(gn.shape)],
        out_specs=[pl.BlockSpec((tb, width), lambda b, j: (b * nj + j, 0)),
                   pl.BlockSpec((1, heads, dk, dk), lambda b, j: (b, 0, 0, 0))],
        out_shape=[jax.ShapeDtypeStruct((batch * seq, width), BF16),
                   jax.ShapeDtypeStruct((batch, heads, dk, dk), F32)],
        scratch_shapes=[pltpu.VMEM((heads, dk, dk), F32)],
        compiler_params=_cparams(("parallel", "arbitrary")), name="hgrn_long")(proj, proj, proj, proj, lb, gn)


def _hgrn_short_body(q_ref, f_ref, v_ref, og_ref, lb_ref, gn_ref, s0_ref, o_ref, snew_ref, *, c, heads, dk, scale):
    def one_seq(sq, carry):
        rs = pl.ds(pl.multiple_of(sq * c, c), c)
        for h in range(heads):
            hs = slice(h * dk, (h + 1) * dk)
            q_dec, k_dec, k_end, decay = _hgrn_gates(q_ref[rs, hs], f_ref[rs, hs], lb_ref[:, hs], scale, c)
            q_dec = q_dec.astype(BF16)
            v = v_ref[rs, hs].astype(BF16)
            scores = _causal_scores(q_dec, k_dec.astype(BF16))
            s0 = s0_ref[sq, h]
            o = (jnp.dot(q_dec, s0.astype(BF16), preferred_element_type=F32)
                 + jnp.dot(scores, v, preferred_element_type=F32))
            decay_col = jnp.broadcast_to(decay[:1], (dk, dk)).T
            snew_ref[sq, h] = decay_col * s0 + lax.dot_general(
                k_end.astype(BF16), v, (((0,), (0,)), ((), ())), preferred_element_type=F32)
            o_ref[rs, hs] = _gated_out(o, gn_ref[:, hs], og_ref[rs, hs])
        return carry

    lax.fori_loop(0, s0_ref.shape[0], one_seq, 0, unroll=HGRN_SEQ_UNROLL)


def _hgrn_short(proj, lb, gn, s0, seq, row_off):
    batch, heads, dk, _ = s0.shape
    width = heads * dk
    nb = HGRN_SEQ_TILE
    rows = nb * seq
    off = row_off // rows
    body = functools.partial(_hgrn_short_body, c=seq, heads=heads, dk=dk, scale=dk ** -0.5)

    def col(k):
        return pl.BlockSpec((rows, width), lambda i, k=k: (off + i, k))

    state_spec = pl.BlockSpec((nb, heads, dk, dk), lambda i: (i, 0, 0, 0))
    return pl.pallas_call(
        body, grid=(batch // nb,),
        in_specs=[col(0), col(1), col(2), col(3), _resident(lb.shape), _resident(gn.shape), state_spec],
        out_specs=[pl.BlockSpec((rows, width), lambda i: (i, 0)), state_spec],
        out_shape=[jax.ShapeDtypeStruct((batch * seq, width), BF16), jax.ShapeDtypeStruct(s0.shape, F32)],
        compiler_params=_cparams(("parallel",)), name="hgrn_short")(proj, proj, proj, proj, lb, gn, s0)


def _first_index_of_max(vals):
    m = vals[0]
    for v in vals[1:]:
        m = jnp.maximum(m, v)
    idx = jnp.full(m.shape, len(vals), I32)
    for e in range(len(vals) - 1, -1, -1):
        idx = jnp.where(vals[e] == m, e, idx)
    return m, idx


def _route_tile(lg, ids_ref, w_ref, rk_ref, cnt_ref, carry_scr, groups, experts):
    i = pl.program_id(0)
    tile = lg.shape[1]
    n_exp = groups * experts

    @pl.when(i == 0)
    def _():
        carry_scr[...] = jnp.zeros_like(carry_scr)

    gl = [lg[g:g + 1, :] for g in range(groups)]
    gmax, gidx = _first_index_of_max(gl)
    denom = jnp.exp(gl[0] - gmax)
    for g in range(1, groups):
        denom = denom + jnp.exp(gl[g] - gmax)
    g_w = 1.0 / denom

    el = []
    for e in range(experts):
        v = lg[groups + e:groups + e + 1, :]
        for g in range(1, groups):
            r = groups + g * experts + e
            v = jnp.where(gidx == g, lg[r:r + 1, :], v)
        el.append(v)
    v1, i1 = _first_index_of_max(el)
    rest = [jnp.where(i1 == e, -jnp.inf, el[e]) for e in range(experts)]
    v2, i2 = _first_index_of_max(rest)
    t = jnp.exp(v2 - v1)
    inv = 1.0 / (1.0 + t)
    e1 = gidx * experts + i1
    e2 = gidx * experts + i2

    erow = lax.broadcasted_iota(I32, (n_exp, tile), 0)
    oh1 = (erow == e1).astype(F32)
    oh2 = (erow == e2).astype(F32)
    oh = oh1 + oh2
    before = (lax.broadcasted_iota(I32, (tile, tile), 0) < lax.broadcasted_iota(I32, (tile, tile), 1))
    cnt = jnp.dot(oh.astype(BF16), before.astype(BF16), preferred_element_type=F32) + carry_scr[:, 0:1]
    ids_ref[0:1, :] = e1
    ids_ref[1:2, :] = e2
    w_ref[0:1, :] = inv * g_w
    w_ref[1:2, :] = (t * inv) * g_w
    rk_ref[0:1, :] = jnp.sum(oh1 * cnt, axis=0, keepdims=True).astype(I32)
    rk_ref[1:2, :] = jnp.sum(oh2 * cnt, axis=0, keepdims=True).astype(I32)
    carry_scr[...] = carry_scr[...] + jnp.sum(oh, axis=1, keepdims=True)

    @pl.when(i == pl.num_programs(0) - 1)
    def _():
        cnt_ref[...] = carry_scr[...]


def _merge_body(op_ref, os_ref, ybp_ref, ybs_ref, xp_ref, xs_ref, gm_ref, wgt_ref, wa_ref, wo_ref, gf_ref, wr_ref,
                br_ref, h_ref, xn_ref, ids_ref, w_ref, rk_ref, cnt_ref, carry_scr, *, n_first, groups, experts):
    tm, d = h_ref.shape
    first = pl.program_id(0) < n_first

    def pick(a_ref, b_ref, rb):
        return jnp.where(first, a_ref[rb, :], b_ref[rb, :])

    blocks = [slice(r, r + MERGE_ROW_BLOCK) for r in range(0, tm, MERGE_ROW_BLOCK)]
    xs = [pick(xp_ref, xs_ref, rb) for rb in blocks]
    gates = [jnp.dot(_rmsnorm(x, gm_ref[...]).astype(BF16), wgt_ref[...], preferred_element_type=F32) for x in xs]
    y_as = [jnp.dot(pick(op_ref, os_ref, rb), wa_ref[...], preferred_element_type=F32) for rb in blocks]
    logits = []
    for rb, x, g, y_a in zip(blocks, xs, gates, y_as):
        merged = jax.nn.sigmoid(g[:, :d]) * y_a + jax.nn.sigmoid(g[:, d:]) * pick(ybp_ref, ybs_ref, rb)
        h = x + jnp.dot(merged.astype(BF16), wo_ref[...], preferred_element_type=F32)
        h_ref[rb, :] = h
        xn = _rmsnorm(h, gf_ref[...])
        _store_token_tiles(xn_ref, xn, row0=rb.start)
        logits.append(lax.dot_general(wr_ref[...], xn.astype(BF16), (((1,), (1,)), ((), ())),
                                      preferred_element_type=F32))
    logits_t = jnp.concatenate(logits, axis=1) + br_ref[...]
    _route_tile(logits_t, ids_ref, w_ref, rk_ref, cnt_ref, carry_scr, groups, experts)


def _merge(o_p, o_s, yb_p, yb_s, xp, xs, g_mix, w_gates, wa, wo, gf, wr, br, groups, experts):
    d = xp.shape[1]
    total = xp.shape[0] + xs.shape[0]
    tm = TOKEN_TILE
    n_first = xp.shape[0] // tm
    n_exp = groups * experts
    pair = _two_source_specs(tm, d, n_first)
    top = pl.BlockSpec((MOE_TOP_K, tm), lambda i: (0, i))
    weights = [g_mix, w_gates, wa, wo, gf, wr, br]
    return pl.pallas_call(
        functools.partial(_merge_body, n_first=n_first, groups=groups, experts=experts), grid=(total // tm,),
        in_specs=pair + pair + pair + [_resident(w.shape) for w in weights],
        out_specs=[pl.BlockSpec((tm, d), lambda i: (i, 0)), pl.BlockSpec((tm * SUBLANES, LANES), lambda i: (i, 0)),
                   top, top, top, pl.BlockSpec((n_exp, LANES), lambda i: (0, 0))],
        out_shape=[jax.ShapeDtypeStruct((total, d), F32), jax.ShapeDtypeStruct((total * SUBLANES, LANES), F32),
                   jax.ShapeDtypeStruct((MOE_TOP_K, total), I32), jax.ShapeDtypeStruct((MOE_TOP_K, total), F32),
                   jax.ShapeDtypeStruct((MOE_TOP_K, total), I32), jax.ShapeDtypeStruct((n_exp, LANES), F32)],
        scratch_shapes=[pltpu.VMEM((n_exp, LANES), F32)],
        compiler_params=_cparams(("arbitrary",)), name="merge_route")(o_p, o_s, yb_p, yb_s, xp, xs, *weights)


def _row_copy(src, dst, sem):
    return pltpu.make_async_copy(src, dst, sem)


def _token_rows(r):
    return pl.ds(pl.multiple_of(r * SUBLANES, SUBLANES), SUBLANES)


def _dispatch_body(pos_ref, x_ref, o_hbm, ring, zero_scr, sem, pad_sem, *, n_sorted):
    i = pl.program_id(0)
    tile = x_ref.shape[0] // SUBLANES
    par = lax.rem(i, 2)

    @pl.when(i == 0)
    def _():
        zero_scr[...] = jnp.zeros_like(zero_scr)
        pad = _row_copy(zero_scr, o_hbm.at[pl.ds(n_sorted * SUBLANES, zero_scr.shape[0])], pad_sem.at[0])
        pad.start()
        pad.wait()

    ring[par] = x_ref[...]

    def issue(r, carry):
        for k in range(MOE_TOP_K):
            p = pos_ref[MOE_TOP_K * r + k]
            _row_copy(ring.at[par, _token_rows(r)], o_hbm.at[_token_rows(p)], sem.at[par, k]).start(priority=k)
        return carry

    lax.fori_loop(0, tile, issue, 0, unroll=8)

    def drain(slot):
        for k in range(MOE_TOP_K):
            _row_copy(ring.at[slot], o_hbm.at[pl.ds(0, tile * SUBLANES)], sem.at[slot, k]).wait()

    @pl.when(i > 0)
    def _():
        drain(1 - par)

    @pl.when(i == pl.num_programs(0) - 1)
    def _():
        drain(par)


def _dispatch(pos3, xn_tiles):
    total = xn_tiles.shape[0] // SUBLANES
    tile = DISPATCH_TILE
    n_sorted = total * MOE_TOP_K
    pad = EXPERT_WINDOWS[-1]
    return pl.pallas_call(
        functools.partial(_dispatch_body, n_sorted=n_sorted), grid=(total // tile,),
        in_specs=[pl.BlockSpec((MOE_TOP_K * tile,), lambda i: (i,), memory_space=pltpu.SMEM),
                  pl.BlockSpec((tile * SUBLANES, LANES), lambda i: (i, 0))],
        out_specs=pl.BlockSpec(memory_space=pl.ANY),
        out_shape=jax.ShapeDtypeStruct(((n_sorted + pad) * SUBLANES, LANES), xn_tiles.dtype),
        scratch_shapes=[pltpu.VMEM((2, tile * SUBLANES, LANES), xn_tiles.dtype),
                        pltpu.VMEM((pad * SUBLANES, LANES), xn_tiles.dtype),
                        pltpu.SemaphoreType.DMA((2, MOE_TOP_K)), pltpu.SemaphoreType.DMA((1,))],
        compiler_params=_cparams(("arbitrary",)), name="dispatch")(pos3, xn_tiles)


def _experts_body(it_exp, it_row, it_cls, it_first, it_next, n_items, xs_hbm, wg_hbm, wu_hbm, wd_hbm, ys_hbm,
                  xbuf, ybuf, wg_s, wu_s, wd_s, wg_b, wu_b, wd_b, sem_in, sem_out, sem_w):
    j = pl.program_id(0)
    n = n_items[0]
    pad = EXPERT_WINDOWS[-1]
    slot = lax.rem(j, 2)

    def weight_copies(e, s):
        return [pltpu.make_async_copy(hbm.at[e], stage.at[s], sem_w.at[s, t])
                for t, (hbm, stage) in enumerate(((wg_hbm, wg_s), (wu_hbm, wu_s), (wd_hbm, wd_s)))]

    def by_size(item, fn):
        for ci, m in enumerate(EXPERT_WINDOWS):
            pl.when(it_cls[item] == ci)(functools.partial(fn, m))

    def window(item, m):
        return pl.ds(pl.multiple_of(it_row[item] * SUBLANES, SUBLANES), m * SUBLANES)

    def in_copy(item, s, m):
        return pltpu.make_async_copy(xs_hbm.at[window(item, m)], xbuf.at[s, pl.ds(0, m * SUBLANES)], sem_in.at[s])

    def out_copy(item, s, m):
        return pltpu.make_async_copy(ybuf.at[s, pl.ds(0, m * SUBLANES)], ys_hbm.at[window(item, m)], sem_out.at[s])

    def compute(m):
        x = _load_token_tiles(xbuf, m, (slot,)).astype(BF16)
        hg = jnp.dot(x, wg_b[...], preferred_element_type=F32)
        hu = jnp.dot(x, wu_b[...], preferred_element_type=F32)
        hid = (jax.nn.silu(hg) * hu).astype(BF16)
        _store_token_tiles(ybuf, jnp.dot(hid, wd_b[...], preferred_element_type=F32), (slot,))

    @pl.when(j < n)
    def _():
        @pl.when(j == 0)
        def _():
            by_size(0, lambda m: in_copy(0, 0, m).start())
            for c in weight_copies(it_exp[0], it_first[0] - 1):
                c.start()
            tail_rows = pl.ds(0, pad * SUBLANES)
            ybuf[1, tail_rows, :] = jnp.zeros((pad * SUBLANES, LANES), F32)
            tail = pltpu.make_async_copy(
                ybuf.at[1, tail_rows], ys_hbm.at[pl.ds(ys_hbm.shape[0] - pad * SUBLANES, pad * SUBLANES)],
                sem_out.at[1])
            tail.start()
            tail.wait()

        @pl.when(j + 1 < n)
        def _():
            by_size(j + 1, lambda m: in_copy(j + 1, 1 - slot, m).start())

        @pl.when(it_first[j] > 0)
        def _():
            s = it_first[j] - 1
            for c in weight_copies(it_exp[j], s):
                c.wait()
            wg_b[...] = wg_s[s].astype(BF16)
            wu_b[...] = wu_s[s].astype(BF16)
            wd_b[...] = wd_s[s].astype(BF16)

            @pl.when(it_next[j] >= 0)
            def _():
                for c in weight_copies(it_next[j], 1 - s):
                    c.start()

        by_size(j, lambda m: in_copy(j, slot, m).wait())
        by_size(j, compute)

        @pl.when(j > 0)
        def _():
            by_size(j - 1, lambda m: out_copy(j - 1, 1 - slot, m).wait())

        by_size(j, lambda m: out_copy(j, slot, m).start())

        @pl.when(j == n - 1)
        def _():
            by_size(j, lambda m: out_copy(j, slot, m).wait())


def _experts(items, xs, wg, wu, wd):
    d, de = wg.shape[1], wg.shape[2]
    tm = EXPERT_TILE
    max_items = items[0].shape[0]
    grid_spec = pltpu.PrefetchScalarGridSpec(
        num_scalar_prefetch=6, grid=(max_items,),
        in_specs=[pl.BlockSpec(memory_space=pl.ANY)] * 4,
        out_specs=pl.BlockSpec(memory_space=pl.ANY),
        scratch_shapes=[pltpu.VMEM((2, tm * SUBLANES, LANES), F32), pltpu.VMEM((2, tm * SUBLANES, LANES), F32),
                        pltpu.VMEM((2, d, de), F32), pltpu.VMEM((2, d, de), F32), pltpu.VMEM((2, de, d), F32),
                        pltpu.VMEM((d, de), BF16), pltpu.VMEM((d, de), BF16), pltpu.VMEM((de, d), BF16),
                        pltpu.SemaphoreType.DMA((2,)), pltpu.SemaphoreType.DMA((2,)),
                        pltpu.SemaphoreType.DMA((2, 3))])
    return pl.pallas_call(
        _experts_body, grid_spec=grid_spec, out_shape=jax.ShapeDtypeStruct(xs.shape, F32),
        compiler_params=_cparams(("arbitrary",)), name="experts")(*items, xs, wg, wu, wd)


def _combine_body(pos_ref, pos_next_ref, h_ref, w_ref, g_ref, ys_hbm, y_ref, buf, sem, *, final_norm):
    i = pl.program_id(0)
    tile = h_ref.shape[0]
    par = lax.rem(i, 2)

    def gather(table, slot):
        def issue(r, carry):
            for k in range(MOE_TOP_K):
                p = table[MOE_TOP_K * r + k]
                _row_copy(ys_hbm.at[_token_rows(p)], buf.at[slot, k, _token_rows(r)],
                          sem.at[slot, k]).start(priority=k)
            return carry

        lax.fori_loop(0, tile, issue, 0, unroll=8)

    @pl.when(i == 0)
    def _():
        gather(pos_ref, 0)

    @pl.when(i + 1 < pl.num_programs(0))
    def _():
        gather(pos_next_ref, 1 - par)

    for k in range(MOE_TOP_K):
        _row_copy(ys_hbm.at[pl.ds(0, tile * SUBLANES)], buf.at[par, k], sem.at[par, k]).wait()
    h = h_ref[...] + (w_ref[:, 0:1] * _load_token_tiles(buf, tile, (par, 0))
                      + w_ref[:, 1:2] * _load_token_tiles(buf, tile, (par, 1)))
    y_ref[...] = _rmsnorm(h, g_ref[...]) if final_norm else h


def _combine(pos3, h_all, w_t, g, ys, rows, row_off, final_norm):
    d = h_all.shape[1]
    tile = COMBINE_TILE
    off = row_off // tile
    last_block = h_all.shape[0] // tile - 1
    return pl.pallas_call(
        functools.partial(_combine_body, final_norm=final_norm), grid=(rows // tile,),
        in_specs=[pl.BlockSpec((MOE_TOP_K * tile,), lambda i: (off + i,), memory_space=pltpu.SMEM),
                  pl.BlockSpec((MOE_TOP_K * tile,), lambda i: (jnp.minimum(off + i + 1, last_block),),
                               memory_space=pltpu.SMEM),
                  pl.BlockSpec((tile, d), lambda i: (off + i, 0)),
                  pl.BlockSpec((tile, MOE_TOP_K), lambda i: (off + i, 0)),
                  _resident(g.shape),
                  pl.BlockSpec(memory_space=pl.ANY)],
        out_specs=pl.BlockSpec((tile, d), lambda i: (i, 0)),
        out_shape=jax.ShapeDtypeStruct((rows, d), F32),
        scratch_shapes=[pltpu.VMEM((2, MOE_TOP_K, tile * SUBLANES, LANES), F32),
                        pltpu.SemaphoreType.DMA((2, MOE_TOP_K))],
        compiler_params=_cparams(("arbitrary",)), name="combine")(pos3, pos3, h_all, w_t, g, ys)


def _lookup(table, idx):
    sel = idx[None] == jnp.arange(table.shape[0], dtype=I32).reshape((-1,) + (1,) * idx.ndim)
    return jnp.sum(jnp.where(sel, table.reshape(sel.shape[:1] + (1,) * idx.ndim), 0), axis=0)


def _work_items(counts, n_sorted):
    big, mid, small = EXPERT_WINDOWS
    n_exp = counts.shape[0]
    max_items = n_sorted // big + 2 * n_exp
    ends = jnp.cumsum(counts)
    starts = ends - counts
    units = (counts % big + small - 1) // small
    n_big = counts // big + (units == big // small)
    units = jnp.where(units == big // small, 0, units)
    n_mid = units // (mid // small)
    n_e = n_big + n_mid + units % (mid // small)
    item_end = jnp.cumsum(n_e)
    item_start = item_end - n_e
    n_items = item_end[-1]
    j = jnp.minimum(jnp.arange(max_items, dtype=I32), n_items - 1)
    e = jnp.sum((item_end[None, :] <= j[:, None]).astype(I32), axis=1)
    k = j - _lookup(item_start, e)
    nb, nm = _lookup(n_big, e), _lookup(n_mid, e)
    cls = jnp.where(k < nb, 0, jnp.where(k < nb + nm, 1, 2))
    row = _lookup(starts, e) + jnp.where(cls == 0, k * big, nb * big + jnp.where(cls == 1, 0, nm * mid))
    ordinal = jnp.cumsum((n_e > 0).astype(I32)) - 1
    first = jnp.where(k == 0, 1 + _lookup(ordinal, e) % 2, 0)
    nxt_item = _lookup(item_end, e)
    nxt = jnp.where(nxt_item < n_items, jnp.sum((item_end[None, :] <= nxt_item[:, None]).astype(I32), axis=1), -1)
    return (e, row.astype(I32), cls.astype(I32), first.astype(I32), nxt.astype(I32),
            n_items.reshape(1).astype(I32)), starts


def _s5_discretise(lam_re, lam_im, log_dt, b_re, b_im, c_re, c_im):
    g, p = lam_re.shape
    ch = b_re.shape[-1]
    lam_re = lam_re.astype(F32)
    lam_im = lam_im.astype(F32)
    dt = jnp.exp(log_dt.astype(F32))[:, None]
    mag = jnp.exp(lam_re * dt)
    ab_re = mag * jnp.cos(lam_im * dt)
    ab_im = mag * jnp.sin(lam_im * dt)
    den = lam_re * lam_re + lam_im * lam_im
    nr = ab_re - 1.0
    coef_re = (nr * lam_re + ab_im * lam_im) / den
    coef_im = (ab_im * lam_re - nr * lam_im) / den
    bb_re = coef_re[..., None] * b_re - coef_im[..., None] * b_im
    bb_im = coef_re[..., None] * b_im + coef_im[..., None] * b_re
    gh = g // 2
    eye = jnp.eye(gh, dtype=F32)

    def in_block(m):
        return jnp.einsum("gpc,gh->gchp", m, eye).reshape(gh * ch, gh * p)

    def out_block(m):
        return jnp.einsum("gcp,gh->gphc", m, eye).reshape(gh * p, gh * ch)

    bb = jnp.stack([jnp.concatenate([in_block(bb_re[k * gh:(k + 1) * gh]), in_block(bb_im[k * gh:(k + 1) * gh])],
                                    axis=1) for k in range(2)]).astype(BF16)
    cc = jnp.stack([jnp.concatenate([out_block(c_re[k * gh:(k + 1) * gh]), out_block(-c_im[k * gh:(k + 1) * gh])],
                                    axis=0) for k in range(2)]).astype(BF16)
    return ab_re.reshape(1, g * p), ab_im.reshape(1, g * p), bb, cc


def kernel(x_prompt, x_sample, state_hgrn, state_s5_re, state_s5_im, norm_mix_g, w_in, hgrn_lb_raw, hgrn_onorm_g, w_branch_a, s5_lambda_re, s5_lambda_im, s5_log_dt, s5_b_re, s5_b_im, s5_c_re, s5_c_im, s5_d, w_glu, b_glu, w_out, norm_ffn_g, w_router_group, b_router_group, w_router_expert, b_router_expert, w_exp_gate, w_exp_up, w_exp_down, norm_final_g):
    depth = norm_mix_g.shape[0]
    bp, lp, d = x_prompt.shape
    bs, ls, _ = x_sample.shape
    heads, dk = state_hgrn.shape[2], state_hgrn.shape[3]
    kw = heads * dk
    s5_groups, s5_state = state_s5_re.shape[2], state_s5_re.shape[3]
    s5_width = s5_d.shape[-1]
    nstate = s5_groups * s5_state
    moe_groups, _, experts = w_router_expert.shape[1:]
    n_exp = moe_groups * experts
    rows_p, rows_s = bp * lp, bs * ls
    total = rows_p + rows_s
    n_sorted = total * MOE_TOP_K
    assert kw == d and state_hgrn.shape[4] == dk, "column blocks assume key width == value width == model width"
    assert d == SUBLANES * LANES, "token-tile layout holds one token per (8, 128) tile"
    assert s5_groups % 2 == 0 and bp % SUBLANES == 0 and bs % HGRN_SEQ_TILE == 0

    lb_all = jnp.cumsum(jax.nn.softmax(hgrn_lb_raw.astype(F32), axis=0), axis=0)

    hp = x_prompt.reshape(rows_p, d)
    hs = x_sample.reshape(rows_s, d)
    hg_p, re_p, im_p, hg_s, re_s, im_s = [], [], [], [], [], []
    zeros_state = jnp.zeros((bp // SUBLANES, SUBLANES, nstate), F32)

    for l in range(depth):
        w = w_in[l]
        g_mix = norm_mix_g[l].reshape(1, d)
        w_u = w[:, 4 * kw:4 * kw + s5_width].astype(BF16)
        w_gates = w[:, 4 * kw + s5_width:].astype(BF16)
        proj = _in_proj(hp, hs, g_mix, w[:, :4 * kw].astype(BF16))

        ar, ai, bb, cc = _s5_discretise(s5_lambda_re[l], s5_lambda_im[l], s5_log_dt[l], s5_b_re[l], s5_b_im[l],
                                        s5_c_re[l], s5_c_im[l])
        s5_args = (ar, ai, bb, cc, s5_d[l].reshape(1, s5_width), w_glu[l].astype(BF16), b_glu[l].reshape(1, -1))
        yb_p, fr_p, fi_p = _s5_branch(hp, bp, lp, g_mix, w_u, zeros_state, zeros_state, *s5_args)
        yb_s, fr_s, fi_s = _s5_branch(hs, bs, ls, g_mix, w_u,
                                      state_s5_re[l].reshape(bs // SUBLANES, SUBLANES, nstate),
                                      state_s5_im[l].reshape(bs // SUBLANES, SUBLANES, nstate), *s5_args)

        lb = lb_all[l].reshape(1, kw)
        gn = hgrn_onorm_g[l].reshape(1, kw)
        o_p, hgp = _hgrn_long(proj, lb, gn, bp, lp, heads, dk, 0)
        o_s, hgs = _hgrn_short(proj, lb, gn, state_hgrn[l].astype(F32), ls, rows_p)

        nr = -(-(moe_groups + n_exp) // SUBLANES) * SUBLANES
        wr = jnp.concatenate([w_router_group[l].T, w_router_expert[l].transpose(0, 2, 1).reshape(n_exp, d)], axis=0)
        wr = jnp.pad(wr, ((0, nr - wr.shape[0]), (0, 0))).astype(BF16)
        br = jnp.pad(jnp.concatenate([b_router_group[l], b_router_expert[l].reshape(n_exp)]),
                     (0, nr - moe_groups - n_exp)).reshape(nr, 1).astype(F32)
        h_all, xn_all, ids, wts, ranks, cnt = _merge(
            o_p, o_s, yb_p.reshape(rows_p, d), yb_s.reshape(rows_s, d), hp, hs, g_mix, w_gates,
            w_branch_a[l].astype(BF16), w_out[l].astype(BF16), norm_ffn_g[l].reshape(1, d), wr, br,
            moe_groups, experts)
        items, starts = _work_items(cnt[:, 0].astype(I32), n_sorted)
        pos = _lookup(starts, ids) + ranks
        pos3 = pos.T.reshape(-1)
        xs = _dispatch(pos3, xn_all)
        ys = _experts(items, xs, w_exp_gate[l], w_exp_up[l], w_exp_down[l])

        last = l == depth - 1
        g_out = norm_final_g.reshape(1, d)
        hp = _combine(pos3, h_all, wts.T, g_out, ys, rows_p, 0, last)
        hs = _combine(pos3, h_all, wts.T, g_out, ys, rows_s, rows_p, last)

        hg_p.append(hgp)
        hg_s.append(hgs)
        re_p.append(fr_p.reshape(bp, s5_groups, s5_state))
        im_p.append(fi_p.reshape(bp, s5_groups, s5_state))
        re_s.append(fr_s.reshape(bs, s5_groups, s5_state))
        im_s.append(fi_s.reshape(bs, s5_groups, s5_state))

    y_prompt = hp.reshape(bp, lp, d).astype(x_prompt.dtype)
    y_sample = hs.reshape(bs, ls, d).astype(x_sample.dtype)
    return (y_prompt, y_sample, jnp.stack(hg_p), jnp.stack(re_p), jnp.stack(im_p),
            jnp.stack(hg_s), jnp.stack(re_s), jnp.stack(im_s))
```

```python
import functools

import jax
import jax.numpy as jnp
from jax import lax
from jax.experimental import pallas as pl
from jax.experimental.pallas import tpu as pltpu

F32 = jnp.float32
BF16 = jnp.bfloat16
I32 = jnp.int32

RMS_EPS = 1e-6
HG_CHUNK = 64
MOE_TOP_K = 2

V7X_VMEM_BYTES = 64 * 1024 * 1024
VMEM_LIMIT_BYTES = V7X_VMEM_BYTES - 8 * 1024 * 1024
SUBLANES = 8
LANES = 128

TOKEN_TILE = 512
MERGE_ROW_BLOCK = 512
DISPATCH_TILE = 1024
COMBINE_TILE = 512
EXPERT_TILE = 512
EXPERT_WINDOWS = (EXPERT_TILE, EXPERT_TILE // 2, EXPERT_TILE // 4)
S5_TIME_TILE = 128
S5_ROW_BLOCK = 1024
HGRN_TIME_TILE = 1024
HGRN_SEQ_TILE = 16
HGRN_CHUNK_UNROLL = 16
HGRN_SEQ_UNROLL = 8
PROJ_COL_TILE = 512


def _cparams(sem):
    return pltpu.CompilerParams(dimension_semantics=sem, vmem_limit_bytes=VMEM_LIMIT_BYTES)


def _resident(shape):
    nd = len(shape)
    return pl.BlockSpec(shape, lambda *_: (0,) * nd, pipeline_mode=pl.Buffered(1))


def _rmsnorm(x, g):
    return x * lax.rsqrt(jnp.mean(x * x, axis=-1, keepdims=True) + RMS_EPS) * g


def _two_source_specs(tm, width, n_first):
    return [pl.BlockSpec((tm, width), lambda i: (jnp.minimum(i, n_first - 1), 0)),
            pl.BlockSpec((tm, width), lambda i: (jnp.maximum(i - n_first, 0), 0))]


def _pick(first_ref, second_ref, n_first):
    return jnp.where(pl.program_id(0) < n_first, first_ref[...], second_ref[...])


def _store_token_tiles(ref, x, lead=(), row0=0):
    rows = x.shape[0]
    for c in range(SUBLANES):
        ref[lead + (pl.ds(row0 * SUBLANES + c, rows, stride=SUBLANES), slice(None))] = x[:, c * LANES:(c + 1) * LANES]


def _load_token_tiles(ref, rows, lead=()):
    return jnp.concatenate([ref[lead + (pl.ds(c, rows, stride=SUBLANES), slice(None))] for c in range(SUBLANES)],
                           axis=-1)


def _inproj_body(xp_ref, xs_ref, g_ref, w_ref, o_ref, *, n_first):
    xb = _rmsnorm(_pick(xp_ref, xs_ref, n_first), g_ref[...]).astype(BF16)
    for j in range(0, w_ref.shape[1], PROJ_COL_TILE):
        o_ref[:, j:j + PROJ_COL_TILE] = jnp.dot(xb, w_ref[:, j:j + PROJ_COL_TILE], preferred_element_type=F32)


def _in_proj(xp, xs, g, w):
    d = xp.shape[1]
    n = w.shape[1]
    tm = TOKEN_TILE
    total = xp.shape[0] + xs.shape[0]
    n_first = xp.shape[0] // tm
    return pl.pallas_call(
        functools.partial(_inproj_body, n_first=n_first), grid=(total // tm,),
        in_specs=_two_source_specs(tm, d, n_first) + [_resident((1, d)), _resident((d, n))],
        out_specs=pl.BlockSpec((tm, n), lambda i: (i, 0)),
        out_shape=jax.ShapeDtypeStruct((total, n), F32),
        compiler_params=_cparams(("parallel",)), name="in_proj")(xp, xs, g, w)


def _s5_body(*refs, tt, groups, nstate, column_inputs):
    n_x = len(refs) - 19
    x_refs = refs[:n_x]
    (gm_ref, wu_ref, h0r_ref, h0i_ref, ar_ref, ai_ref, bb_ref, cc_ref, d_ref, wg_ref, bg_ref,
     y_ref, hr_out, hi_out, hr_scr, hi_scr, bu_scr, x_scr, y_scr) = refs[n_x:]
    j = pl.program_id(1)
    half = nstate // 2
    d = x_scr.shape[0] * LANES
    kw = wu_ref.shape[-1] // 2

    @pl.when(j == 0)
    def _():
        hr_scr[...] = h0r_ref[...]
        hi_scr[...] = h0i_ref[...]

    if column_inputs:
        for s in range(d // LANES):
            for g in range(groups):
                for t in range(tt):
                    r0 = (g * tt + t) * SUBLANES
                    x_scr[s, r0:r0 + SUBLANES, :] = x_refs[s][pl.ds(g * SUBLANES * tt + t, SUBLANES, stride=tt), :]
    else:
        for b in range(SUBLANES):
            xb = x_refs[b][...]
            for s in range(d // LANES):
                x_scr[s, pl.ds(b, tt, stride=SUBLANES), :] = xb[:, s * LANES:(s + 1) * LANES]
    rows = groups * tt * SUBLANES
    row_blocks = [slice(r, min(r + S5_ROW_BLOCK, rows)) for r in range(0, rows, S5_ROW_BLOCK)]
    us = []
    for rb in row_blocks:
        x = jnp.concatenate([x_scr[s, rb, :] for s in range(d // LANES)], axis=-1)
        u = jnp.dot(_rmsnorm(x, gm_ref[...]).astype(BF16), wu_ref[...], preferred_element_type=F32)
        us.append(u)
        ub16 = u.astype(BF16)
        for kt in range(2):
            ukt = ub16[:, kt * kw:(kt + 1) * kw]
            bu_scr[rb, kt * half:(kt + 1) * half] = jnp.dot(ukt, bb_ref[kt, :, :half], preferred_element_type=F32)
            bu_scr[rb, nstate + kt * half:nstate + (kt + 1) * half] = jnp.dot(
                ukt, bb_ref[kt, :, half:], preferred_element_type=F32)

    lane_chunk = 512
    for lc in range(nstate // lane_chunk):
        lo = lc * lane_chunk
        re_sl = slice(lo, lo + lane_chunk)
        im_sl = slice(nstate + lo, nstate + lo + lane_chunk)
        ar = jnp.broadcast_to(ar_ref[:, re_sl], (SUBLANES, lane_chunk))
        ai = jnp.broadcast_to(ai_ref[:, re_sl], (SUBLANES, lane_chunk))

        for g in range(groups):
            hr, hi = hr_scr[g, :, re_sl], hi_scr[g, :, re_sl]
            for t in range(tt):
                r0 = (g * tt + t) * SUBLANES
                rs = slice(r0, r0 + SUBLANES)
                hr, hi = (ar * hr - ai * hi + bu_scr[rs, re_sl], ar * hi + ai * hr + bu_scr[rs, im_sl])
                bu_scr[rs, re_sl] = hr
                bu_scr[rs, im_sl] = hi
            hr_scr[g, :, re_sl] = hr
            hi_scr[g, :, re_sl] = hi

    dm = wg_ref.shape[-1] // 2
    for rb, u in zip(row_blocks, us):
        ys = []
        for n in range(2):
            h_re = bu_scr[rb, n * half:(n + 1) * half].astype(BF16)
            h_im = bu_scr[rb, nstate + n * half:nstate + (n + 1) * half].astype(BF16)
            ys.append(jnp.dot(h_re, cc_ref[n, :half, :], preferred_element_type=F32)
                      + jnp.dot(h_im, cc_ref[n, half:, :], preferred_element_type=F32))
        y = jnp.concatenate(ys, axis=-1) + d_ref[...] * u
        z = jnp.dot(jax.nn.gelu(y).astype(BF16), wg_ref[...], preferred_element_type=F32) + bg_ref[...]
        yb = z[:, :dm] * jax.nn.sigmoid(z[:, dm:])
        for s in range(dm // LANES):
            y_scr[s, rb, :] = yb[:, s * LANES:(s + 1) * LANES]
    for g in range(groups):
        for b in range(SUBLANES):
            for s in range(dm // LANES):
                y_ref[g * SUBLANES + b, :, s * LANES:(s + 1) * LANES] = y_scr[
                    s, pl.ds(g * tt * SUBLANES + b, tt, stride=SUBLANES), :]

    @pl.when(j == pl.num_programs(1) - 1)
    def _():
        hr_out[...] = hr_scr[...]
        hi_out[...] = hi_scr[...]


def _s5_branch(x2d, batch, seq, g_mix, w_u, h0r, h0i, ar, ai, bb, cc, d_skip, w_glu, b_glu):
    nstate = ar.shape[-1]
    d = x2d.shape[1]
    dm = w_glu.shape[1] // 2
    tt = min(S5_TIME_TILE, seq)
    nj = seq // tt
    ngroups = batch // SUBLANES
    column_inputs = nj == 1
    groups = min(ngroups, max(1, S5_ROW_BLOCK // (tt * SUBLANES))) if column_inputs else 1
    rows = groups * tt * SUBLANES
    body = functools.partial(_s5_body, tt=tt, groups=groups, nstate=nstate, column_inputs=column_inputs)
    if column_inputs:
        x_specs = [pl.BlockSpec((rows, LANES), lambda i, j, s=s: (i, s)) for s in range(d // LANES)]
    else:
        x_specs = [pl.BlockSpec((tt, d), lambda i, j, b=b: ((i * SUBLANES + b) * nj + j, 0))
                   for b in range(SUBLANES)]
    state_spec = pl.BlockSpec((groups, SUBLANES, nstate), lambda i, j: (i, 0, 0))
    return pl.pallas_call(
        body, grid=(ngroups // groups, nj),
        in_specs=x_specs + [
            _resident(g_mix.shape), _resident(w_u.shape),
            state_spec, state_spec, _resident(ar.shape), _resident(ai.shape), _resident(bb.shape),
            _resident(cc.shape), _resident(d_skip.shape), _resident(w_glu.shape), _resident(b_glu.shape)],
        out_specs=[pl.BlockSpec((groups * SUBLANES, tt, dm), lambda i, j: (i, j, 0)), state_spec, state_spec],
        out_shape=[jax.ShapeDtypeStruct((batch, seq, dm), F32),
                   jax.ShapeDtypeStruct((ngroups, SUBLANES, nstate), F32),
                   jax.ShapeDtypeStruct((ngroups, SUBLANES, nstate), F32)],
        scratch_shapes=[pltpu.VMEM((groups, SUBLANES, nstate), F32), pltpu.VMEM((groups, SUBLANES, nstate), F32),
                        pltpu.VMEM((rows, 2 * nstate), F32),
                        pltpu.VMEM((d // LANES, rows, LANES), F32),
                        pltpu.VMEM((dm // LANES, rows, LANES), F32)],
        compiler_params=_cparams(("parallel", "arbitrary")), name="s5_branch")(
            *([x2d] * len(x_specs)), g_mix, w_u, h0r, h0i, ar, ai, bb, cc, d_skip, w_glu, b_glu)


def _cumsum_rows(x, c):
    row = lax.broadcasted_iota(I32, x.shape, 0) & (c - 1)
    s = 1
    while s < c:
        x = x + jnp.where(row >= s, pltpu.roll(x, s, axis=0), 0.0)
        s *= 2
    return x


def _hgrn_gates(q, fr, lb, scale, c):
    rows, n = q.shape
    f = lb + (1.0 - lb) * jax.nn.sigmoid(fr)
    k = 1.0 - f
    b = _cumsum_rows(jnp.log(f), c)
    b3 = b.reshape(rows // c, c, n)
    b_last = jnp.broadcast_to(b3[:, c - 1:c, :], b3.shape).reshape(rows, n)
    q_dec = (q * scale) * jnp.exp(b)
    k_dec = k * jnp.exp(-b)
    k_end = k * jnp.exp(b_last - b)
    return q_dec, k_dec, k_end, jnp.exp(b_last)


def _causal_scores(q_dec, k_dec):
    c = q_dec.shape[0]
    s = lax.dot_general(q_dec, k_dec, (((1,), (1,)), ((), ())), preferred_element_type=F32)
    keep = lax.broadcasted_iota(I32, (c, c), 0) >= lax.broadcasted_iota(I32, (c, c), 1)
    return jnp.where(keep, s, 0.0).astype(BF16)


def _gated_out(o, gn, og):
    o = o * lax.rsqrt(jnp.mean(o * o, axis=-1, keepdims=True) + RMS_EPS) * gn
    return (o * jax.nn.silu(og)).astype(BF16)


def _hgrn_long_body(q_ref, f_ref, v_ref, og_ref, lb_ref, gn_ref, o_ref, sfin_ref, st_scr, *, c, heads, dk, scale):
    j = pl.program_id(1)

    @pl.when(j == 0)
    def _():
        st_scr[...] = jnp.zeros_like(st_scr)

    def chunk(ci, carry):
        rs = pl.ds(pl.multiple_of(ci * c, c), c)
        for h in range(heads):
            hs = slice(h * dk, (h + 1) * dk)
            q_dec, k_dec, k_end, decay = _hgrn_gates(q_ref[rs, hs], f_ref[rs, hs], lb_ref[:, hs], scale, c)
            q_dec = q_dec.astype(BF16)
            v = v_ref[rs, hs].astype(BF16)
            scores = _causal_scores(q_dec, k_dec.astype(BF16))
            st = st_scr[h]
            o = (lax.dot_general(q_dec, st.astype(BF16), (((1,), (1,)), ((), ())), preferred_element_type=F32)
                 + jnp.dot(scores, v, preferred_element_type=F32))
            st_scr[h] = decay[:1] * st + lax.dot_general(
                v, k_end.astype(BF16), (((0,), (0,)), ((), ())), preferred_element_type=F32)
            o_ref[rs, hs] = _gated_out(o, gn_ref[:, hs], og_ref[rs, hs])
        return carry

    lax.fori_loop(0, q_ref.shape[0] // c, chunk, 0, unroll=HGRN_CHUNK_UNROLL)

    @pl.when(j == pl.num_programs(1) - 1)
    def _():
        for h in range(heads):
            sfin_ref[0, h] = st_scr[h].T


def _hgrn_long(proj, lb, gn, batch, seq, heads, dk, row_off):
    width = heads * dk
    tb = min(HGRN_TIME_TILE, seq)
    nj = seq // tb
    off = row_off // tb
    c = min(HG_CHUNK, seq)
    body = functools.partial(_hgrn_long_body, c=c, heads=heads, dk=dk, scale=dk ** -0.5)

    def col(k):
        return pl.BlockSpec((tb, width), lambda b, j, k=k: (off + b * nj + j, k))

    return pl.pallas_call(
        body, grid=(batch, nj),
        in_specs=[col(0), col(1), col(2), col(3), _resident(lb.shape), _resident(gn.shape)],
        out_specs=[pl.BlockSpec((tb, width), lambda b, j: (b * nj + j, 0)),
                   pl.BlockSpec((1, heads, dk, dk), lambda b, j: (b, 0, 0, 0))],
        out_shape=[jax.ShapeDtypeStruct((batch * seq, width), BF16),
                   jax.ShapeDtypeStruct((batch, heads, dk, dk), F32)],
        scratch_shapes=[pltpu.VMEM((heads, dk, dk), F32)],
        compiler_params=_cparams(("parallel", "arbitrary")), name="hgrn_long")(proj, proj, proj, proj, lb, gn)


def _hgrn_short_body(q_ref, f_ref, v_ref, og_ref, lb_ref, gn_ref, s0_ref, o_ref, snew_ref, *, c, heads, dk, scale):
    def one_seq(sq, carry):
        rs = pl.ds(pl.multiple_of(sq * c, c), c)
        for h in range(heads):
            hs = slice(h * dk, (h + 1) * dk)
            q_dec, k_dec, k_end, decay = _hgrn_gates(q_ref[rs, hs], f_ref[rs, hs], lb_ref[:, hs], scale, c)
            q_dec = q_dec.astype(BF16)
            v = v_ref[rs, hs].astype(BF16)
            scores = _causal_scores(q_dec, k_dec.astype(BF16))
            s0 = s0_ref[sq, h]
            o = (jnp.dot(q_dec, s0.astype(BF16), preferred_element_type=F32)
                 + jnp.dot(scores, v, preferred_element_type=F32))
            decay_col = jnp.broadcast_to(decay[:1], (dk, dk)).T
            snew_ref[sq, h] = decay_col * s0 + lax.dot_general(
                k_end.astype(BF16), v, (((0,), (0,)), ((), ())), preferred_element_type=F32)
            o_ref[rs, hs] = _gated_out(o, gn_ref[:, hs], og_ref[rs, hs])
        return carry

    lax.fori_loop(0, s0_ref.shape[0], one_seq, 0, unroll=HGRN_SEQ_UNROLL)


def _hgrn_short(proj, lb, gn, s0, seq, row_off):
    batch, heads, dk, _ = s0.shape
    width = heads * dk
    nb = HGRN_SEQ_TILE
    rows = nb * seq
    off = row_off // rows
    body = functools.partial(_hgrn_short_body, c=seq, heads=heads, dk=dk, scale=dk ** -0.5)

    def col(k):
        return pl.BlockSpec((rows, width), lambda i, k=k: (off + i, k))

    state_spec = pl.BlockSpec((nb, heads, dk, dk), lambda i: (i, 0, 0, 0))
    return pl.pallas_call(
        body, grid=(batch // nb,),
        in_specs=[col(0), col(1), col(2), col(3), _resident(lb.shape), _resident(gn.shape), state_spec],
        out_specs=[pl.BlockSpec((rows, width), lambda i: (i, 0)), state_spec],
        out_shape=[jax.ShapeDtypeStruct((batch * seq, width), BF16), jax.ShapeDtypeStruct(s0.shape, F32)],
        compiler_params=_cparams(("parallel",)), name="hgrn_short")(proj, proj, proj, proj, lb, gn, s0)


def _first_index_of_max(vals):
    m = vals[0]
    for v in vals[1:]:
        m = jnp.maximum(m, v)
    idx = jnp.full(m.shape, len(vals), I32)
    for e in range(len(vals) - 1, -1, -1):
        idx = jnp.where(vals[e] == m, e, idx)
    return m, idx


def _route_tile(lg, ids_ref, w_ref, rk_ref, cnt_ref, carry_scr, groups, experts):
    i = pl.program_id(0)
    tile = lg.shape[1]
    n_exp = groups * experts

    @pl.when(i == 0)
    def _():
        carry_scr[...] = jnp.zeros_like(carry_scr)

    gl = [lg[g:g + 1, :] for g in range(groups)]
    gmax, gidx = _first_index_of_max(gl)
    denom = jnp.exp(gl[0] - gmax)
    for g in range(1, groups):
        denom = denom + jnp.exp(gl[g] - gmax)
    g_w = 1.0 / denom

    el = []
    for e in range(experts):
        v = lg[groups + e:groups + e + 1, :]
        for g in range(1, groups):
            r = groups + g * experts + e
            v = jnp.where(gidx == g, lg[r:r + 1, :], v)
        el.append(v)
    v1, i1 = _first_index_of_max(el)
    rest = [jnp.where(i1 == e, -jnp.inf, el[e]) for e in range(experts)]
    v2, i2 = _first_index_of_max(rest)
    t = jnp.exp(v2 - v1)
    inv = 1.0 / (1.0 + t)
    e1 = gidx * experts + i1
    e2 = gidx * experts + i2

    erow = lax.broadcasted_iota(I32, (n_exp, tile), 0)
    oh1 = (erow == e1).astype(F32)
    oh2 = (erow == e2).astype(F32)
    oh = oh1 + oh2
    before = (lax.broadcasted_iota(I32, (tile, tile), 0) < lax.broadcasted_iota(I32, (tile, tile), 1))
    cnt = jnp.dot(oh.astype(BF16), before.astype(BF16), preferred_element_type=F32) + carry_scr[:, 0:1]
    ids_ref[0:1, :] = e1
    ids_ref[1:2, :] = e2
    w_ref[0:1, :] = inv * g_w
    w_ref[1:2, :] = (t * inv) * g_w
    rk_ref[0:1, :] = jnp.sum(oh1 * cnt, axis=0, keepdims=True).astype(I32)
    rk_ref[1:2, :] = jnp.sum(oh2 * cnt, axis=0, keepdims=True).astype(I32)
    carry_scr[...] = carry_scr[...] + jnp.sum(oh, axis=1, keepdims=True)

    @pl.when(i == pl.num_programs(0) - 1)
    def _():
        cnt_ref[...] = carry_scr[...]


def _merge_body(op_ref, os_ref, ybp_ref, ybs_ref, xp_ref, xs_ref, gm_ref, wgt_ref, wa_ref, wo_ref, gf_ref, wr_ref,
                br_ref, h_ref, xn_ref, ids_ref, w_ref, rk_ref, cnt_ref, carry_scr, *, n_first, groups, experts):
    tm, d = h_ref.shape
    first = pl.program_id(0) < n_first

    def pick(a_ref, b_ref, rb):
        return jnp.where(first, a_ref[rb, :], b_ref[rb, :])

    blocks = [slice(r, r + MERGE_ROW_BLOCK) for r in range(0, tm, MERGE_ROW_BLOCK)]
    xs = [pick(xp_ref, xs_ref, rb) for rb in blocks]
    gates = [jnp.dot(_rmsnorm(x, gm_ref[...]).astype(BF16), wgt_ref[...], preferred_element_type=F32) for x in xs]
    y_as = [jnp.dot(pick(op_ref, os_ref, rb), wa_ref[...], preferred_element_type=F32) for rb in blocks]
    logits = []
    for rb, x, g, y_a in zip(blocks, xs, gates, y_as):
        merged = jax.nn.sigmoid(g[:, :d]) * y_a + jax.nn.sigmoid(g[:, d:]) * pick(ybp_ref, ybs_ref, rb)
        h = x + jnp.dot(merged.astype(BF16), wo_ref[...], preferred_element_type=F32)
        h_ref[rb, :] = h
        xn = _rmsnorm(h, gf_ref[...])
        _store_token_tiles(xn_ref, xn, row0=rb.start)
        logits.append(lax.dot_general(wr_ref[...], xn.astype(BF16), (((1,), (1,)), ((), ())),
                                      preferred_element_type=F32))
    logits_t = jnp.concatenate(logits, axis=1) + br_ref[...]
    _route_tile(logits_t, ids_ref, w_ref, rk_ref, cnt_ref, carry_scr, groups, experts)


def _merge(o_p, o_s, yb_p, yb_s, xp, xs, g_mix, w_gates, wa, wo, gf, wr, br, groups, experts):
    d = xp.shape[1]
    total = xp.shape[0] + xs.shape[0]
    tm = TOKEN_TILE
    n_first = xp.shape[0] // tm
    n_exp = groups * experts
    pair = _two_source_specs(tm, d, n_first)
    top = pl.BlockSpec((MOE_TOP_K, tm), lambda i: (0, i))
    weights = [g_mix, w_gates, wa, wo, gf, wr, br]
    return pl.pallas_call(
        functools.partial(_merge_body, n_first=n_first, groups=groups, experts=experts), grid=(total // tm,),
        in_specs=pair + pair + pair + [_resident(w.shape) for w in weights],
        out_specs=[pl.BlockSpec((tm, d), lambda i: (i, 0)), pl.BlockSpec((tm * SUBLANES, LANES), lambda i: (i, 0)),
                   top, top, top, pl.BlockSpec((n_exp, LANES), lambda i: (0, 0))],
        out_shape=[jax.ShapeDtypeStruct((total, d), F32), jax.ShapeDtypeStruct((total * SUBLANES, LANES), F32),
                   jax.ShapeDtypeStruct((MOE_TOP_K, total), I32), jax.ShapeDtypeStruct((MOE_TOP_K, total), F32),
                   jax.ShapeDtypeStruct((MOE_TOP_K, total), I32), jax.ShapeDtypeStruct((n_exp, LANES), F32)],
        scratch_shapes=[pltpu.VMEM((n_exp, LANES), F32)],
        compiler_params=_cparams(("arbitrary",)), name="merge_route")(o_p, o_s, yb_p, yb_s, xp, xs, *weights)


def _row_copy(src, dst, sem):
    return pltpu.make_async_copy(src, dst, sem)


def _token_rows(r):
    return pl.ds(pl.multiple_of(r * SUBLANES, SUBLANES), SUBLANES)


def _dispatch_body(pos_ref, x_ref, o_hbm, ring, zero_scr, sem, pad_sem, *, n_sorted):
    i = pl.program_id(0)
    tile = x_ref.shape[0] // SUBLANES
    par = lax.rem(i, 2)

    @pl.when(i == 0)
    def _():
        zero_scr[...] = jnp.zeros_like(zero_scr)
        pad = _row_copy(zero_scr, o_hbm.at[pl.ds(n_sorted * SUBLANES, zero_scr.shape[0])], pad_sem.at[0])
        pad.start()
        pad.wait()

    ring[par] = x_ref[...]

    def issue(r, carry):
        for k in range(MOE_TOP_K):
            p = pos_ref[MOE_TOP_K * r + k]
            _row_copy(ring.at[par, _token_rows(r)], o_hbm.at[_token_rows(p)], sem.at[par, k]).start(priority=k)
        return carry

    lax.fori_loop(0, tile, issue, 0, unroll=8)

    def drain(slot):
        for k in range(MOE_TOP_K):
            _row_copy(ring.at[slot], o_hbm.at[pl.ds(0, tile * SUBLANES)], sem.at[slot, k]).wait()

    @pl.when(i > 0)
    def _():
        drain(1 - par)

    @pl.when(i == pl.num_programs(0) - 1)
    def _():
        drain(par)


def _dispatch(pos3, xn_tiles):
    total = xn_tiles.shape[0] // SUBLANES
    tile = DISPATCH_TILE
    n_sorted = total * MOE_TOP_K
    pad = EXPERT_WINDOWS[-1]
    return pl.pallas_call(
        functools.partial(_dispatch_body, n_sorted=n_sorted), grid=(total // tile,),
        in_specs=[pl.BlockSpec((MOE_TOP_K * tile,), lambda i: (i,), memory_space=pltpu.SMEM),
                  pl.BlockSpec((tile * SUBLANES, LANES), lambda i: (i, 0))],
        out_specs=pl.BlockSpec(memory_space=pl.ANY),
        out_shape=jax.ShapeDtypeStruct(((n_sorted + pad) * SUBLANES, LANES), xn_tiles.dtype),
        scratch_shapes=[pltpu.VMEM((2, tile * SUBLANES, LANES), xn_tiles.dtype),
                        pltpu.VMEM((pad * SUBLANES, LANES), xn_tiles.dtype),
                        pltpu.SemaphoreType.DMA((2, MOE_TOP_K)), pltpu.SemaphoreType.DMA((1,))],
        compiler_params=_cparams(("arbitrary",)), name="dispatch")(pos3, xn_tiles)


def _experts_body(it_exp, it_row, it_cls, it_first, it_next, n_items, xs_hbm, wg_hbm, wu_hbm, wd_hbm, ys_hbm,
                  xbuf, ybuf, wg_s, wu_s, wd_s, wg_b, wu_b, wd_b, sem_in, sem_out, sem_w):
    j = pl.program_id(0)
    n = n_items[0]
    pad = EXPERT_WINDOWS[-1]
    slot = lax.rem(j, 2)

    def weight_copies(e, s):
        return [pltpu.make_async_copy(hbm.at[e], stage.at[s], sem_w.at[s, t])
                for t, (hbm, stage) in enumerate(((wg_hbm, wg_s), (wu_hbm, wu_s), (wd_hbm, wd_s)))]

    def by_size(item, fn):
        for ci, m in enumerate(EXPERT_WINDOWS):
            pl.when(it_cls[item] == ci)(functools.partial(fn, m))

    def window(item, m):
        return pl.ds(pl.multiple_of(it_row[item] * SUBLANES, SUBLANES), m * SUBLANES)

    def in_copy(item, s, m):
        return pltpu.make_async_copy(xs_hbm.at[window(item, m)], xbuf.at[s, pl.ds(0, m * SUBLANES)], sem_in.at[s])

    def out_copy(item, s, m):
        return pltpu.make_async_copy(ybuf.at[s, pl.ds(0, m * SUBLANES)], ys_hbm.at[window(item, m)], sem_out.at[s])

    def compute(m):
        x = _load_token_tiles(xbuf, m, (slot,)).astype(BF16)
        hg = jnp.dot(x, wg_b[...], preferred_element_type=F32)
        hu = jnp.dot(x, wu_b[...], preferred_element_type=F32)
        hid = (jax.nn.silu(hg) * hu).astype(BF16)
        _store_token_tiles(ybuf, jnp.dot(hid, wd_b[...], preferred_element_type=F32), (slot,))

    @pl.when(j < n)
    def _():
        @pl.when(j == 0)
        def _():
            by_size(0, lambda m: in_copy(0, 0, m).start())
            for c in weight_copies(it_exp[0], it_first[0] - 1):
                c.start()
            tail_rows = pl.ds(0, pad * SUBLANES)
            ybuf[1, tail_rows, :] = jnp.zeros((pad * SUBLANES, LANES), F32)
            tail = pltpu.make_async_copy(
                ybuf.at[1, tail_rows], ys_hbm.at[pl.ds(ys_hbm.shape[0] - pad * SUBLANES, pad * SUBLANES)],
                sem_out.at[1])
            tail.start()
            tail.wait()

        @pl.when(j + 1 < n)
        def _():
            by_size(j + 1, lambda m: in_copy(j + 1, 1 - slot, m).start())

        @pl.when(it_first[j] > 0)
        def _():
            s = it_first[j] - 1
            for c in weight_copies(it_exp[j], s):
                c.wait()
            wg_b[...] = wg_s[s].astype(BF16)
            wu_b[...] = wu_s[s].astype(BF16)
            wd_b[...] = wd_s[s].astype(BF16)

            @pl.when(it_next[j] >= 0)
            def _():
                for c in weight_copies(it_next[j], 1 - s):
                    c.start()

        by_size(j, lambda m: in_copy(j, slot, m).wait())
        by_size(j, compute)

        @pl.when(j > 0)
        def _():
            by_size(j - 1, lambda m: out_copy(j - 1, 1 - slot, m).wait())

        by_size(j, lambda m: out_copy(j, slot, m).start())

        @pl.when(j == n - 1)
        def _():
            by_size(j, lambda m: out_copy(j, slot, m).wait())


def _experts(items, xs, wg, wu, wd):
    d, de = wg.shape[1], wg.shape[2]
    tm = EXPERT_TILE
    max_items = items[0].shape[0]
    grid_spec = pltpu.PrefetchScalarGridSpec(
        num_scalar_prefetch=6, grid=(max_items,),
        in_specs=[pl.BlockSpec(memory_space=pl.ANY)] * 4,
        out_specs=pl.BlockSpec(memory_space=pl.ANY),
        scratch_shapes=[pltpu.VMEM((2, tm * SUBLANES, LANES), F32), pltpu.VMEM((2, tm * SUBLANES, LANES), F32),
                        pltpu.VMEM((2, d, de), F32), pltpu.VMEM((2, d, de), F32), pltpu.VMEM((2, de, d), F32),
                        pltpu.VMEM((d, de), BF16), pltpu.VMEM((d, de), BF16), pltpu.VMEM((de, d), BF16),
                        pltpu.SemaphoreType.DMA((2,)), pltpu.SemaphoreType.DMA((2,)),
                        pltpu.SemaphoreType.DMA((2, 3))])
    return pl.pallas_call(
        _experts_body, grid_spec=grid_spec, out_shape=jax.ShapeDtypeStruct(xs.shape, F32),
        compiler_params=_cparams(("arbitrary",)), name="experts")(*items, xs, wg, wu, wd)


def _combine_body(pos_ref, pos_next_ref, h_ref, w_ref, g_ref, ys_hbm, y_ref, buf, sem, *, final_norm):
    i = pl.program_id(0)
    tile = h_ref.shape[0]
    par = lax.rem(i, 2)

    def gather(table, slot):
        def issue(r, carry):
            for k in range(MOE_TOP_K):
                p = table[MOE_TOP_K * r + k]
                _row_copy(ys_hbm.at[_token_rows(p)], buf.at[slot, k, _token_rows(r)],
                          sem.at[slot, k]).start(priority=k)
            return carry

        lax.fori_loop(0, tile, issue, 0, unroll=8)

    @pl.when(i == 0)
    def _():
        gather(pos_ref, 0)

    @pl.when(i + 1 < pl.num_programs(0))
    def _():
        gather(pos_next_ref, 1 - par)

    for k in range(MOE_TOP_K):
        _row_copy(ys_hbm.at[pl.ds(0, tile * SUBLANES)], buf.at[par, k], sem.at[par, k]).wait()
    h = h_ref[...] + (w_ref[:, 0:1] * _load_token_tiles(buf, tile, (par, 0))
                      + w_ref[:, 1:2] * _load_token_tiles(buf, tile, (par, 1)))
    y_ref[...] = _rmsnorm(h, g_ref[...]) if final_norm else h


def _combine(pos3, h_all, w_t, g, ys, rows, row_off, final_norm):
    d = h_all.shape[1]
    tile = COMBINE_TILE
    off = row_off // tile
    last_block = h_all.shape[0] // tile - 1
    return pl.pallas_call(
        functools.partial(_combine_body, final_norm=final_norm), grid=(rows // tile,),
        in_specs=[pl.BlockSpec((MOE_TOP_K * tile,), lambda i: (off + i,), memory_space=pltpu.SMEM),
                  pl.BlockSpec((MOE_TOP_K * tile,), lambda i: (jnp.minimum(off + i + 1, last_block),),
                               memory_space=pltpu.SMEM),
                  pl.BlockSpec((tile, d), lambda i: (off + i, 0)),
                  pl.BlockSpec((tile, MOE_TOP_K), lambda i: (off + i, 0)),
                  _resident(g.shape),
                  pl.BlockSpec(memory_space=pl.ANY)],
        out_specs=pl.BlockSpec((tile, d), lambda i: (i, 0)),
        out_shape=jax.ShapeDtypeStruct((rows, d), F32),
        scratch_shapes=[pltpu.VMEM((2, MOE_TOP_K, tile * SUBLANES, LANES), F32),
                        pltpu.SemaphoreType.DMA((2, MOE_TOP_K))],
        compiler_params=_cparams(("arbitrary",)), name="combine")(pos3, pos3, h_all, w_t, g, ys)


def _lookup(table, idx):
    sel = idx[None] == jnp.arange(table.shape[0], dtype=I32).reshape((-1,) + (1,) * idx.ndim)
    return jnp.sum(jnp.where(sel, table.reshape(sel.shape[:1] + (1,) * idx.ndim), 0), axis=0)


def _work_items(counts, n_sorted):
    big, mid, small = EXPERT_WINDOWS
    n_exp = counts.shape[0]
    max_items = n_sorted // big + 2 * n_exp
    ends = jnp.cumsum(counts)
    starts = ends - counts
    units = (counts % big + small - 1) // small
    n_big = counts // big + (units == big // small)
    units = jnp.where(units == big // small, 0, units)
    n_mid = units // (mid // small)
    n_e = n_big + n_mid + units % (mid // small)
    item_end = jnp.cumsum(n_e)
    item_start = item_end - n_e
    n_items = item_end[-1]
    j = jnp.minimum(jnp.arange(max_items, dtype=I32), n_items - 1)
    e = jnp.sum((item_end[None, :] <= j[:, None]).astype(I32), axis=1)
    k = j - _lookup(item_start, e)
    nb, nm = _lookup(n_big, e), _lookup(n_mid, e)
    cls = jnp.where(k < nb, 0, jnp.where(k < nb + nm, 1, 2))
    row = _lookup(starts, e) + jnp.where(cls == 0, k * big, nb * big + jnp.where(cls == 1, 0, nm * mid))
    ordinal = jnp.cumsum((n_e > 0).astype(I32)) - 1
    first = jnp.where(k == 0, 1 + _lookup(ordinal, e) % 2, 0)
    nxt_item = _lookup(item_end, e)
    nxt = jnp.where(nxt_item < n_items, jnp.sum((item_end[None, :] <= nxt_item[:, None]).astype(I32), axis=1), -1)
    return (e, row.astype(I32), cls.astype(I32), first.astype(I32), nxt.astype(I32),
            n_items.reshape(1).astype(I32)), starts


def _s5_discretise(lam_re, lam_im, log_dt, b_re, b_im, c_re, c_im):
    g, p = lam_re.shape
    ch = b_re.shape[-1]
    lam_re = lam_re.astype(F32)
    lam_im = lam_im.astype(F32)
    dt = jnp.exp(log_dt.astype(F32))[:, None]
    mag = jnp.exp(lam_re * dt)
    ab_re = mag * jnp.cos(lam_im * dt)
    ab_im = mag * jnp.sin(lam_im * dt)
    den = lam_re * lam_re + lam_im * lam_im
    nr = ab_re - 1.0
    coef_re = (nr * lam_re + ab_im * lam_im) / den
    coef_im = (ab_im * lam_re - nr * lam_im) / den
    bb_re = coef_re[..., None] * b_re - coef_im[..., None] * b_im
    bb_im = coef_re[..., None] * b_im + coef_im[..., None] * b_re
    gh = g // 2
    eye = jnp.eye(gh, dtype=F32)

    def in_block(m):
        return jnp.einsum("gpc,gh->gchp", m, eye).reshape(gh * ch, gh * p)

    def out_block(m):
        return jnp.einsum("gcp,gh->gphc", m, eye).reshape(gh * p, gh * ch)

    bb = jnp.stack([jnp.concatenate([in_block(bb_re[k * gh:(k + 1) * gh]), in_block(bb_im[k * gh:(k + 1) * gh])],
                                    axis=1) for k in range(2)]).astype(BF16)
    cc = jnp.stack([jnp.concatenate([out_block(c_re[k * gh:(k + 1) * gh]), out_block(-c_im[k * gh:(k + 1) * gh])],
                                    axis=0) for k in range(2)]).astype(BF16)
    return ab_re.reshape(1, g * p), ab_im.reshape(1, g * p), bb, cc


def kernel(x_prompt, x_sample, state_hgrn, state_s5_re, state_s5_im, norm_mix_g, w_in, hgrn_lb_raw, hgrn_onorm_g, w_branch_a, s5_lambda_re, s5_lambda_im, s5_log_dt, s5_b_re, s5_b_im, s5_c_re, s5_c_im, s5_d, w_glu, b_glu, w_out, norm_ffn_g, w_router_group, b_router_group, w_router_expert, b_router_expert, w_exp_gate, w_exp_up, w_exp_down, norm_final_g):
    depth = norm_mix_g.shape[0]
    bp, lp, d = x_prompt.shape
    bs, ls, _ = x_sample.shape
    heads, dk = state_hgrn.shape[2], state_hgrn.shape[3]
    kw = heads * dk
    s5_groups, s5_state = state_s5_re.shape[2], state_s5_re.shape[3]
    s5_width = s5_d.shape[-1]
    nstate = s5_groups * s5_state
    moe_groups, _, experts = w_router_expert.shape[1:]
    n_exp = moe_groups * experts
    rows_p, rows_s = bp * lp, bs * ls
    total = rows_p + rows_s
    n_sorted = total * MOE_TOP_K
    assert kw == d and state_hgrn.shape[4] == dk, "column blocks assume key width == value width == model width"
    assert d == SUBLANES * LANES, "token-tile layout holds one token per (8, 128) tile"
    assert s5_groups % 2 == 0 and bp % SUBLANES == 0 and bs % HGRN_SEQ_TILE == 0

    lb_all = jnp.cumsum(jax.nn.softmax(hgrn_lb_raw.astype(F32), axis=0), axis=0)

    hp = x_prompt.reshape(rows_p, d)
    hs = x_sample.reshape(rows_s, d)
    hg_p, re_p, im_p, hg_s, re_s, im_s = [], [], [], [], [], []
    zeros_state = jnp.zeros((bp // SUBLANES, SUBLANES, nstate), F32)

    for l in range(depth):
        w = w_in[l]
        g_mix = norm_mix_g[l].reshape(1, d)
        w_u = w[:, 4 * kw:4 * kw + s5_width].astype(BF16)
        w_gates = w[:, 4 * kw + s5_width:].astype(BF16)
        proj = _in_proj(hp, hs, g_mix, w[:, :4 * kw].astype(BF16))

        ar, ai, bb, cc = _s5_discretise(s5_lambda_re[l], s5_lambda_im[l], s5_log_dt[l], s5_b_re[l], s5_b_im[l],
                                        s5_c_re[l], s5_c_im[l])
        s5_args = (ar, ai, bb, cc, s5_d[l].reshape(1, s5_width), w_glu[l].astype(BF16), b_glu[l].reshape(1, -1))
        yb_p, fr_p, fi_p = _s5_branch(hp, bp, lp, g_mix, w_u, zeros_state, zeros_state, *s5_args)
        yb_s, fr_s, fi_s = _s5_branch(hs, bs, ls, g_mix, w_u,
                                      state_s5_re[l].reshape(bs // SUBLANES, SUBLANES, nstate),
                                      state_s5_im[l].reshape(bs // SUBLANES, SUBLANES, nstate), *s5_args)

        lb = lb_all[l].reshape(1, kw)
        gn = hgrn_onorm_g[l].reshape(1, kw)
        o_p, hgp = _hgrn_long(proj, lb, gn, bp, lp, heads, dk, 0)
        o_s, hgs = _hgrn_short(proj, lb, gn, state_hgrn[l].astype(F32), ls, rows_p)

        nr = -(-(moe_groups + n_exp) // SUBLANES) * SUBLANES
        wr = jnp.concatenate([w_router_group[l].T, w_router_expert[l].transpose(0, 2, 1).reshape(n_exp, d)], axis=0)
        wr = jnp.pad(wr, ((0, nr - wr.shape[0]), (0, 0))).astype(BF16)
        br = jnp.pad(jnp.concatenate([b_router_group[l], b_router_expert[l].reshape(n_exp)]),
                     (0, nr - moe_groups - n_exp)).reshape(nr, 1).astype(F32)
        h_all, xn_all, ids, wts, ranks, cnt = _merge(
            o_p, o_s, yb_p.reshape(rows_p, d), yb_s.reshape(rows_s, d), hp, hs, g_mix, w_gates,
            w_branch_a[l].astype(BF16), w_out[l].astype(BF16), norm_ffn_g[l].reshape(1, d), wr, br,
            moe_groups, experts)
        items, starts = _work_items(cnt[:, 0].astype(I32), n_sorted)
        pos = _lookup(starts, ids) + ranks
        pos3 = pos.T.reshape(-1)
        xs = _dispatch(pos3, xn_all)
        ys = _experts(items, xs, w_exp_gate[l], w_exp_up[l], w_exp_down[l])

        last = l == depth - 1
        g_out = norm_final_g.reshape(1, d)
        hp = _combine(pos3, h_all, wts.T, g_out, ys, rows_p, 0, last)
        hs = _combine(pos3, h_all, wts.T, g_out, ys, rows_s, rows_p, last)

        hg_p.append(hgp)
        hg_s.append(hgs)
        re_p.append(fr_p.reshape(bp, s5_groups, s5_state))
        im_p.append(fi_p.reshape(bp, s5_groups, s5_state))
        re_s.append(fr_s.reshape(bs, s5_groups, s5_state))
        im_s.append(fi_s.reshape(bs, s5_groups, s5_state))

    y_prompt = hp.reshape(bp, lp, d).astype(x_prompt.dtype)
    y_sample = hs.reshape(bs, ls, d).astype(x_sample.dtype)
    return (y_prompt, y_sample, jnp.stack(hg_p), jnp.stack(re_p), jnp.stack(im_p),
            jnp.stack(hg_s), jnp.stack(re_s), jnp.stack(im_s))
```

```python
import functools
import math

import jax
import jax.numpy as jnp
from jax import lax
from jax.experimental import pallas as pl
from jax.experimental.pallas import tpu as pltpu

F32 = jnp.float32
BF16 = jnp.bfloat16
I32 = jnp.int32

RMS_EPS = 1e-6
HG_CHUNK = 64
MOE_TOP_K = 2

V7X_VMEM_BYTES = 64 * 1024 * 1024
VMEM_LIMIT_BYTES = V7X_VMEM_BYTES - 8 * 1024 * 1024
SUBLANES = 8
LANES = 128

TOKEN_TILE = 512
MERGE_ROW_BLOCK = 512
DISPATCH_TILE = 1024
COMBINE_TILE = 512
EXPERT_TILE = 512
EXPERT_WINDOWS = (EXPERT_TILE, EXPERT_TILE // 2, EXPERT_TILE // 4)
S5_TIME_TILE = 128
S5_ROW_BLOCK = 1024
HGRN_TIME_TILE = 1024
HGRN_SEQ_TILE = 16
HGRN_CHUNK_UNROLL = 16
HGRN_SEQ_UNROLL = 8
PROJ_COL_TILE = 512


def _cparams(sem):
    return pltpu.CompilerParams(dimension_semantics=sem, vmem_limit_bytes=VMEM_LIMIT_BYTES)


def _resident(shape):
    nd = len(shape)
    return pl.BlockSpec(shape, lambda *_: (0,) * nd, pipeline_mode=pl.Buffered(1))


def _rmsnorm(x, g):
    return x * lax.rsqrt(jnp.mean(x * x, axis=-1, keepdims=True) + RMS_EPS) * g


def _two_source_specs(tm, width, n_first):
    return [pl.BlockSpec((tm, width), lambda i: (jnp.minimum(i, n_first - 1), 0)),
            pl.BlockSpec((tm, width), lambda i: (jnp.maximum(i - n_first, 0), 0))]


def _pick(first_ref, second_ref, n_first):
    return jnp.where(pl.program_id(0) < n_first, first_ref[...], second_ref[...])


def _store_token_tiles(ref, x, lead=(), row0=0):
    rows = x.shape[0]
    for c in range(SUBLANES):
        ref[lead + (pl.ds(row0 * SUBLANES + c, rows, stride=SUBLANES), slice(None))] = x[:, c * LANES:(c + 1) * LANES]


def _load_token_tiles(ref, rows, lead=()):
    return jnp.concatenate([ref[lead + (pl.ds(c, rows, stride=SUBLANES), slice(None))] for c in range(SUBLANES)],
                           axis=-1)


def _inproj_body(xp_ref, xs_ref, g_ref, w_ref, o_ref, wb_scr, *, n_first):
    @pl.when(pl.program_id(0) == 0)
    def _():
        wb_scr[...] = w_ref[...].astype(BF16)

    xb = _rmsnorm(_pick(xp_ref, xs_ref, n_first), g_ref[...]).astype(BF16)
    for j in range(0, wb_scr.shape[1], PROJ_COL_TILE):
        o_ref[:, j:j + PROJ_COL_TILE] = jnp.dot(xb, wb_scr[:, j:j + PROJ_COL_TILE], preferred_element_type=F32)


def _in_proj(xp, xs, g, w, n):
    d = xp.shape[1]
    tm = TOKEN_TILE
    total = xp.shape[0] + xs.shape[0]
    n_first = xp.shape[0] // tm
    return pl.pallas_call(
        functools.partial(_inproj_body, n_first=n_first), grid=(total // tm,),
        in_specs=_two_source_specs(tm, d, n_first) + [
            _resident((1, d)), pl.BlockSpec((d, n), lambda i: (0, 0), pipeline_mode=pl.Buffered(1))],
        out_specs=pl.BlockSpec((tm, n), lambda i: (i, 0)),
        out_shape=jax.ShapeDtypeStruct((total, n), F32),
        scratch_shapes=[pltpu.VMEM((d, n), BF16)],
        compiler_params=_cparams(("arbitrary",)), name="in_proj")(xp, xs, g, w)


def _s5_body(*refs, tt, groups, nstate, column_inputs):
    n_x = len(refs) - 19
    x_refs = refs[:n_x]
    (gm_ref, wu_ref, h0r_ref, h0i_ref, ar_ref, ai_ref, bb_ref, cc_ref, d_ref, wg_ref, bg_ref,
     y_ref, hr_out, hi_out, hr_scr, hi_scr, bu_scr, x_scr, y_scr) = refs[n_x:]
    j = pl.program_id(1)
    half = nstate // 2
    d = x_scr.shape[0] * LANES
    kw = wu_ref.shape[-1] // 2

    @pl.when(j == 0)
    def _():
        hr_scr[...] = h0r_ref[...]
        hi_scr[...] = h0i_ref[...]

    if column_inputs:
        for s in range(d // LANES):
            for g in range(groups):
                for t in range(tt):
                    r0 = (g * tt + t) * SUBLANES
                    x_scr[s, r0:r0 + SUBLANES, :] = x_refs[s][pl.ds(g * SUBLANES * tt + t, SUBLANES, stride=tt), :]
    else:
        for b in range(SUBLANES):
            xb = x_refs[b][...]
            for s in range(d // LANES):
                x_scr[s, pl.ds(b, tt, stride=SUBLANES), :] = xb[:, s * LANES:(s + 1) * LANES]
    rows = groups * tt * SUBLANES
    row_blocks = [slice(r, min(r + S5_ROW_BLOCK, rows)) for r in range(0, rows, S5_ROW_BLOCK)]
    us = []
    for rb in row_blocks:
        x = jnp.concatenate([x_scr[s, rb, :] for s in range(d // LANES)], axis=-1)
        u = jnp.dot(_rmsnorm(x, gm_ref[...]).astype(BF16), wu_ref[...].astype(BF16), preferred_element_type=F32)
        us.append(u)
        ub16 = u.astype(BF16)
        for kt in range(2):
            ukt = ub16[:, kt * kw:(kt + 1) * kw]
            bu_scr[rb, kt * half:(kt + 1) * half] = jnp.dot(ukt, bb_ref[kt, :, :half], preferred_element_type=F32)
            bu_scr[rb, nstate + kt * half:nstate + (kt + 1) * half] = jnp.dot(
                ukt, bb_ref[kt, :, half:], preferred_element_type=F32)

    lane_chunk = 512
    for lc in range(nstate // lane_chunk):
        lo = lc * lane_chunk
        re_sl = slice(lo, lo + lane_chunk)
        im_sl = slice(nstate + lo, nstate + lo + lane_chunk)
        ar = jnp.broadcast_to(ar_ref[:, re_sl], (SUBLANES, lane_chunk))
        ai = jnp.broadcast_to(ai_ref[:, re_sl], (SUBLANES, lane_chunk))

        for g in range(groups):
            hr, hi = hr_scr[g, :, re_sl], hi_scr[g, :, re_sl]
            for t in range(tt):
                r0 = (g * tt + t) * SUBLANES
                rs = slice(r0, r0 + SUBLANES)
                hr, hi = (ar * hr - ai * hi + bu_scr[rs, re_sl], ar * hi + ai * hr + bu_scr[rs, im_sl])
                bu_scr[rs, re_sl] = hr
                bu_scr[rs, im_sl] = hi
            hr_scr[g, :, re_sl] = hr
            hi_scr[g, :, re_sl] = hi

    dm = wg_ref.shape[-1] // 2
    for rb, u in zip(row_blocks, us):
        ys = []
        for n in range(2):
            h_re = bu_scr[rb, n * half:(n + 1) * half].astype(BF16)
            h_im = bu_scr[rb, nstate + n * half:nstate + (n + 1) * half].astype(BF16)
            ys.append(jnp.dot(h_re, cc_ref[n, :half, :], preferred_element_type=F32)
                      + jnp.dot(h_im, cc_ref[n, half:, :], preferred_element_type=F32))
        y = jnp.concatenate(ys, axis=-1) + d_ref[...] * u
        z = jnp.dot(jax.nn.gelu(y).astype(BF16), wg_ref[...], preferred_element_type=F32) + bg_ref[...]
        yb = z[:, :dm] * jax.nn.sigmoid(z[:, dm:])
        for s in range(dm // LANES):
            y_scr[s, rb, :] = yb[:, s * LANES:(s + 1) * LANES]
    for g in range(groups):
        for b in range(SUBLANES):
            for s in range(dm // LANES):
                y_ref[g * SUBLANES + b, :, s * LANES:(s + 1) * LANES] = y_scr[
                    s, pl.ds(g * tt * SUBLANES + b, tt, stride=SUBLANES), :]

    @pl.when(j == pl.num_programs(1) - 1)
    def _():
        hr_out[...] = hr_scr[...]
        hi_out[...] = hi_scr[...]


def _s5_branch(x2d, batch, seq, g_mix, w_in, u_col, h0r, h0i, ar, ai, bb, cc, d_skip, w_glu, b_glu):
    nstate = ar.shape[-1]
    d = x2d.shape[1]
    dm = w_glu.shape[1] // 2
    tt = min(S5_TIME_TILE, seq)
    nj = seq // tt
    ngroups = batch // SUBLANES
    column_inputs = nj == 1
    groups = min(ngroups, max(1, S5_ROW_BLOCK // (tt * SUBLANES))) if column_inputs else 1
    rows = groups * tt * SUBLANES
    body = functools.partial(_s5_body, tt=tt, groups=groups, nstate=nstate, column_inputs=column_inputs)
    if column_inputs:
        x_specs = [pl.BlockSpec((rows, LANES), lambda i, j, s=s: (i, s)) for s in range(d // LANES)]
    else:
        x_specs = [pl.BlockSpec((tt, d), lambda i, j, b=b: ((i * SUBLANES + b) * nj + j, 0))
                   for b in range(SUBLANES)]
    state_spec = pl.BlockSpec((groups, SUBLANES, nstate), lambda i, j: (i, 0, 0))
    width = d_skip.shape[-1]
    return pl.pallas_call(
        body, grid=(ngroups // groups, nj),
        in_specs=x_specs + [
            _resident(g_mix.shape),
            pl.BlockSpec((d, width), lambda i, j: (0, u_col // width), pipeline_mode=pl.Buffered(1)),
            state_spec, state_spec, _resident(ar.shape), _resident(ai.shape), _resident(bb.shape),
            _resident(cc.shape), _resident(d_skip.shape), _resident(w_glu.shape), _resident(b_glu.shape)],
        out_specs=[pl.BlockSpec((groups * SUBLANES, tt, dm), lambda i, j: (i, j, 0)), state_spec, state_spec],
        out_shape=[jax.ShapeDtypeStruct((batch, seq, dm), F32),
                   jax.ShapeDtypeStruct((ngroups, SUBLANES, nstate), F32),
                   jax.ShapeDtypeStruct((ngroups, SUBLANES, nstate), F32)],
        scratch_shapes=[pltpu.VMEM((groups, SUBLANES, nstate), F32), pltpu.VMEM((groups, SUBLANES, nstate), F32),
                        pltpu.VMEM((rows, 2 * nstate), F32),
                        pltpu.VMEM((d // LANES, rows, LANES), F32),
                        pltpu.VMEM((dm // LANES, rows, LANES), F32)],
        compiler_params=_cparams(("parallel", "arbitrary")), name="s5_branch")(
            *([x2d] * len(x_specs)), g_mix, w_in, h0r, h0i, ar, ai, bb, cc, d_skip, w_glu, b_glu)


def _cumsum_rows(x, c):
    row = lax.broadcasted_iota(I32, x.shape, 0) & (c - 1)
    s = 1
    while s < c:
        x = x + jnp.where(row >= s, pltpu.roll(x, s, axis=0), 0.0)
        s *= 2
    return x


def _hgrn_gates(q, fr, lb, scale, c):
    rows, n = q.shape
    f = lb + (1.0 - lb) * jax.nn.sigmoid(fr)
    k = 1.0 - f
    b = _cumsum_rows(jnp.log(f), c)
    b3 = b.reshape(rows // c, c, n)
    b_last = jnp.broadcast_to(b3[:, c - 1:c, :], b3.shape).reshape(rows, n)
    q_dec = (q * scale) * jnp.exp(b)
    k_dec = k * jnp.exp(-b)
    k_end = k * jnp.exp(b_last - b)
    return q_dec, k_dec, k_end, jnp.exp(b_last)


def _causal_scores(q_dec, k_dec):
    c = q_dec.shape[0]
    s = lax.dot_general(q_dec, k_dec, (((1,), (1,)), ((), ())), preferred_element_type=F32)
    keep = lax.broadcasted_iota(I32, (c, c), 0) >= lax.broadcasted_iota(I32, (c, c), 1)
    return jnp.where(keep, s, 0.0).astype(BF16)


def _gated_out(o, gn, og):
    o = o * lax.rsqrt(jnp.mean(o * o, axis=-1, keepdims=True) + RMS_EPS) * gn
    return (o * jax.nn.silu(og)).astype(BF16)


def _hgrn_long_body(q_ref, f_ref, v_ref, og_ref, lb_ref, gn_ref, o_ref, sfin_ref, st_scr, *, c, heads, dk, scale):
    j = pl.program_id(1)

    @pl.when(j == 0)
    def _():
        st_scr[...] = jnp.zeros_like(st_scr)

    def chunk(ci, carry):
        rs = pl.ds(pl.multiple_of(ci * c, c), c)
        for h in range(heads):
            hs = slice(h * dk, (h + 1) * dk)
            q_dec, k_dec, k_end, decay = _hgrn_gates(q_ref[rs, hs], f_ref[rs, hs], lb_ref[:, hs], scale, c)
            q_dec = q_dec.astype(BF16)
            v = v_ref[rs, hs].astype(BF16)
            scores = _causal_scores(q_dec, k_dec.astype(BF16))
            st = st_scr[h]
            o = (lax.dot_general(q_dec, st.astype(BF16), (((1,), (1,)), ((), ())), preferred_element_type=F32)
                 + jnp.dot(scores, v, preferred_element_type=F32))
            st_scr[h] = decay[:1] * st + lax.dot_general(
                v, k_end.astype(BF16), (((0,), (0,)), ((), ())), preferred_element_type=F32)
            o_ref[rs, hs] = _gated_out(o, gn_ref[:, hs], og_ref[rs, hs])
        return carry

    lax.fori_loop(0, q_ref.shape[0] // c, chunk, 0, unroll=HGRN_CHUNK_UNROLL)

    @pl.when(j == pl.num_programs(1) - 1)
    def _():
        for h in range(heads):
            sfin_ref[0, h] = st_scr[h].T


def _hgrn_long(proj, lb, gn, batch, seq, heads, dk, row_off):
    width = heads * dk
    tb = min(HGRN_TIME_TILE, seq)
    nj = seq // tb
    off = row_off // tb
    c = min(HG_CHUNK, seq)
    body = functools.partial(_hgrn_long_body, c=c, heads=heads, dk=dk, scale=dk ** -0.5)

    def col(k):
        return pl.BlockSpec((tb, width), lambda b, j, k=k: (off + b * nj + j, k))

    return pl.pallas_call(
        body, grid=(batch, nj),
        in_specs=[col(0), col(1), col(2), col(3), _resident(lb.shape), _resident(gn.shape)],
        out_specs=[pl.BlockSpec((tb, width), lambda b, j: (b * nj + j, 0)),
                   pl.BlockSpec((1, heads, dk, dk), lambda b, j: (b, 0, 0, 0))],
        out_shape=[jax.ShapeDtypeStruct((batch * seq, width), BF16),
                   jax.ShapeDtypeStruct((batch, heads, dk, dk), F32)],
        scratch_shapes=[pltpu.VMEM((heads, dk, dk), F32)],
        compiler_params=_cparams(("parallel", "arbitrary")), name="hgrn_long")(proj, proj, proj, proj, lb, gn)


def _hgrn_short_body(q_ref, f_ref, v_ref, og_ref, lb_ref, gn_ref, s0_ref, o_ref, snew_ref, *, c, heads, dk, scale):
    def one_seq(sq, carry):
        rs = pl.ds(pl.multiple_of(sq * c, c), c)
        for h in range(heads):
            hs = slice(h * dk, (h + 1) * dk)
            q_dec, k_dec, k_end, decay = _hgrn_gates(q_ref[rs, hs], f_ref[rs, hs], lb_ref[:, hs], scale, c)
            q_dec = q_dec.astype(BF16)
            v = v_ref[rs, hs].astype(BF16)
            scores = _causal_scores(q_dec, k_dec.astype(BF16))
            s0 = s0_ref[sq, h]
            o = (jnp.dot(q_dec, s0.astype(BF16), preferred_element_type=F32)
                 + jnp.dot(scores, v, preferred_element_type=F32))
            decay_col = jnp.broadcast_to(decay[:1], (dk, dk)).T
            snew_ref[sq, h] = decay_col * s0 + lax.dot_general(
                k_end.astype(BF16), v, (((0,), (0,)), ((), ())), preferred_element_type=F32)
            o_ref[rs, hs] = _gated_out(o, gn_ref[:, hs], og_ref[rs, hs])
        return carry

    lax.fori_loop(0, s0_ref.shape[0], one_seq, 0, unroll=HGRN_SEQ_UNROLL)


def _hgrn_short(proj, lb, gn, s0, seq, row_off):
    batch, heads, dk, _ = s0.shape
    width = heads * dk
    nb = HGRN_SEQ_TILE
    rows = nb * seq
    off = row_off // rows
    body = functools.partial(_hgrn_short_body, c=seq, heads=heads, dk=dk, scale=dk ** -0.5)

    def col(k):
        return pl.BlockSpec((rows, width), lambda i, k=k: (off + i, k))

    state_spec = pl.BlockSpec((nb, heads, dk, dk), lambda i: (i, 0, 0, 0))
    return pl.pallas_call(
        body, grid=(batch // nb,),
        in_specs=[col(0), col(1), col(2), col(3), _resident(lb.shape), _resident(gn.shape), state_spec],
        out_specs=[pl.BlockSpec((rows, width), lambda i: (i, 0)), state_spec],
        out_shape=[jax.ShapeDtypeStruct((batch * seq, width), BF16), jax.ShapeDtypeStruct(s0.shape, F32)],
        compiler_params=_cparams(("parallel",)), name="hgrn_short")(proj, proj, proj, proj, lb, gn, s0)


def _first_index_of_max(vals):
    m = vals[0]
    for v in vals[1:]:
        m = jnp.maximum(m, v)
    idx = jnp.full(m.shape, len(vals), I32)
    for e in range(len(vals) - 1, -1, -1):
        idx = jnp.where(vals[e] == m, e, idx)
    return m, idx


def _route_tile(lg, ids_ref, w_ref, rk_ref, cnt_ref, carry_scr, groups, experts):
    i = pl.program_id(0)
    tile = lg.shape[1]
    n_exp = groups * experts

    @pl.when(i == 0)
    def _():
        carry_scr[...] = jnp.zeros_like(carry_scr)

    gl = [lg[g:g + 1, :] for g in range(groups)]
    gmax, gidx = _first_index_of_max(gl)
    denom = jnp.exp(gl[0] - gmax)
    for g in range(1, groups):
        denom = denom + jnp.exp(gl[g] - gmax)
    g_w = 1.0 / denom

    el = []
    for e in range(experts):
        v = lg[groups + e:groups + e + 1, :]
        for g in range(1, groups):
            r = groups + g * experts + e
            v = jnp.where(gidx == g, lg[r:r + 1, :], v)
        el.append(v)
    v1, i1 = _first_index_of_max(el)
    rest = [jnp.where(i1 == e, -jnp.inf, el[e]) for e in range(experts)]
    v2, i2 = _first_index_of_max(rest)
    t = jnp.exp(v2 - v1)
    inv = 1.0 / (1.0 + t)
    e1 = gidx * experts + i1
    e2 = gidx * experts + i2

    erow = lax.broadcasted_iota(I32, (n_exp, tile), 0)
    oh1 = (erow == e1).astype(F32)
    oh2 = (erow == e2).astype(F32)
    oh = oh1 + oh2
    before = (lax.broadcasted_iota(I32, (tile, tile), 0) < lax.broadcasted_iota(I32, (tile, tile), 1))
    cnt = jnp.dot(oh.astype(BF16), before.astype(BF16), preferred_element_type=F32) + carry_scr[:, 0:1]
    ids_ref[0:1, :] = e1
    ids_ref[1:2, :] = e2
    w_ref[0:1, :] = inv * g_w
    w_ref[1:2, :] = (t * inv) * g_w
    rk_ref[0:1, :] = jnp.sum(oh1 * cnt, axis=0, keepdims=True).astype(I32)
    rk_ref[1:2, :] = jnp.sum(oh2 * cnt, axis=0, keepdims=True).astype(I32)
    carry_scr[...] = carry_scr[...] + jnp.sum(oh, axis=1, keepdims=True)

    @pl.when(i == pl.num_programs(0) - 1)
    def _():
        cnt_ref[...] = carry_scr[...]


def _merge_body(*refs, n_first, groups, experts, n_gate_blocks):
    op_ref, os_ref, ybp_ref, ybs_ref, xp_ref, xs_ref, gm_ref = refs[:7]
    gate_w_refs = refs[7:7 + n_gate_blocks]
    (wa_ref, wo_ref, gf_ref, wr_ref, br_ref, h_ref, xn_ref, ids_ref, w_ref, rk_ref, cnt_ref, carry_scr,
     wgt_ref) = refs[7 + n_gate_blocks:]
    tm, d = h_ref.shape
    first = pl.program_id(0) < n_first

    @pl.when(pl.program_id(0) == 0)
    def _():
        gw = gate_w_refs[0].shape[1]
        for q, ref in enumerate(gate_w_refs):
            wgt_ref[:, q * gw:(q + 1) * gw] = ref[...].astype(BF16)

    def pick(a_ref, b_ref, rb):
        return jnp.where(first, a_ref[rb, :], b_ref[rb, :])

    blocks = [slice(r, r + MERGE_ROW_BLOCK) for r in range(0, tm, MERGE_ROW_BLOCK)]
    xs = [pick(xp_ref, xs_ref, rb) for rb in blocks]
    gates = [jnp.dot(_rmsnorm(x, gm_ref[...]).astype(BF16), wgt_ref[...], preferred_element_type=F32) for x in xs]
    y_as = [jnp.dot(pick(op_ref, os_ref, rb), wa_ref[...], preferred_element_type=F32) for rb in blocks]
    logits = []
    for rb, x, g, y_a in zip(blocks, xs, gates, y_as):
        merged = jax.nn.sigmoid(g[:, :d]) * y_a + jax.nn.sigmoid(g[:, d:]) * pick(ybp_ref, ybs_ref, rb)
        h = x + jnp.dot(merged.astype(BF16), wo_ref[...], preferred_element_type=F32)
        h_ref[rb, :] = h
        xn = _rmsnorm(h, gf_ref[...])
        _store_token_tiles(xn_ref, xn, row0=rb.start)
        logits.append(lax.dot_general(wr_ref[...], xn.astype(BF16), (((1,), (1,)), ((), ())),
                                      preferred_element_type=F32))
    logits_t = jnp.concatenate(logits, axis=1) + br_ref[...]
    _route_tile(logits_t, ids_ref, w_ref, rk_ref, cnt_ref, carry_scr, groups, experts)


def _merge(o_p, o_s, yb_p, yb_s, xp, xs, g_mix, w_in, gate_col, wa, wo, gf, wr, br, groups, experts):
    d = xp.shape[1]
    total = xp.shape[0] + xs.shape[0]
    tm = TOKEN_TILE
    n_first = xp.shape[0] // tm
    n_exp = groups * experts
    pair = _two_source_specs(tm, d, n_first)
    top = pl.BlockSpec((MOE_TOP_K, tm), lambda i: (0, i))
    gw = math.gcd(gate_col, 2 * d)
    n_gate_blocks = 2 * d // gw
    gate_specs = [pl.BlockSpec((d, gw), lambda i, q=q: (0, gate_col // gw + q), pipeline_mode=pl.Buffered(1))
                  for q in range(n_gate_blocks)]
    weights = [wa, wo, gf, wr, br]
    return pl.pallas_call(
        functools.partial(_merge_body, n_first=n_first, groups=groups, experts=experts,
                          n_gate_blocks=n_gate_blocks),
        grid=(total // tm,),
        in_specs=pair + pair + pair + [_resident(g_mix.shape)] + gate_specs + [_resident(w.shape) for w in weights],
        out_specs=[pl.BlockSpec((tm, d), lambda i: (i, 0)), pl.BlockSpec((tm * SUBLANES, LANES), lambda i: (i, 0)),
                   top, top, top, pl.BlockSpec((n_exp, LANES), lambda i: (0, 0))],
        out_shape=[jax.ShapeDtypeStruct((total, d), F32), jax.ShapeDtypeStruct((total * SUBLANES, LANES), F32),
                   jax.ShapeDtypeStruct((MOE_TOP_K, total), I32), jax.ShapeDtypeStruct((MOE_TOP_K, total), F32),
                   jax.ShapeDtypeStruct((MOE_TOP_K, total), I32), jax.ShapeDtypeStruct((n_exp, LANES), F32)],
        scratch_shapes=[pltpu.VMEM((n_exp, LANES), F32), pltpu.VMEM((d, 2 * d), BF16)],
        compiler_params=_cparams(("arbitrary",)), name="merge_route")(
            o_p, o_s, yb_p, yb_s, xp, xs, g_mix, *([w_in] * n_gate_blocks), *weights)


def _row_copy(src, dst, sem):
    return pltpu.make_async_copy(src, dst, sem)


def _token_rows(r):
    return pl.ds(pl.multiple_of(r * SUBLANES, SUBLANES), SUBLANES)


def _dispatch_body(pos_ref, x_ref, o_hbm, ring, zero_scr, sem, pad_sem, *, n_sorted):
    i = pl.program_id(0)
    tile = x_ref.shape[0] // SUBLANES
    par = lax.rem(i, 2)

    @pl.when(i == 0)
    def _():
        zero_scr[...] = jnp.zeros_like(zero_scr)
        pad = _row_copy(zero_scr, o_hbm.at[pl.ds(n_sorted * SUBLANES, zero_scr.shape[0])], pad_sem.at[0])
        pad.start()
        pad.wait()

    ring[par] = x_ref[...]

    def issue(r, carry):
        for k in range(MOE_TOP_K):
            p = pos_ref[MOE_TOP_K * r + k]
            _row_copy(ring.at[par, _token_rows(r)], o_hbm.at[_token_rows(p)], sem.at[par, k]).start(priority=k)
        return carry

    lax.fori_loop(0, tile, issue, 0, unroll=8)

    def drain(slot):
        for k in range(MOE_TOP_K):
            _row_copy(ring.at[slot], o_hbm.at[pl.ds(0, tile * SUBLANES)], sem.at[slot, k]).wait()

    @pl.when(i > 0)
    def _():
        drain(1 - par)

    @pl.when(i == pl.num_programs(0) - 1)
    def _():
        drain(par)


def _dispatch(pos3, xn_tiles):
    total = xn_tiles.shape[0] // SUBLANES
    tile = DISPATCH_TILE
    n_sorted = total * MOE_TOP_K
    pad = EXPERT_WINDOWS[-1]
    return pl.pallas_call(
        functools.partial(_dispatch_body, n_sorted=n_sorted), grid=(total // tile,),
        in_specs=[pl.BlockSpec((MOE_TOP_K * tile,), lambda i: (i,), memory_space=pltpu.SMEM),
                  pl.BlockSpec((tile * SUBLANES, LANES), lambda i: (i, 0))],
        out_specs=pl.BlockSpec(memory_space=pl.ANY),
        out_shape=jax.ShapeDtypeStruct(((n_sorted + pad) * SUBLANES, LANES), xn_tiles.dtype),
        scratch_shapes=[pltpu.VMEM((2, tile * SUBLANES, LANES), xn_tiles.dtype),
                        pltpu.VMEM((pad * SUBLANES, LANES), xn_tiles.dtype),
                        pltpu.SemaphoreType.DMA((2, MOE_TOP_K)), pltpu.SemaphoreType.DMA((1,))],
        compiler_params=_cparams(("arbitrary",)), name="dispatch")(pos3, xn_tiles)


def _experts_body(it_exp, it_row, it_cls, it_first, it_next, n_items, xs_hbm, wg_hbm, wu_hbm, wd_hbm, ys_hbm,
                  xbuf, ybuf, wg_s, wu_s, wd_s, wg_b, wu_b, wd_b, sem_in, sem_out, sem_w):
    j = pl.program_id(0)
    n = n_items[0]
    pad = EXPERT_WINDOWS[-1]
    slot = lax.rem(j, 2)

    def weight_copies(e, s):
        return [pltpu.make_async_copy(hbm.at[e], stage.at[s], sem_w.at[s, t])
                for t, (hbm, stage) in enumerate(((wg_hbm, wg_s), (wu_hbm, wu_s), (wd_hbm, wd_s)))]

    def by_size(item, fn):
        for ci, m in enumerate(EXPERT_WINDOWS):
            pl.when(it_cls[item] == ci)(functools.partial(fn, m))

    def window(item, m):
        return pl.ds(pl.multiple_of(it_row[item] * SUBLANES, SUBLANES), m * SUBLANES)

    def in_copy(item, s, m):
        return pltpu.make_async_copy(xs_hbm.at[window(item, m)], xbuf.at[s, pl.ds(0, m * SUBLANES)], sem_in.at[s])

    def out_copy(item, s, m):
        return pltpu.make_async_copy(ybuf.at[s, pl.ds(0, m * SUBLANES)], ys_hbm.at[window(item, m)], sem_out.at[s])

    def compute(m):
        x = _load_token_tiles(xbuf, m, (slot,)).astype(BF16)
        hg = jnp.dot(x, wg_b[...], preferred_element_type=F32)
        hu = jnp.dot(x, wu_b[...], preferred_element_type=F32)
        hid = (jax.nn.silu(hg) * hu).astype(BF16)
        _store_token_tiles(ybuf, jnp.dot(hid, wd_b[...], preferred_element_type=F32), (slot,))

    @pl.when(j < n)
    def _():
        @pl.when(j == 0)
        def _():
            by_size(0, lambda m: in_copy(0, 0, m).start())
            for c in weight_copies(it_exp[0], it_first[0] - 1):
                c.start()
            tail_rows = pl.ds(0, pad * SUBLANES)
            ybuf[1, tail_rows, :] = jnp.zeros((pad * SUBLANES, LANES), F32)
            tail = pltpu.make_async_copy(
                ybuf.at[1, tail_rows], ys_hbm.at[pl.ds(ys_hbm.shape[0] - pad * SUBLANES, pad * SUBLANES)],
                sem_out.at[1])
            tail.start()
            tail.wait()

        @pl.when(j + 1 < n)
        def _():
            by_size(j + 1, lambda m: in_copy(j + 1, 1 - slot, m).start())

        @pl.when(it_first[j] > 0)
        def _():
            s = it_first[j] - 1
            for c in weight_copies(it_exp[j], s):
                c.wait()
            wg_b[...] = wg_s[s].astype(BF16)
            wu_b[...] = wu_s[s].astype(BF16)
            wd_b[...] = wd_s[s].astype(BF16)

            @pl.when(it_next[j] >= 0)
            def _():
                for c in weight_copies(it_next[j], 1 - s):
                    c.start()

        by_size(j, lambda m: in_copy(j, slot, m).wait())
        by_size(j, compute)

        @pl.when(j > 0)
        def _():
            by_size(j - 1, lambda m: out_copy(j - 1, 1 - slot, m).wait())

        by_size(j, lambda m: out_copy(j, slot, m).start())

        @pl.when(j == n - 1)
        def _():
            by_size(j, lambda m: out_copy(j, slot, m).wait())


def _experts(items, xs, wg, wu, wd):
    d, de = wg.shape[1], wg.shape[2]
    tm = EXPERT_TILE
    max_items = items[0].shape[0]
    grid_spec = pltpu.PrefetchScalarGridSpec(
        num_scalar_prefetch=6, grid=(max_items,),
        in_specs=[pl.BlockSpec(memory_space=pl.ANY)] * 4,
        out_specs=pl.BlockSpec(memory_space=pl.ANY),
        scratch_shapes=[pltpu.VMEM((2, tm * SUBLANES, LANES), F32), pltpu.VMEM((2, tm * SUBLANES, LANES), F32),
                        pltpu.VMEM((2, d, de), F32), pltpu.VMEM((2, d, de), F32), pltpu.VMEM((2, de, d), F32),
                        pltpu.VMEM((d, de), BF16), pltpu.VMEM((d, de), BF16), pltpu.VMEM((de, d), BF16),
                        pltpu.SemaphoreType.DMA((2,)), pltpu.SemaphoreType.DMA((2,)),
                        pltpu.SemaphoreType.DMA((2, 3))])
    return pl.pallas_call(
        _experts_body, grid_spec=grid_spec, out_shape=jax.ShapeDtypeStruct(xs.shape, F32),
        compiler_params=_cparams(("arbitrary",)), name="experts")(*items, xs, wg, wu, wd)


def _combine_body(pos_ref, pos_next_ref, h_ref, w_ref, g_ref, ys_hbm, y_ref, buf, sem, *, final_norm):
    i = pl.program_id(0)
    tile = h_ref.shape[0]
    par = lax.rem(i, 2)

    def gather(table, slot):
        def issue(r, carry):
            for k in range(MOE_TOP_K):
                p = table[MOE_TOP_K * r + k]
                _row_copy(ys_hbm.at[_token_rows(p)], buf.at[slot, k, _token_rows(r)],
                          sem.at[slot, k]).start(priority=k)
            return carry

        lax.fori_loop(0, tile, issue, 0, unroll=8)

    @pl.when(i == 0)
    def _():
        gather(pos_ref, 0)

    @pl.when(i + 1 < pl.num_programs(0))
    def _():
        gather(pos_next_ref, 1 - par)

    for k in range(MOE_TOP_K):
        _row_copy(ys_hbm.at[pl.ds(0, tile * SUBLANES)], buf.at[par, k], sem.at[par, k]).wait()
    h = h_ref[...] + (w_ref[:, 0:1] * _load_token_tiles(buf, tile, (par, 0))
                      + w_ref[:, 1:2] * _load_token_tiles(buf, tile, (par, 1)))
    y_ref[...] = _rmsnorm(h, g_ref[...]) if final_norm else h


def _combine(pos3, h_all, w_t, g, ys, rows, row_off, final_norm):
    d = h_all.shape[1]
    tile = COMBINE_TILE
    off = row_off // tile
    last_block = h_all.shape[0] // tile - 1
    return pl.pallas_call(
        functools.partial(_combine_body, final_norm=final_norm), grid=(rows // tile,),
        in_specs=[pl.BlockSpec((MOE_TOP_K * tile,), lambda i: (off + i,), memory_space=pltpu.SMEM),
                  pl.BlockSpec((MOE_TOP_K * tile,), lambda i: (jnp.minimum(off + i + 1, last_block),),
                               memory_space=pltpu.SMEM),
                  pl.BlockSpec((tile, d), lambda i: (off + i, 0)),
                  pl.BlockSpec((tile, MOE_TOP_K), lambda i: (off + i, 0)),
                  _resident(g.shape),
                  pl.BlockSpec(memory_space=pl.ANY)],
        out_specs=pl.BlockSpec((tile, d), lambda i: (i, 0)),
        out_shape=jax.ShapeDtypeStruct((rows, d), F32),
        scratch_shapes=[pltpu.VMEM((2, MOE_TOP_K, tile * SUBLANES, LANES), F32),
                        pltpu.SemaphoreType.DMA((2, MOE_TOP_K))],
        compiler_params=_cparams(("arbitrary",)), name="combine")(pos3, pos3, h_all, w_t, g, ys)


def _lookup(table, idx):
    sel = idx[None] == jnp.arange(table.shape[0], dtype=I32).reshape((-1,) + (1,) * idx.ndim)
    return jnp.sum(jnp.where(sel, table.reshape(sel.shape[:1] + (1,) * idx.ndim), 0), axis=0)


def _work_items(counts, n_sorted):
    big, mid, small = EXPERT_WINDOWS
    n_exp = counts.shape[0]
    max_items = n_sorted // big + 2 * n_exp
    ends = jnp.cumsum(counts)
    starts = ends - counts
    units = (counts % big + small - 1) // small
    n_big = counts // big + (units == big // small)
    units = jnp.where(units == big // small, 0, units)
    n_mid = units // (mid // small)
    n_e = n_big + n_mid + units % (mid // small)
    item_end = jnp.cumsum(n_e)
    item_start = item_end - n_e
    n_items = item_end[-1]
    j = jnp.minimum(jnp.arange(max_items, dtype=I32), n_items - 1)
    e = jnp.sum((item_end[None, :] <= j[:, None]).astype(I32), axis=1)
    k = j - _lookup(item_start, e)
    nb, nm = _lookup(n_big, e), _lookup(n_mid, e)
    cls = jnp.where(k < nb, 0, jnp.where(k < nb + nm, 1, 2))
    row = _lookup(starts, e) + jnp.where(cls == 0, k * big, nb * big + jnp.where(cls == 1, 0, nm * mid))
    ordinal = jnp.cumsum((n_e > 0).astype(I32)) - 1
    first = jnp.where(k == 0, 1 + _lookup(ordinal, e) % 2, 0)
    nxt_item = _lookup(item_end, e)
    nxt = jnp.where(nxt_item < n_items, jnp.sum((item_end[None, :] <= nxt_item[:, None]).astype(I32), axis=1), -1)
    return (e, row.astype(I32), cls.astype(I32), first.astype(I32), nxt.astype(I32),
            n_items.reshape(1).astype(I32)), starts


def _s5_discretise(lam_re, lam_im, log_dt, b_re, b_im, c_re, c_im):
    g, p = lam_re.shape
    ch = b_re.shape[-1]
    lam_re = lam_re.astype(F32)
    lam_im = lam_im.astype(F32)
    dt = jnp.exp(log_dt.astype(F32))[:, None]
    mag = jnp.exp(lam_re * dt)
    ab_re = mag * jnp.cos(lam_im * dt)
    ab_im = mag * jnp.sin(lam_im * dt)
    den = lam_re * lam_re + lam_im * lam_im
    nr = ab_re - 1.0
    coef_re = (nr * lam_re + ab_im * lam_im) / den
    coef_im = (ab_im * lam_re - nr * lam_im) / den
    bb_re = coef_re[..., None] * b_re - coef_im[..., None] * b_im
    bb_im = coef_re[..., None] * b_im + coef_im[..., None] * b_re
    gh = g // 2
    eye = jnp.eye(gh, dtype=F32)

    def in_block(m):
        return jnp.einsum("gpc,gh->gchp", m, eye).reshape(gh * ch, gh * p)

    def out_block(m):
        return jnp.einsum("gcp,gh->gphc", m, eye).reshape(gh * p, gh * ch)

    bb = jnp.stack([jnp.concatenate([in_block(bb_re[k * gh:(k + 1) * gh]), in_block(bb_im[k * gh:(k + 1) * gh])],
                                    axis=1) for k in range(2)]).astype(BF16)
    cc = jnp.stack([jnp.concatenate([out_block(c_re[k * gh:(k + 1) * gh]), out_block(-c_im[k * gh:(k + 1) * gh])],
                                    axis=0) for k in range(2)]).astype(BF16)
    return ab_re.reshape(1, g * p), ab_im.reshape(1, g * p), bb, cc


def kernel(x_prompt, x_sample, state_hgrn, state_s5_re, state_s5_im, norm_mix_g, w_in, hgrn_lb_raw, hgrn_onorm_g, w_branch_a, s5_lambda_re, s5_lambda_im, s5_log_dt, s5_b_re, s5_b_im, s5_c_re, s5_c_im, s5_d, w_glu, b_glu, w_out, norm_ffn_g, w_router_group, b_router_group, w_router_expert, b_router_expert, w_exp_gate, w_exp_up, w_exp_down, norm_final_g):
    depth = norm_mix_g.shape[0]
    bp, lp, d = x_prompt.shape
    bs, ls, _ = x_sample.shape
    heads, dk = state_hgrn.shape[2], state_hgrn.shape[3]
    kw = heads * dk
    s5_groups, s5_state = state_s5_re.shape[2], state_s5_re.shape[3]
    s5_width = s5_d.shape[-1]
    nstate = s5_groups * s5_state
    moe_groups, _, experts = w_router_expert.shape[1:]
    n_exp = moe_groups * experts
    rows_p, rows_s = bp * lp, bs * ls
    total = rows_p + rows_s
    n_sorted = total * MOE_TOP_K
    assert kw == d and state_hgrn.shape[4] == dk, "column blocks assume key width == value width == model width"
    assert d == SUBLANES * LANES, "token-tile layout holds one token per (8, 128) tile"
    assert s5_groups % 2 == 0 and bp % SUBLANES == 0 and bs % HGRN_SEQ_TILE == 0
    assert (4 * kw) % s5_width == 0, "the S5 input columns must start on a multiple of their width"

    lb_all = jnp.cumsum(jax.nn.softmax(hgrn_lb_raw.astype(F32), axis=0), axis=0)

    hp = x_prompt.reshape(rows_p, d)
    hs = x_sample.reshape(rows_s, d)
    hg_p, re_p, im_p, hg_s, re_s, im_s = [], [], [], [], [], []
    zeros_state = jnp.zeros((bp // SUBLANES, SUBLANES, nstate), F32)

    for l in range(depth):
        w = w_in[l]
        g_mix = norm_mix_g[l].reshape(1, d)
        u_col = 4 * kw
        gate_col = u_col + s5_width
        proj = _in_proj(hp, hs, g_mix, w, u_col)

        ar, ai, bb, cc = _s5_discretise(s5_lambda_re[l], s5_lambda_im[l], s5_log_dt[l], s5_b_re[l], s5_b_im[l],
                                        s5_c_re[l], s5_c_im[l])
        s5_args = (ar, ai, bb, cc, s5_d[l].reshape(1, s5_width), w_glu[l].astype(BF16), b_glu[l].reshape(1, -1))
        yb_p, fr_p, fi_p = _s5_branch(hp, bp, lp, g_mix, w, u_col, zeros_state, zeros_state, *s5_args)
        yb_s, fr_s, fi_s = _s5_branch(hs, bs, ls, g_mix, w, u_col,
                                      state_s5_re[l].reshape(bs // SUBLANES, SUBLANES, nstate),
                                      state_s5_im[l].reshape(bs // SUBLANES, SUBLANES, nstate), *s5_args)

        lb = lb_all[l].reshape(1, kw)
        gn = hgrn_onorm_g[l].reshape(1, kw)
        o_p, hgp = _hgrn_long(proj, lb, gn, bp, lp, heads, dk, 0)
        o_s, hgs = _hgrn_short(proj, lb, gn, state_hgrn[l].astype(F32), ls, rows_p)

        nr = -(-(moe_groups + n_exp) // SUBLANES) * SUBLANES
        wr = jnp.concatenate([w_router_group[l].T, w_router_expert[l].transpose(0, 2, 1).reshape(n_exp, d)], axis=0)
        wr = jnp.pad(wr, ((0, nr - wr.shape[0]), (0, 0))).astype(BF16)
        br = jnp.pad(jnp.concatenate([b_router_group[l], b_router_expert[l].reshape(n_exp)]),
                     (0, nr - moe_groups - n_exp)).reshape(nr, 1).astype(F32)
        h_all, xn_all, ids, wts, ranks, cnt = _merge(
            o_p, o_s, yb_p.reshape(rows_p, d), yb_s.reshape(rows_s, d), hp, hs, g_mix, w, gate_col,
            w_branch_a[l].astype(BF16), w_out[l].astype(BF16), norm_ffn_g[l].reshape(1, d), wr, br,
            moe_groups, experts)
        items, starts = _work_items(cnt[:, 0].astype(I32), n_sorted)
        pos = _lookup(starts, ids) + ranks
        pos3 = pos.T.reshape(-1)
        xs = _dispatch(pos3, xn_all)
        ys = _experts(items, xs, w_exp_gate[l], w_exp_up[l], w_exp_down[l])

        last = l == depth - 1
        g_out = norm_final_g.reshape(1, d)
        hp = _combine(pos3, h_all, wts.T, g_out, ys, rows_p, 0, last)
        hs = _combine(pos3, h_all, wts.T, g_out, ys, rows_s, rows_p, last)

        hg_p.append(hgp)
        hg_s.append(hgs)
        re_p.append(fr_p.reshape(bp, s5_groups, s5_state))
        im_p.append(fi_p.reshape(bp, s5_groups, s5_state))
        re_s.append(fr_s.reshape(bs, s5_groups, s5_state))
        im_s.append(fi_s.reshape(bs, s5_groups, s5_state))

    y_prompt = hp.reshape(bp, lp, d).astype(x_prompt.dtype)
    y_sample = hs.reshape(bs, ls, d).astype(x_sample.dtype)
    return (y_prompt, y_sample, jnp.stack(hg_p), jnp.stack(re_p), jnp.stack(im_p),
            jnp.stack(hg_s), jnp.stack(re_s), jnp.stack(im_s))
```

```python
import functools
import math

import jax
import jax.numpy as jnp
from jax import lax
from jax.experimental import pallas as pl
from jax.experimental.pallas import tpu as pltpu

F32 = jnp.float32
BF16 = jnp.bfloat16
I32 = jnp.int32

RMS_EPS = 1e-6
HG_CHUNK = 64
MOE_TOP_K = 2

V7X_VMEM_BYTES = 64 * 1024 * 1024
VMEM_LIMIT_BYTES = V7X_VMEM_BYTES - 8 * 1024 * 1024
SUBLANES = 8
LANES = 128

TOKEN_TILE = 512
DISPATCH_TILE = 1024
COMBINE_TILE = 512
EXPERT_TILE = 512
EXPERT_WINDOWS = (EXPERT_TILE, EXPERT_TILE // 2, EXPERT_TILE // 4)
S5_TIME_TILE = 128
S5_ROW_BLOCK = 1024
HGRN_TIME_TILE = 1024
HGRN_SEQ_TILE = 16
HGRN_CHUNK_UNROLL = 16
HGRN_SEQ_UNROLL = 8
PROJ_COL_TILE = 512


def _cparams(sem):
    return pltpu.CompilerParams(dimension_semantics=sem, vmem_limit_bytes=VMEM_LIMIT_BYTES)


def _resident(shape):
    nd = len(shape)
    return pl.BlockSpec(shape, lambda *_: (0,) * nd, pipeline_mode=pl.Buffered(1))


def _rmsnorm(x, g):
    return x * lax.rsqrt(jnp.mean(x * x, axis=-1, keepdims=True) + RMS_EPS) * g


def _two_source_specs(tm, width, n_first):
    return [pl.BlockSpec((tm, width), lambda i: (jnp.minimum(i, n_first - 1), 0)),
            pl.BlockSpec((tm, width), lambda i: (jnp.maximum(i - n_first, 0), 0))]


def _pick(first_ref, second_ref, n_first):
    return jnp.where(pl.program_id(0) < n_first, first_ref[...], second_ref[...])


def _store_token_tiles(ref, x, lead=()):
    rows = x.shape[0]
    for c in range(SUBLANES):
        ref[lead + (pl.ds(c, rows, stride=SUBLANES), slice(None))] = x[:, c * LANES:(c + 1) * LANES]


def _load_token_tiles(ref, rows, lead=()):
    return jnp.concatenate([ref[lead + (pl.ds(c, rows, stride=SUBLANES), slice(None))] for c in range(SUBLANES)],
                           axis=-1)


def _inproj_body(xp_ref, xs_ref, g_ref, w_ref, o_ref, wb_scr, *, n_first):
    @pl.when(pl.program_id(0) == 0)
    def _():
        wb_scr[...] = w_ref[...].astype(BF16)

    xb = _rmsnorm(_pick(xp_ref, xs_ref, n_first), g_ref[...]).astype(BF16)
    for j in range(0, wb_scr.shape[1], PROJ_COL_TILE):
        o_ref[:, j:j + PROJ_COL_TILE] = jnp.dot(xb, wb_scr[:, j:j + PROJ_COL_TILE], preferred_element_type=F32)


def _in_proj(xp, xs, g, w, n):
    d = xp.shape[1]
    tm = TOKEN_TILE
    total = xp.shape[0] + xs.shape[0]
    n_first = xp.shape[0] // tm
    return pl.pallas_call(
        functools.partial(_inproj_body, n_first=n_first), grid=(total // tm,),
        in_specs=_two_source_specs(tm, d, n_first) + [
            _resident((1, d)), pl.BlockSpec((d, n), lambda i: (0, 0), pipeline_mode=pl.Buffered(1))],
        out_specs=pl.BlockSpec((tm, n), lambda i: (i, 0)),
        out_shape=jax.ShapeDtypeStruct((total, n), F32),
        scratch_shapes=[pltpu.VMEM((d, n), BF16)],
        compiler_params=_cparams(("arbitrary",)), name="in_proj")(xp, xs, g, w)


def _s5_body(*refs, tt, groups, nstate, column_inputs):
    n_x = len(refs) - 19
    x_refs = refs[:n_x]
    (gm_ref, wu_ref, h0r_ref, h0i_ref, ar_ref, ai_ref, bb_ref, cc_ref, d_ref, wg_ref, bg_ref,
     y_ref, hr_out, hi_out, hr_scr, hi_scr, bu_scr, x_scr, y_scr) = refs[n_x:]
    j = pl.program_id(1)
    half = nstate // 2
    d = x_scr.shape[0] * LANES
    kw = wu_ref.shape[-1] // 2

    @pl.when(j == 0)
    def _():
        hr_scr[...] = h0r_ref[...]
        hi_scr[...] = h0i_ref[...]

    if column_inputs:
        for s in range(d // LANES):
            for g in range(groups):
                for t in range(tt):
                    r0 = (g * tt + t) * SUBLANES
                    x_scr[s, r0:r0 + SUBLANES, :] = x_refs[s][pl.ds(g * SUBLANES * tt + t, SUBLANES, stride=tt), :]
    else:
        for b in range(SUBLANES):
            xb = x_refs[b][...]
            for s in range(d // LANES):
                x_scr[s, pl.ds(b, tt, stride=SUBLANES), :] = xb[:, s * LANES:(s + 1) * LANES]
    x = jnp.concatenate([x_scr[s] for s in range(d // LANES)], axis=-1)
    u = jnp.dot(_rmsnorm(x, gm_ref[...]).astype(BF16), wu_ref[...].astype(BF16), preferred_element_type=F32)
    ub16 = u.astype(BF16)
    for kt in range(2):
        ukt = ub16[:, kt * kw:(kt + 1) * kw]
        bu_scr[:, kt * half:(kt + 1) * half] = jnp.dot(ukt, bb_ref[kt, :, :half], preferred_element_type=F32)
        bu_scr[:, nstate + kt * half:nstate + (kt + 1) * half] = jnp.dot(
            ukt, bb_ref[kt, :, half:], preferred_element_type=F32)

    lane_chunk = 512
    for lc in range(nstate // lane_chunk):
        lo = lc * lane_chunk
        re_sl = slice(lo, lo + lane_chunk)
        im_sl = slice(nstate + lo, nstate + lo + lane_chunk)
        ar = jnp.broadcast_to(ar_ref[:, re_sl], (SUBLANES, lane_chunk))
        ai = jnp.broadcast_to(ai_ref[:, re_sl], (SUBLANES, lane_chunk))

        for g in range(groups):
            hr, hi = hr_scr[g, :, re_sl], hi_scr[g, :, re_sl]
            for t in range(tt):
                r0 = (g * tt + t) * SUBLANES
                rs = slice(r0, r0 + SUBLANES)
                hr, hi = (ar * hr - ai * hi + bu_scr[rs, re_sl], ar * hi + ai * hr + bu_scr[rs, im_sl])
                bu_scr[rs, re_sl] = hr
                bu_scr[rs, im_sl] = hi
            hr_scr[g, :, re_sl] = hr
            hi_scr[g, :, re_sl] = hi

    dm = wg_ref.shape[-1] // 2
    ys = []
    for n in range(2):
        h_re = bu_scr[:, n * half:(n + 1) * half].astype(BF16)
        h_im = bu_scr[:, nstate + n * half:nstate + (n + 1) * half].astype(BF16)
        ys.append(jnp.dot(h_re, cc_ref[n, :half, :], preferred_element_type=F32)
                  + jnp.dot(h_im, cc_ref[n, half:, :], preferred_element_type=F32))
    y = jnp.concatenate(ys, axis=-1) + d_ref[...] * u
    z = jnp.dot(jax.nn.gelu(y).astype(BF16), wg_ref[...], preferred_element_type=F32) + bg_ref[...]
    yb = z[:, :dm] * jax.nn.sigmoid(z[:, dm:])
    for s in range(dm // LANES):
        y_scr[s] = yb[:, s * LANES:(s + 1) * LANES]
    for g in range(groups):
        for b in range(SUBLANES):
            for s in range(dm // LANES):
                y_ref[g * SUBLANES + b, :, s * LANES:(s + 1) * LANES] = y_scr[
                    s, pl.ds(g * tt * SUBLANES + b, tt, stride=SUBLANES), :]

    @pl.when(j == pl.num_programs(1) - 1)
    def _():
        hr_out[...] = hr_scr[...]
        hi_out[...] = hi_scr[...]


def _s5_branch(x2d, batch, seq, g_mix, w_in, u_col, h0r, h0i, ar, ai, bb, cc, d_skip, w_glu, b_glu):
    nstate = ar.shape[-1]
    d = x2d.shape[1]
    dm = w_glu.shape[1] // 2
    tt = min(S5_TIME_TILE, seq)
    nj = seq // tt
    ngroups = batch // SUBLANES
    column_inputs = nj == 1
    groups = min(ngroups, max(1, S5_ROW_BLOCK // (tt * SUBLANES))) if column_inputs else 1
    rows = groups * tt * SUBLANES
    body = functools.partial(_s5_body, tt=tt, groups=groups, nstate=nstate, column_inputs=column_inputs)
    if column_inputs:
        x_specs = [pl.BlockSpec((rows, LANES), lambda i, j, s=s: (i, s)) for s in range(d // LANES)]
    else:
        x_specs = [pl.BlockSpec((tt, d), lambda i, j, b=b: ((i * SUBLANES + b) * nj + j, 0))
                   for b in range(SUBLANES)]
    state_spec = pl.BlockSpec((groups, SUBLANES, nstate), lambda i, j: (i, 0, 0))
    width = d_skip.shape[-1]
    return pl.pallas_call(
        body, grid=(ngroups // groups, nj),
        in_specs=x_specs + [
            _resident(g_mix.shape),
            pl.BlockSpec((d, width), lambda i, j: (0, u_col // width), pipeline_mode=pl.Buffered(1)),
            state_spec, state_spec, _resident(ar.shape), _resident(ai.shape), _resident(bb.shape),
            _resident(cc.shape), _resident(d_skip.shape), _resident(w_glu.shape), _resident(b_glu.shape)],
        out_specs=[pl.BlockSpec((groups * SUBLANES, tt, dm), lambda i, j: (i, j, 0)), state_spec, state_spec],
        out_shape=[jax.ShapeDtypeStruct((batch, seq, dm), F32),
                   jax.ShapeDtypeStruct((ngroups, SUBLANES, nstate), F32),
                   jax.ShapeDtypeStruct((ngroups, SUBLANES, nstate), F32)],
        scratch_shapes=[pltpu.VMEM((groups, SUBLANES, nstate), F32), pltpu.VMEM((groups, SUBLANES, nstate), F32),
                        pltpu.VMEM((rows, 2 * nstate), F32),
                        pltpu.VMEM((d // LANES, rows, LANES), F32),
                        pltpu.VMEM((dm // LANES, rows, LANES), F32)],
        compiler_params=_cparams(("parallel", "arbitrary")), name="s5_branch")(
            *([x2d] * len(x_specs)), g_mix, w_in, h0r, h0i, ar, ai, bb, cc, d_skip, w_glu, b_glu)


def _cumsum_rows(x, c):
    row = lax.broadcasted_iota(I32, x.shape, 0) & (c - 1)
    s = 1
    while s < c:
        x = x + jnp.where(row >= s, pltpu.roll(x, s, axis=0), 0.0)
        s *= 2
    return x


def _hgrn_gates(q, fr, lb, scale, c):
    rows, n = q.shape
    f = lb + (1.0 - lb) * jax.nn.sigmoid(fr)
    k = 1.0 - f
    b = _cumsum_rows(jnp.log(f), c)
    b3 = b.reshape(rows // c, c, n)
    b_last = jnp.broadcast_to(b3[:, c - 1:c, :], b3.shape).reshape(rows, n)
    q_dec = (q * scale) * jnp.exp(b)
    k_dec = k * jnp.exp(-b)
    k_end = k * jnp.exp(b_last - b)
    return q_dec, k_dec, k_end, jnp.exp(b_last)


def _causal_scores(q_dec, k_dec):
    c = q_dec.shape[0]
    s = lax.dot_general(q_dec, k_dec, (((1,), (1,)), ((), ())), preferred_element_type=F32)
    keep = lax.broadcasted_iota(I32, (c, c), 0) >= lax.broadcasted_iota(I32, (c, c), 1)
    return jnp.where(keep, s, 0.0).astype(BF16)


def _gated_out(o, gn, og):
    o = o * lax.rsqrt(jnp.mean(o * o, axis=-1, keepdims=True) + RMS_EPS) * gn
    return (o * jax.nn.silu(og)).astype(BF16)


def _hgrn_long_body(q_ref, f_ref, v_ref, og_ref, lb_ref, gn_ref, o_ref, sfin_ref, st_scr, *, c, heads, dk, scale):
    j = pl.program_id(1)

    @pl.when(j == 0)
    def _():
        st_scr[...] = jnp.zeros_like(st_scr)

    def chunk(ci, carry):
        rs = pl.ds(pl.multiple_of(ci * c, c), c)
        for h in range(heads):
            hs = slice(h * dk, (h + 1) * dk)
            q_dec, k_dec, k_end, decay = _hgrn_gates(q_ref[rs, hs], f_ref[rs, hs], lb_ref[:, hs], scale, c)
            q_dec = q_dec.astype(BF16)
            v = v_ref[rs, hs].astype(BF16)
            scores = _causal_scores(q_dec, k_dec.astype(BF16))
            st = st_scr[h]
            o = (lax.dot_general(q_dec, st.astype(BF16), (((1,), (1,)), ((), ())), preferred_element_type=F32)
                 + jnp.dot(scores, v, preferred_element_type=F32))
            st_scr[h] = decay[:1] * st + lax.dot_general(
                v, k_end.astype(BF16), (((0,), (0,)), ((), ())), preferred_element_type=F32)
            o_ref[rs, hs] = _gated_out(o, gn_ref[:, hs], og_ref[rs, hs])
        return carry

    lax.fori_loop(0, q_ref.shape[0] // c, chunk, 0, unroll=HGRN_CHUNK_UNROLL)

    @pl.when(j == pl.num_programs(1) - 1)
    def _():
        for h in range(heads):
            sfin_ref[0, h] = st_scr[h].T


def _hgrn_long(proj, lb, gn, batch, seq, heads, dk, row_off):
    width = heads * dk
    tb = min(HGRN_TIME_TILE, seq)
    nj = seq // tb
    off = row_off // tb
    c = min(HG_CHUNK, seq)
    body = functools.partial(_hgrn_long_body, c=c, heads=heads, dk=dk, scale=dk ** -0.5)

    def col(k):
        return pl.BlockSpec((tb, width), lambda b, j, k=k: (off + b * nj + j, k))

    return pl.pallas_call(
        body, grid=(batch, nj),
        in_specs=[col(0), col(1), col(2), col(3), _resident(lb.shape), _resident(gn.shape)],
        out_specs=[pl.BlockSpec((tb, width), lambda b, j: (b * nj + j, 0)),
                   pl.BlockSpec((1, heads, dk, dk), lambda b, j: (b, 0, 0, 0))],
        out_shape=[jax.ShapeDtypeStruct((batch * seq, width), BF16),
                   jax.ShapeDtypeStruct((batch, heads, dk, dk), F32)],
        scratch_shapes=[pltpu.VMEM((heads, dk, dk), F32)],
        compiler_params=_cparams(("parallel", "arbitrary")), name="hgrn_long")(proj, proj, proj, proj, lb, gn)


def _hgrn_short_body(q_ref, f_ref, v_ref, og_ref, lb_ref, gn_ref, s0_ref, o_ref, snew_ref, *, c, heads, dk, scale):
    def one_seq(sq, carry):
        rs = pl.ds(pl.multiple_of(sq * c, c), c)
        for h in range(heads):
            hs = slice(h * dk, (h + 1) * dk)
            q_dec, k_dec, k_end, decay = _hgrn_gates(q_ref[rs, hs], f_ref[rs, hs], lb_ref[:, hs], scale, c)
            q_dec = q_dec.astype(BF16)
            v = v_ref[rs, hs].astype(BF16)
            scores = _causal_scores(q_dec, k_dec.astype(BF16))
            s0 = s0_ref[sq, h]
            o = (jnp.dot(q_dec, s0.astype(BF16), preferred_element_type=F32)
                 + jnp.dot(scores, v, preferred_element_type=F32))
            decay_col = jnp.broadcast_to(decay[:1], (dk, dk)).T
            snew_ref[sq, h] = decay_col * s0 + lax.dot_general(
                k_end.astype(BF16), v, (((0,), (0,)), ((), ())), preferred_element_type=F32)
            o_ref[rs, hs] = _gated_out(o, gn_ref[:, hs], og_ref[rs, hs])
        return carry

    lax.fori_loop(0, s0_ref.shape[0], one_seq, 0, unroll=HGRN_SEQ_UNROLL)


def _hgrn_short(proj, lb, gn, s0, seq, row_off):
    batch, heads, dk, _ = s0.shape
    width = heads * dk
    nb = HGRN_SEQ_TILE
    rows = nb * seq
    off = row_off // rows
    body = functools.partial(_hgrn_short_body, c=seq, heads=heads, dk=dk, scale=dk ** -0.5)

    def col(k):
        return pl.BlockSpec((rows, width), lambda i, k=k: (off + i, k))

    state_spec = pl.BlockSpec((nb, heads, dk, dk), lambda i: (i, 0, 0, 0))
    return pl.pallas_call(
        body, grid=(batch // nb,),
        in_specs=[col(0), col(1), col(2), col(3), _resident(lb.shape), _resident(gn.shape), state_spec],
        out_specs=[pl.BlockSpec((rows, width), lambda i: (i, 0)), state_spec],
        out_shape=[jax.ShapeDtypeStruct((batch * seq, width), BF16), jax.ShapeDtypeStruct(s0.shape, F32)],
        compiler_params=_cparams(("parallel",)), name="hgrn_short")(proj, proj, proj, proj, lb, gn, s0)


def _first_index_of_max(vals):
    m = vals[0]
    for v in vals[1:]:
        m = jnp.maximum(m, v)
    idx = jnp.full(m.shape, len(vals), I32)
    for e in range(len(vals) - 1, -1, -1):
        idx = jnp.where(vals[e] == m, e, idx)
    return m, idx


def _route_tile(lg, ids_ref, w_ref, rk_ref, cnt_ref, carry_scr, groups, experts):
    i = pl.program_id(0)
    tile = lg.shape[1]
    n_exp = groups * experts

    @pl.when(i == 0)
    def _():
        carry_scr[...] = jnp.zeros_like(carry_scr)

    gl = [lg[g:g + 1, :] for g in range(groups)]
    gmax, gidx = _first_index_of_max(gl)
    denom = jnp.exp(gl[0] - gmax)
    for g in range(1, groups):
        denom = denom + jnp.exp(gl[g] - gmax)
    g_w = 1.0 / denom

    el = []
    for e in range(experts):
        v = lg[groups + e:groups + e + 1, :]
        for g in range(1, groups):
            r = groups + g * experts + e
            v = jnp.where(gidx == g, lg[r:r + 1, :], v)
        el.append(v)
    v1, i1 = _first_index_of_max(el)
    rest = [jnp.where(i1 == e, -jnp.inf, el[e]) for e in range(experts)]
    v2, i2 = _first_index_of_max(rest)
    t = jnp.exp(v2 - v1)
    inv = 1.0 / (1.0 + t)
    e1 = gidx * experts + i1
    e2 = gidx * experts + i2

    erow = lax.broadcasted_iota(I32, (n_exp, tile), 0)
    oh1 = (erow == e1).astype(F32)
    oh2 = (erow == e2).astype(F32)
    oh = oh1 + oh2
    before = (lax.broadcasted_iota(I32, (tile, tile), 0) < lax.broadcasted_iota(I32, (tile, tile), 1))
    cnt = jnp.dot(oh.astype(BF16), before.astype(BF16), preferred_element_type=F32) + carry_scr[:, 0:1]
    ids_ref[0:1, :] = e1
    ids_ref[1:2, :] = e2
    w_ref[0:1, :] = inv * g_w
    w_ref[1:2, :] = (t * inv) * g_w
    rk_ref[0:1, :] = jnp.sum(oh1 * cnt, axis=0, keepdims=True).astype(I32)
    rk_ref[1:2, :] = jnp.sum(oh2 * cnt, axis=0, keepdims=True).astype(I32)
    carry_scr[...] = carry_scr[...] + jnp.sum(oh, axis=1, keepdims=True)

    @pl.when(i == pl.num_programs(0) - 1)
    def _():
        cnt_ref[...] = carry_scr[...]


def _merge_body(*refs, n_first, groups, experts, n_gate_blocks):
    op_ref, os_ref, ybp_ref, ybs_ref, xp_ref, xs_ref, gm_ref = refs[:7]
    gate_w_refs = refs[7:7 + n_gate_blocks]
    (wa_ref, wo_ref, gf_ref, wr_ref, br_ref, h_ref, xn_ref, ids_ref, w_ref, rk_ref, cnt_ref, carry_scr,
     wgt_ref) = refs[7 + n_gate_blocks:]
    d = h_ref.shape[1]

    @pl.when(pl.program_id(0) == 0)
    def _():
        gw = gate_w_refs[0].shape[1]
        for q, ref in enumerate(gate_w_refs):
            wgt_ref[:, q * gw:(q + 1) * gw] = ref[...].astype(BF16)

    x = _pick(xp_ref, xs_ref, n_first)
    gates = jnp.dot(_rmsnorm(x, gm_ref[...]).astype(BF16), wgt_ref[...], preferred_element_type=F32)
    y_a = jnp.dot(_pick(op_ref, os_ref, n_first), wa_ref[...], preferred_element_type=F32)
    merged = jax.nn.sigmoid(gates[:, :d]) * y_a + jax.nn.sigmoid(gates[:, d:]) * _pick(ybp_ref, ybs_ref, n_first)
    h = x + jnp.dot(merged.astype(BF16), wo_ref[...], preferred_element_type=F32)
    h_ref[...] = h
    xn = _rmsnorm(h, gf_ref[...])
    _store_token_tiles(xn_ref, xn)
    logits_t = lax.dot_general(wr_ref[...], xn.astype(BF16), (((1,), (1,)), ((), ())),
                               preferred_element_type=F32) + br_ref[...]
    _route_tile(logits_t, ids_ref, w_ref, rk_ref, cnt_ref, carry_scr, groups, experts)


def _merge(o_p, o_s, yb_p, yb_s, xp, xs, g_mix, w_in, gate_col, wa, wo, gf, wr, br, groups, experts):
    d = xp.shape[1]
    total = xp.shape[0] + xs.shape[0]
    tm = TOKEN_TILE
    n_first = xp.shape[0] // tm
    n_exp = groups * experts
    pair = _two_source_specs(tm, d, n_first)
    top = pl.BlockSpec((MOE_TOP_K, tm), lambda i: (0, i))
    gw = math.gcd(gate_col, 2 * d)
    n_gate_blocks = 2 * d // gw
    gate_specs = [pl.BlockSpec((d, gw), lambda i, q=q: (0, gate_col // gw + q), pipeline_mode=pl.Buffered(1))
                  for q in range(n_gate_blocks)]
    weights = [wa, wo, gf, wr, br]
    return pl.pallas_call(
        functools.partial(_merge_body, n_first=n_first, groups=groups, experts=experts,
                          n_gate_blocks=n_gate_blocks),
        grid=(total // tm,),
        in_specs=pair + pair + pair + [_resident(g_mix.shape)] + gate_specs + [_resident(w.shape) for w in weights],
        out_specs=[pl.BlockSpec((tm, d), lambda i: (i, 0)), pl.BlockSpec((tm * SUBLANES, LANES), lambda i: (i, 0)),
                   top, top, top, pl.BlockSpec((n_exp, LANES), lambda i: (0, 0))],
        out_shape=[jax.ShapeDtypeStruct((total, d), F32), jax.ShapeDtypeStruct((total * SUBLANES, LANES), F32),
                   jax.ShapeDtypeStruct((MOE_TOP_K, total), I32), jax.ShapeDtypeStruct((MOE_TOP_K, total), F32),
                   jax.ShapeDtypeStruct((MOE_TOP_K, total), I32), jax.ShapeDtypeStruct((n_exp, LANES), F32)],
        scratch_shapes=[pltpu.VMEM((n_exp, LANES), F32), pltpu.VMEM((d, 2 * d), BF16)],
        compiler_params=_cparams(("arbitrary",)), name="merge_route")(
            o_p, o_s, yb_p, yb_s, xp, xs, g_mix, *([w_in] * n_gate_blocks), *weights)


def _row_copy(src, dst, sem):
    return pltpu.make_async_copy(src, dst, sem)


def _token_rows(r):
    return pl.ds(pl.multiple_of(r * SUBLANES, SUBLANES), SUBLANES)


def _dispatch_body(*refs, n_sorted):
    pos_refs = refs[:MOE_TOP_K]
    x_ref, o_hbm, ring, zero_scr, sem, pad_sem = refs[MOE_TOP_K:]
    i = pl.program_id(0)
    tile = x_ref.shape[0] // SUBLANES
    par = lax.rem(i, 2)

    @pl.when(i == 0)
    def _():
        zero_scr[...] = jnp.zeros_like(zero_scr)
        pad = _row_copy(zero_scr, o_hbm.at[pl.ds(n_sorted * SUBLANES, zero_scr.shape[0])], pad_sem.at[0])
        pad.start()
        pad.wait()

    ring[par] = x_ref[...]

    def issue(r, carry):
        for k in range(MOE_TOP_K):
            p = pos_refs[k][r]
            _row_copy(ring.at[par, _token_rows(r)], o_hbm.at[_token_rows(p)], sem.at[par, k]).start(priority=k)
        return carry

    lax.fori_loop(0, tile, issue, 0, unroll=8)

    def drain(slot):
        for k in range(MOE_TOP_K):
            _row_copy(ring.at[slot], o_hbm.at[pl.ds(0, tile * SUBLANES)], sem.at[slot, k]).wait()

    @pl.when(i > 0)
    def _():
        drain(1 - par)

    @pl.when(i == pl.num_programs(0) - 1)
    def _():
        drain(par)


def _dispatch(pos_slots, xn_tiles):
    total = xn_tiles.shape[0] // SUBLANES
    tile = DISPATCH_TILE
    n_sorted = total * MOE_TOP_K
    pad = EXPERT_WINDOWS[-1]
    return pl.pallas_call(
        functools.partial(_dispatch_body, n_sorted=n_sorted), grid=(total // tile,),
        in_specs=[pl.BlockSpec((tile,), lambda i: (i,), memory_space=pltpu.SMEM)] * MOE_TOP_K + [
            pl.BlockSpec((tile * SUBLANES, LANES), lambda i: (i, 0))],
        out_specs=pl.BlockSpec(memory_space=pl.ANY),
        out_shape=jax.ShapeDtypeStruct(((n_sorted + pad) * SUBLANES, LANES), xn_tiles.dtype),
        scratch_shapes=[pltpu.VMEM((2, tile * SUBLANES, LANES), xn_tiles.dtype),
                        pltpu.VMEM((pad * SUBLANES, LANES), xn_tiles.dtype),
                        pltpu.SemaphoreType.DMA((2, MOE_TOP_K)), pltpu.SemaphoreType.DMA((1,))],
        compiler_params=_cparams(("arbitrary",)), name="dispatch")(*pos_slots, xn_tiles)


def _experts_body(it_exp, it_row, it_cls, it_first, it_next, n_items, xs_hbm, wg_hbm, wu_hbm, wd_hbm, ys_hbm,
                  xbuf, ybuf, wg_s, wu_s, wd_s, wg_b, wu_b, wd_b, sem_in, sem_out, sem_w):
    j = pl.program_id(0)
    n = n_items[0]
    pad = EXPERT_WINDOWS[-1]
    slot = lax.rem(j, 2)

    def weight_copies(e, s):
        return [pltpu.make_async_copy(hbm.at[e], stage.at[s], sem_w.at[s, t])
                for t, (hbm, stage) in enumerate(((wg_hbm, wg_s), (wu_hbm, wu_s), (wd_hbm, wd_s)))]

    def by_size(item, fn):
        for ci, m in enumerate(EXPERT_WINDOWS):
            pl.when(it_cls[item] == ci)(functools.partial(fn, m))

    def window(item, m):
        return pl.ds(pl.multiple_of(it_row[item] * SUBLANES, SUBLANES), m * SUBLANES)

    def in_copy(item, s, m):
        return pltpu.make_async_copy(xs_hbm.at[window(item, m)], xbuf.at[s, pl.ds(0, m * SUBLANES)], sem_in.at[s])

    def out_copy(item, s, m):
        return pltpu.make_async_copy(ybuf.at[s, pl.ds(0, m * SUBLANES)], ys_hbm.at[window(item, m)], sem_out.at[s])

    def compute(m):
        x = _load_token_tiles(xbuf, m, (slot,)).astype(BF16)
        hg = jnp.dot(x, wg_b[...], preferred_element_type=F32)
        hu = jnp.dot(x, wu_b[...], preferred_element_type=F32)
        hid = (jax.nn.silu(hg) * hu).astype(BF16)
        _store_token_tiles(ybuf, jnp.dot(hid, wd_b[...], preferred_element_type=F32), (slot,))

    @pl.when(j < n)
    def _():
        @pl.when(j == 0)
        def _():
            by_size(0, lambda m: in_copy(0, 0, m).start())
            for c in weight_copies(it_exp[0], it_first[0] - 1):
                c.start()
            tail_rows = pl.ds(0, pad * SUBLANES)
            ybuf[1, tail_rows, :] = jnp.zeros((pad * SUBLANES, LANES), F32)
            tail = pltpu.make_async_copy(
                ybuf.at[1, tail_rows], ys_hbm.at[pl.ds(ys_hbm.shape[0] - pad * SUBLANES, pad * SUBLANES)],
                sem_out.at[1])
            tail.start()
            tail.wait()

        @pl.when(j + 1 < n)
        def _():
            by_size(j + 1, lambda m: in_copy(j + 1, 1 - slot, m).start())

        @pl.when(it_first[j] > 0)
        def _():
            s = it_first[j] - 1
            for c in weight_copies(it_exp[j], s):
                c.wait()
            wg_b[...] = wg_s[s].astype(BF16)
            wu_b[...] = wu_s[s].astype(BF16)
            wd_b[...] = wd_s[s].astype(BF16)

            @pl.when(it_next[j] >= 0)
            def _():
                for c in weight_copies(it_next[j], 1 - s):
                    c.start()

        by_size(j, lambda m: in_copy(j, slot, m).wait())
        by_size(j, compute)

        @pl.when(j > 0)
        def _():
            by_size(j - 1, lambda m: out_copy(j - 1, 1 - slot, m).wait())

        by_size(j, lambda m: out_copy(j, slot, m).start())

        @pl.when(j == n - 1)
        def _():
            by_size(j, lambda m: out_copy(j, slot, m).wait())


def _experts(items, xs, wg, wu, wd):
    d, de = wg.shape[1], wg.shape[2]
    tm = EXPERT_TILE
    max_items = items[0].shape[0]
    grid_spec = pltpu.PrefetchScalarGridSpec(
        num_scalar_prefetch=6, grid=(max_items,),
        in_specs=[pl.BlockSpec(memory_space=pl.ANY)] * 4,
        out_specs=pl.BlockSpec(memory_space=pl.ANY),
        scratch_shapes=[pltpu.VMEM((2, tm * SUBLANES, LANES), F32), pltpu.VMEM((2, tm * SUBLANES, LANES), F32),
                        pltpu.VMEM((2, d, de), F32), pltpu.VMEM((2, d, de), F32), pltpu.VMEM((2, de, d), F32),
                        pltpu.VMEM((d, de), BF16), pltpu.VMEM((d, de), BF16), pltpu.VMEM((de, d), BF16),
                        pltpu.SemaphoreType.DMA((2,)), pltpu.SemaphoreType.DMA((2,)),
                        pltpu.SemaphoreType.DMA((2, 3))])
    return pl.pallas_call(
        _experts_body, grid_spec=grid_spec, out_shape=jax.ShapeDtypeStruct(xs.shape, F32),
        compiler_params=_cparams(("arbitrary",)), name="experts")(*items, xs, wg, wu, wd)


def _combine_body(*refs, final_norm):
    pos_refs, pos_next_refs = refs[:MOE_TOP_K], refs[MOE_TOP_K:2 * MOE_TOP_K]
    h_ref, w_ref, g_ref, ys_hbm, y_ref, buf, sem = refs[2 * MOE_TOP_K:]
    i = pl.program_id(0)
    tile = h_ref.shape[0]
    par = lax.rem(i, 2)

    def gather(tables, slot):
        def issue(r, carry):
            for k in range(MOE_TOP_K):
                p = tables[k][r]
                _row_copy(ys_hbm.at[_token_rows(p)], buf.at[slot, k, _token_rows(r)],
                          sem.at[slot, k]).start(priority=k)
            return carry

        lax.fori_loop(0, tile, issue, 0, unroll=8)

    @pl.when(i == 0)
    def _():
        gather(pos_refs, 0)

    @pl.when(i + 1 < pl.num_programs(0))
    def _():
        gather(pos_next_refs, 1 - par)

    for k in range(MOE_TOP_K):
        _row_copy(ys_hbm.at[pl.ds(0, tile * SUBLANES)], buf.at[par, k], sem.at[par, k]).wait()
    h = h_ref[...] + (w_ref[:, 0:1] * _load_token_tiles(buf, tile, (par, 0))
                      + w_ref[:, 1:2] * _load_token_tiles(buf, tile, (par, 1)))
    y_ref[...] = _rmsnorm(h, g_ref[...]) if final_norm else h


def _combine(pos_slots, h_all, w_t, g, ys, rows, row_off, final_norm):
    d = h_all.shape[1]
    tile = COMBINE_TILE
    off = row_off // tile
    last_block = h_all.shape[0] // tile - 1
    this_tile = pl.BlockSpec((tile,), lambda i: (off + i,), memory_space=pltpu.SMEM)
    next_tile = pl.BlockSpec((tile,), lambda i: (jnp.minimum(off + i + 1, last_block),), memory_space=pltpu.SMEM)
    return pl.pallas_call(
        functools.partial(_combine_body, final_norm=final_norm), grid=(rows // tile,),
        in_specs=[this_tile] * MOE_TOP_K + [next_tile] * MOE_TOP_K + [
                  pl.BlockSpec((tile, d), lambda i: (off + i, 0)),
                  pl.BlockSpec((tile, MOE_TOP_K), lambda i: (off + i, 0)),
                  _resident(g.shape),
                  pl.BlockSpec(memory_space=pl.ANY)],
        out_specs=pl.BlockSpec((tile, d), lambda i: (i, 0)),
        out_shape=jax.ShapeDtypeStruct((rows, d), F32),
        scratch_shapes=[pltpu.VMEM((2, MOE_TOP_K, tile * SUBLANES, LANES), F32),
                        pltpu.SemaphoreType.DMA((2, MOE_TOP_K))],
        compiler_params=_cparams(("arbitrary",)), name="combine")(*pos_slots, *pos_slots, h_all, w_t, g, ys)


def _lookup(table, idx):
    sel = idx[None] == jnp.arange(table.shape[0], dtype=I32).reshape((-1,) + (1,) * idx.ndim)
    return jnp.sum(jnp.where(sel, table.reshape(sel.shape[:1] + (1,) * idx.ndim), 0), axis=0)


def _work_items(counts, n_sorted):
    big, mid, small = EXPERT_WINDOWS
    n_exp = counts.shape[0]
    max_items = n_sorted // big + 2 * n_exp
    ends = jnp.cumsum(counts)
    starts = ends - counts
    units = (counts % big + small - 1) // small
    n_big = counts // big + (units == big // small)
    units = jnp.where(units == big // small, 0, units)
    n_mid = units // (mid // small)
    n_e = n_big + n_mid + units % (mid // small)
    item_end = jnp.cumsum(n_e)
    item_start = item_end - n_e
    n_items = item_end[-1]
    j = jnp.minimum(jnp.arange(max_items, dtype=I32), n_items - 1)
    e = jnp.sum((item_end[None, :] <= j[:, None]).astype(I32), axis=1)
    k = j - _lookup(item_start, e)
    nb, nm = _lookup(n_big, e), _lookup(n_mid, e)
    cls = jnp.where(k < nb, 0, jnp.where(k < nb + nm, 1, 2))
    row = _lookup(starts, e) + jnp.where(cls == 0, k * big, nb * big + jnp.where(cls == 1, 0, nm * mid))
    ordinal = jnp.cumsum((n_e > 0).astype(I32)) - 1
    first = jnp.where(k == 0, 1 + _lookup(ordinal, e) % 2, 0)
    nxt_item = _lookup(item_end, e)
    nxt = jnp.where(nxt_item < n_items, jnp.sum((item_end[None, :] <= nxt_item[:, None]).astype(I32), axis=1), -1)
    return (e, row.astype(I32), cls.astype(I32), first.astype(I32), nxt.astype(I32),
            n_items.reshape(1).astype(I32)), starts


def _s5_discretise(lam_re, lam_im, log_dt, b_re, b_im, c_re, c_im):
    g, p = lam_re.shape
    ch = b_re.shape[-1]
    lam_re = lam_re.astype(F32)
    lam_im = lam_im.astype(F32)
    dt = jnp.exp(log_dt.astype(F32))[:, None]
    mag = jnp.exp(lam_re * dt)
    ab_re = mag * jnp.cos(lam_im * dt)
    ab_im = mag * jnp.sin(lam_im * dt)
    den = lam_re * lam_re + lam_im * lam_im
    nr = ab_re - 1.0
    coef_re = (nr * lam_re + ab_im * lam_im) / den
    coef_im = (ab_im * lam_re - nr * lam_im) / den
    bb_re = coef_re[..., None] * b_re - coef_im[..., None] * b_im
    bb_im = coef_re[..., None] * b_im + coef_im[..., None] * b_re
    gh = g // 2
    eye = jnp.eye(gh, dtype=F32)

    def in_block(m):
        return jnp.einsum("gpc,gh->gchp", m, eye).reshape(gh * ch, gh * p)

    def out_block(m):
        return jnp.einsum("gcp,gh->gphc", m, eye).reshape(gh * p, gh * ch)

    bb = jnp.stack([jnp.concatenate([in_block(bb_re[k * gh:(k + 1) * gh]), in_block(bb_im[k * gh:(k + 1) * gh])],
                                    axis=1) for k in range(2)]).astype(BF16)
    cc = jnp.stack([jnp.concatenate([out_block(c_re[k * gh:(k + 1) * gh]), out_block(-c_im[k * gh:(k + 1) * gh])],
                                    axis=0) for k in range(2)]).astype(BF16)
    return ab_re.reshape(1, g * p), ab_im.reshape(1, g * p), bb, cc


def kernel(x_prompt, x_sample, state_hgrn, state_s5_re, state_s5_im, norm_mix_g, w_in, hgrn_lb_raw, hgrn_onorm_g, w_branch_a, s5_lambda_re, s5_lambda_im, s5_log_dt, s5_b_re, s5_b_im, s5_c_re, s5_c_im, s5_d, w_glu, b_glu, w_out, norm_ffn_g, w_router_group, b_router_group, w_router_expert, b_router_expert, w_exp_gate, w_exp_up, w_exp_down, norm_final_g):
    depth = norm_mix_g.shape[0]
    bp, lp, d = x_prompt.shape
    bs, ls, _ = x_sample.shape
    heads, dk = state_hgrn.shape[2], state_hgrn.shape[3]
    kw = heads * dk
    s5_groups, s5_state = state_s5_re.shape[2], state_s5_re.shape[3]
    s5_width = s5_d.shape[-1]
    nstate = s5_groups * s5_state
    moe_groups, _, experts = w_router_expert.shape[1:]
    n_exp = moe_groups * experts
    rows_p, rows_s = bp * lp, bs * ls
    total = rows_p + rows_s
    n_sorted = total * MOE_TOP_K
    assert kw == d and state_hgrn.shape[4] == dk, "column blocks assume key width == value width == model width"
    assert d == SUBLANES * LANES, "token-tile layout holds one token per (8, 128) tile"
    assert s5_groups % 2 == 0 and bp % SUBLANES == 0 and bs % HGRN_SEQ_TILE == 0
    assert (4 * kw) % s5_width == 0, "the S5 input columns must start on a multiple of their width"

    lb_all = jnp.cumsum(jax.nn.softmax(hgrn_lb_raw.astype(F32), axis=0), axis=0)

    hp = x_prompt.reshape(rows_p, d)
    hs = x_sample.reshape(rows_s, d)
    hg_p, re_p, im_p, hg_s, re_s, im_s = [], [], [], [], [], []
    zeros_state = jnp.zeros((bp // SUBLANES, SUBLANES, nstate), F32)

    for l in range(depth):
        w = w_in[l]
        g_mix = norm_mix_g[l].reshape(1, d)
        u_col = 4 * kw
        gate_col = u_col + s5_width
        proj = _in_proj(hp, hs, g_mix, w, u_col)

        ar, ai, bb, cc = _s5_discretise(s5_lambda_re[l], s5_lambda_im[l], s5_log_dt[l], s5_b_re[l], s5_b_im[l],
                                        s5_c_re[l], s5_c_im[l])
        s5_args = (ar, ai, bb, cc, s5_d[l].reshape(1, s5_width), w_glu[l].astype(BF16), b_glu[l].reshape(1, -1))
        yb_p, fr_p, fi_p = _s5_branch(hp, bp, lp, g_mix, w, u_col, zeros_state, zeros_state, *s5_args)
        yb_s, fr_s, fi_s = _s5_branch(hs, bs, ls, g_mix, w, u_col,
                                      state_s5_re[l].reshape(bs // SUBLANES, SUBLANES, nstate),
                                      state_s5_im[l].reshape(bs // SUBLANES, SUBLANES, nstate), *s5_args)

        lb = lb_all[l].reshape(1, kw)
        gn = hgrn_onorm_g[l].reshape(1, kw)
        o_p, hgp = _hgrn_long(proj, lb, gn, bp, lp, heads, dk, 0)
        o_s, hgs = _hgrn_short(proj, lb, gn, state_hgrn[l].astype(F32), ls, rows_p)

        nr = -(-(moe_groups + n_exp) // SUBLANES) * SUBLANES
        wr = jnp.concatenate([w_router_group[l].T, w_router_expert[l].transpose(0, 2, 1).reshape(n_exp, d)], axis=0)
        wr = jnp.pad(wr, ((0, nr - wr.shape[0]), (0, 0))).astype(BF16)
        br = jnp.pad(jnp.concatenate([b_router_group[l], b_router_expert[l].reshape(n_exp)]),
                     (0, nr - moe_groups - n_exp)).reshape(nr, 1).astype(F32)
        h_all, xn_all, ids, wts, ranks, cnt = _merge(
            o_p, o_s, yb_p.reshape(rows_p, d), yb_s.reshape(rows_s, d), hp, hs, g_mix, w, gate_col,
            w_branch_a[l].astype(BF16), w_out[l].astype(BF16), norm_ffn_g[l].reshape(1, d), wr, br,
            moe_groups, experts)
        items, starts = _work_items(cnt[:, 0].astype(I32), n_sorted)
        pos_slots = [_lookup(starts, ids[k]) + ranks[k] for k in range(MOE_TOP_K)]
        xs = _dispatch(pos_slots, xn_all)
        ys = _experts(items, xs, w_exp_gate[l], w_exp_up[l], w_exp_down[l])

        last = l == depth - 1
        g_out = norm_final_g.reshape(1, d)
        hp = _combine(pos_slots, h_all, wts.T, g_out, ys, rows_p, 0, last)
        hs = _combine(pos_slots, h_all, wts.T, g_out, ys, rows_s, rows_p, last)

        hg_p.append(hgp)
        hg_s.append(hgs)
        re_p.append(fr_p.reshape(bp, s5_groups, s5_state))
        im_p.append(fi_p.reshape(bp, s5_groups, s5_state))
        re_s.append(fr_s.reshape(bs, s5_groups, s5_state))
        im_s.append(fi_s.reshape(bs, s5_groups, s5_state))

    y_prompt = hp.reshape(bp, lp, d).astype(x_prompt.dtype)
    y_sample = hs.reshape(bs, ls, d).astype(x_sample.dtype)
    return (y_prompt, y_sample, jnp.stack(hg_p), jnp.stack(re_p), jnp.stack(im_p),
            jnp.stack(hg_s), jnp.stack(re_s), jnp.stack(im_s))
```

```python
import functools
import math

import jax
import jax.numpy as jnp
from jax import lax
from jax.experimental import pallas as pl
from jax.experimental.pallas import tpu as pltpu

F32 = jnp.float32
BF16 = jnp.bfloat16
I32 = jnp.int32

RMS_EPS = 1e-6
HG_CHUNK = 64
MOE_TOP_K = 2

V7X_VMEM_BYTES = 64 * 1024 * 1024
VMEM_LIMIT_BYTES = V7X_VMEM_BYTES - 8 * 1024 * 1024
SUBLANES = 8
LANES = 128

TOKEN_TILE = 512
DISPATCH_TILE = 1024
COMBINE_TILE = 512
EXPERT_TILE = 512
EXPERT_WINDOWS = (EXPERT_TILE, EXPERT_TILE // 2, EXPERT_TILE // 4)
S5_TIME_TILE = 128
S5_ROW_BLOCK = 1024
HGRN_TIME_TILE = 1024
HGRN_SEQ_TILE = 16
HGRN_CHUNK_UNROLL = 16
HGRN_SEQ_UNROLL = 8
PROJ_COL_TILE = 512


def _cparams(sem):
    return pltpu.CompilerParams(dimension_semantics=sem, vmem_limit_bytes=VMEM_LIMIT_BYTES)


def _resident(shape):
    nd = len(shape)
    return pl.BlockSpec(shape, lambda *_: (0,) * nd, pipeline_mode=pl.Buffered(1))


def _rmsnorm(x, g):
    return x * lax.rsqrt(jnp.mean(x * x, axis=-1, keepdims=True) + RMS_EPS) * g


def _two_source_specs(tm, width, n_first):
    return [pl.BlockSpec((tm, width), lambda i: (jnp.minimum(i, n_first - 1), 0)),
            pl.BlockSpec((tm, width), lambda i: (jnp.maximum(i - n_first, 0), 0))]


def _pick(first_ref, second_ref, n_first):
    return jnp.where(pl.program_id(0) < n_first, first_ref[...], second_ref[...])


def _store_token_tiles(ref, x, lead=()):
    rows = x.shape[0]
    for c in range(SUBLANES):
        ref[lead + (pl.ds(c, rows, stride=SUBLANES), slice(None))] = x[:, c * LANES:(c + 1) * LANES]


def _load_token_tiles(ref, rows, lead=()):
    return jnp.concatenate([ref[lead + (pl.ds(c, rows, stride=SUBLANES), slice(None))] for c in range(SUBLANES)],
                           axis=-1)


def _inproj_body(xp_ref, xs_ref, g_ref, w_ref, o_ref, wb_scr, *, n_first):
    @pl.when(pl.program_id(0) == 0)
    def _():
        wb_scr[...] = w_ref[...].astype(BF16)

    xb = _rmsnorm(_pick(xp_ref, xs_ref, n_first), g_ref[...]).astype(BF16)
    for j in range(0, wb_scr.shape[1], PROJ_COL_TILE):
        o_ref[:, j:j + PROJ_COL_TILE] = jnp.dot(xb, wb_scr[:, j:j + PROJ_COL_TILE], preferred_element_type=F32)


def _in_proj(xp, xs, g, w, n):
    d = xp.shape[1]
    tm = TOKEN_TILE
    total = xp.shape[0] + xs.shape[0]
    n_first = xp.shape[0] // tm
    return pl.pallas_call(
        functools.partial(_inproj_body, n_first=n_first), grid=(total // tm,),
        in_specs=_two_source_specs(tm, d, n_first) + [
            _resident((1, d)), pl.BlockSpec((d, n), lambda i: (0, 0), pipeline_mode=pl.Buffered(1))],
        out_specs=pl.BlockSpec((tm, n), lambda i: (i, 0)),
        out_shape=jax.ShapeDtypeStruct((total, n), F32),
        scratch_shapes=[pltpu.VMEM((d, n), BF16)],
        compiler_params=_cparams(("arbitrary",)), name="in_proj")(xp, xs, g, w)


def _s5_body(*refs, tt, groups, nstate, column_inputs):
    n_x = len(refs) - 19
    x_refs = refs[:n_x]
    (gm_ref, wu_ref, h0r_ref, h0i_ref, ar_ref, ai_ref, bb_ref, cc_ref, d_ref, wg_ref, bg_ref,
     y_ref, hr_out, hi_out, hr_scr, hi_scr, bu_scr, x_scr, y_scr) = refs[n_x:]
    j = pl.program_id(1)
    half = nstate // 2
    d = x_scr.shape[0] * LANES
    kw = wu_ref.shape[-1] // 2

    @pl.when(j == 0)
    def _():
        hr_scr[...] = h0r_ref[...]
        hi_scr[...] = h0i_ref[...]

    if column_inputs:
        for s in range(d // LANES):
            for g in range(groups):
                for t in range(tt):
                    r0 = (g * tt + t) * SUBLANES
                    x_scr[s, r0:r0 + SUBLANES, :] = x_refs[s][pl.ds(g * SUBLANES * tt + t, SUBLANES, stride=tt), :]
    else:
        for b in range(SUBLANES):
            xb = x_refs[b][...]
            for s in range(d // LANES):
                x_scr[s, pl.ds(b, tt, stride=SUBLANES), :] = xb[:, s * LANES:(s + 1) * LANES]
    x = jnp.concatenate([x_scr[s] for s in range(d // LANES)], axis=-1)
    u = jnp.dot(_rmsnorm(x, gm_ref[...]).astype(BF16), wu_ref[...].astype(BF16), preferred_element_type=F32)
    ub16 = u.astype(BF16)
    for kt in range(2):
        ukt = ub16[:, kt * kw:(kt + 1) * kw]
        bu_scr[:, kt * half:(kt + 1) * half] = jnp.dot(ukt, bb_ref[kt, :, :half], preferred_element_type=F32)
        bu_scr[:, nstate + kt * half:nstate + (kt + 1) * half] = jnp.dot(
            ukt, bb_ref[kt, :, half:], preferred_element_type=F32)

    lane_chunk = 512
    for lc in range(nstate // lane_chunk):
        lo = lc * lane_chunk
        re_sl = slice(lo, lo + lane_chunk)
        im_sl = slice(nstate + lo, nstate + lo + lane_chunk)
        ar = jnp.broadcast_to(ar_ref[:, re_sl], (SUBLANES, lane_chunk))
        ai = jnp.broadcast_to(ai_ref[:, re_sl], (SUBLANES, lane_chunk))

        for g in range(groups):
            hr, hi = hr_scr[g, :, re_sl], hi_scr[g, :, re_sl]
            for t in range(tt):
                r0 = (g * tt + t) * SUBLANES
                rs = slice(r0, r0 + SUBLANES)
                hr, hi = (ar * hr - ai * hi + bu_scr[rs, re_sl], ar * hi + ai * hr + bu_scr[rs, im_sl])
                bu_scr[rs, re_sl] = hr
                bu_scr[rs, im_sl] = hi
            hr_scr[g, :, re_sl] = hr
            hi_scr[g, :, re_sl] = hi

    dm = wg_ref.shape[-1] // 2
    ys = []
    for n in range(2):
        h_re = bu_scr[:, n * half:(n + 1) * half].astype(BF16)
        h_im = bu_scr[:, nstate + n * half:nstate + (n + 1) * half].astype(BF16)
        ys.append(jnp.dot(h_re, cc_ref[n, :half, :], preferred_element_type=F32)
                  + jnp.dot(h_im, cc_ref[n, half:, :], preferred_element_type=F32))
    y = jnp.concatenate(ys, axis=-1) + d_ref[...] * u
    z = jnp.dot(jax.nn.gelu(y).astype(BF16), wg_ref[...], preferred_element_type=F32) + bg_ref[...]
    yb = z[:, :dm] * jax.nn.sigmoid(z[:, dm:])
    for s in range(dm // LANES):
        y_scr[s] = yb[:, s * LANES:(s + 1) * LANES]
    for g in range(groups):
        for b in range(SUBLANES):
            for s in range(dm // LANES):
                y_ref[g * SUBLANES + b, :, s * LANES:(s + 1) * LANES] = y_scr[
                    s, pl.ds(g * tt * SUBLANES + b, tt, stride=SUBLANES), :]

    @pl.when(j == pl.num_programs(1) - 1)
    def _():
        hr_out[...] = hr_scr[...]
        hi_out[...] = hi_scr[...]


def _s5_branch(x2d, batch, seq, g_mix, w_in, u_col, h0r, h0i, ar, ai, bb, cc, d_skip, w_glu, b_glu):
    nstate = ar.shape[-1]
    d = x2d.shape[1]
    dm = w_glu.shape[1] // 2
    tt = min(S5_TIME_TILE, seq)
    nj = seq // tt
    ngroups = batch // SUBLANES
    column_inputs = nj == 1
    groups = min(ngroups, max(1, S5_ROW_BLOCK // (tt * SUBLANES))) if column_inputs else 1
    rows = groups * tt * SUBLANES
    body = functools.partial(_s5_body, tt=tt, groups=groups, nstate=nstate, column_inputs=column_inputs)
    if column_inputs:
        x_specs = [pl.BlockSpec((rows, LANES), lambda i, j, s=s: (i, s)) for s in range(d // LANES)]
    else:
        x_specs = [pl.BlockSpec((tt, d), lambda i, j, b=b: ((i * SUBLANES + b) * nj + j, 0))
                   for b in range(SUBLANES)]
    state_spec = pl.BlockSpec((groups, SUBLANES, nstate), lambda i, j: (i, 0, 0))
    width = d_skip.shape[-1]
    return pl.pallas_call(
        body, grid=(ngroups // groups, nj),
        in_specs=x_specs + [
            _resident(g_mix.shape),
            pl.BlockSpec((d, width), lambda i, j: (0, u_col // width), pipeline_mode=pl.Buffered(1)),
            state_spec, state_spec, _resident(ar.shape), _resident(ai.shape), _resident(bb.shape),
            _resident(cc.shape), _resident(d_skip.shape), _resident(w_glu.shape), _resident(b_glu.shape)],
        out_specs=[pl.BlockSpec((groups * SUBLANES, tt, dm), lambda i, j: (i, j, 0)), state_spec, state_spec],
        out_shape=[jax.ShapeDtypeStruct((batch, seq, dm), F32),
                   jax.ShapeDtypeStruct((ngroups, SUBLANES, nstate), F32),
                   jax.ShapeDtypeStruct((ngroups, SUBLANES, nstate), F32)],
        scratch_shapes=[pltpu.VMEM((groups, SUBLANES, nstate), F32), pltpu.VMEM((groups, SUBLANES, nstate), F32),
                        pltpu.VMEM((rows, 2 * nstate), F32),
                        pltpu.VMEM((d // LANES, rows, LANES), F32),
                        pltpu.VMEM((dm // LANES, rows, LANES), F32)],
        compiler_params=_cparams(("parallel", "arbitrary")), name="s5_branch")(
            *([x2d] * len(x_specs)), g_mix, w_in, h0r, h0i, ar, ai, bb, cc, d_skip, w_glu, b_glu)


def _cumsum_rows(x, c):
    row = lax.broadcasted_iota(I32, x.shape, 0) & (c - 1)
    s = 1
    while s < c:
        x = x + jnp.where(row >= s, pltpu.roll(x, s, axis=0), 0.0)
        s *= 2
    return x


def _hgrn_gates(q, fr, lb, scale, c):
    rows, n = q.shape
    f = lb + (1.0 - lb) * jax.nn.sigmoid(fr)
    k = 1.0 - f
    b = _cumsum_rows(jnp.log(f), c)
    b3 = b.reshape(rows // c, c, n)
    b_last = jnp.broadcast_to(b3[:, c - 1:c, :], b3.shape).reshape(rows, n)
    q_dec = (q * scale) * jnp.exp(b)
    k_dec = k * jnp.exp(-b)
    k_end = k * jnp.exp(b_last - b)
    return q_dec, k_dec, k_end, jnp.exp(b_last)


def _causal_scores(q_dec, k_dec):
    c = q_dec.shape[0]
    s = lax.dot_general(q_dec, k_dec, (((1,), (1,)), ((), ())), preferred_element_type=F32)
    keep = lax.broadcasted_iota(I32, (c, c), 0) >= lax.broadcasted_iota(I32, (c, c), 1)
    return jnp.where(keep, s, 0.0).astype(BF16)


def _gated_out(o, gn, og):
    o = o * lax.rsqrt(jnp.mean(o * o, axis=-1, keepdims=True) + RMS_EPS) * gn
    return (o * jax.nn.silu(og)).astype(BF16)


def _hgrn_long_body(q_ref, f_ref, v_ref, og_ref, lb_ref, gn_ref, o_ref, sfin_ref, st_scr, *, c, heads, dk, scale):
    j = pl.program_id(1)

    @pl.when(j == 0)
    def _():
        st_scr[...] = jnp.zeros_like(st_scr)

    def chunk(ci, carry):
        rs = pl.ds(pl.multiple_of(ci * c, c), c)
        for h in range(heads):
            hs = slice(h * dk, (h + 1) * dk)
            q_dec, k_dec, k_end, decay = _hgrn_gates(q_ref[rs, hs], f_ref[rs, hs], lb_ref[:, hs], scale, c)
            q_dec = q_dec.astype(BF16)
            v = v_ref[rs, hs].astype(BF16)
            scores = _causal_scores(q_dec, k_dec.astype(BF16))
            st = st_scr[h]
            o = (lax.dot_general(q_dec, st.astype(BF16), (((1,), (1,)), ((), ())), preferred_element_type=F32)
                 + jnp.dot(scores, v, preferred_element_type=F32))
            st_scr[h] = decay[:1] * st + lax.dot_general(
                v, k_end.astype(BF16), (((0,), (0,)), ((), ())), preferred_element_type=F32)
            o_ref[rs, hs] = _gated_out(o, gn_ref[:, hs], og_ref[rs, hs])
        return carry

    lax.fori_loop(0, q_ref.shape[0] // c, chunk, 0, unroll=HGRN_CHUNK_UNROLL)

    @pl.when(j == pl.num_programs(1) - 1)
    def _():
        for h in range(heads):
            sfin_ref[0, h] = st_scr[h].T


def _hgrn_long(proj, lb, gn, batch, seq, heads, dk, row_off):
    width = heads * dk
    tb = min(HGRN_TIME_TILE, seq)
    nj = seq // tb
    off = row_off // tb
    c = min(HG_CHUNK, seq)
    body = functools.partial(_hgrn_long_body, c=c, heads=heads, dk=dk, scale=dk ** -0.5)

    def col(k):
        return pl.BlockSpec((tb, width), lambda b, j, k=k: (off + b * nj + j, k))

    return pl.pallas_call(
        body, grid=(batch, nj),
        in_specs=[col(0), col(1), col(2), col(3), _resident(lb.shape), _resident(gn.shape)],
        out_specs=[pl.BlockSpec((tb, width), lambda b, j: (b * nj + j, 0)),
                   pl.BlockSpec((1, heads, dk, dk), lambda b, j: (b, 0, 0, 0))],
        out_shape=[jax.ShapeDtypeStruct((batch * seq, width), BF16),
                   jax.ShapeDtypeStruct((batch, heads, dk, dk), F32)],
        scratch_shapes=[pltpu.VMEM((heads, dk, dk), F32)],
        compiler_params=_cparams(("parallel", "arbitrary")), name="hgrn_long")(proj, proj, proj, proj, lb, gn)


def _hgrn_short_body(q_ref, f_ref, v_ref, og_ref, lb_ref, gn_ref, s0_ref, o_ref, snew_ref, *, c, heads, dk, scale):
    def one_seq(sq, carry):
        rs = pl.ds(pl.multiple_of(sq * c, c), c)
        for h in range(heads):
            hs = slice(h * dk, (h + 1) * dk)
            q_dec, k_dec, k_end, decay = _hgrn_gates(q_ref[rs, hs], f_ref[rs, hs], lb_ref[:, hs], scale, c)
            q_dec = q_dec.astype(BF16)
            v = v_ref[rs, hs].astype(BF16)
            scores = _causal_scores(q_dec, k_dec.astype(BF16))
            s0 = s0_ref[sq, h]
            o = (jnp.dot(q_dec, s0.astype(BF16), preferred_element_type=F32)
                 + jnp.dot(scores, v, preferred_element_type=F32))
            decay_col = jnp.broadcast_to(decay[:1], (dk, dk)).T
            snew_ref[sq, h] = decay_col * s0 + lax.dot_general(
                k_end.astype(BF16), v, (((0,), (0,)), ((), ())), preferred_element_type=F32)
            o_ref[rs, hs] = _gated_out(o, gn_ref[:, hs], og_ref[rs, hs])
        return carry

    lax.fori_loop(0, s0_ref.shape[0], one_seq, 0, unroll=HGRN_SEQ_UNROLL)


def _hgrn_short(proj, lb, gn, s0, seq, row_off):
    batch, heads, dk, _ = s0.shape
    width = heads * dk
    nb = HGRN_SEQ_TILE
    rows = nb * seq
    off = row_off // rows
    body = functools.partial(_hgrn_short_body, c=seq, heads=heads, dk=dk, scale=dk ** -0.5)

    def col(k):
        return pl.BlockSpec((rows, width), lambda i, k=k: (off + i, k))

    state_spec = pl.BlockSpec((nb, heads, dk, dk), lambda i: (i, 0, 0, 0))
    return pl.pallas_call(
        body, grid=(batch // nb,),
        in_specs=[col(0), col(1), col(2), col(3), _resident(lb.shape), _resident(gn.shape), state_spec],
        out_specs=[pl.BlockSpec((rows, width), lambda i: (i, 0)), state_spec],
        out_shape=[jax.ShapeDtypeStruct((batch * seq, width), BF16), jax.ShapeDtypeStruct(s0.shape, F32)],
        compiler_params=_cparams(("parallel",)), name="hgrn_short")(proj, proj, proj, proj, lb, gn, s0)


def _first_index_of_max(vals):
    m = vals[0]
    for v in vals[1:]:
        m = jnp.maximum(m, v)
    idx = jnp.full(m.shape, len(vals), I32)
    for e in range(len(vals) - 1, -1, -1):
        idx = jnp.where(vals[e] == m, e, idx)
    return m, idx


def _route_tile(lg, ids_ref, w_ref, rk_ref, cnt_ref, carry_scr, groups, experts):
    i = pl.program_id(0)
    tile = lg.shape[1]
    n_exp = groups * experts

    @pl.when(i == 0)
    def _():
        carry_scr[...] = jnp.zeros_like(carry_scr)

    gl = [lg[g:g + 1, :] for g in range(groups)]
    gmax, gidx = _first_index_of_max(gl)
    denom = jnp.exp(gl[0] - gmax)
    for g in range(1, groups):
        denom = denom + jnp.exp(gl[g] - gmax)
    g_w = 1.0 / denom

    el = []
    for e in range(experts):
        v = lg[groups + e:groups + e + 1, :]
        for g in range(1, groups):
            r = groups + g * experts + e
            v = jnp.where(gidx == g, lg[r:r + 1, :], v)
        el.append(v)
    v1, i1 = _first_index_of_max(el)
    rest = [jnp.where(i1 == e, -jnp.inf, el[e]) for e in range(experts)]
    v2, i2 = _first_index_of_max(rest)
    t = jnp.exp(v2 - v1)
    inv = 1.0 / (1.0 + t)
    e1 = gidx * experts + i1
    e2 = gidx * experts + i2

    erow = lax.broadcasted_iota(I32, (n_exp, tile), 0)
    oh1 = (erow == e1).astype(F32)
    oh2 = (erow == e2).astype(F32)
    oh = oh1 + oh2
    before = (lax.broadcasted_iota(I32, (tile, tile), 0) < lax.broadcasted_iota(I32, (tile, tile), 1))
    cnt = jnp.dot(oh.astype(BF16), before.astype(BF16), preferred_element_type=F32) + carry_scr[:, 0:1]
    ids_ref[0:1, :] = e1
    ids_ref[1:2, :] = e2
    w_ref[0:1, :] = inv * g_w
    w_ref[1:2, :] = (t * inv) * g_w
    rk_ref[0:1, :] = jnp.sum(oh1 * cnt, axis=0, keepdims=True).astype(I32)
    rk_ref[1:2, :] = jnp.sum(oh2 * cnt, axis=0, keepdims=True).astype(I32)
    carry_scr[...] = carry_scr[...] + jnp.sum(oh, axis=1, keepdims=True)

    @pl.when(i == pl.num_programs(0) - 1)
    def _():
        cnt_ref[...] = carry_scr[...]


def _merge_body(*refs, n_first, groups, experts, n_gate_blocks):
    op_ref, os_ref, ybp_ref, ybs_ref, xp_ref, xs_ref, gm_ref = refs[:7]
    gate_w_refs = refs[7:7 + n_gate_blocks]
    (wa_ref, wo_ref, gf_ref, wr_ref, br_ref, h_ref, xn_ref, ids_ref, w_ref, rk_ref, cnt_ref, carry_scr,
     wgt_ref) = refs[7 + n_gate_blocks:]
    d = h_ref.shape[1]

    @pl.when(pl.program_id(0) == 0)
    def _():
        gw = gate_w_refs[0].shape[1]
        for q, ref in enumerate(gate_w_refs):
            wgt_ref[:, q * gw:(q + 1) * gw] = ref[...].astype(BF16)

    x = _pick(xp_ref, xs_ref, n_first)
    gates = jnp.dot(_rmsnorm(x, gm_ref[...]).astype(BF16), wgt_ref[...], preferred_element_type=F32)
    y_a = jnp.dot(_pick(op_ref, os_ref, n_first), wa_ref[...], preferred_element_type=F32)
    merged = jax.nn.sigmoid(gates[:, :d]) * y_a + jax.nn.sigmoid(gates[:, d:]) * _pick(ybp_ref, ybs_ref, n_first)
    h = x + jnp.dot(merged.astype(BF16), wo_ref[...], preferred_element_type=F32)
    h_ref[...] = h
    xn = _rmsnorm(h, gf_ref[...])
    _store_token_tiles(xn_ref, xn)
    logits_t = lax.dot_general(wr_ref[...], xn.astype(BF16), (((1,), (1,)), ((), ())),
                               preferred_element_type=F32) + br_ref[...]
    _route_tile(logits_t, ids_ref, w_ref, rk_ref, cnt_ref, carry_scr, groups, experts)


def _merge(o_p, o_s, yb_p, yb_s, xp, xs, g_mix, w_in, gate_col, wa, wo, gf, wr, br, groups, experts):
    d = xp.shape[1]
    total = xp.shape[0] + xs.shape[0]
    tm = TOKEN_TILE
    n_first = xp.shape[0] // tm
    n_exp = groups * experts
    pair = _two_source_specs(tm, d, n_first)
    top = pl.BlockSpec((MOE_TOP_K, tm), lambda i: (0, i))
    gw = math.gcd(gate_col, 2 * d)
    n_gate_blocks = 2 * d // gw
    gate_specs = [pl.BlockSpec((d, gw), lambda i, q=q: (0, gate_col // gw + q), pipeline_mode=pl.Buffered(1))
                  for q in range(n_gate_blocks)]
    weights = [wa, wo, gf, wr, br]
    return pl.pallas_call(
        functools.partial(_merge_body, n_first=n_first, groups=groups, experts=experts,
                          n_gate_blocks=n_gate_blocks),
        grid=(total // tm,),
        in_specs=pair + pair + pair + [_resident(g_mix.shape)] + gate_specs + [_resident(w.shape) for w in weights],
        out_specs=[pl.BlockSpec((tm, d), lambda i: (i, 0)), pl.BlockSpec((tm * SUBLANES, LANES), lambda i: (i, 0)),
                   top, top, top, pl.BlockSpec((n_exp, LANES), lambda i: (0, 0))],
        out_shape=[jax.ShapeDtypeStruct((total, d), F32), jax.ShapeDtypeStruct((total * SUBLANES, LANES), F32),
                   jax.ShapeDtypeStruct((MOE_TOP_K, total), I32), jax.ShapeDtypeStruct((MOE_TOP_K, total), F32),
                   jax.ShapeDtypeStruct((MOE_TOP_K, total), I32), jax.ShapeDtypeStruct((n_exp, LANES), F32)],
        scratch_shapes=[pltpu.VMEM((n_exp, LANES), F32), pltpu.VMEM((d, 2 * d), BF16)],
        compiler_params=_cparams(("arbitrary",)), name="merge_route")(
            o_p, o_s, yb_p, yb_s, xp, xs, g_mix, *([w_in] * n_gate_blocks), *weights)


def _row_copy(src, dst, sem):
    return pltpu.make_async_copy(src, dst, sem)


def _token_rows(r):
    return pl.ds(pl.multiple_of(r * SUBLANES, SUBLANES), SUBLANES)


def _dispatch_body(*refs, n_sorted):
    pos_refs = refs[:MOE_TOP_K]
    x_ref, o_hbm, ring, zero_scr, sem, pad_sem = refs[MOE_TOP_K:]
    i = pl.program_id(0)
    tile = x_ref.shape[0] // SUBLANES
    par = lax.rem(i, 2)

    @pl.when(i == 0)
    def _():
        zero_scr[...] = jnp.zeros_like(zero_scr)
        pad = _row_copy(zero_scr, o_hbm.at[pl.ds(n_sorted * SUBLANES, zero_scr.shape[0])], pad_sem.at[0])
        pad.start()
        pad.wait()

    ring[par] = x_ref[...]

    def issue(r, carry):
        for k in range(MOE_TOP_K):
            p = pos_refs[k][r]
            _row_copy(ring.at[par, _token_rows(r)], o_hbm.at[_token_rows(p)], sem.at[par, k]).start(priority=k)
        return carry

    lax.fori_loop(0, tile, issue, 0, unroll=8)

    def drain(slot):
        for k in range(MOE_TOP_K):
            _row_copy(ring.at[slot], o_hbm.at[pl.ds(0, tile * SUBLANES)], sem.at[slot, k]).wait()

    @pl.when(i > 0)
    def _():
        drain(1 - par)

    @pl.when(i == pl.num_programs(0) - 1)
    def _():
        drain(par)


def _dispatch(pos_slots, xn_tiles):
    total = xn_tiles.shape[0] // SUBLANES
    tile = DISPATCH_TILE
    n_sorted = total * MOE_TOP_K
    pad = EXPERT_WINDOWS[-1]
    return pl.pallas_call(
        functools.partial(_dispatch_body, n_sorted=n_sorted), grid=(total // tile,),
        in_specs=[pl.BlockSpec((tile,), lambda i: (i,), memory_space=pltpu.SMEM)] * MOE_TOP_K + [
            pl.BlockSpec((tile * SUBLANES, LANES), lambda i: (i, 0))],
        out_specs=pl.BlockSpec(memory_space=pl.ANY),
        out_shape=jax.ShapeDtypeStruct(((n_sorted + pad) * SUBLANES, LANES), xn_tiles.dtype),
        scratch_shapes=[pltpu.VMEM((2, tile * SUBLANES, LANES), xn_tiles.dtype),
                        pltpu.VMEM((pad * SUBLANES, LANES), xn_tiles.dtype),
                        pltpu.SemaphoreType.DMA((2, MOE_TOP_K)), pltpu.SemaphoreType.DMA((1,))],
        compiler_params=_cparams(("arbitrary",)), name="dispatch")(*pos_slots, xn_tiles)


def _experts_body(it_exp, it_row, it_cls, it_first, it_next, n_items, xs_hbm, wg_hbm, wu_hbm, wd_hbm, ys_hbm,
                  xbuf, ybuf, wg_s, wu_s, wd_s, wg_b, wu_b, wd_b, sem_in, sem_out, sem_w):
    j = pl.program_id(0)
    n = n_items[0]
    pad = EXPERT_WINDOWS[-1]
    slot = lax.rem(j, 2)

    def weight_copies(e, s):
        return [pltpu.make_async_copy(hbm.at[e], stage.at[s], sem_w.at[s, t])
                for t, (hbm, stage) in enumerate(((wg_hbm, wg_s), (wu_hbm, wu_s), (wd_hbm, wd_s)))]

    def by_size(item, fn):
        for ci, m in enumerate(EXPERT_WINDOWS):
            pl.when(it_cls[item] == ci)(functools.partial(fn, m))

    def window(item, m):
        return pl.ds(pl.multiple_of(it_row[item] * SUBLANES, SUBLANES), m * SUBLANES)

    def in_copy(item, s, m):
        return pltpu.make_async_copy(xs_hbm.at[window(item, m)], xbuf.at[s, pl.ds(0, m * SUBLANES)], sem_in.at[s])

    def out_copy(item, s, m):
        return pltpu.make_async_copy(ybuf.at[s, pl.ds(0, m * SUBLANES)], ys_hbm.at[window(item, m)], sem_out.at[s])

    def compute(m):
        x = _load_token_tiles(xbuf, m, (slot,)).astype(BF16)
        hg = jnp.dot(x, wg_b[...], preferred_element_type=F32)
        hu = jnp.dot(x, wu_b[...], preferred_element_type=F32)
        hid = (jax.nn.silu(hg) * hu).astype(BF16)
        _store_token_tiles(ybuf, jnp.dot(hid, wd_b[...], preferred_element_type=F32), (slot,))

    @pl.when(j < n)
    def _():
        @pl.when(j == 0)
        def _():
            by_size(0, lambda m: in_copy(0, 0, m).start())
            for c in weight_copies(it_exp[0], it_first[0] - 1):
                c.start()
            tail_rows = pl.ds(0, pad * SUBLANES)
            ybuf[1, tail_rows, :] = jnp.zeros((pad * SUBLANES, LANES), F32)
            tail = pltpu.make_async_copy(
                ybuf.at[1, tail_rows], ys_hbm.at[pl.ds(ys_hbm.shape[0] - pad * SUBLANES, pad * SUBLANES)],
                sem_out.at[1])
            tail.start()
            tail.wait()

        @pl.when(j + 1 < n)
        def _():
            by_size(j + 1, lambda m: in_copy(j + 1, 1 - slot, m).start())

        @pl.when(it_first[j] > 0)
        def _():
            s = it_first[j] - 1
            for c in weight_copies(it_exp[j], s):
                c.wait()
            wg_b[...] = wg_s[s].astype(BF16)
            wu_b[...] = wu_s[s].astype(BF16)
            wd_b[...] = wd_s[s].astype(BF16)

            @pl.when(it_next[j] >= 0)
            def _():
                for c in weight_copies(it_next[j], 1 - s):
                    c.start()

        by_size(j, lambda m: in_copy(j, slot, m).wait())
        by_size(j, compute)

        @pl.when(j > 0)
        def _():
            by_size(j - 1, lambda m: out_copy(j - 1, 1 - slot, m).wait())

        by_size(j, lambda m: out_copy(j, slot, m).start())

        @pl.when(j == n - 1)
        def _():
            by_size(j, lambda m: out_copy(j, slot, m).wait())


def _experts(items, xs, wg, wu, wd):
    d, de = wg.shape[1], wg.shape[2]
    tm = EXPERT_TILE
    max_items = items[0].shape[0]
    grid_spec = pltpu.PrefetchScalarGridSpec(
        num_scalar_prefetch=6, grid=(max_items,),
        in_specs=[pl.BlockSpec(memory_space=pl.ANY)] * 4,
        out_specs=pl.BlockSpec(memory_space=pl.ANY),
        scratch_shapes=[pltpu.VMEM((2, tm * SUBLANES, LANES), F32), pltpu.VMEM((2, tm * SUBLANES, LANES), F32),
                        pltpu.VMEM((2, d, de), F32), pltpu.VMEM((2, d, de), F32), pltpu.VMEM((2, de, d), F32),
                        pltpu.VMEM((d, de), BF16), pltpu.VMEM((d, de), BF16), pltpu.VMEM((de, d), BF16),
                        pltpu.SemaphoreType.DMA((2,)), pltpu.SemaphoreType.DMA((2,)),
                        pltpu.SemaphoreType.DMA((2, 3))])
    return pl.pallas_call(
        _experts_body, grid_spec=grid_spec, out_shape=jax.ShapeDtypeStruct(xs.shape, F32),
        compiler_params=_cparams(("arbitrary",)), name="experts")(*items, xs, wg, wu, wd)


def _combine_body(*refs, final_norm):
    pos_refs, pos_next_refs = refs[:MOE_TOP_K], refs[MOE_TOP_K:2 * MOE_TOP_K]
    h_ref, w_ref, g_ref, ys_hbm, y_ref, buf, sem = refs[2 * MOE_TOP_K:]
    i = pl.program_id(0)
    tile = h_ref.shape[0]
    par = lax.rem(i, 2)

    def gather(tables, slot):
        def issue(r, carry):
            for k in range(MOE_TOP_K):
                p = tables[k][r]
                _row_copy(ys_hbm.at[_token_rows(p)], buf.at[slot, k, _token_rows(r)],
                          sem.at[slot, k]).start(priority=k)
            return carry

        lax.fori_loop(0, tile, issue, 0, unroll=8)

    @pl.when(i == 0)
    def _():
        gather(pos_refs, 0)

    for k in range(MOE_TOP_K):
        _row_copy(ys_hbm.at[pl.ds(0, tile * SUBLANES)], buf.at[par, k], sem.at[par, k]).wait()

    def combine_tile():
        h = h_ref[...] + (w_ref[:, 0:1] * _load_token_tiles(buf, tile, (par, 0))
                          + w_ref[:, 1:2] * _load_token_tiles(buf, tile, (par, 1)))
        y_ref[...] = _rmsnorm(h, g_ref[...]) if final_norm else h

    for slot in range(2):
        @pl.when(jnp.logical_and(i + 1 < pl.num_programs(0), par != slot))
        def _(slot=slot):
            for r in range(tile):
                for k in range(MOE_TOP_K):
                    p = pos_next_refs[k][r]
                    _row_copy(ys_hbm.at[_token_rows(p)], buf.at[slot, k, pl.ds(r * SUBLANES, SUBLANES)],
                              sem.at[slot, k]).start(priority=k)
            combine_tile()

    @pl.when(i + 1 == pl.num_programs(0))
    def _():
        combine_tile()


def _combine(pos_slots, h_all, w_t, g, ys, rows, row_off, final_norm):
    d = h_all.shape[1]
    tile = COMBINE_TILE
    off = row_off // tile
    last_block = h_all.shape[0] // tile - 1
    this_tile = pl.BlockSpec((tile,), lambda i: (off + i,), memory_space=pltpu.SMEM)
    next_tile = pl.BlockSpec((tile,), lambda i: (jnp.minimum(off + i + 1, last_block),), memory_space=pltpu.SMEM)
    return pl.pallas_call(
        functools.partial(_combine_body, final_norm=final_norm), grid=(rows // tile,),
        in_specs=[this_tile] * MOE_TOP_K + [next_tile] * MOE_TOP_K + [
                  pl.BlockSpec((tile, d), lambda i: (off + i, 0)),
                  pl.BlockSpec((tile, MOE_TOP_K), lambda i: (off + i, 0)),
                  _resident(g.shape),
                  pl.BlockSpec(memory_space=pl.ANY)],
        out_specs=pl.BlockSpec((tile, d), lambda i: (i, 0)),
        out_shape=jax.ShapeDtypeStruct((rows, d), F32),
        scratch_shapes=[pltpu.VMEM((2, MOE_TOP_K, tile * SUBLANES, LANES), F32),
                        pltpu.SemaphoreType.DMA((2, MOE_TOP_K))],
        compiler_params=_cparams(("arbitrary",)), name="combine")(*pos_slots, *pos_slots, h_all, w_t, g, ys)


def _lookup(table, idx):
    sel = idx[None] == jnp.arange(table.shape[0], dtype=I32).reshape((-1,) + (1,) * idx.ndim)
    return jnp.sum(jnp.where(sel, table.reshape(sel.shape[:1] + (1,) * idx.ndim), 0), axis=0)


def _work_items(counts, n_sorted):
    big, mid, small = EXPERT_WINDOWS
    n_exp = counts.shape[0]
    max_items = n_sorted // big + 2 * n_exp
    ends = jnp.cumsum(counts)
    starts = ends - counts
    units = (counts % big + small - 1) // small
    n_big = counts // big + (units == big // small)
    units = jnp.where(units == big // small, 0, units)
    n_mid = units // (mid // small)
    n_e = n_big + n_mid + units % (mid // small)
    item_end = jnp.cumsum(n_e)
    item_start = item_end - n_e
    n_items = item_end[-1]
    j = jnp.minimum(jnp.arange(max_items, dtype=I32), n_items - 1)
    e = jnp.sum((item_end[None, :] <= j[:, None]).astype(I32), axis=1)
    k = j - _lookup(item_start, e)
    nb, nm = _lookup(n_big, e), _lookup(n_mid, e)
    cls = jnp.where(k < nb, 0, jnp.where(k < nb + nm, 1, 2))
    row = _lookup(starts, e) + jnp.where(cls == 0, k * big, nb * big + jnp.where(cls == 1, 0, nm * mid))
    ordinal = jnp.cumsum((n_e > 0).astype(I32)) - 1
    first = jnp.where(k == 0, 1 + _lookup(ordinal, e) % 2, 0)
    nxt_item = _lookup(item_end, e)
    nxt = jnp.where(nxt_item < n_items, jnp.sum((item_end[None, :] <= nxt_item[:, None]).astype(I32), axis=1), -1)
    return (e, row.astype(I32), cls.astype(I32), first.astype(I32), nxt.astype(I32),
            n_items.reshape(1).astype(I32)), starts


def _s5_discretise(lam_re, lam_im, log_dt, b_re, b_im, c_re, c_im):
    g, p = lam_re.shape
    ch = b_re.shape[-1]
    lam_re = lam_re.astype(F32)
    lam_im = lam_im.astype(F32)
    dt = jnp.exp(log_dt.astype(F32))[:, None]
    mag = jnp.exp(lam_re * dt)
    ab_re = mag * jnp.cos(lam_im * dt)
    ab_im = mag * jnp.sin(lam_im * dt)
    den = lam_re * lam_re + lam_im * lam_im
    nr = ab_re - 1.0
    coef_re = (nr * lam_re + ab_im * lam_im) / den
    coef_im = (ab_im * lam_re - nr * lam_im) / den
    bb_re = coef_re[..., None] * b_re - coef_im[..., None] * b_im
    bb_im = coef_re[..., None] * b_im + coef_im[..., None] * b_re
    gh = g // 2
    eye = jnp.eye(gh, dtype=F32)

    def in_block(m):
        return jnp.einsum("gpc,gh->gchp", m, eye).reshape(gh * ch, gh * p)

    def out_block(m):
        return jnp.einsum("gcp,gh->gphc", m, eye).reshape(gh * p, gh * ch)

    bb = jnp.stack([jnp.concatenate([in_block(bb_re[k * gh:(k + 1) * gh]), in_block(bb_im[k * gh:(k + 1) * gh])],
                                    axis=1) for k in range(2)]).astype(BF16)
    cc = jnp.stack([jnp.concatenate([out_block(c_re[k * gh:(k + 1) * gh]), out_block(-c_im[k * gh:(k + 1) * gh])],
                                    axis=0) for k in range(2)]).astype(BF16)
    return ab_re.reshape(1, g * p), ab_im.reshape(1, g * p), bb, cc


def kernel(x_prompt, x_sample, state_hgrn, state_s5_re, state_s5_im, norm_mix_g, w_in, hgrn_lb_raw, hgrn_onorm_g, w_branch_a, s5_lambda_re, s5_lambda_im, s5_log_dt, s5_b_re, s5_b_im, s5_c_re, s5_c_im, s5_d, w_glu, b_glu, w_out, norm_ffn_g, w_router_group, b_router_group, w_router_expert, b_router_expert, w_exp_gate, w_exp_up, w_exp_down, norm_final_g):
    depth = norm_mix_g.shape[0]
    bp, lp, d = x_prompt.shape
    bs, ls, _ = x_sample.shape
    heads, dk = state_hgrn.shape[2], state_hgrn.shape[3]
    kw = heads * dk
    s5_groups, s5_state = state_s5_re.shape[2], state_s5_re.shape[3]
    s5_width = s5_d.shape[-1]
    nstate = s5_groups * s5_state
    moe_groups, _, experts = w_router_expert.shape[1:]
    n_exp = moe_groups * experts
    rows_p, rows_s = bp * lp, bs * ls
    total = rows_p + rows_s
    n_sorted = total * MOE_TOP_K
    assert kw == d and state_hgrn.shape[4] == dk, "column blocks assume key width == value width == model width"
    assert d == SUBLANES * LANES, "token-tile layout holds one token per (8, 128) tile"
    assert s5_groups % 2 == 0 and bp % SUBLANES == 0 and bs % HGRN_SEQ_TILE == 0
    assert (4 * kw) % s5_width == 0, "the S5 input columns must start on a multiple of their width"

    lb_all = jnp.cumsum(jax.nn.softmax(hgrn_lb_raw.astype(F32), axis=0), axis=0)

    hp = x_prompt.reshape(rows_p, d)
    hs = x_sample.reshape(rows_s, d)
    hg_p, re_p, im_p, hg_s, re_s, im_s = [], [], [], [], [], []
    zeros_state = jnp.zeros((bp // SUBLANES, SUBLANES, nstate), F32)

    for l in range(depth):
        w = w_in[l]
        g_mix = norm_mix_g[l].reshape(1, d)
        u_col = 4 * kw
        gate_col = u_col + s5_width
        proj = _in_proj(hp, hs, g_mix, w, u_col)

        ar, ai, bb, cc = _s5_discretise(s5_lambda_re[l], s5_lambda_im[l], s5_log_dt[l], s5_b_re[l], s5_b_im[l],
                                        s5_c_re[l], s5_c_im[l])
        s5_args = (ar, ai, bb, cc, s5_d[l].reshape(1, s5_width), w_glu[l].astype(BF16), b_glu[l].reshape(1, -1))
        yb_p, fr_p, fi_p = _s5_branch(hp, bp, lp, g_mix, w, u_col, zeros_state, zeros_state, *s5_args)
        yb_s, fr_s, fi_s = _s5_branch(hs, bs, ls, g_mix, w, u_col,
                                      state_s5_re[l].reshape(bs // SUBLANES, SUBLANES, nstate),
                                      state_s5_im[l].reshape(bs // SUBLANES, SUBLANES, nstate), *s5_args)

        lb = lb_all[l].reshape(1, kw)
        gn = hgrn_onorm_g[l].reshape(1, kw)
        o_p, hgp = _hgrn_long(proj, lb, gn, bp, lp, heads, dk, 0)
        o_s, hgs = _hgrn_short(proj, lb, gn, state_hgrn[l].astype(F32), ls, rows_p)

        nr = -(-(moe_groups + n_exp) // SUBLANES) * SUBLANES
        wr = jnp.concatenate([w_router_group[l].T, w_router_expert[l].transpose(0, 2, 1).reshape(n_exp, d)], axis=0)
        wr = jnp.pad(wr, ((0, nr - wr.shape[0]), (0, 0))).astype(BF16)
        br = jnp.pad(jnp.concatenate([b_router_group[l], b_router_expert[l].reshape(n_exp)]),
                     (0, nr - moe_groups - n_exp)).reshape(nr, 1).astype(F32)
        h_all, xn_all, ids, wts, ranks, cnt = _merge(
            o_p, o_s, yb_p.reshape(rows_p, d), yb_s.reshape(rows_s, d), hp, hs, g_mix, w, gate_col,
            w_branch_a[l].astype(BF16), w_out[l].astype(BF16), norm_ffn_g[l].reshape(1, d), wr, br,
            moe_groups, experts)
        items, starts = _work_items(cnt[:, 0].astype(I32), n_sorted)
        pos_slots = [_lookup(starts, ids[k]) + ranks[k] for k in range(MOE_TOP_K)]
        xs = _dispatch(pos_slots, xn_all)
        ys = _experts(items, xs, w_exp_gate[l], w_exp_up[l], w_exp_down[l])

        last = l == depth - 1
        g_out = norm_final_g.reshape(1, d)
        hp = _combine(pos_slots, h_all, wts.T, g_out, ys, rows_p, 0, last)
        hs = _combine(pos_slots, h_all, wts.T, g_out, ys, rows_s, rows_p, last)

        hg_p.append(hgp)
        hg_s.append(hgs)
        re_p.append(fr_p.reshape(bp, s5_groups, s5_state))
        im_p.append(fi_p.reshape(bp, s5_groups, s5_state))
        re_s.append(fr_s.reshape(bs, s5_groups, s5_state))
        im_s.append(fi_s.reshape(bs, s5_groups, s5_state))

    y_prompt = hp.reshape(bp, lp, d).astype(x_prompt.dtype)
    y_sample = hs.reshape(bs, ls, d).astype(x_sample.dtype)
    return (y_prompt, y_sample, jnp.stack(hg_p), jnp.stack(re_p), jnp.stack(im_p),
            jnp.stack(hg_s), jnp.stack(re_s), jnp.stack(im_s))
```

```python
import functools
import math

import jax
import jax.numpy as jnp
from jax import lax
from jax.experimental import pallas as pl
from jax.experimental.pallas import tpu as pltpu

F32 = jnp.float32
BF16 = jnp.bfloat16
I32 = jnp.int32

RMS_EPS = 1e-6
HG_CHUNK = 64
MOE_TOP_K = 2

V7X_VMEM_BYTES = 64 * 1024 * 1024
VMEM_LIMIT_BYTES = V7X_VMEM_BYTES - 8 * 1024 * 1024
SUBLANES = 8
LANES = 128

TOKEN_TILE = 512
DISPATCH_TILE = 1024
COMBINE_TILE = 512
EXPERT_TILE = 1024
EXPERT_WINDOWS = (EXPERT_TILE, EXPERT_TILE // 2, EXPERT_TILE // 4)
S5_TIME_TILE = 128
S5_ROW_BLOCK = 1024
HGRN_TIME_TILE = 1024
HGRN_SEQ_TILE = 16
HGRN_CHUNK_UNROLL = 16
HGRN_SEQ_UNROLL = 8
PROJ_COL_TILE = 512
S5_SCAN_LANES = 512
ROW_COPY_UNROLL = 8


def _cparams(sem):
    return pltpu.CompilerParams(dimension_semantics=sem, vmem_limit_bytes=VMEM_LIMIT_BYTES)


def _resident(shape):
    nd = len(shape)
    return pl.BlockSpec(shape, lambda *_: (0,) * nd, pipeline_mode=pl.Buffered(1))


def _rmsnorm(x, g):
    return x * lax.rsqrt(jnp.mean(x * x, axis=-1, keepdims=True) + RMS_EPS) * g


def _two_source_specs(tm, width, n_first):
    return [pl.BlockSpec((tm, width), lambda i: (jnp.minimum(i, n_first - 1), 0)),
            pl.BlockSpec((tm, width), lambda i: (jnp.maximum(i - n_first, 0), 0))]


def _pick(first_ref, second_ref, n_first):
    return jnp.where(pl.program_id(0) < n_first, first_ref[...], second_ref[...])


def _store_token_tiles(ref, x, lead=()):
    rows = x.shape[0]
    for c in range(SUBLANES):
        ref[lead + (pl.ds(c, rows, stride=SUBLANES), slice(None))] = x[:, c * LANES:(c + 1) * LANES]


def _load_token_tiles(ref, rows, lead=()):
    return jnp.concatenate([ref[lead + (pl.ds(c, rows, stride=SUBLANES), slice(None))] for c in range(SUBLANES)],
                           axis=-1)


def _inproj_body(xp_ref, xs_ref, g_ref, w_ref, o_ref, wb_scr, *, n_first):
    @pl.when(pl.program_id(0) == 0)
    def _():
        wb_scr[...] = w_ref[...].astype(BF16)

    xb = _rmsnorm(_pick(xp_ref, xs_ref, n_first), g_ref[...]).astype(BF16)
    for j in range(0, wb_scr.shape[1], PROJ_COL_TILE):
        o_ref[:, j:j + PROJ_COL_TILE] = jnp.dot(xb, wb_scr[:, j:j + PROJ_COL_TILE], preferred_element_type=F32)


def _in_proj(xp, xs, g, w, n):
    d = xp.shape[1]
    tm = TOKEN_TILE
    total = xp.shape[0] + xs.shape[0]
    n_first = xp.shape[0] // tm
    return pl.pallas_call(
        functools.partial(_inproj_body, n_first=n_first), grid=(total // tm,),
        in_specs=_two_source_specs(tm, d, n_first) + [
            _resident((1, d)), pl.BlockSpec((d, n), lambda i: (0, 0), pipeline_mode=pl.Buffered(1))],
        out_specs=pl.BlockSpec((tm, n), lambda i: (i, 0)),
        out_shape=jax.ShapeDtypeStruct((total, n), F32),
        scratch_shapes=[pltpu.VMEM((d, n), BF16)],
        compiler_params=_cparams(("arbitrary",)), name="in_proj")(xp, xs, g, w)


def _s5_body(*refs, tt, groups, nstate, column_inputs):
    n_x = len(refs) - 19
    x_refs = refs[:n_x]
    (gm_ref, wu_ref, h0r_ref, h0i_ref, ar_ref, ai_ref, bb_ref, cc_ref, d_ref, wg_ref, bg_ref,
     y_ref, hr_out, hi_out, hr_scr, hi_scr, bu_scr, x_scr, y_scr) = refs[n_x:]
    j = pl.program_id(1)
    half = nstate // 2
    d = x_scr.shape[0] * LANES
    kw = wu_ref.shape[-1] // 2

    @pl.when(j == 0)
    def _():
        hr_scr[...] = h0r_ref[...]
        hi_scr[...] = h0i_ref[...]

    if column_inputs:
        for s in range(d // LANES):
            for g in range(groups):
                for t in range(tt):
                    r0 = (g * tt + t) * SUBLANES
                    x_scr[s, r0:r0 + SUBLANES, :] = x_refs[s][pl.ds(g * SUBLANES * tt + t, SUBLANES, stride=tt), :]
    else:
        for b in range(SUBLANES):
            xb = x_refs[b][...]
            for s in range(d // LANES):
                x_scr[s, pl.ds(b, tt, stride=SUBLANES), :] = xb[:, s * LANES:(s + 1) * LANES]
    x = jnp.concatenate([x_scr[s] for s in range(d // LANES)], axis=-1)
    u = jnp.dot(_rmsnorm(x, gm_ref[...]).astype(BF16), wu_ref[...].astype(BF16), preferred_element_type=F32)
    ub16 = u.astype(BF16)
    for kt in range(2):
        ukt = ub16[:, kt * kw:(kt + 1) * kw]
        bu_scr[:, kt * half:(kt + 1) * half] = jnp.dot(ukt, bb_ref[kt, :, :half], preferred_element_type=F32)
        bu_scr[:, nstate + kt * half:nstate + (kt + 1) * half] = jnp.dot(
            ukt, bb_ref[kt, :, half:], preferred_element_type=F32)

    lane_chunk = S5_SCAN_LANES
    for lc in range(nstate // lane_chunk):
        lo = lc * lane_chunk
        re_sl = slice(lo, lo + lane_chunk)
        im_sl = slice(nstate + lo, nstate + lo + lane_chunk)
        ar = jnp.broadcast_to(ar_ref[:, re_sl], (SUBLANES, lane_chunk))
        ai = jnp.broadcast_to(ai_ref[:, re_sl], (SUBLANES, lane_chunk))

        for g in range(groups):
            hr, hi = hr_scr[g, :, re_sl], hi_scr[g, :, re_sl]
            for t in range(tt):
                r0 = (g * tt + t) * SUBLANES
                rs = slice(r0, r0 + SUBLANES)
                hr, hi = (ar * hr - ai * hi + bu_scr[rs, re_sl], ar * hi + ai * hr + bu_scr[rs, im_sl])
                bu_scr[rs, re_sl] = hr
                bu_scr[rs, im_sl] = hi
            hr_scr[g, :, re_sl] = hr
            hi_scr[g, :, re_sl] = hi

    dm = wg_ref.shape[-1] // 2
    ys = []
    for n in range(2):
        h_re = bu_scr[:, n * half:(n + 1) * half].astype(BF16)
        h_im = bu_scr[:, nstate + n * half:nstate + (n + 1) * half].astype(BF16)
        ys.append(jnp.dot(h_re, cc_ref[n, :half, :], preferred_element_type=F32)
                  + jnp.dot(h_im, cc_ref[n, half:, :], preferred_element_type=F32))
    y = jnp.concatenate(ys, axis=-1) + d_ref[...] * u
    z = jnp.dot(jax.nn.gelu(y).astype(BF16), wg_ref[...], preferred_element_type=F32) + bg_ref[...]
    yb = z[:, :dm] * jax.nn.sigmoid(z[:, dm:])
    for s in range(dm // LANES):
        y_scr[s] = yb[:, s * LANES:(s + 1) * LANES]
    for g in range(groups):
        for b in range(SUBLANES):
            for s in range(dm // LANES):
                y_ref[g * SUBLANES + b, :, s * LANES:(s + 1) * LANES] = y_scr[
                    s, pl.ds(g * tt * SUBLANES + b, tt, stride=SUBLANES), :]

    @pl.when(j == pl.num_programs(1) - 1)
    def _():
        hr_out[...] = hr_scr[...]
        hi_out[...] = hi_scr[...]


def _s5_branch(x2d, batch, seq, g_mix, w_in, u_col, h0r, h0i, ar, ai, bb, cc, d_skip, w_glu, b_glu):
    nstate = ar.shape[-1]
    d = x2d.shape[1]
    dm = w_glu.shape[1] // 2
    tt = min(S5_TIME_TILE, seq)
    nj = seq // tt
    ngroups = batch // SUBLANES
    column_inputs = nj == 1
    groups = min(ngroups, max(1, S5_ROW_BLOCK // (tt * SUBLANES))) if column_inputs else 1
    rows = groups * tt * SUBLANES
    body = functools.partial(_s5_body, tt=tt, groups=groups, nstate=nstate, column_inputs=column_inputs)
    if column_inputs:
        x_specs = [pl.BlockSpec((rows, LANES), lambda i, j, s=s: (i, s)) for s in range(d // LANES)]
    else:
        x_specs = [pl.BlockSpec((tt, d), lambda i, j, b=b: ((i * SUBLANES + b) * nj + j, 0))
                   for b in range(SUBLANES)]
    state_spec = pl.BlockSpec((groups, SUBLANES, nstate), lambda i, j: (i, 0, 0))
    width = d_skip.shape[-1]
    return pl.pallas_call(
        body, grid=(ngroups // groups, nj),
        in_specs=x_specs + [
            _resident(g_mix.shape),
            pl.BlockSpec((d, width), lambda i, j: (0, u_col // width), pipeline_mode=pl.Buffered(1)),
            state_spec, state_spec, _resident(ar.shape), _resident(ai.shape), _resident(bb.shape),
            _resident(cc.shape), _resident(d_skip.shape), _resident(w_glu.shape), _resident(b_glu.shape)],
        out_specs=[pl.BlockSpec((groups * SUBLANES, tt, dm), lambda i, j: (i, j, 0)), state_spec, state_spec],
        out_shape=[jax.ShapeDtypeStruct((batch, seq, dm), F32),
                   jax.ShapeDtypeStruct((ngroups, SUBLANES, nstate), F32),
                   jax.ShapeDtypeStruct((ngroups, SUBLANES, nstate), F32)],
        scratch_shapes=[pltpu.VMEM((groups, SUBLANES, nstate), F32), pltpu.VMEM((groups, SUBLANES, nstate), F32),
                        pltpu.VMEM((rows, 2 * nstate), F32),
                        pltpu.VMEM((d // LANES, rows, LANES), F32),
                        pltpu.VMEM((dm // LANES, rows, LANES), F32)],
        compiler_params=_cparams(("parallel", "arbitrary")), name="s5_branch")(
            *([x2d] * len(x_specs)), g_mix, w_in, h0r, h0i, ar, ai, bb, cc, d_skip, w_glu, b_glu)


def _cumsum_rows(x, c):
    row = lax.broadcasted_iota(I32, x.shape, 0) & (c - 1)
    s = 1
    while s < c:
        x = x + jnp.where(row >= s, pltpu.roll(x, s, axis=0), 0.0)
        s *= 2
    return x


def _hgrn_gates(q, fr, lb, scale, c):
    rows, n = q.shape
    f = lb + (1.0 - lb) * jax.nn.sigmoid(fr)
    k = 1.0 - f
    b = _cumsum_rows(jnp.log(f), c)
    b3 = b.reshape(rows // c, c, n)
    b_last = jnp.broadcast_to(b3[:, c - 1:c, :], b3.shape).reshape(rows, n)
    q_dec = (q * scale) * jnp.exp(b)
    k_dec = k * jnp.exp(-b)
    k_end = k * jnp.exp(b_last - b)
    return q_dec, k_dec, k_end, jnp.exp(b_last)


def _causal_scores(q_dec, k_dec):
    c = q_dec.shape[0]
    s = lax.dot_general(q_dec, k_dec, (((1,), (1,)), ((), ())), preferred_element_type=F32)
    keep = lax.broadcasted_iota(I32, (c, c), 0) >= lax.broadcasted_iota(I32, (c, c), 1)
    return jnp.where(keep, s, 0.0).astype(BF16)


def _gated_out(o, gn, og):
    o = o * lax.rsqrt(jnp.mean(o * o, axis=-1, keepdims=True) + RMS_EPS) * gn
    return (o * jax.nn.silu(og)).astype(BF16)


def _hgrn_long_body(q_ref, f_ref, v_ref, og_ref, lb_ref, gn_ref, o_ref, sfin_ref, st_scr, *, c, heads, dk, scale):
    j = pl.program_id(1)

    @pl.when(j == 0)
    def _():
        st_scr[...] = jnp.zeros_like(st_scr)

    def chunk(ci, carry):
        rs = pl.ds(pl.multiple_of(ci * c, c), c)
        for h in range(heads):
            hs = slice(h * dk, (h + 1) * dk)
            q_dec, k_dec, k_end, decay = _hgrn_gates(q_ref[rs, hs], f_ref[rs, hs], lb_ref[:, hs], scale, c)
            q_dec = q_dec.astype(BF16)
            v = v_ref[rs, hs].astype(BF16)
            scores = _causal_scores(q_dec, k_dec.astype(BF16))
            st = st_scr[h]
            o = (lax.dot_general(q_dec, st.astype(BF16), (((1,), (1,)), ((), ())), preferred_element_type=F32)
                 + jnp.dot(scores, v, preferred_element_type=F32))
            st_scr[h] = decay[:1] * st + lax.dot_general(
                v, k_end.astype(BF16), (((0,), (0,)), ((), ())), preferred_element_type=F32)
            o_ref[rs, hs] = _gated_out(o, gn_ref[:, hs], og_ref[rs, hs])
        return carry

    lax.fori_loop(0, q_ref.shape[0] // c, chunk, 0, unroll=HGRN_CHUNK_UNROLL)

    @pl.when(j == pl.num_programs(1) - 1)
    def _():
        for h in range(heads):
            sfin_ref[0, h] = st_scr[h].T


def _hgrn_long(proj, lb, gn, batch, seq, heads, dk, row_off):
    width = heads * dk
    tb = min(HGRN_TIME_TILE, seq)
    nj = seq // tb
    off = row_off // tb
    c = min(HG_CHUNK, seq)
    body = functools.partial(_hgrn_long_body, c=c, heads=heads, dk=dk, scale=dk ** -0.5)

    def col(k):
        return pl.BlockSpec((tb, width), lambda b, j, k=k: (off + b * nj + j, k))

    return pl.pallas_call(
        body, grid=(batch, nj),
        in_specs=[col(0), col(1), col(2), col(3), _resident(lb.shape), _resident(gn.shape)],
        out_specs=[pl.BlockSpec((tb, width), lambda b, j: (b * nj + j, 0)),
                   pl.BlockSpec((1, heads, dk, dk), lambda b, j: (b, 0, 0, 0))],
        out_shape=[jax.ShapeDtypeStruct((batch * seq, width), BF16),
                   jax.ShapeDtypeStruct((batch, heads, dk, dk), F32)],
        scratch_shapes=[pltpu.VMEM((heads, dk, dk), F32)],
        compiler_params=_cparams(("parallel", "arbitrary")), name="hgrn_long")(proj, proj, proj, proj, lb, gn)


def _hgrn_short_body(q_ref, f_ref, v_ref, og_ref, lb_ref, gn_ref, s0_ref, o_ref, snew_ref, *, c, heads, dk, scale):
    def one_seq(sq, carry):
        rs = pl.ds(pl.multiple_of(sq * c, c), c)
        for h in range(heads):
            hs = slice(h * dk, (h + 1) * dk)
            q_dec, k_dec, k_end, decay = _hgrn_gates(q_ref[rs, hs], f_ref[rs, hs], lb_ref[:, hs], scale, c)
            q_dec = q_dec.astype(BF16)
            v = v_ref[rs, hs].astype(BF16)
            scores = _causal_scores(q_dec, k_dec.astype(BF16))
            s0 = s0_ref[sq, h]
            o = (jnp.dot(q_dec, s0.astype(BF16), preferred_element_type=F32)
                 + jnp.dot(scores, v, preferred_element_type=F32))
            decay_col = jnp.broadcast_to(decay[:1], (dk, dk)).T
            snew_ref[sq, h] = decay_col * s0 + lax.dot_general(
                k_end.astype(BF16), v, (((0,), (0,)), ((), ())), preferred_element_type=F32)
            o_ref[rs, hs] = _gated_out(o, gn_ref[:, hs], og_ref[rs, hs])
        return carry

    lax.fori_loop(0, s0_ref.shape[0], one_seq, 0, unroll=HGRN_SEQ_UNROLL)


def _hgrn_short(proj, lb, gn, s0, seq, row_off):
    batch, heads, dk, _ = s0.shape
    width = heads * dk
    nb = HGRN_SEQ_TILE
    rows = nb * seq
    off = row_off // rows
    body = functools.partial(_hgrn_short_body, c=seq, heads=heads, dk=dk, scale=dk ** -0.5)

    def col(k):
        return pl.BlockSpec((rows, width), lambda i, k=k: (off + i, k))

    state_spec = pl.BlockSpec((nb, heads, dk, dk), lambda i: (i, 0, 0, 0))
    return pl.pallas_call(
        body, grid=(batch // nb,),
        in_specs=[col(0), col(1), col(2), col(3), _resident(lb.shape), _resident(gn.shape), state_spec],
        out_specs=[pl.BlockSpec((rows, width), lambda i: (i, 0)), state_spec],
        out_shape=[jax.ShapeDtypeStruct((batch * seq, width), BF16), jax.ShapeDtypeStruct(s0.shape, F32)],
        compiler_params=_cparams(("parallel",)), name="hgrn_short")(proj, proj, proj, proj, lb, gn, s0)


def _first_index_of_max(vals):
    m = vals[0]
    for v in vals[1:]:
        m = jnp.maximum(m, v)
    idx = jnp.full(m.shape, len(vals), I32)
    for e in range(len(vals) - 1, -1, -1):
        idx = jnp.where(vals[e] == m, e, idx)
    return m, idx


def _route_tile(lg, ids_ref, w_ref, rk_ref, cnt_ref, carry_scr, groups, experts):
    i = pl.program_id(0)
    tile = lg.shape[1]
    n_exp = groups * experts

    @pl.when(i == 0)
    def _():
        carry_scr[...] = jnp.zeros_like(carry_scr)

    gl = [lg[g:g + 1, :] for g in range(groups)]
    gmax, gidx = _first_index_of_max(gl)
    denom = jnp.exp(gl[0] - gmax)
    for g in range(1, groups):
        denom = denom + jnp.exp(gl[g] - gmax)
    g_w = 1.0 / denom

    el = []
    for e in range(experts):
        v = lg[groups + e:groups + e + 1, :]
        for g in range(1, groups):
            r = groups + g * experts + e
            v = jnp.where(gidx == g, lg[r:r + 1, :], v)
        el.append(v)
    v1, i1 = _first_index_of_max(el)
    rest = [jnp.where(i1 == e, -jnp.inf, el[e]) for e in range(experts)]
    v2, i2 = _first_index_of_max(rest)
    t = jnp.exp(v2 - v1)
    inv = 1.0 / (1.0 + t)
    e1 = gidx * experts + i1
    e2 = gidx * experts + i2

    erow = lax.broadcasted_iota(I32, (n_exp, tile), 0)
    oh1 = (erow == e1).astype(F32)
    oh2 = (erow == e2).astype(F32)
    oh = oh1 + oh2
    before = (lax.broadcasted_iota(I32, (tile, tile), 0) < lax.broadcasted_iota(I32, (tile, tile), 1))
    cnt = jnp.dot(oh.astype(BF16), before.astype(BF16), preferred_element_type=F32) + carry_scr[:, 0:1]
    ids_ref[0:1, :] = e1
    ids_ref[1:2, :] = e2
    w_ref[0:1, :] = inv * g_w
    w_ref[1:2, :] = (t * inv) * g_w
    rk_ref[0:1, :] = jnp.sum(oh1 * cnt, axis=0, keepdims=True).astype(I32)
    rk_ref[1:2, :] = jnp.sum(oh2 * cnt, axis=0, keepdims=True).astype(I32)
    carry_scr[...] = carry_scr[...] + jnp.sum(oh, axis=1, keepdims=True)

    @pl.when(i == pl.num_programs(0) - 1)
    def _():
        cnt_ref[...] = carry_scr[...]


def _merge_body(*refs, n_first, groups, experts, n_gate_blocks):
    op_ref, os_ref, ybp_ref, ybs_ref, xp_ref, xs_ref, gm_ref = refs[:7]
    gate_w_refs = refs[7:7 + n_gate_blocks]
    (wa_ref, wo_ref, gf_ref, wr_ref, br_ref, h_ref, xn_ref, ids_ref, w_ref, rk_ref, cnt_ref, carry_scr,
     wgt_ref) = refs[7 + n_gate_blocks:]
    d = h_ref.shape[1]

    @pl.when(pl.program_id(0) == 0)
    def _():
        gw = gate_w_refs[0].shape[1]
        for q, ref in enumerate(gate_w_refs):
            wgt_ref[:, q * gw:(q + 1) * gw] = ref[...].astype(BF16)

    x = _pick(xp_ref, xs_ref, n_first)
    gates = jnp.dot(_rmsnorm(x, gm_ref[...]).astype(BF16), wgt_ref[...], preferred_element_type=F32)
    y_a = jnp.dot(_pick(op_ref, os_ref, n_first), wa_ref[...], preferred_element_type=F32)
    merged = jax.nn.sigmoid(gates[:, :d]) * y_a + jax.nn.sigmoid(gates[:, d:]) * _pick(ybp_ref, ybs_ref, n_first)
    h = x + jnp.dot(merged.astype(BF16), wo_ref[...], preferred_element_type=F32)
    h_ref[...] = h
    xn = _rmsnorm(h, gf_ref[...])
    _store_token_tiles(xn_ref, xn)
    logits_t = lax.dot_general(wr_ref[...], xn.astype(BF16), (((1,), (1,)), ((), ())),
                               preferred_element_type=F32) + br_ref[...]
    _route_tile(logits_t, ids_ref, w_ref, rk_ref, cnt_ref, carry_scr, groups, experts)


def _merge(o_p, o_s, yb_p, yb_s, xp, xs, g_mix, w_in, gate_col, wa, wo, gf, wr, br, groups, experts):
    d = xp.shape[1]
    total = xp.shape[0] + xs.shape[0]
    tm = TOKEN_TILE
    n_first = xp.shape[0] // tm
    n_exp = groups * experts
    pair = _two_source_specs(tm, d, n_first)
    top = pl.BlockSpec((MOE_TOP_K, tm), lambda i: (0, i))
    gw = math.gcd(gate_col, 2 * d)
    n_gate_blocks = 2 * d // gw
    gate_specs = [pl.BlockSpec((d, gw), lambda i, q=q: (0, gate_col // gw + q), pipeline_mode=pl.Buffered(1))
                  for q in range(n_gate_blocks)]
    weights = [wa, wo, gf, wr, br]
    return pl.pallas_call(
        functools.partial(_merge_body, n_first=n_first, groups=groups, experts=experts,
                          n_gate_blocks=n_gate_blocks),
        grid=(total // tm,),
        in_specs=pair + pair + pair + [_resident(g_mix.shape)] + gate_specs + [_resident(w.shape) for w in weights],
        out_specs=[pl.BlockSpec((tm, d), lambda i: (i, 0)), pl.BlockSpec((tm * SUBLANES, LANES), lambda i: (i, 0)),
                   top, top, top, pl.BlockSpec((n_exp, LANES), lambda i: (0, 0))],
        out_shape=[jax.ShapeDtypeStruct((total, d), F32), jax.ShapeDtypeStruct((total * SUBLANES, LANES), F32),
                   jax.ShapeDtypeStruct((MOE_TOP_K, total), I32), jax.ShapeDtypeStruct((MOE_TOP_K, total), F32),
                   jax.ShapeDtypeStruct((MOE_TOP_K, total), I32), jax.ShapeDtypeStruct((n_exp, LANES), F32)],
        scratch_shapes=[pltpu.VMEM((n_exp, LANES), F32), pltpu.VMEM((d, 2 * d), BF16)],
        compiler_params=_cparams(("arbitrary",)), name="merge_route")(
            o_p, o_s, yb_p, yb_s, xp, xs, g_mix, *([w_in] * n_gate_blocks), *weights)


def _row_copy(src, dst, sem):
    return pltpu.make_async_copy(src, dst, sem)


def _token_rows(r):
    return pl.ds(pl.multiple_of(r * SUBLANES, SUBLANES), SUBLANES)


def _dispatch_body(*refs, n_sorted):
    pos_refs = refs[:MOE_TOP_K]
    x_ref, o_hbm, ring, zero_scr, sem, pad_sem = refs[MOE_TOP_K:]
    i = pl.program_id(0)
    tile = x_ref.shape[0] // SUBLANES
    par = lax.rem(i, 2)

    @pl.when(i == 0)
    def _():
        zero_scr[...] = jnp.zeros_like(zero_scr)
        pad = _row_copy(zero_scr, o_hbm.at[pl.ds(n_sorted * SUBLANES, zero_scr.shape[0])], pad_sem.at[0])
        pad.start()
        pad.wait()

    ring[par] = x_ref[...]

    def issue(r, carry):
        for k in range(MOE_TOP_K):
            p = pos_refs[k][r]
            _row_copy(ring.at[par, _token_rows(r)], o_hbm.at[_token_rows(p)], sem.at[par, k]).start(priority=k)
        return carry

    lax.fori_loop(0, tile, issue, 0, unroll=ROW_COPY_UNROLL)

    def drain(slot):
        for k in range(MOE_TOP_K):
            _row_copy(ring.at[slot], o_hbm.at[pl.ds(0, tile * SUBLANES)], sem.at[slot, k]).wait()

    @pl.when(i > 0)
    def _():
        drain(1 - par)

    @pl.when(i == pl.num_programs(0) - 1)
    def _():
        drain(par)


def _dispatch(pos_slots, xn_tiles):
    total = xn_tiles.shape[0] // SUBLANES
    tile = DISPATCH_TILE
    n_sorted = total * MOE_TOP_K
    pad = EXPERT_WINDOWS[-1]
    return pl.pallas_call(
        functools.partial(_dispatch_body, n_sorted=n_sorted), grid=(total // tile,),
        in_specs=[pl.BlockSpec((tile,), lambda i: (i,), memory_space=pltpu.SMEM)] * MOE_TOP_K + [
            pl.BlockSpec((tile * SUBLANES, LANES), lambda i: (i, 0))],
        out_specs=pl.BlockSpec(memory_space=pl.ANY),
        out_shape=jax.ShapeDtypeStruct(((n_sorted + pad) * SUBLANES, LANES), xn_tiles.dtype),
        scratch_shapes=[pltpu.VMEM((2, tile * SUBLANES, LANES), xn_tiles.dtype),
                        pltpu.VMEM((pad * SUBLANES, LANES), xn_tiles.dtype),
                        pltpu.SemaphoreType.DMA((2, MOE_TOP_K)), pltpu.SemaphoreType.DMA((1,))],
        compiler_params=_cparams(("arbitrary",)), name="dispatch")(*pos_slots, xn_tiles)


def _experts_body(it_exp, it_row, it_cls, it_first, it_next, n_items, xs_hbm, wg_hbm, wu_hbm, wd_hbm, ys_hbm,
                  xbuf, ybuf, wg_s, wu_s, wd_s, wg_b, wu_b, wd_b, sem_in, sem_out, sem_w):
    j = pl.program_id(0)
    n = n_items[0]
    pad = EXPERT_WINDOWS[-1]
    slot = lax.rem(j, 2)

    def weight_copies(e, s):
        return [pltpu.make_async_copy(hbm.at[e], stage.at[s], sem_w.at[s, t])
                for t, (hbm, stage) in enumerate(((wg_hbm, wg_s), (wu_hbm, wu_s), (wd_hbm, wd_s)))]

    def by_size(item, fn):
        for ci, m in enumerate(EXPERT_WINDOWS):
            pl.when(it_cls[item] == ci)(functools.partial(fn, m))

    def window(item, m):
        return pl.ds(pl.multiple_of(it_row[item] * SUBLANES, SUBLANES), m * SUBLANES)

    def in_copy(item, s, m):
        return pltpu.make_async_copy(xs_hbm.at[window(item, m)], xbuf.at[s, pl.ds(0, m * SUBLANES)], sem_in.at[s])

    def out_copy(item, s, m):
        return pltpu.make_async_copy(ybuf.at[s, pl.ds(0, m * SUBLANES)], ys_hbm.at[window(item, m)], sem_out.at[s])

    def compute(m):
        x = _load_token_tiles(xbuf, m, (slot,)).astype(BF16)
        hg = jnp.dot(x, wg_b[...], preferred_element_type=F32)
        hu = jnp.dot(x, wu_b[...], preferred_element_type=F32)
        hid = (jax.nn.silu(hg) * hu).astype(BF16)
        _store_token_tiles(ybuf, jnp.dot(hid, wd_b[...], preferred_element_type=F32), (slot,))

    @pl.when(j < n)
    def _():
        @pl.when(j == 0)
        def _():
            by_size(0, lambda m: in_copy(0, 0, m).start())
            for c in weight_copies(it_exp[0], it_first[0] - 1):
                c.start()
            tail_rows = pl.ds(0, pad * SUBLANES)
            ybuf[1, tail_rows, :] = jnp.zeros((pad * SUBLANES, LANES), F32)
            tail = pltpu.make_async_copy(
                ybuf.at[1, tail_rows], ys_hbm.at[pl.ds(ys_hbm.shape[0] - pad * SUBLANES, pad * SUBLANES)],
                sem_out.at[1])
            tail.start()
            tail.wait()

        @pl.when(j + 1 < n)
        def _():
            by_size(j + 1, lambda m: in_copy(j + 1, 1 - slot, m).start())

        @pl.when(it_first[j] > 0)
        def _():
            s = it_first[j] - 1
            for c in weight_copies(it_exp[j], s):
                c.wait()
            wg_b[...] = wg_s[s].astype(BF16)
            wu_b[...] = wu_s[s].astype(BF16)
            wd_b[...] = wd_s[s].astype(BF16)

            @pl.when(it_next[j] >= 0)
            def _():
                for c in weight_copies(it_next[j], 1 - s):
                    c.start()

        by_size(j, lambda m: in_copy(j, slot, m).wait())
        by_size(j, compute)

        @pl.when(j > 0)
        def _():
            by_size(j - 1, lambda m: out_copy(j - 1, 1 - slot, m).wait())

        by_size(j, lambda m: out_copy(j, slot, m).start())

        @pl.when(j == n - 1)
        def _():
            by_size(j, lambda m: out_copy(j, slot, m).wait())


def _experts(items, xs, wg, wu, wd):
    d, de = wg.shape[1], wg.shape[2]
    tm = EXPERT_TILE
    max_items = items[0].shape[0]
    grid_spec = pltpu.PrefetchScalarGridSpec(
        num_scalar_prefetch=6, grid=(max_items,),
        in_specs=[pl.BlockSpec(memory_space=pl.ANY)] * 4,
        out_specs=pl.BlockSpec(memory_space=pl.ANY),
        scratch_shapes=[pltpu.VMEM((2, tm * SUBLANES, LANES), F32), pltpu.VMEM((2, tm * SUBLANES, LANES), F32),
                        pltpu.VMEM((2, d, de), F32), pltpu.VMEM((2, d, de), F32), pltpu.VMEM((2, de, d), F32),
                        pltpu.VMEM((d, de), BF16), pltpu.VMEM((d, de), BF16), pltpu.VMEM((de, d), BF16),
                        pltpu.SemaphoreType.DMA((2,)), pltpu.SemaphoreType.DMA((2,)),
                        pltpu.SemaphoreType.DMA((2, 3))])
    return pl.pallas_call(
        _experts_body, grid_spec=grid_spec, out_shape=jax.ShapeDtypeStruct(xs.shape, F32),
        compiler_params=_cparams(("arbitrary",)), name="experts")(*items, xs, wg, wu, wd)


def _combine_body(*refs, final_norm):
    pos_refs, pos_next_refs = refs[:MOE_TOP_K], refs[MOE_TOP_K:2 * MOE_TOP_K]
    h_ref, w_ref, g_ref, ys_hbm, y_ref, buf, sem = refs[2 * MOE_TOP_K:]
    i = pl.program_id(0)
    tile = h_ref.shape[0]
    par = lax.rem(i, 2)

    def gather(tables, slot):
        def issue(r, carry):
            for k in range(MOE_TOP_K):
                p = tables[k][r]
                _row_copy(ys_hbm.at[_token_rows(p)], buf.at[slot, k, _token_rows(r)],
                          sem.at[slot, k]).start(priority=k)
            return carry

        lax.fori_loop(0, tile, issue, 0, unroll=ROW_COPY_UNROLL)

    @pl.when(i == 0)
    def _():
        gather(pos_refs, 0)

    for k in range(MOE_TOP_K):
        _row_copy(ys_hbm.at[pl.ds(0, tile * SUBLANES)], buf.at[par, k], sem.at[par, k]).wait()

    def combine_tile():
        h = h_ref[...] + (w_ref[:, 0:1] * _load_token_tiles(buf, tile, (par, 0))
                          + w_ref[:, 1:2] * _load_token_tiles(buf, tile, (par, 1)))
        y_ref[...] = _rmsnorm(h, g_ref[...]) if final_norm else h

    for slot in range(2):
        @pl.when(jnp.logical_and(i + 1 < pl.num_programs(0), par != slot))
        def _(slot=slot):
            for r in range(tile):
                for k in range(MOE_TOP_K):
                    p = pos_next_refs[k][r]
                    _row_copy(ys_hbm.at[_token_rows(p)], buf.at[slot, k, pl.ds(r * SUBLANES, SUBLANES)],
                              sem.at[slot, k]).start(priority=k)
            combine_tile()

    @pl.when(i + 1 == pl.num_programs(0))
    def _():
        combine_tile()


def _combine(pos_slots, h_all, w_t, g, ys, rows, row_off, final_norm):
    d = h_all.shape[1]
    tile = COMBINE_TILE
    off = row_off // tile
    last_block = h_all.shape[0] // tile - 1
    this_tile = pl.BlockSpec((tile,), lambda i: (off + i,), memory_space=pltpu.SMEM)
    next_tile = pl.BlockSpec((tile,), lambda i: (jnp.minimum(off + i + 1, last_block),), memory_space=pltpu.SMEM)
    return pl.pallas_call(
        functools.partial(_combine_body, final_norm=final_norm), grid=(rows // tile,),
        in_specs=[this_tile] * MOE_TOP_K + [next_tile] * MOE_TOP_K + [
                  pl.BlockSpec((tile, d), lambda i: (off + i, 0)),
                  pl.BlockSpec((tile, MOE_TOP_K), lambda i: (off + i, 0)),
                  _resident(g.shape),
                  pl.BlockSpec(memory_space=pl.ANY)],
        out_specs=pl.BlockSpec((tile, d), lambda i: (i, 0)),
        out_shape=jax.ShapeDtypeStruct((rows, d), F32),
        scratch_shapes=[pltpu.VMEM((2, MOE_TOP_K, tile * SUBLANES, LANES), F32),
                        pltpu.SemaphoreType.DMA((2, MOE_TOP_K))],
        compiler_params=_cparams(("arbitrary",)), name="combine")(*pos_slots, *pos_slots, h_all, w_t, g, ys)


def _lookup(table, idx):
    sel = idx[None] == jnp.arange(table.shape[0], dtype=I32).reshape((-1,) + (1,) * idx.ndim)
    return jnp.sum(jnp.where(sel, table.reshape(sel.shape[:1] + (1,) * idx.ndim), 0), axis=0)


def _work_items(counts, n_sorted):
    big, mid, small = EXPERT_WINDOWS
    n_exp = counts.shape[0]
    max_items = n_sorted // big + 2 * n_exp
    ends = jnp.cumsum(counts)
    starts = ends - counts
    units = (counts % big + small - 1) // small
    n_big = counts // big + (units == big // small)
    units = jnp.where(units == big // small, 0, units)
    n_mid = units // (mid // small)
    n_e = n_big + n_mid + units % (mid // small)
    item_end = jnp.cumsum(n_e)
    item_start = item_end - n_e
    n_items = item_end[-1]
    j = jnp.minimum(jnp.arange(max_items, dtype=I32), n_items - 1)
    e = jnp.sum((item_end[None, :] <= j[:, None]).astype(I32), axis=1)
    k = j - _lookup(item_start, e)
    nb, nm = _lookup(n_big, e), _lookup(n_mid, e)
    cls = jnp.where(k < nb, 0, jnp.where(k < nb + nm, 1, 2))
    row = _lookup(starts, e) + jnp.where(cls == 0, k * big, nb * big + jnp.where(cls == 1, 0, nm * mid))
    ordinal = jnp.cumsum((n_e > 0).astype(I32)) - 1
    first = jnp.where(k == 0, 1 + _lookup(ordinal, e) % 2, 0)
    nxt_item = _lookup(item_end, e)
    nxt = jnp.where(nxt_item < n_items, jnp.sum((item_end[None, :] <= nxt_item[:, None]).astype(I32), axis=1), -1)
    return (e, row.astype(I32), cls.astype(I32), first.astype(I32), nxt.astype(I32),
            n_items.reshape(1).astype(I32)), starts


def _s5_discretise(lam_re, lam_im, log_dt, b_re, b_im, c_re, c_im):
    g, p = lam_re.shape
    ch = b_re.shape[-1]
    lam_re = lam_re.astype(F32)
    lam_im = lam_im.astype(F32)
    dt = jnp.exp(log_dt.astype(F32))[:, None]
    mag = jnp.exp(lam_re * dt)
    ab_re = mag * jnp.cos(lam_im * dt)
    ab_im = mag * jnp.sin(lam_im * dt)
    den = lam_re * lam_re + lam_im * lam_im
    nr = ab_re - 1.0
    coef_re = (nr * lam_re + ab_im * lam_im) / den
    coef_im = (ab_im * lam_re - nr * lam_im) / den
    bb_re = coef_re[..., None] * b_re - coef_im[..., None] * b_im
    bb_im = coef_re[..., None] * b_im + coef_im[..., None] * b_re
    gh = g // 2
    eye = jnp.eye(gh, dtype=F32)

    def in_block(m):
        return jnp.einsum("gpc,gh->gchp", m, eye).reshape(gh * ch, gh * p)

    def out_block(m):
        return jnp.einsum("gcp,gh->gphc", m, eye).reshape(gh * p, gh * ch)

    bb = jnp.stack([jnp.concatenate([in_block(bb_re[k * gh:(k + 1) * gh]), in_block(bb_im[k * gh:(k + 1) * gh])],
                                    axis=1) for k in range(2)]).astype(BF16)
    cc = jnp.stack([jnp.concatenate([out_block(c_re[k * gh:(k + 1) * gh]), out_block(-c_im[k * gh:(k + 1) * gh])],
                                    axis=0) for k in range(2)]).astype(BF16)
    return ab_re.reshape(1, g * p), ab_im.reshape(1, g * p), bb, cc


def kernel(x_prompt, x_sample, state_hgrn, state_s5_re, state_s5_im, norm_mix_g, w_in, hgrn_lb_raw, hgrn_onorm_g, w_branch_a, s5_lambda_re, s5_lambda_im, s5_log_dt, s5_b_re, s5_b_im, s5_c_re, s5_c_im, s5_d, w_glu, b_glu, w_out, norm_ffn_g, w_router_group, b_router_group, w_router_expert, b_router_expert, w_exp_gate, w_exp_up, w_exp_down, norm_final_g):
    depth = norm_mix_g.shape[0]
    bp, lp, d = x_prompt.shape
    bs, ls, _ = x_sample.shape
    heads, dk = state_hgrn.shape[2], state_hgrn.shape[3]
    kw = heads * dk
    s5_groups, s5_state = state_s5_re.shape[2], state_s5_re.shape[3]
    s5_width = s5_d.shape[-1]
    nstate = s5_groups * s5_state
    moe_groups, _, experts = w_router_expert.shape[1:]
    n_exp = moe_groups * experts
    rows_p, rows_s = bp * lp, bs * ls
    total = rows_p + rows_s
    n_sorted = total * MOE_TOP_K
    assert kw == d and state_hgrn.shape[4] == dk, "column blocks assume key width == value width == model width"
    assert d == SUBLANES * LANES, "token-tile layout holds one token per (8, 128) tile"
    assert s5_groups % 2 == 0 and bp % SUBLANES == 0 and bs % HGRN_SEQ_TILE == 0
    assert (4 * kw) % s5_width == 0, "the S5 input columns must start on a multiple of their width"

    lb_all = jnp.cumsum(jax.nn.softmax(hgrn_lb_raw.astype(F32), axis=0), axis=0)

    hp = x_prompt.reshape(rows_p, d)
    hs = x_sample.reshape(rows_s, d)
    hg_p, re_p, im_p, hg_s, re_s, im_s = [], [], [], [], [], []
    zeros_state = jnp.zeros((bp // SUBLANES, SUBLANES, nstate), F32)

    for l in range(depth):
        w = w_in[l]
        g_mix = norm_mix_g[l].reshape(1, d)
        u_col = 4 * kw
        gate_col = u_col + s5_width
        proj = _in_proj(hp, hs, g_mix, w, u_col)

        ar, ai, bb, cc = _s5_discretise(s5_lambda_re[l], s5_lambda_im[l], s5_log_dt[l], s5_b_re[l], s5_b_im[l],
                                        s5_c_re[l], s5_c_im[l])
        s5_args = (ar, ai, bb, cc, s5_d[l].reshape(1, s5_width), w_glu[l].astype(BF16), b_glu[l].reshape(1, -1))
        yb_p, fr_p, fi_p = _s5_branch(hp, bp, lp, g_mix, w, u_col, zeros_state, zeros_state, *s5_args)
        yb_s, fr_s, fi_s = _s5_branch(hs, bs, ls, g_mix, w, u_col,
                                      state_s5_re[l].reshape(bs // SUBLANES, SUBLANES, nstate),
                                      state_s5_im[l].reshape(bs // SUBLANES, SUBLANES, nstate), *s5_args)

        lb = lb_all[l].reshape(1, kw)
        gn = hgrn_onorm_g[l].reshape(1, kw)
        o_p, hgp = _hgrn_long(proj, lb, gn, bp, lp, heads, dk, 0)
        o_s, hgs = _hgrn_short(proj, lb, gn, state_hgrn[l].astype(F32), ls, rows_p)

        nr = -(-(moe_groups + n_exp) // SUBLANES) * SUBLANES
        wr = jnp.concatenate([w_router_group[l].T, w_router_expert[l].transpose(0, 2, 1).reshape(n_exp, d)], axis=0)
        wr = jnp.pad(wr, ((0, nr - wr.shape[0]), (0, 0))).astype(BF16)
        br = jnp.pad(jnp.concatenate([b_router_group[l], b_router_expert[l].reshape(n_exp)]),
                     (0, nr - moe_groups - n_exp)).reshape(nr, 1).astype(F32)
        h_all, xn_all, ids, wts, ranks, cnt = _merge(
            o_p, o_s, yb_p.reshape(rows_p, d), yb_s.reshape(rows_s, d), hp, hs, g_mix, w, gate_col,
            w_branch_a[l].astype(BF16), w_out[l].astype(BF16), norm_ffn_g[l].reshape(1, d), wr, br,
            moe_groups, experts)
        items, starts = _work_items(cnt[:, 0].astype(I32), n_sorted)
        pos_slots = [_lookup(starts, ids[k]) + ranks[k] for k in range(MOE_TOP_K)]
        xs = _dispatch(pos_slots, xn_all)
        ys = _experts(items, xs, w_exp_gate[l], w_exp_up[l], w_exp_down[l])

        last = l == depth - 1
        g_out = norm_final_g.reshape(1, d)
        hp = _combine(pos_slots, h_all, wts.T, g_out, ys, rows_p, 0, last)
        hs = _combine(pos_slots, h_all, wts.T, g_out, ys, rows_s, rows_p, last)

        hg_p.append(hgp)
        hg_s.append(hgs)
        re_p.append(fr_p.reshape(bp, s5_groups, s5_state))
        im_p.append(fi_p.reshape(bp, s5_groups, s5_state))
        re_s.append(fr_s.reshape(bs, s5_groups, s5_state))
        im_s.append(fi_s.reshape(bs, s5_groups, s5_state))

    y_prompt = hp.reshape(bp, lp, d).astype(x_prompt.dtype)
    y_sample = hs.reshape(bs, ls, d).astype(x_sample.dtype)
    return (y_prompt, y_sample, jnp.stack(hg_p), jnp.stack(re_p), jnp.stack(im_p),
            jnp.stack(hg_s), jnp.stack(re_s), jnp.stack(im_s))
```

```python
import functools
import math

import jax
import jax.numpy as jnp
from jax import lax
from jax.experimental import pallas as pl
from jax.experimental.pallas import tpu as pltpu

F32 = jnp.float32
BF16 = jnp.bfloat16
I32 = jnp.int32

RMS_EPS = 1e-6
HG_CHUNK = 64
MOE_TOP_K = 2

V7X_VMEM_BYTES = 64 * 1024 * 1024
VMEM_LIMIT_BYTES = V7X_VMEM_BYTES - 8 * 1024 * 1024
SUBLANES = 8
LANES = 128

TOKEN_TILE = 512
DISPATCH_TILE = 1024
COMBINE_TILE = 512
EXPERT_TILE = 512
EXPERT_WINDOWS = (EXPERT_TILE, EXPERT_TILE // 2, EXPERT_TILE // 4)
S5_TIME_TILE = 128
S5_ROW_BLOCK = 1024
HGRN_TIME_TILE = 1024
HGRN_SEQ_TILE = 16
HGRN_CHUNK_UNROLL = 16
HGRN_SEQ_UNROLL = 8
PROJ_COL_TILE = 512
S5_SCAN_LANES = 512
ROW_COPY_UNROLL = 8


def _cparams(sem):
    return pltpu.CompilerParams(dimension_semantics=sem, vmem_limit_bytes=VMEM_LIMIT_BYTES)


def _resident(shape):
    nd = len(shape)
    return pl.BlockSpec(shape, lambda *_: (0,) * nd, pipeline_mode=pl.Buffered(1))


def _rmsnorm(x, g):
    return x * lax.rsqrt(jnp.mean(x * x, axis=-1, keepdims=True) + RMS_EPS) * g


def _two_source_specs(tm, width, n_first):
    return [pl.BlockSpec((tm, width), lambda i: (jnp.minimum(i, n_first - 1), 0)),
            pl.BlockSpec((tm, width), lambda i: (jnp.maximum(i - n_first, 0), 0))]


def _pick(first_ref, second_ref, n_first):
    return jnp.where(pl.program_id(0) < n_first, first_ref[...], second_ref[...])


def _store_token_tiles(ref, x, lead=()):
    rows = x.shape[0]
    for c in range(SUBLANES):
        ref[lead + (pl.ds(c, rows, stride=SUBLANES), slice(None))] = x[:, c * LANES:(c + 1) * LANES]


def _load_token_tiles(ref, rows, lead=()):
    return jnp.concatenate([ref[lead + (pl.ds(c, rows, stride=SUBLANES), slice(None))] for c in range(SUBLANES)],
                           axis=-1)


def _inproj_body(xp_ref, xs_ref, g_ref, w_ref, o_ref, wb_scr, *, n_first):
    @pl.when(pl.program_id(0) == 0)
    def _():
        wb_scr[...] = w_ref[...].astype(BF16)

    xb = _rmsnorm(_pick(xp_ref, xs_ref, n_first), g_ref[...]).astype(BF16)
    for j in range(0, wb_scr.shape[1], PROJ_COL_TILE):
        o_ref[:, j:j + PROJ_COL_TILE] = jnp.dot(xb, wb_scr[:, j:j + PROJ_COL_TILE], preferred_element_type=F32)


def _in_proj(xp, xs, g, w, n):
    d = xp.shape[1]
    tm = TOKEN_TILE
    total = xp.shape[0] + xs.shape[0]
    n_first = xp.shape[0] // tm
    return pl.pallas_call(
        functools.partial(_inproj_body, n_first=n_first), grid=(total // tm,),
        in_specs=_two_source_specs(tm, d, n_first) + [
            _resident((1, d)), pl.BlockSpec((d, n), lambda i: (0, 0), pipeline_mode=pl.Buffered(1))],
        out_specs=pl.BlockSpec((tm, n), lambda i: (i, 0)),
        out_shape=jax.ShapeDtypeStruct((total, n), F32),
        scratch_shapes=[pltpu.VMEM((d, n), BF16)],
        compiler_params=_cparams(("arbitrary",)), name="in_proj")(xp, xs, g, w)


def _s5_body(*refs, tt, groups, nstate, column_inputs):
    n_x = len(refs) - 19
    x_refs = refs[:n_x]
    (gm_ref, wu_ref, h0r_ref, h0i_ref, ar_ref, ai_ref, bb_ref, cc_ref, d_ref, wg_ref, bg_ref,
     y_ref, hr_out, hi_out, hr_scr, hi_scr, bu_scr, x_scr, y_scr) = refs[n_x:]
    j = pl.program_id(1)
    half = nstate // 2
    d = x_scr.shape[0] * LANES
    kw = wu_ref.shape[-1] // 2

    @pl.when(j == 0)
    def _():
        hr_scr[...] = h0r_ref[...]
        hi_scr[...] = h0i_ref[...]

    if column_inputs:
        for s in range(d // LANES):
            for g in range(groups):
                for t in range(tt):
                    r0 = (g * tt + t) * SUBLANES
                    x_scr[s, r0:r0 + SUBLANES, :] = x_refs[s][pl.ds(g * SUBLANES * tt + t, SUBLANES, stride=tt), :]
    else:
        for b in range(SUBLANES):
            xb = x_refs[b][...]
            for s in range(d // LANES):
                x_scr[s, pl.ds(b, tt, stride=SUBLANES), :] = xb[:, s * LANES:(s + 1) * LANES]
    x = jnp.concatenate([x_scr[s] for s in range(d // LANES)], axis=-1)
    u = jnp.dot(_rmsnorm(x, gm_ref[...]).astype(BF16), wu_ref[...].astype(BF16), preferred_element_type=F32)
    ub16 = u.astype(BF16)
    for kt in range(2):
        ukt = ub16[:, kt * kw:(kt + 1) * kw]
        bu_scr[:, kt * half:(kt + 1) * half] = jnp.dot(ukt, bb_ref[kt, :, :half], preferred_element_type=F32)
        bu_scr[:, nstate + kt * half:nstate + (kt + 1) * half] = jnp.dot(
            ukt, bb_ref[kt, :, half:], preferred_element_type=F32)

    lane_chunk = S5_SCAN_LANES
    for lc in range(nstate // lane_chunk):
        lo = lc * lane_chunk
        re_sl = slice(lo, lo + lane_chunk)
        im_sl = slice(nstate + lo, nstate + lo + lane_chunk)
        ar = jnp.broadcast_to(ar_ref[:, re_sl], (SUBLANES, lane_chunk))
        ai = jnp.broadcast_to(ai_ref[:, re_sl], (SUBLANES, lane_chunk))

        for g in range(groups):
            hr, hi = hr_scr[g, :, re_sl], hi_scr[g, :, re_sl]
            for t in range(tt):
                r0 = (g * tt + t) * SUBLANES
                rs = slice(r0, r0 + SUBLANES)
                hr, hi = (ar * hr - ai * hi + bu_scr[rs, re_sl], ar * hi + ai * hr + bu_scr[rs, im_sl])
                bu_scr[rs, re_sl] = hr
                bu_scr[rs, im_sl] = hi
            hr_scr[g, :, re_sl] = hr
            hi_scr[g, :, re_sl] = hi

    dm = wg_ref.shape[-1] // 2
    ys = []
    for n in range(2):
        h_re = bu_scr[:, n * half:(n + 1) * half].astype(BF16)
        h_im = bu_scr[:, nstate + n * half:nstate + (n + 1) * half].astype(BF16)
        ys.append(jnp.dot(h_re, cc_ref[n, :half, :], preferred_element_type=F32)
                  + jnp.dot(h_im, cc_ref[n, half:, :], preferred_element_type=F32))
    y = jnp.concatenate(ys, axis=-1) + d_ref[...] * u
    z = jnp.dot(jax.nn.gelu(y).astype(BF16), wg_ref[...], preferred_element_type=F32) + bg_ref[...]
    yb = z[:, :dm] * jax.nn.sigmoid(z[:, dm:])
    for s in range(dm // LANES):
        y_scr[s] = yb[:, s * LANES:(s + 1) * LANES]
    for g in range(groups):
        for b in range(SUBLANES):
            for s in range(dm // LANES):
                y_ref[g * SUBLANES + b, :, s * LANES:(s + 1) * LANES] = y_scr[
                    s, pl.ds(g * tt * SUBLANES + b, tt, stride=SUBLANES), :]

    @pl.when(j == pl.num_programs(1) - 1)
    def _():
        hr_out[...] = hr_scr[...]
        hi_out[...] = hi_scr[...]


def _s5_branch(x2d, batch, seq, g_mix, w_in, u_col, h0r, h0i, ar, ai, bb, cc, d_skip, w_glu, b_glu):
    nstate = ar.shape[-1]
    d = x2d.shape[1]
    dm = w_glu.shape[1] // 2
    tt = min(S5_TIME_TILE, seq)
    nj = seq // tt
    ngroups = batch // SUBLANES
    column_inputs = nj == 1
    groups = min(ngroups, max(1, S5_ROW_BLOCK // (tt * SUBLANES))) if column_inputs else 1
    rows = groups * tt * SUBLANES
    body = functools.partial(_s5_body, tt=tt, groups=groups, nstate=nstate, column_inputs=column_inputs)
    if column_inputs:
        x_specs = [pl.BlockSpec((rows, LANES), lambda i, j, s=s: (i, s)) for s in range(d // LANES)]
    else:
        x_specs = [pl.BlockSpec((tt, d), lambda i, j, b=b: ((i * SUBLANES + b) * nj + j, 0))
                   for b in range(SUBLANES)]
    state_spec = pl.BlockSpec((groups, SUBLANES, nstate), lambda i, j: (i, 0, 0))
    width = d_skip.shape[-1]
    return pl.pallas_call(
        body, grid=(ngroups // groups, nj),
        in_specs=x_specs + [
            _resident(g_mix.shape),
            pl.BlockSpec((d, width), lambda i, j: (0, u_col // width), pipeline_mode=pl.Buffered(1)),
            state_spec, state_spec, _resident(ar.shape), _resident(ai.shape), _resident(bb.shape),
            _resident(cc.shape), _resident(d_skip.shape), _resident(w_glu.shape), _resident(b_glu.shape)],
        out_specs=[pl.BlockSpec((groups * SUBLANES, tt, dm), lambda i, j: (i, j, 0)), state_spec, state_spec],
        out_shape=[jax.ShapeDtypeStruct((batch, seq, dm), F32),
                   jax.ShapeDtypeStruct((ngroups, SUBLANES, nstate), F32),
                   jax.ShapeDtypeStruct((ngroups, SUBLANES, nstate), F32)],
        scratch_shapes=[pltpu.VMEM((groups, SUBLANES, nstate), F32), pltpu.VMEM((groups, SUBLANES, nstate), F32),
                        pltpu.VMEM((rows, 2 * nstate), F32),
                        pltpu.VMEM((d // LANES, rows, LANES), F32),
                        pltpu.VMEM((dm // LANES, rows, LANES), F32)],
        compiler_params=_cparams(("parallel", "arbitrary")), name="s5_branch")(
            *([x2d] * len(x_specs)), g_mix, w_in, h0r, h0i, ar, ai, bb, cc, d_skip, w_glu, b_glu)


def _cumsum_rows(x, c):
    row = lax.broadcasted_iota(I32, x.shape, 0) & (c - 1)
    s = 1
    while s < c:
        x = x + jnp.where(row >= s, pltpu.roll(x, s, axis=0), 0.0)
        s *= 2
    return x


def _hgrn_gates(q, fr, lb, scale, c):
    rows, n = q.shape
    f = lb + (1.0 - lb) * jax.nn.sigmoid(fr)
    k = 1.0 - f
    b = _cumsum_rows(jnp.log(f), c)
    b3 = b.reshape(rows // c, c, n)
    b_last = jnp.broadcast_to(b3[:, c - 1:c, :], b3.shape).reshape(rows, n)
    q_dec = (q * scale) * jnp.exp(b)
    k_dec = k * jnp.exp(-b)
    k_end = k * jnp.exp(b_last - b)
    return q_dec, k_dec, k_end, jnp.exp(b_last)


def _causal_scores(q_dec, k_dec):
    c = q_dec.shape[0]
    s = lax.dot_general(q_dec, k_dec, (((1,), (1,)), ((), ())), preferred_element_type=F32)
    keep = lax.broadcasted_iota(I32, (c, c), 0) >= lax.broadcasted_iota(I32, (c, c), 1)
    return jnp.where(keep, s, 0.0).astype(BF16)


def _gated_out(o, gn, og):
    o = o * lax.rsqrt(jnp.mean(o * o, axis=-1, keepdims=True) + RMS_EPS) * gn
    return (o * jax.nn.silu(og)).astype(BF16)


def _hgrn_long_body(q_ref, f_ref, v_ref, og_ref, lb_ref, gn_ref, o_ref, sfin_ref, st_scr, *, c, heads, dk, scale):
    j = pl.program_id(1)

    @pl.when(j == 0)
    def _():
        st_scr[...] = jnp.zeros_like(st_scr)

    def chunk(ci, carry):
        rs = pl.ds(pl.multiple_of(ci * c, c), c)
        for h in range(heads):
            hs = slice(h * dk, (h + 1) * dk)
            q_dec, k_dec, k_end, decay = _hgrn_gates(q_ref[rs, hs], f_ref[rs, hs], lb_ref[:, hs], scale, c)
            q_dec = q_dec.astype(BF16)
            v = v_ref[rs, hs].astype(BF16)
            scores = _causal_scores(q_dec, k_dec.astype(BF16))
            st = st_scr[h]
            o = (lax.dot_general(q_dec, st.astype(BF16), (((1,), (1,)), ((), ())), preferred_element_type=F32)
                 + jnp.dot(scores, v, preferred_element_type=F32))
            st_scr[h] = decay[:1] * st + lax.dot_general(
                v, k_end.astype(BF16), (((0,), (0,)), ((), ())), preferred_element_type=F32)
            o_ref[rs, hs] = _gated_out(o, gn_ref[:, hs], og_ref[rs, hs])
        return carry

    lax.fori_loop(0, q_ref.shape[0] // c, chunk, 0, unroll=HGRN_CHUNK_UNROLL)

    @pl.when(j == pl.num_programs(1) - 1)
    def _():
        for h in range(heads):
            sfin_ref[0, h] = st_scr[h].T


def _hgrn_long(proj, lb, gn, batch, seq, heads, dk, row_off):
    width = heads * dk
    tb = min(HGRN_TIME_TILE, seq)
    nj = seq // tb
    off = row_off // tb
    c = min(HG_CHUNK, seq)
    body = functools.partial(_hgrn_long_body, c=c, heads=heads, dk=dk, scale=dk ** -0.5)

    def col(k):
        return pl.BlockSpec((tb, width), lambda b, j, k=k: (off + b * nj + j, k))

    return pl.pallas_call(
        body, grid=(batch, nj),
        in_specs=[col(0), col(1), col(2), col(3), _resident(lb.shape), _resident(gn.shape)],
        out_specs=[pl.BlockSpec((tb, width), lambda b, j: (b * nj + j, 0)),
                   pl.BlockSpec((1, heads, dk, dk), lambda b, j: (b, 0, 0, 0))],
        out_shape=[jax.ShapeDtypeStruct((batch * seq, width), BF16),
                   jax.ShapeDtypeStruct((batch, heads, dk, dk), F32)],
        scratch_shapes=[pltpu.VMEM((heads, dk, dk), F32)],
        compiler_params=_cparams(("parallel", "arbitrary")), name="hgrn_long")(proj, proj, proj, proj, lb, gn)


def _hgrn_short_body(q_ref, f_ref, v_ref, og_ref, lb_ref, gn_ref, s0_ref, o_ref, snew_ref, *, c, heads, dk, scale):
    def one_seq(sq, carry):
        rs = pl.ds(pl.multiple_of(sq * c, c), c)
        for h in range(heads):
            hs = slice(h * dk, (h + 1) * dk)
            q_dec, k_dec, k_end, decay = _hgrn_gates(q_ref[rs, hs], f_ref[rs, hs], lb_ref[:, hs], scale, c)
            q_dec = q_dec.astype(BF16)
            v = v_ref[rs, hs].astype(BF16)
            scores = _causal_scores(q_dec, k_dec.astype(BF16))
            s0 = s0_ref[sq, h]
            o = (jnp.dot(q_dec, s0.astype(BF16), preferred_element_type=F32)
                 + jnp.dot(scores, v, preferred_element_type=F32))
            decay_col = jnp.broadcast_to(decay[:1], (dk, dk)).T
            snew_ref[sq, h] = decay_col * s0 + lax.dot_general(
                k_end.astype(BF16), v, (((0,), (0,)), ((), ())), preferred_element_type=F32)
            o_ref[rs, hs] = _gated_out(o, gn_ref[:, hs], og_ref[rs, hs])
        return carry

    lax.fori_loop(0, s0_ref.shape[0], one_seq, 0, unroll=HGRN_SEQ_UNROLL)


def _hgrn_short(proj, lb, gn, s0, seq, row_off):
    batch, heads, dk, _ = s0.shape
    width = heads * dk
    nb = HGRN_SEQ_TILE
    rows = nb * seq
    off = row_off // rows
    body = functools.partial(_hgrn_short_body, c=seq, heads=heads, dk=dk, scale=dk ** -0.5)

    def col(k):
        return pl.BlockSpec((rows, width), lambda i, k=k: (off + i, k))

    state_spec = pl.BlockSpec((nb, heads, dk, dk), lambda i: (i, 0, 0, 0))
    return pl.pallas_call(
        body, grid=(batch // nb,),
        in_specs=[col(0), col(1), col(2), col(3), _resident(lb.shape), _resident(gn.shape), state_spec],
        out_specs=[pl.BlockSpec((rows, width), lambda i: (i, 0)), state_spec],
        out_shape=[jax.ShapeDtypeStruct((batch * seq, width), BF16), jax.ShapeDtypeStruct(s0.shape, F32)],
        compiler_params=_cparams(("parallel",)), name="hgrn_short")(proj, proj, proj, proj, lb, gn, s0)


def _first_index_of_max(vals):
    m = vals[0]
    for v in vals[1:]:
        m = jnp.maximum(m, v)
    idx = jnp.full(m.shape, len(vals), I32)
    for e in range(len(vals) - 1, -1, -1):
        idx = jnp.where(vals[e] == m, e, idx)
    return m, idx


def _route_tile(lg, ids_ref, w_ref, rk_ref, cnt_ref, carry_scr, groups, experts):
    i = pl.program_id(0)
    tile = lg.shape[1]
    n_exp = groups * experts

    @pl.when(i == 0)
    def _():
        carry_scr[...] = jnp.zeros_like(carry_scr)

    gl = [lg[g:g + 1, :] for g in range(groups)]
    gmax, gidx = _first_index_of_max(gl)
    denom = jnp.exp(gl[0] - gmax)
    for g in range(1, groups):
        denom = denom + jnp.exp(gl[g] - gmax)
    g_w = 1.0 / denom

    el = []
    for e in range(experts):
        v = lg[groups + e:groups + e + 1, :]
        for g in range(1, groups):
            r = groups + g * experts + e
            v = jnp.where(gidx == g, lg[r:r + 1, :], v)
        el.append(v)
    v1, i1 = _first_index_of_max(el)
    rest = [jnp.where(i1 == e, -jnp.inf, el[e]) for e in range(experts)]
    v2, i2 = _first_index_of_max(rest)
    t = jnp.exp(v2 - v1)
    inv = 1.0 / (1.0 + t)
    e1 = gidx * experts + i1
    e2 = gidx * experts + i2

    erow = lax.broadcasted_iota(I32, (n_exp, tile), 0)
    oh1 = (erow == e1).astype(F32)
    oh2 = (erow == e2).astype(F32)
    oh = oh1 + oh2
    before = (lax.broadcasted_iota(I32, (tile, tile), 0) < lax.broadcasted_iota(I32, (tile, tile), 1))
    cnt = jnp.dot(oh.astype(BF16), before.astype(BF16), preferred_element_type=F32) + carry_scr[:, 0:1]
    ids_ref[0:1, :] = e1
    ids_ref[1:2, :] = e2
    w_ref[0:1, :] = inv * g_w
    w_ref[1:2, :] = (t * inv) * g_w
    rk_ref[0:1, :] = jnp.sum(oh1 * cnt, axis=0, keepdims=True).astype(I32)
    rk_ref[1:2, :] = jnp.sum(oh2 * cnt, axis=0, keepdims=True).astype(I32)
    carry_scr[...] = carry_scr[...] + jnp.sum(oh, axis=1, keepdims=True)

    @pl.when(i == pl.num_programs(0) - 1)
    def _():
        cnt_ref[...] = carry_scr[...]


def _merge_body(*refs, n_first, groups, experts, n_gate_blocks):
    op_ref, os_ref, ybp_ref, ybs_ref, xp_ref, xs_ref, gm_ref = refs[:7]
    gate_w_refs = refs[7:7 + n_gate_blocks]
    (wa_ref, wo_ref, gf_ref, wr_ref, br_ref, h_ref, xn_ref, ids_ref, w_ref, rk_ref, cnt_ref, carry_scr,
     wgt_ref) = refs[7 + n_gate_blocks:]
    d = h_ref.shape[1]

    @pl.when(pl.program_id(0) == 0)
    def _():
        gw = gate_w_refs[0].shape[1]
        for q, ref in enumerate(gate_w_refs):
            wgt_ref[:, q * gw:(q + 1) * gw] = ref[...].astype(BF16)

    x = _pick(xp_ref, xs_ref, n_first)
    gates = jnp.dot(_rmsnorm(x, gm_ref[...]).astype(BF16), wgt_ref[...], preferred_element_type=F32)
    y_a = jnp.dot(_pick(op_ref, os_ref, n_first), wa_ref[...], preferred_element_type=F32)
    merged = jax.nn.sigmoid(gates[:, :d]) * y_a + jax.nn.sigmoid(gates[:, d:]) * _pick(ybp_ref, ybs_ref, n_first)
    h = x + jnp.dot(merged.astype(BF16), wo_ref[...], preferred_element_type=F32)
    h_ref[...] = h
    xn = _rmsnorm(h, gf_ref[...])
    _store_token_tiles(xn_ref, xn)
    logits_t = lax.dot_general(wr_ref[...], xn.astype(BF16), (((1,), (1,)), ((), ())),
                               preferred_element_type=F32) + br_ref[...]
    _route_tile(logits_t, ids_ref, w_ref, rk_ref, cnt_ref, carry_scr, groups, experts)


def _merge(o_p, o_s, yb_p, yb_s, xp, xs, g_mix, w_in, gate_col, wa, wo, gf, wr, br, groups, experts):
    d = xp.shape[1]
    total = xp.shape[0] + xs.shape[0]
    tm = TOKEN_TILE
    n_first = xp.shape[0] // tm
    n_exp = groups * experts
    pair = _two_source_specs(tm, d, n_first)
    top = pl.BlockSpec((MOE_TOP_K, tm), lambda i: (0, i))
    gw = math.gcd(gate_col, 2 * d)
    n_gate_blocks = 2 * d // gw
    gate_specs = [pl.BlockSpec((d, gw), lambda i, q=q: (0, gate_col // gw + q), pipeline_mode=pl.Buffered(1))
                  for q in range(n_gate_blocks)]
    weights = [wa, wo, gf, wr, br]
    return pl.pallas_call(
        functools.partial(_merge_body, n_first=n_first, groups=groups, experts=experts,
                          n_gate_blocks=n_gate_blocks),
        grid=(total // tm,),
        in_specs=pair + pair + pair + [_resident(g_mix.shape)] + gate_specs + [_resident(w.shape) for w in weights],
        out_specs=[pl.BlockSpec((tm, d), lambda i: (i, 0)), pl.BlockSpec((tm * SUBLANES, LANES), lambda i: (i, 0)),
                   top, top, top, pl.BlockSpec((n_exp, LANES), lambda i: (0, 0))],
        out_shape=[jax.ShapeDtypeStruct((total, d), F32), jax.ShapeDtypeStruct((total * SUBLANES, LANES), F32),
                   jax.ShapeDtypeStruct((MOE_TOP_K, total), I32), jax.ShapeDtypeStruct((MOE_TOP_K, total), F32),
                   jax.ShapeDtypeStruct((MOE_TOP_K, total), I32), jax.ShapeDtypeStruct((n_exp, LANES), F32)],
        scratch_shapes=[pltpu.VMEM((n_exp, LANES), F32), pltpu.VMEM((d, 2 * d), BF16)],
        compiler_params=_cparams(("arbitrary",)), name="merge_route")(
            o_p, o_s, yb_p, yb_s, xp, xs, g_mix, *([w_in] * n_gate_blocks), *weights)


def _row_copy(src, dst, sem):
    return pltpu.make_async_copy(src, dst, sem)


def _token_rows(r):
    return pl.ds(pl.multiple_of(r * SUBLANES, SUBLANES), SUBLANES)


def _dispatch_body(*refs, n_sorted):
    pos_refs = refs[:MOE_TOP_K]
    x_ref, o_hbm, ring, zero_scr, sem, pad_sem = refs[MOE_TOP_K:]
    i = pl.program_id(0)
    tile = x_ref.shape[0] // SUBLANES
    par = lax.rem(i, 2)

    @pl.when(i == 0)
    def _():
        zero_scr[...] = jnp.zeros_like(zero_scr)
        pad = _row_copy(zero_scr, o_hbm.at[pl.ds(n_sorted * SUBLANES, zero_scr.shape[0])], pad_sem.at[0])
        pad.start()
        pad.wait()

    ring[par] = x_ref[...]

    def issue(r, carry):
        for k in range(MOE_TOP_K):
            p = pos_refs[k][r]
            _row_copy(ring.at[par, _token_rows(r)], o_hbm.at[_token_rows(p)], sem.at[par, k]).start(priority=k)
        return carry

    lax.fori_loop(0, tile, issue, 0, unroll=ROW_COPY_UNROLL)

    def drain(slot):
        for k in range(MOE_TOP_K):
            _row_copy(ring.at[slot], o_hbm.at[pl.ds(0, tile * SUBLANES)], sem.at[slot, k]).wait()

    @pl.when(i > 0)
    def _():
        drain(1 - par)

    @pl.when(i == pl.num_programs(0) - 1)
    def _():
        drain(par)


def _dispatch(pos_slots, xn_tiles):
    total = xn_tiles.shape[0] // SUBLANES
    tile = DISPATCH_TILE
    n_sorted = total * MOE_TOP_K
    pad = EXPERT_WINDOWS[-1]
    return pl.pallas_call(
        functools.partial(_dispatch_body, n_sorted=n_sorted), grid=(total // tile,),
        in_specs=[pl.BlockSpec((tile,), lambda i: (i,), memory_space=pltpu.SMEM)] * MOE_TOP_K + [
            pl.BlockSpec((tile * SUBLANES, LANES), lambda i: (i, 0))],
        out_specs=pl.BlockSpec(memory_space=pl.ANY),
        out_shape=jax.ShapeDtypeStruct(((n_sorted + pad) * SUBLANES, LANES), xn_tiles.dtype),
        scratch_shapes=[pltpu.VMEM((2, tile * SUBLANES, LANES), xn_tiles.dtype),
                        pltpu.VMEM((pad * SUBLANES, LANES), xn_tiles.dtype),
                        pltpu.SemaphoreType.DMA((2, MOE_TOP_K)), pltpu.SemaphoreType.DMA((1,))],
        compiler_params=_cparams(("arbitrary",)), name="dispatch")(*pos_slots, xn_tiles)


def _experts_body(it_exp, it_row, it_cls, it_first, it_next, n_items, xs_hbm, wg_hbm, wu_hbm, wd_hbm, ys_hbm,
                  xbuf, ybuf, wg_s, wu_s, wd_s, wg_b, wu_b, wd_b, sem_in, sem_out, sem_w):
    j = pl.program_id(0)
    n = n_items[0]
    pad = EXPERT_WINDOWS[-1]
    slot = lax.rem(j, 2)

    def weight_copies(e, s):
        return [pltpu.make_async_copy(hbm.at[e], stage.at[s], sem_w.at[s, t])
                for t, (hbm, stage) in enumerate(((wg_hbm, wg_s), (wu_hbm, wu_s), (wd_hbm, wd_s)))]

    def by_size(item, fn):
        for ci, m in enumerate(EXPERT_WINDOWS):
            pl.when(it_cls[item] == ci)(functools.partial(fn, m))

    def window(item, m):
        return pl.ds(pl.multiple_of(it_row[item] * SUBLANES, SUBLANES), m * SUBLANES)

    def in_copy(item, s, m):
        return pltpu.make_async_copy(xs_hbm.at[window(item, m)], xbuf.at[s, pl.ds(0, m * SUBLANES)], sem_in.at[s])

    def out_copy(item, s, m):
        return pltpu.make_async_copy(ybuf.at[s, pl.ds(0, m * SUBLANES)], ys_hbm.at[window(item, m)], sem_out.at[s])

    def compute(m):
        x = _load_token_tiles(xbuf, m, (slot,)).astype(BF16)
        hg = jnp.dot(x, wg_b[...], preferred_element_type=F32)
        hu = jnp.dot(x, wu_b[...], preferred_element_type=F32)
        hid = (jax.nn.silu(hg) * hu).astype(BF16)
        _store_token_tiles(ybuf, jnp.dot(hid, wd_b[...], preferred_element_type=F32), (slot,))

    @pl.when(j < n)
    def _():
        @pl.when(j == 0)
        def _():
            by_size(0, lambda m: in_copy(0, 0, m).start())
            for c in weight_copies(it_exp[0], it_first[0] - 1):
                c.start()
            tail_rows = pl.ds(0, pad * SUBLANES)
            ybuf[1, tail_rows, :] = jnp.zeros((pad * SUBLANES, LANES), F32)
            tail = pltpu.make_async_copy(
                ybuf.at[1, tail_rows], ys_hbm.at[pl.ds(ys_hbm.shape[0] - pad * SUBLANES, pad * SUBLANES)],
                sem_out.at[1])
            tail.start()
            tail.wait()

        @pl.when(j + 1 < n)
        def _():
            by_size(j + 1, lambda m: in_copy(j + 1, 1 - slot, m).start())

        @pl.when(it_first[j] > 0)
        def _():
            s = it_first[j] - 1
            for c in weight_copies(it_exp[j], s):
                c.wait()
            wg_b[...] = wg_s[s].astype(BF16)
            wu_b[...] = wu_s[s].astype(BF16)
            wd_b[...] = wd_s[s].astype(BF16)

            @pl.when(it_next[j] >= 0)
            def _():
                for c in weight_copies(it_next[j], 1 - s):
                    c.start()

        by_size(j, lambda m: in_copy(j, slot, m).wait())
        by_size(j, compute)

        @pl.when(j > 0)
        def _():
            by_size(j - 1, lambda m: out_copy(j - 1, 1 - slot, m).wait())

        by_size(j, lambda m: out_copy(j, slot, m).start())

        @pl.when(j == n - 1)
        def _():
            by_size(j, lambda m: out_copy(j, slot, m).wait())


def _experts(items, xs, wg, wu, wd):
    d, de = wg.shape[1], wg.shape[2]
    tm = EXPERT_TILE
    max_items = items[0].shape[0]
    grid_spec = pltpu.PrefetchScalarGridSpec(
        num_scalar_prefetch=6, grid=(max_items,),
        in_specs=[pl.BlockSpec(memory_space=pl.ANY)] * 4,
        out_specs=pl.BlockSpec(memory_space=pl.ANY),
        scratch_shapes=[pltpu.VMEM((2, tm * SUBLANES, LANES), F32), pltpu.VMEM((2, tm * SUBLANES, LANES), F32),
                        pltpu.VMEM((2, d, de), F32), pltpu.VMEM((2, d, de), F32), pltpu.VMEM((2, de, d), F32),
                        pltpu.VMEM((d, de), BF16), pltpu.VMEM((d, de), BF16), pltpu.VMEM((de, d), BF16),
                        pltpu.SemaphoreType.DMA((2,)), pltpu.SemaphoreType.DMA((2,)),
                        pltpu.SemaphoreType.DMA((2, 3))])
    return pl.pallas_call(
        _experts_body, grid_spec=grid_spec, out_shape=jax.ShapeDtypeStruct(xs.shape, F32),
        compiler_params=_cparams(("arbitrary",)), name="experts")(*items, xs, wg, wu, wd)


def _combine_body(*refs, final_norm):
    pos_refs, pos_next_refs = refs[:MOE_TOP_K], refs[MOE_TOP_K:2 * MOE_TOP_K]
    h_ref, w_ref, g_ref, ys_hbm, y_ref, buf, sem = refs[2 * MOE_TOP_K:]
    i = pl.program_id(0)
    tile = h_ref.shape[0]
    par = lax.rem(i, 2)

    def gather(tables, slot):
        def issue(r, carry):
            for k in range(MOE_TOP_K):
                p = tables[k][r]
                _row_copy(ys_hbm.at[_token_rows(p)], buf.at[slot, k, _token_rows(r)],
                          sem.at[slot, k]).start(priority=k)
            return carry

        lax.fori_loop(0, tile, issue, 0, unroll=ROW_COPY_UNROLL)

    @pl.when(i == 0)
    def _():
        gather(pos_refs, 0)

    for k in range(MOE_TOP_K):
        _row_copy(ys_hbm.at[pl.ds(0, tile * SUBLANES)], buf.at[par, k], sem.at[par, k]).wait()

    def combine_tile():
        h = h_ref[...] + (w_ref[:, 0:1] * _load_token_tiles(buf, tile, (par, 0))
                          + w_ref[:, 1:2] * _load_token_tiles(buf, tile, (par, 1)))
        y_ref[...] = _rmsnorm(h, g_ref[...]) if final_norm else h

    for slot in range(2):
        @pl.when(jnp.logical_and(i + 1 < pl.num_programs(0), par != slot))
        def _(slot=slot):
            for r in range(tile):
                for k in range(MOE_TOP_K):
                    p = pos_next_refs[k][r]
                    _row_copy(ys_hbm.at[_token_rows(p)], buf.at[slot, k, pl.ds(r * SUBLANES, SUBLANES)],
                              sem.at[slot, k]).start(priority=k)
            combine_tile()

    @pl.when(i + 1 == pl.num_programs(0))
    def _():
        combine_tile()


def _combine(pos_slots, h_all, w_t, g, ys, rows, row_off, final_norm):
    d = h_all.shape[1]
    tile = COMBINE_TILE
    off = row_off // tile
    last_block = h_all.shape[0] // tile - 1
    this_tile = pl.BlockSpec((tile,), lambda i: (off + i,), memory_space=pltpu.SMEM)
    next_tile = pl.BlockSpec((tile,), lambda i: (jnp.minimum(off + i + 1, last_block),), memory_space=pltpu.SMEM)
    return pl.pallas_call(
        functools.partial(_combine_body, final_norm=final_norm), grid=(rows // tile,),
        in_specs=[this_tile] * MOE_TOP_K + [next_tile] * MOE_TOP_K + [
                  pl.BlockSpec((tile, d), lambda i: (off + i, 0)),
                  pl.BlockSpec((tile, MOE_TOP_K), lambda i: (off + i, 0)),
                  _resident(g.shape),
                  pl.BlockSpec(memory_space=pl.ANY)],
        out_specs=pl.BlockSpec((tile, d), lambda i: (i, 0)),
        out_shape=jax.ShapeDtypeStruct((rows, d), F32),
        scratch_shapes=[pltpu.VMEM((2, MOE_TOP_K, tile * SUBLANES, LANES), F32),
                        pltpu.SemaphoreType.DMA((2, MOE_TOP_K))],
        compiler_params=_cparams(("arbitrary",)), name="combine")(*pos_slots, *pos_slots, h_all, w_t, g, ys)


def _lookup(table, idx):
    sel = idx[None] == jnp.arange(table.shape[0], dtype=I32).reshape((-1,) + (1,) * idx.ndim)
    return jnp.sum(jnp.where(sel, table.reshape(sel.shape[:1] + (1,) * idx.ndim), 0), axis=0)


def _work_items(counts, n_sorted):
    big, mid, small = EXPERT_WINDOWS
    n_exp = counts.shape[0]
    max_items = n_sorted // big + 2 * n_exp
    ends = jnp.cumsum(counts)
    starts = ends - counts
    units = (counts % big + small - 1) // small
    n_big = counts // big + (units == big // small)
    units = jnp.where(units == big // small, 0, units)
    n_mid = units // (mid // small)
    n_e = n_big + n_mid + units % (mid // small)
    item_end = jnp.cumsum(n_e)
    item_start = item_end - n_e
    n_items = item_end[-1]
    j = jnp.minimum(jnp.arange(max_items, dtype=I32), n_items - 1)
    e = jnp.sum((item_end[None, :] <= j[:, None]).astype(I32), axis=1)
    k = j - _lookup(item_start, e)
    nb, nm = _lookup(n_big, e), _lookup(n_mid, e)
    cls = jnp.where(k < nb, 0, jnp.where(k < nb + nm, 1, 2))
    row = _lookup(starts, e) + jnp.where(cls == 0, k * big, nb * big + jnp.where(cls == 1, 0, nm * mid))
    ordinal = jnp.cumsum((n_e > 0).astype(I32)) - 1
    first = jnp.where(k == 0, 1 + _lookup(ordinal, e) % 2, 0)
    nxt_item = _lookup(item_end, e)
    nxt = jnp.where(nxt_item < n_items, jnp.sum((item_end[None, :] <= nxt_item[:, None]).astype(I32), axis=1), -1)
    return (e, row.astype(I32), cls.astype(I32), first.astype(I32), nxt.astype(I32),
            n_items.reshape(1).astype(I32)), starts


def _s5_discretise(lam_re, lam_im, log_dt, b_re, b_im, c_re, c_im):
    g, p = lam_re.shape
    ch = b_re.shape[-1]
    lam_re = lam_re.astype(F32)
    lam_im = lam_im.astype(F32)
    dt = jnp.exp(log_dt.astype(F32))[:, None]
    mag = jnp.exp(lam_re * dt)
    ab_re = mag * jnp.cos(lam_im * dt)
    ab_im = mag * jnp.sin(lam_im * dt)
    den = lam_re * lam_re + lam_im * lam_im
    nr = ab_re - 1.0
    coef_re = (nr * lam_re + ab_im * lam_im) / den
    coef_im = (ab_im * lam_re - nr * lam_im) / den
    bb_re = coef_re[..., None] * b_re - coef_im[..., None] * b_im
    bb_im = coef_re[..., None] * b_im + coef_im[..., None] * b_re
    gh = g // 2
    eye = jnp.eye(gh, dtype=F32)

    def in_block(m):
        return jnp.einsum("gpc,gh->gchp", m, eye).reshape(gh * ch, gh * p)

    def out_block(m):
        return jnp.einsum("gcp,gh->gphc", m, eye).reshape(gh * p, gh * ch)

    bb = jnp.stack([jnp.concatenate([in_block(bb_re[k * gh:(k + 1) * gh]), in_block(bb_im[k * gh:(k + 1) * gh])],
                                    axis=1) for k in range(2)]).astype(BF16)
    cc = jnp.stack([jnp.concatenate([out_block(c_re[k * gh:(k + 1) * gh]), out_block(-c_im[k * gh:(k + 1) * gh])],
                                    axis=0) for k in range(2)]).astype(BF16)
    return ab_re.reshape(1, g * p), ab_im.reshape(1, g * p), bb, cc


def kernel(x_prompt, x_sample, state_hgrn, state_s5_re, state_s5_im, norm_mix_g, w_in, hgrn_lb_raw, hgrn_onorm_g, w_branch_a, s5_lambda_re, s5_lambda_im, s5_log_dt, s5_b_re, s5_b_im, s5_c_re, s5_c_im, s5_d, w_glu, b_glu, w_out, norm_ffn_g, w_router_group, b_router_group, w_router_expert, b_router_expert, w_exp_gate, w_exp_up, w_exp_down, norm_final_g):
    depth = norm_mix_g.shape[0]
    bp, lp, d = x_prompt.shape
    bs, ls, _ = x_sample.shape
    heads, dk = state_hgrn.shape[2], state_hgrn.shape[3]
    kw = heads * dk
    s5_groups, s5_state = state_s5_re.shape[2], state_s5_re.shape[3]
    s5_width = s5_d.shape[-1]
    nstate = s5_groups * s5_state
    moe_groups, _, experts = w_router_expert.shape[1:]
    n_exp = moe_groups * experts
    rows_p, rows_s = bp * lp, bs * ls
    total = rows_p + rows_s
    n_sorted = total * MOE_TOP_K
    assert kw == d and state_hgrn.shape[4] == dk, "column blocks assume key width == value width == model width"
    assert d == SUBLANES * LANES, "token-tile layout holds one token per (8, 128) tile"
    assert s5_groups % 2 == 0 and bp % SUBLANES == 0 and bs % HGRN_SEQ_TILE == 0
    assert (4 * kw) % s5_width == 0, "the S5 input columns must start on a multiple of their width"

    lb_all = jnp.cumsum(jax.nn.softmax(hgrn_lb_raw.astype(F32), axis=0), axis=0)

    hp = x_prompt.reshape(rows_p, d)
    hs = x_sample.reshape(rows_s, d)
    hg_p, re_p, im_p, hg_s, re_s, im_s = [], [], [], [], [], []
    zeros_state = jnp.zeros((bp // SUBLANES, SUBLANES, nstate), F32)

    for l in range(depth):
        w = w_in[l]
        g_mix = norm_mix_g[l].reshape(1, d)
        u_col = 4 * kw
        gate_col = u_col + s5_width
        proj = _in_proj(hp, hs, g_mix, w, u_col)

        ar, ai, bb, cc = _s5_discretise(s5_lambda_re[l], s5_lambda_im[l], s5_log_dt[l], s5_b_re[l], s5_b_im[l],
                                        s5_c_re[l], s5_c_im[l])
        s5_args = (ar, ai, bb, cc, s5_d[l].reshape(1, s5_width), w_glu[l].astype(BF16), b_glu[l].reshape(1, -1))
        yb_p, fr_p, fi_p = _s5_branch(hp, bp, lp, g_mix, w, u_col, zeros_state, zeros_state, *s5_args)
        yb_s, fr_s, fi_s = _s5_branch(hs, bs, ls, g_mix, w, u_col,
                                      state_s5_re[l].reshape(bs // SUBLANES, SUBLANES, nstate),
                                      state_s5_im[l].reshape(bs // SUBLANES, SUBLANES, nstate), *s5_args)

        lb = lb_all[l].reshape(1, kw)
        gn = hgrn_onorm_g[l].reshape(1, kw)
        o_p, hgp = _hgrn_long(proj, lb, gn, bp, lp, heads, dk, 0)
        o_s, hgs = _hgrn_short(proj, lb, gn, state_hgrn[l].astype(F32), ls, rows_p)

        nr = -(-(moe_groups + n_exp) // SUBLANES) * SUBLANES
        wr = jnp.concatenate([w_router_group[l].T, w_router_expert[l].transpose(0, 2, 1).reshape(n_exp, d)], axis=0)
        wr = jnp.pad(wr, ((0, nr - wr.shape[0]), (0, 0))).astype(BF16)
        br = jnp.pad(jnp.concatenate([b_router_group[l], b_router_expert[l].reshape(n_exp)]),
                     (0, nr - moe_groups - n_exp)).reshape(nr, 1).astype(F32)
        h_all, xn_all, ids, wts, ranks, cnt = _merge(
            o_p, o_s, yb_p.reshape(rows_p, d), yb_s.reshape(rows_s, d), hp, hs, g_mix, w, gate_col,
            w_branch_a[l].astype(BF16), w_out[l].astype(BF16), norm_ffn_g[l].reshape(1, d), wr, br,
            moe_groups, experts)
        items, starts = _work_items(cnt[:, 0].astype(I32), n_sorted)
        pos_slots = [_lookup(starts, ids[k]) + ranks[k] for k in range(MOE_TOP_K)]
        xs = _dispatch(pos_slots, xn_all)
        ys = _experts(items, xs, w_exp_gate[l], w_exp_up[l], w_exp_down[l])

        last = l == depth - 1
        g_out = norm_final_g.reshape(1, d)
        hp = _combine(pos_slots, h_all, wts.T, g_out, ys, rows_p, 0, last)
        hs = _combine(pos_slots, h_all, wts.T, g_out, ys, rows_s, rows_p, last)

        hg_p.append(hgp)
        hg_s.append(hgs)
        re_p.append(fr_p.reshape(bp, s5_groups, s5_state))
        im_p.append(fi_p.reshape(bp, s5_groups, s5_state))
        re_s.append(fr_s.reshape(bs, s5_groups, s5_state))
        im_s.append(fi_s.reshape(bs, s5_groups, s5_state))

    y_prompt = hp.reshape(bp, lp, d).astype(x_prompt.dtype)
    y_sample = hs.reshape(bs, ls, d).astype(x_sample.dtype)
    return (y_prompt, y_sample, jnp.stack(hg_p), jnp.stack(re_p), jnp.stack(im_p),
            jnp.stack(hg_s), jnp.stack(re_s), jnp.stack(im_s))
```

```python
import functools
import math

import jax
import jax.numpy as jnp
from jax import lax
from jax.experimental import pallas as pl
from jax.experimental.pallas import tpu as pltpu

F32 = jnp.float32
BF16 = jnp.bfloat16
I32 = jnp.int32

RMS_EPS = 1e-6
HG_CHUNK = 64
MOE_TOP_K = 2

V7X_VMEM_BYTES = 64 * 1024 * 1024
VMEM_LIMIT_BYTES = V7X_VMEM_BYTES - 8 * 1024 * 1024
SUBLANES = 8
LANES = 128

TOKEN_TILE = 512
DISPATCH_TILE = 1024
COMBINE_TILE = 512
EXPERT_TILE = 512
EXPERT_WINDOWS = (EXPERT_TILE, EXPERT_TILE // 2, EXPERT_TILE // 4)
S5_TIME_TILE = 128
S5_ROW_BLOCK = 1024
HGRN_TIME_TILE = 1024
HGRN_SEQ_TILE = 16
HGRN_CHUNK_UNROLL = 16
HGRN_SEQ_UNROLL = 8
PROJ_COL_TILE = 512
S5_SCAN_LANES = 512
ROW_COPY_UNROLL = 8


def _cparams(sem):
    return pltpu.CompilerParams(dimension_semantics=sem, vmem_limit_bytes=VMEM_LIMIT_BYTES)


def _resident(shape):
    nd = len(shape)
    return pl.BlockSpec(shape, lambda *_: (0,) * nd, pipeline_mode=pl.Buffered(1))


def _rmsnorm(x, g):
    return x * lax.rsqrt(jnp.mean(x * x, axis=-1, keepdims=True) + RMS_EPS) * g


def _two_source_specs(tm, width, n_first):
    return [pl.BlockSpec((tm, width), lambda i: (jnp.minimum(i, n_first - 1), 0)),
            pl.BlockSpec((tm, width), lambda i: (jnp.maximum(i - n_first, 0), 0))]


def _pick(first_ref, second_ref, n_first):
    return jnp.where(pl.program_id(0) < n_first, first_ref[...], second_ref[...])


def _store_token_tiles(ref, x, lead=()):
    rows = x.shape[0]
    for c in range(SUBLANES):
        ref[lead + (pl.ds(c, rows, stride=SUBLANES), slice(None))] = x[:, c * LANES:(c + 1) * LANES]


def _load_token_tiles(ref, rows, lead=()):
    return jnp.concatenate([ref[lead + (pl.ds(c, rows, stride=SUBLANES), slice(None))] for c in range(SUBLANES)],
                           axis=-1)


def _inproj_body(xp_ref, xs_ref, g_ref, w_ref, o_ref, wb_scr, *, n_first):
    @pl.when(pl.program_id(0) == 0)
    def _():
        wb_scr[...] = w_ref[...].astype(BF16)

    xb = _rmsnorm(_pick(xp_ref, xs_ref, n_first), g_ref[...]).astype(BF16)
    for j in range(0, wb_scr.shape[1], PROJ_COL_TILE):
        o_ref[:, j:j + PROJ_COL_TILE] = jnp.dot(xb, wb_scr[:, j:j + PROJ_COL_TILE], preferred_element_type=F32)


def _in_proj(xp, xs, g, w, n):
    d = xp.shape[1]
    tm = TOKEN_TILE
    total = xp.shape[0] + xs.shape[0]
    n_first = xp.shape[0] // tm
    return pl.pallas_call(
        functools.partial(_inproj_body, n_first=n_first), grid=(total // tm,),
        in_specs=_two_source_specs(tm, d, n_first) + [
            _resident((1, d)), pl.BlockSpec((d, n), lambda i: (0, 0), pipeline_mode=pl.Buffered(1))],
        out_specs=pl.BlockSpec((tm, n), lambda i: (i, 0)),
        out_shape=jax.ShapeDtypeStruct((total, n), F32),
        scratch_shapes=[pltpu.VMEM((d, n), BF16)],
        compiler_params=_cparams(("arbitrary",)), name="in_proj")(xp, xs, g, w)


def _s5_body(*refs, tt, groups, nstate, column_inputs):
    n_x = len(refs) - 19
    x_refs = refs[:n_x]
    (gm_ref, wu_ref, h0r_ref, h0i_ref, ar_ref, ai_ref, bb_ref, cc_ref, d_ref, wg_ref, bg_ref,
     y_ref, hr_out, hi_out, hr_scr, hi_scr, bu_scr, x_scr, y_scr) = refs[n_x:]
    j = pl.program_id(1)
    half = nstate // 2
    d = x_scr.shape[0] * LANES
    kw = wu_ref.shape[-1] // 2

    @pl.when(j == 0)
    def _():
        hr_scr[...] = h0r_ref[...]
        hi_scr[...] = h0i_ref[...]

    if column_inputs:
        for s in range(d // LANES):
            for g in range(groups):
                for t in range(tt):
                    r0 = (g * tt + t) * SUBLANES
                    x_scr[s, r0:r0 + SUBLANES, :] = x_refs[s][pl.ds(g * SUBLANES * tt + t, SUBLANES, stride=tt), :]
    else:
        for b in range(SUBLANES):
            xb = x_refs[b][...]
            for s in range(d // LANES):
                x_scr[s, pl.ds(b, tt, stride=SUBLANES), :] = xb[:, s * LANES:(s + 1) * LANES]
    x = jnp.concatenate([x_scr[s] for s in range(d // LANES)], axis=-1)
    u = jnp.dot(_rmsnorm(x, gm_ref[...]).astype(BF16), wu_ref[...].astype(BF16), preferred_element_type=F32)
    ub16 = u.astype(BF16)
    for kt in range(2):
        ukt = ub16[:, kt * kw:(kt + 1) * kw]
        bu_scr[:, kt * half:(kt + 1) * half] = jnp.dot(ukt, bb_ref[kt, :, :half], preferred_element_type=F32)
        bu_scr[:, nstate + kt * half:nstate + (kt + 1) * half] = jnp.dot(
            ukt, bb_ref[kt, :, half:], preferred_element_type=F32)

    lane_chunk = S5_SCAN_LANES
    for lc in range(nstate // lane_chunk):
        lo = lc * lane_chunk
        re_sl = slice(lo, lo + lane_chunk)
        im_sl = slice(nstate + lo, nstate + lo + lane_chunk)
        ar = jnp.broadcast_to(ar_ref[:, re_sl], (SUBLANES, lane_chunk))
        ai = jnp.broadcast_to(ai_ref[:, re_sl], (SUBLANES, lane_chunk))

        for g in range(groups):
            hr, hi = hr_scr[g, :, re_sl], hi_scr[g, :, re_sl]
            for t in range(tt):
                r0 = (g * tt + t) * SUBLANES
                rs = slice(r0, r0 + SUBLANES)
                hr, hi = (ar * hr - ai * hi + bu_scr[rs, re_sl], ar * hi + ai * hr + bu_scr[rs, im_sl])
                bu_scr[rs, re_sl] = hr
                bu_scr[rs, im_sl] = hi
            hr_scr[g, :, re_sl] = hr
            hi_scr[g, :, re_sl] = hi

    dm = wg_ref.shape[-1] // 2
    ys = []
    for n in range(2):
        h_re = bu_scr[:, n * half:(n + 1) * half].astype(BF16)
        h_im = bu_scr[:, nstate + n * half:nstate + (n + 1) * half].astype(BF16)
        ys.append(jnp.dot(h_re, cc_ref[n, :half, :], preferred_element_type=F32)
                  + jnp.dot(h_im, cc_ref[n, half:, :], preferred_element_type=F32))
    y = jnp.concatenate(ys, axis=-1) + d_ref[...] * u
    z = jnp.dot(jax.nn.gelu(y).astype(BF16), wg_ref[...], preferred_element_type=F32) + bg_ref[...]
    yb = z[:, :dm] * jax.nn.sigmoid(z[:, dm:])
    for s in range(dm // LANES):
        y_scr[s] = yb[:, s * LANES:(s + 1) * LANES]
    for g in range(groups):
        for b in range(SUBLANES):
            for s in range(dm // LANES):
                y_ref[g * SUBLANES + b, :, s * LANES:(s + 1) * LANES] = y_scr[
                    s, pl.ds(g * tt * SUBLANES + b, tt, stride=SUBLANES), :]

    @pl.when(j == pl.num_programs(1) - 1)
    def _():
        hr_out[...] = hr_scr[...]
        hi_out[...] = hi_scr[...]


def _s5_branch(x2d, batch, seq, g_mix, w_in, u_col, h0r, h0i, ar, ai, bb, cc, d_skip, w_glu, b_glu):
    nstate = ar.shape[-1]
    d = x2d.shape[1]
    dm = w_glu.shape[1] // 2
    tt = min(S5_TIME_TILE, seq)
    nj = seq // tt
    ngroups = batch // SUBLANES
    column_inputs = nj == 1
    groups = min(ngroups, max(1, S5_ROW_BLOCK // (tt * SUBLANES))) if column_inputs else 1
    rows = groups * tt * SUBLANES
    body = functools.partial(_s5_body, tt=tt, groups=groups, nstate=nstate, column_inputs=column_inputs)
    if column_inputs:
        x_specs = [pl.BlockSpec((rows, LANES), lambda i, j, s=s: (i, s)) for s in range(d // LANES)]
    else:
        x_specs = [pl.BlockSpec((tt, d), lambda i, j, b=b: ((i * SUBLANES + b) * nj + j, 0))
                   for b in range(SUBLANES)]
    state_spec = pl.BlockSpec((groups, SUBLANES, nstate), lambda i, j: (i, 0, 0))
    width = d_skip.shape[-1]
    return pl.pallas_call(
        body, grid=(ngroups // groups, nj),
        in_specs=x_specs + [
            _resident(g_mix.shape),
            pl.BlockSpec((d, width), lambda i, j: (0, u_col // width), pipeline_mode=pl.Buffered(1)),
            state_spec, state_spec, _resident(ar.shape), _resident(ai.shape), _resident(bb.shape),
            _resident(cc.shape), _resident(d_skip.shape), _resident(w_glu.shape), _resident(b_glu.shape)],
        out_specs=[pl.BlockSpec((groups * SUBLANES, tt, dm), lambda i, j: (i, j, 0)), state_spec, state_spec],
        out_shape=[jax.ShapeDtypeStruct((batch, seq, dm), F32),
                   jax.ShapeDtypeStruct((ngroups, SUBLANES, nstate), F32),
                   jax.ShapeDtypeStruct((ngroups, SUBLANES, nstate), F32)],
        scratch_shapes=[pltpu.VMEM((groups, SUBLANES, nstate), F32), pltpu.VMEM((groups, SUBLANES, nstate), F32),
                        pltpu.VMEM((rows, 2 * nstate), F32),
                        pltpu.VMEM((d // LANES, rows, LANES), F32),
                        pltpu.VMEM((dm // LANES, rows, LANES), F32)],
        compiler_params=_cparams(("parallel", "arbitrary")), name="s5_branch")(
            *([x2d] * len(x_specs)), g_mix, w_in, h0r, h0i, ar, ai, bb, cc, d_skip, w_glu, b_glu)


def _cumsum_rows(x, c):
    row = lax.broadcasted_iota(I32, x.shape, 0) & (c - 1)
    s = 1
    while s < c:
        x = x + jnp.where(row >= s, pltpu.roll(x, s, axis=0), 0.0)
        s *= 2
    return x


def _hgrn_gates(q, fr, lb, scale, c):
    rows, n = q.shape
    f = lb + (1.0 - lb) * jax.nn.sigmoid(fr)
    k = 1.0 - f
    b = _cumsum_rows(jnp.log(f), c)
    b3 = b.reshape(rows // c, c, n)
    b_last = jnp.broadcast_to(b3[:, c - 1:c, :], b3.shape).reshape(rows, n)
    q_dec = (q * scale) * jnp.exp(b)
    k_dec = k * jnp.exp(-b)
    k_end = k * jnp.exp(b_last - b)
    return q_dec, k_dec, k_end, jnp.exp(b_last)


def _causal_scores(q_dec, k_dec):
    c = q_dec.shape[0]
    s = lax.dot_general(q_dec, k_dec, (((1,), (1,)), ((), ())), preferred_element_type=F32)
    keep = lax.broadcasted_iota(I32, (c, c), 0) >= lax.broadcasted_iota(I32, (c, c), 1)
    return jnp.where(keep, s, 0.0).astype(BF16)


def _gated_out(o, gn, og):
    o = o * lax.rsqrt(jnp.mean(o * o, axis=-1, keepdims=True) + RMS_EPS) * gn
    return (o * jax.nn.silu(og)).astype(BF16)


def _hgrn_long_body(q_ref, f_ref, v_ref, og_ref, lb_ref, gn_ref, o_ref, sfin_ref, st_scr, *, c, heads, dk, scale):
    j = pl.program_id(1)

    @pl.when(j == 0)
    def _():
        st_scr[...] = jnp.zeros_like(st_scr)

    def block_diag(a, b):
        z = jnp.zeros_like(a)
        return jnp.concatenate([jnp.concatenate([a, z], axis=1), jnp.concatenate([z, b], axis=1)], axis=0)

    nt = (((1,), (1,)), ((), ()))
    col = lax.broadcasted_iota(I32, (c, 2 * c), 1) & (c - 1)
    keep = lax.broadcasted_iota(I32, (c, 2 * c), 0) >= col

    def chunk(ci, carry):
        rs = pl.ds(pl.multiple_of(ci * c, c), c)
        for h0 in range(0, heads, 2):
            ps = slice(h0 * dk, (h0 + 2) * dk)
            q_dec, k_dec, k_end, decay = _hgrn_gates(q_ref[rs, ps], f_ref[rs, ps], lb_ref[:, ps], scale, c)
            q_dec = q_dec.astype(BF16)
            k_dec = k_dec.astype(BF16)
            k_end = k_end.astype(BF16)
            v = v_ref[rs, ps].astype(BF16)
            scores = lax.dot_general(q_dec, block_diag(k_dec[:, :dk], k_dec[:, dk:]), nt, preferred_element_type=F32)
            scores = jnp.where(keep, scores, 0.0).astype(BF16)
            st0, st1 = st_scr[h0], st_scr[h0 + 1]
            o = (lax.dot_general(q_dec, block_diag(st0.astype(BF16), st1.astype(BF16)), nt,
                                 preferred_element_type=F32)
                 + jnp.dot(scores, block_diag(v[:, :dk], v[:, dk:]), preferred_element_type=F32))
            for t, st in enumerate((st0, st1)):
                hs = slice(t * dk, (t + 1) * dk)
                gs = slice((h0 + t) * dk, (h0 + t + 1) * dk)
                st_scr[h0 + t] = decay[:1, hs] * st + lax.dot_general(
                    v[:, hs], k_end[:, hs], (((0,), (0,)), ((), ())), preferred_element_type=F32)
                o_ref[rs, gs] = _gated_out(o[:, hs], gn_ref[:, gs], og_ref[rs, gs])
        return carry

    lax.fori_loop(0, q_ref.shape[0] // c, chunk, 0, unroll=HGRN_CHUNK_UNROLL)

    @pl.when(j == pl.num_programs(1) - 1)
    def _():
        for h in range(heads):
            sfin_ref[0, h] = st_scr[h].T


def _hgrn_long(proj, lb, gn, batch, seq, heads, dk, row_off):
    width = heads * dk
    tb = min(HGRN_TIME_TILE, seq)
    nj = seq // tb
    off = row_off // tb
    c = min(HG_CHUNK, seq)
    body = functools.partial(_hgrn_long_body, c=c, heads=heads, dk=dk, scale=dk ** -0.5)

    def col(k):
        return pl.BlockSpec((tb, width), lambda b, j, k=k: (off + b * nj + j, k))

    return pl.pallas_call(
        body, grid=(batch, nj),
        in_specs=[col(0), col(1), col(2), col(3), _resident(lb.shape), _resident(gn.shape)],
        out_specs=[pl.BlockSpec((tb, width), lambda b, j: (b * nj + j, 0)),
                   pl.BlockSpec((1, heads, dk, dk), lambda b, j: (b, 0, 0, 0))],
        out_shape=[jax.ShapeDtypeStruct((batch * seq, width), BF16),
                   jax.ShapeDtypeStruct((batch, heads, dk, dk), F32)],
        scratch_shapes=[pltpu.VMEM((heads, dk, dk), F32)],
        compiler_params=_cparams(("parallel", "arbitrary")), name="hgrn_long")(proj, proj, proj, proj, lb, gn)


def _hgrn_short_body(q_ref, f_ref, v_ref, og_ref, lb_ref, gn_ref, s0_ref, o_ref, snew_ref, *, c, heads, dk, scale):
    def one_seq(sq, carry):
        rs = pl.ds(pl.multiple_of(sq * c, c), c)
        for h in range(heads):
            hs = slice(h * dk, (h + 1) * dk)
            q_dec, k_dec, k_end, decay = _hgrn_gates(q_ref[rs, hs], f_ref[rs, hs], lb_ref[:, hs], scale, c)
            q_dec = q_dec.astype(BF16)
            v = v_ref[rs, hs].astype(BF16)
            scores = _causal_scores(q_dec, k_dec.astype(BF16))
            s0 = s0_ref[sq, h]
            o = (jnp.dot(q_dec, s0.astype(BF16), preferred_element_type=F32)
                 + jnp.dot(scores, v, preferred_element_type=F32))
            decay_col = jnp.broadcast_to(decay[:1], (dk, dk)).T
            snew_ref[sq, h] = decay_col * s0 + lax.dot_general(
                k_end.astype(BF16), v, (((0,), (0,)), ((), ())), preferred_element_type=F32)
            o_ref[rs, hs] = _gated_out(o, gn_ref[:, hs], og_ref[rs, hs])
        return carry

    lax.fori_loop(0, s0_ref.shape[0], one_seq, 0, unroll=HGRN_SEQ_UNROLL)


def _hgrn_short(proj, lb, gn, s0, seq, row_off):
    batch, heads, dk, _ = s0.shape
    width = heads * dk
    nb = HGRN_SEQ_TILE
    rows = nb * seq
    off = row_off // rows
    body = functools.partial(_hgrn_short_body, c=seq, heads=heads, dk=dk, scale=dk ** -0.5)

    def col(k):
        return pl.BlockSpec((rows, width), lambda i, k=k: (off + i, k))

    state_spec = pl.BlockSpec((nb, heads, dk, dk), lambda i: (i, 0, 0, 0))
    return pl.pallas_call(
        body, grid=(batch // nb,),
        in_specs=[col(0), col(1), col(2), col(3), _resident(lb.shape), _resident(gn.shape), state_spec],
        out_specs=[pl.BlockSpec((rows, width), lambda i: (i, 0)), state_spec],
        out_shape=[jax.ShapeDtypeStruct((batch * seq, width), BF16), jax.ShapeDtypeStruct(s0.shape, F32)],
        compiler_params=_cparams(("parallel",)), name="hgrn_short")(proj, proj, proj, proj, lb, gn, s0)


def _first_index_of_max(vals):
    m = vals[0]
    for v in vals[1:]:
        m = jnp.maximum(m, v)
    idx = jnp.full(m.shape, len(vals), I32)
    for e in range(len(vals) - 1, -1, -1):
        idx = jnp.where(vals[e] == m, e, idx)
    return m, idx


def _route_tile(lg, ids_ref, w_ref, rk_ref, cnt_ref, carry_scr, groups, experts):
    i = pl.program_id(0)
    tile = lg.shape[1]
    n_exp = groups * experts

    @pl.when(i == 0)
    def _():
        carry_scr[...] = jnp.zeros_like(carry_scr)

    gl = [lg[g:g + 1, :] for g in range(groups)]
    gmax, gidx = _first_index_of_max(gl)
    denom = jnp.exp(gl[0] - gmax)
    for g in range(1, groups):
        denom = denom + jnp.exp(gl[g] - gmax)
    g_w = 1.0 / denom

    el = []
    for e in range(experts):
        v = lg[groups + e:groups + e + 1, :]
        for g in range(1, groups):
            r = groups + g * experts + e
            v = jnp.where(gidx == g, lg[r:r + 1, :], v)
        el.append(v)
    v1, i1 = _first_index_of_max(el)
    rest = [jnp.where(i1 == e, -jnp.inf, el[e]) for e in range(experts)]
    v2, i2 = _first_index_of_max(rest)
    t = jnp.exp(v2 - v1)
    inv = 1.0 / (1.0 + t)
    e1 = gidx * experts + i1
    e2 = gidx * experts + i2

    erow = lax.broadcasted_iota(I32, (n_exp, tile), 0)
    oh1 = (erow == e1).astype(F32)
    oh2 = (erow == e2).astype(F32)
    oh = oh1 + oh2
    before = (lax.broadcasted_iota(I32, (tile, tile), 0) < lax.broadcasted_iota(I32, (tile, tile), 1))
    cnt = jnp.dot(oh.astype(BF16), before.astype(BF16), preferred_element_type=F32) + carry_scr[:, 0:1]
    ids_ref[0:1, :] = e1
    ids_ref[1:2, :] = e2
    w_ref[0:1, :] = inv * g_w
    w_ref[1:2, :] = (t * inv) * g_w
    rk_ref[0:1, :] = jnp.sum(oh1 * cnt, axis=0, keepdims=True).astype(I32)
    rk_ref[1:2, :] = jnp.sum(oh2 * cnt, axis=0, keepdims=True).astype(I32)
    carry_scr[...] = carry_scr[...] + jnp.sum(oh, axis=1, keepdims=True)

    @pl.when(i == pl.num_programs(0) - 1)
    def _():
        cnt_ref[...] = carry_scr[...]


def _merge_body(*refs, n_first, groups, experts, n_gate_blocks):
    op_ref, os_ref, ybp_ref, ybs_ref, xp_ref, xs_ref, gm_ref = refs[:7]
    gate_w_refs = refs[7:7 + n_gate_blocks]
    (wa_ref, wo_ref, gf_ref, wr_ref, br_ref, h_ref, xn_ref, ids_ref, w_ref, rk_ref, cnt_ref, carry_scr,
     wgt_ref) = refs[7 + n_gate_blocks:]
    d = h_ref.shape[1]

    @pl.when(pl.program_id(0) == 0)
    def _():
        gw = gate_w_refs[0].shape[1]
        for q, ref in enumerate(gate_w_refs):
            wgt_ref[:, q * gw:(q + 1) * gw] = ref[...].astype(BF16)

    x = _pick(xp_ref, xs_ref, n_first)
    gates = jnp.dot(_rmsnorm(x, gm_ref[...]).astype(BF16), wgt_ref[...], preferred_element_type=F32)
    y_a = jnp.dot(_pick(op_ref, os_ref, n_first), wa_ref[...], preferred_element_type=F32)
    merged = jax.nn.sigmoid(gates[:, :d]) * y_a + jax.nn.sigmoid(gates[:, d:]) * _pick(ybp_ref, ybs_ref, n_first)
    h = x + jnp.dot(merged.astype(BF16), wo_ref[...], preferred_element_type=F32)
    h_ref[...] = h
    xn = _rmsnorm(h, gf_ref[...])
    _store_token_tiles(xn_ref, xn)
    logits_t = lax.dot_general(wr_ref[...], xn.astype(BF16), (((1,), (1,)), ((), ())),
                               preferred_element_type=F32) + br_ref[...]
    _route_tile(logits_t, ids_ref, w_ref, rk_ref, cnt_ref, carry_scr, groups, experts)


def _merge(o_p, o_s, yb_p, yb_s, xp, xs, g_mix, w_in, gate_col, wa, wo, gf, wr, br, groups, experts):
    d = xp.shape[1]
    total = xp.shape[0] + xs.shape[0]
    tm = TOKEN_TILE
    n_first = xp.shape[0] // tm
    n_exp = groups * experts
    pair = _two_source_specs(tm, d, n_first)
    top = pl.BlockSpec((MOE_TOP_K, tm), lambda i: (0, i))
    gw = math.gcd(gate_col, 2 * d)
    n_gate_blocks = 2 * d // gw
    gate_specs = [pl.BlockSpec((d, gw), lambda i, q=q: (0, gate_col // gw + q), pipeline_mode=pl.Buffered(1))
                  for q in range(n_gate_blocks)]
    weights = [wa, wo, gf, wr, br]
    return pl.pallas_call(
        functools.partial(_merge_body, n_first=n_first, groups=groups, experts=experts,
                          n_gate_blocks=n_gate_blocks),
        grid=(total // tm,),
        in_specs=pair + pair + pair + [_resident(g_mix.shape)] + gate_specs + [_resident(w.shape) for w in weights],
        out_specs=[pl.BlockSpec((tm, d), lambda i: (i, 0)), pl.BlockSpec((tm * SUBLANES, LANES), lambda i: (i, 0)),
                   top, top, top, pl.BlockSpec((n_exp, LANES), lambda i: (0, 0))],
        out_shape=[jax.ShapeDtypeStruct((total, d), F32), jax.ShapeDtypeStruct((total * SUBLANES, LANES), F32),
                   jax.ShapeDtypeStruct((MOE_TOP_K, total), I32), jax.ShapeDtypeStruct((MOE_TOP_K, total), F32),
                   jax.ShapeDtypeStruct((MOE_TOP_K, total), I32), jax.ShapeDtypeStruct((n_exp, LANES), F32)],
        scratch_shapes=[pltpu.VMEM((n_exp, LANES), F32), pltpu.VMEM((d, 2 * d), BF16)],
        compiler_params=_cparams(("arbitrary",)), name="merge_route")(
            o_p, o_s, yb_p, yb_s, xp, xs, g_mix, *([w_in] * n_gate_blocks), *weights)


def _row_copy(src, dst, sem):
    return pltpu.make_async_copy(src, dst, sem)


def _token_rows(r):
    return pl.ds(pl.multiple_of(r * SUBLANES, SUBLANES), SUBLANES)


def _dispatch_body(*refs, n_sorted):
    pos_refs = refs[:MOE_TOP_K]
    x_ref, o_hbm, ring, zero_scr, sem, pad_sem = refs[MOE_TOP_K:]
    i = pl.program_id(0)
    tile = x_ref.shape[0] // SUBLANES
    par = lax.rem(i, 2)

    @pl.when(i == 0)
    def _():
        zero_scr[...] = jnp.zeros_like(zero_scr)
        pad = _row_copy(zero_scr, o_hbm.at[pl.ds(n_sorted * SUBLANES, zero_scr.shape[0])], pad_sem.at[0])
        pad.start()
        pad.wait()

    ring[par] = x_ref[...]

    def issue(r, carry):
        for k in range(MOE_TOP_K):
            p = pos_refs[k][r]
            _row_copy(ring.at[par, _token_rows(r)], o_hbm.at[_token_rows(p)], sem.at[par, k]).start(priority=k)
        return carry

    lax.fori_loop(0, tile, issue, 0, unroll=ROW_COPY_UNROLL)

    def drain(slot):
        for k in range(MOE_TOP_K):
            _row_copy(ring.at[slot], o_hbm.at[pl.ds(0, tile * SUBLANES)], sem.at[slot, k]).wait()

    @pl.when(i > 0)
    def _():
        drain(1 - par)

    @pl.when(i == pl.num_programs(0) - 1)
    def _():
        drain(par)


def _dispatch(pos_slots, xn_tiles):
    total = xn_tiles.shape[0] // SUBLANES
    tile = DISPATCH_TILE
    n_sorted = total * MOE_TOP_K
    pad = EXPERT_WINDOWS[-1]
    return pl.pallas_call(
        functools.partial(_dispatch_body, n_sorted=n_sorted), grid=(total // tile,),
        in_specs=[pl.BlockSpec((tile,), lambda i: (i,), memory_space=pltpu.SMEM)] * MOE_TOP_K + [
            pl.BlockSpec((tile * SUBLANES, LANES), lambda i: (i, 0))],
        out_specs=pl.BlockSpec(memory_space=pl.ANY),
        out_shape=jax.ShapeDtypeStruct(((n_sorted + pad) * SUBLANES, LANES), xn_tiles.dtype),
        scratch_shapes=[pltpu.VMEM((2, tile * SUBLANES, LANES), xn_tiles.dtype),
                        pltpu.VMEM((pad * SUBLANES, LANES), xn_tiles.dtype),
                        pltpu.SemaphoreType.DMA((2, MOE_TOP_K)), pltpu.SemaphoreType.DMA((1,))],
        compiler_params=_cparams(("arbitrary",)), name="dispatch")(*pos_slots, xn_tiles)


def _experts_body(it_exp, it_row, it_cls, it_first, it_next, n_items, xs_hbm, wg_hbm, wu_hbm, wd_hbm, ys_hbm,
                  xbuf, ybuf, wg_s, wu_s, wd_s, wg_b, wu_b, wd_b, sem_in, sem_out, sem_w):
    j = pl.program_id(0)
    n = n_items[0]
    pad = EXPERT_WINDOWS[-1]
    slot = lax.rem(j, 2)

    def weight_copies(e, s):
        return [pltpu.make_async_copy(hbm.at[e], stage.at[s], sem_w.at[s, t])
                for t, (hbm, stage) in enumerate(((wg_hbm, wg_s), (wu_hbm, wu_s), (wd_hbm, wd_s)))]

    def by_size(item, fn):
        for ci, m in enumerate(EXPERT_WINDOWS):
            pl.when(it_cls[item] == ci)(functools.partial(fn, m))

    def window(item, m):
        return pl.ds(pl.multiple_of(it_row[item] * SUBLANES, SUBLANES), m * SUBLANES)

    def in_copy(item, s, m):
        return pltpu.make_async_copy(xs_hbm.at[window(item, m)], xbuf.at[s, pl.ds(0, m * SUBLANES)], sem_in.at[s])

    def out_copy(item, s, m):
        return pltpu.make_async_copy(ybuf.at[s, pl.ds(0, m * SUBLANES)], ys_hbm.at[window(item, m)], sem_out.at[s])

    def compute(m):
        x = _load_token_tiles(xbuf, m, (slot,)).astype(BF16)
        hg = jnp.dot(x, wg_b[...], preferred_element_type=F32)
        hu = jnp.dot(x, wu_b[...], preferred_element_type=F32)
        hid = (jax.nn.silu(hg) * hu).astype(BF16)
        _store_token_tiles(ybuf, jnp.dot(hid, wd_b[...], preferred_element_type=F32), (slot,))

    @pl.when(j < n)
    def _():
        @pl.when(j == 0)
        def _():
            by_size(0, lambda m: in_copy(0, 0, m).start())
            for c in weight_copies(it_exp[0], it_first[0] - 1):
                c.start()
            tail_rows = pl.ds(0, pad * SUBLANES)
            ybuf[1, tail_rows, :] = jnp.zeros((pad * SUBLANES, LANES), F32)
            tail = pltpu.make_async_copy(
                ybuf.at[1, tail_rows], ys_hbm.at[pl.ds(ys_hbm.shape[0] - pad * SUBLANES, pad * SUBLANES)],
                sem_out.at[1])
            tail.start()
            tail.wait()

        @pl.when(j + 1 < n)
        def _():
            by_size(j + 1, lambda m: in_copy(j + 1, 1 - slot, m).start())

        @pl.when(it_first[j] > 0)
        def _():
            s = it_first[j] - 1
            for c in weight_copies(it_exp[j], s):
                c.wait()
            wg_b[...] = wg_s[s].astype(BF16)
            wu_b[...] = wu_s[s].astype(BF16)
            wd_b[...] = wd_s[s].astype(BF16)

            @pl.when(it_next[j] >= 0)
            def _():
                for c in weight_copies(it_next[j], 1 - s):
                    c.start()

        by_size(j, lambda m: in_copy(j, slot, m).wait())
        by_size(j, compute)

        @pl.when(j > 0)
        def _():
            by_size(j - 1, lambda m: out_copy(j - 1, 1 - slot, m).wait())

        by_size(j, lambda m: out_copy(j, slot, m).start())

        @pl.when(j == n - 1)
        def _():
            by_size(j, lambda m: out_copy(j, slot, m).wait())


def _experts(items, xs, wg, wu, wd):
    d, de = wg.shape[1], wg.shape[2]
    tm = EXPERT_TILE
    max_items = items[0].shape[0]
    grid_spec = pltpu.PrefetchScalarGridSpec(
        num_scalar_prefetch=6, grid=(max_items,),
        in_specs=[pl.BlockSpec(memory_space=pl.ANY)] * 4,
        out_specs=pl.BlockSpec(memory_space=pl.ANY),
        scratch_shapes=[pltpu.VMEM((2, tm * SUBLANES, LANES), F32), pltpu.VMEM((2, tm * SUBLANES, LANES), F32),
                        pltpu.VMEM((2, d, de), F32), pltpu.VMEM((2, d, de), F32), pltpu.VMEM((2, de, d), F32),
                        pltpu.VMEM((d, de), BF16), pltpu.VMEM((d, de), BF16), pltpu.VMEM((de, d), BF16),
                        pltpu.SemaphoreType.DMA((2,)), pltpu.SemaphoreType.DMA((2,)),
                        pltpu.SemaphoreType.DMA((2, 3))])
    return pl.pallas_call(
        _experts_body, grid_spec=grid_spec, out_shape=jax.ShapeDtypeStruct(xs.shape, F32),
        compiler_params=_cparams(("arbitrary",)), name="experts")(*items, xs, wg, wu, wd)


def _combine_body(*refs, final_norm):
    pos_refs, pos_next_refs = refs[:MOE_TOP_K], refs[MOE_TOP_K:2 * MOE_TOP_K]
    h_ref, w_ref, g_ref, ys_hbm, y_ref, buf, sem = refs[2 * MOE_TOP_K:]
    i = pl.program_id(0)
    tile = h_ref.shape[0]
    par = lax.rem(i, 2)

    def gather(tables, slot):
        def issue(r, carry):
            for k in range(MOE_TOP_K):
                p = tables[k][r]
                _row_copy(ys_hbm.at[_token_rows(p)], buf.at[slot, k, _token_rows(r)],
                          sem.at[slot, k]).start(priority=k)
            return carry

        lax.fori_loop(0, tile, issue, 0, unroll=ROW_COPY_UNROLL)

    @pl.when(i == 0)
    def _():
        gather(pos_refs, 0)

    for k in range(MOE_TOP_K):
        _row_copy(ys_hbm.at[pl.ds(0, tile * SUBLANES)], buf.at[par, k], sem.at[par, k]).wait()

    def combine_tile():
        h = h_ref[...] + (w_ref[:, 0:1] * _load_token_tiles(buf, tile, (par, 0))
                          + w_ref[:, 1:2] * _load_token_tiles(buf, tile, (par, 1)))
        y_ref[...] = _rmsnorm(h, g_ref[...]) if final_norm else h

    for slot in range(2):
        @pl.when(jnp.logical_and(i + 1 < pl.num_programs(0), par != slot))
        def _(slot=slot):
            for r in range(tile):
                for k in range(MOE_TOP_K):
                    p = pos_next_refs[k][r]
                    _row_copy(ys_hbm.at[_token_rows(p)], buf.at[slot, k, pl.ds(r * SUBLANES, SUBLANES)],
                              sem.at[slot, k]).start(priority=k)
            combine_tile()

    @pl.when(i + 1 == pl.num_programs(0))
    def _():
        combine_tile()


def _combine(pos_slots, h_all, w_t, g, ys, rows, row_off, final_norm):
    d = h_all.shape[1]
    tile = COMBINE_TILE
    off = row_off // tile
    last_block = h_all.shape[0] // tile - 1
    this_tile = pl.BlockSpec((tile,), lambda i: (off + i,), memory_space=pltpu.SMEM)
    next_tile = pl.BlockSpec((tile,), lambda i: (jnp.minimum(off + i + 1, last_block),), memory_space=pltpu.SMEM)
    return pl.pallas_call(
        functools.partial(_combine_body, final_norm=final_norm), grid=(rows // tile,),
        in_specs=[this_tile] * MOE_TOP_K + [next_tile] * MOE_TOP_K + [
                  pl.BlockSpec((tile, d), lambda i: (off + i, 0)),
                  pl.BlockSpec((tile, MOE_TOP_K), lambda i: (off + i, 0)),
                  _resident(g.shape),
                  pl.BlockSpec(memory_space=pl.ANY)],
        out_specs=pl.BlockSpec((tile, d), lambda i: (i, 0)),
        out_shape=jax.ShapeDtypeStruct((rows, d), F32),
        scratch_shapes=[pltpu.VMEM((2, MOE_TOP_K, tile * SUBLANES, LANES), F32),
                        pltpu.SemaphoreType.DMA((2, MOE_TOP_K))],
        compiler_params=_cparams(("arbitrary",)), name="combine")(*pos_slots, *pos_slots, h_all, w_t, g, ys)


def _lookup(table, idx):
    sel = idx[None] == jnp.arange(table.shape[0], dtype=I32).reshape((-1,) + (1,) * idx.ndim)
    return jnp.sum(jnp.where(sel, table.reshape(sel.shape[:1] + (1,) * idx.ndim), 0), axis=0)


def _work_items(counts, n_sorted):
    big, mid, small = EXPERT_WINDOWS
    n_exp = counts.shape[0]
    max_items = n_sorted // big + 2 * n_exp
    ends = jnp.cumsum(counts)
    starts = ends - counts
    units = (counts % big + small - 1) // small
    n_big = counts // big + (units == big // small)
    units = jnp.where(units == big // small, 0, units)
    n_mid = units // (mid // small)
    n_e = n_big + n_mid + units % (mid // small)
    item_end = jnp.cumsum(n_e)
    item_start = item_end - n_e
    n_items = item_end[-1]
    j = jnp.minimum(jnp.arange(max_items, dtype=I32), n_items - 1)
    e = jnp.sum((item_end[None, :] <= j[:, None]).astype(I32), axis=1)
    k = j - _lookup(item_start, e)
    nb, nm = _lookup(n_big, e), _lookup(n_mid, e)
    cls = jnp.where(k < nb, 0, jnp.where(k < nb + nm, 1, 2))
    row = _lookup(starts, e) + jnp.where(cls == 0, k * big, nb * big + jnp.where(cls == 1, 0, nm * mid))
    ordinal = jnp.cumsum((n_e > 0).astype(I32)) - 1
    first = jnp.where(k == 0, 1 + _lookup(ordinal, e) % 2, 0)
    nxt_item = _lookup(item_end, e)
    nxt = jnp.where(nxt_item < n_items, jnp.sum((item_end[None, :] <= nxt_item[:, None]).astype(I32), axis=1), -1)
    return (e, row.astype(I32), cls.astype(I32), first.astype(I32), nxt.astype(I32),
            n_items.reshape(1).astype(I32)), starts


def _s5_discretise(lam_re, lam_im, log_dt, b_re, b_im, c_re, c_im):
    g, p = lam_re.shape
    ch = b_re.shape[-1]
    lam_re = lam_re.astype(F32)
    lam_im = lam_im.astype(F32)
    dt = jnp.exp(log_dt.astype(F32))[:, None]
    mag = jnp.exp(lam_re * dt)
    ab_re = mag * jnp.cos(lam_im * dt)
    ab_im = mag * jnp.sin(lam_im * dt)
    den = lam_re * lam_re + lam_im * lam_im
    nr = ab_re - 1.0
    coef_re = (nr * lam_re + ab_im * lam_im) / den
    coef_im = (ab_im * lam_re - nr * lam_im) / den
    bb_re = coef_re[..., None] * b_re - coef_im[..., None] * b_im
    bb_im = coef_re[..., None] * b_im + coef_im[..., None] * b_re
    gh = g // 2
    eye = jnp.eye(gh, dtype=F32)

    def in_block(m):
        return jnp.einsum("gpc,gh->gchp", m, eye).reshape(gh * ch, gh * p)

    def out_block(m):
        return jnp.einsum("gcp,gh->gphc", m, eye).reshape(gh * p, gh * ch)

    bb = jnp.stack([jnp.concatenate([in_block(bb_re[k * gh:(k + 1) * gh]), in_block(bb_im[k * gh:(k + 1) * gh])],
                                    axis=1) for k in range(2)]).astype(BF16)
    cc = jnp.stack([jnp.concatenate([out_block(c_re[k * gh:(k + 1) * gh]), out_block(-c_im[k * gh:(k + 1) * gh])],
                                    axis=0) for k in range(2)]).astype(BF16)
    return ab_re.reshape(1, g * p), ab_im.reshape(1, g * p), bb, cc


def kernel(x_prompt, x_sample, state_hgrn, state_s5_re, state_s5_im, norm_mix_g, w_in, hgrn_lb_raw, hgrn_onorm_g, w_branch_a, s5_lambda_re, s5_lambda_im, s5_log_dt, s5_b_re, s5_b_im, s5_c_re, s5_c_im, s5_d, w_glu, b_glu, w_out, norm_ffn_g, w_router_group, b_router_group, w_router_expert, b_router_expert, w_exp_gate, w_exp_up, w_exp_down, norm_final_g):
    depth = norm_mix_g.shape[0]
    bp, lp, d = x_prompt.shape
    bs, ls, _ = x_sample.shape
    heads, dk = state_hgrn.shape[2], state_hgrn.shape[3]
    kw = heads * dk
    s5_groups, s5_state = state_s5_re.shape[2], state_s5_re.shape[3]
    s5_width = s5_d.shape[-1]
    nstate = s5_groups * s5_state
    moe_groups, _, experts = w_router_expert.shape[1:]
    n_exp = moe_groups * experts
    rows_p, rows_s = bp * lp, bs * ls
    total = rows_p + rows_s
    n_sorted = total * MOE_TOP_K
    assert kw == d and state_hgrn.shape[4] == dk, "column blocks assume key width == value width == model width"
    assert d == SUBLANES * LANES, "token-tile layout holds one token per (8, 128) tile"
    assert s5_groups % 2 == 0 and bp % SUBLANES == 0 and bs % HGRN_SEQ_TILE == 0
    assert (4 * kw) % s5_width == 0, "the S5 input columns must start on a multiple of their width"

    lb_all = jnp.cumsum(jax.nn.softmax(hgrn_lb_raw.astype(F32), axis=0), axis=0)

    hp = x_prompt.reshape(rows_p, d)
    hs = x_sample.reshape(rows_s, d)
    hg_p, re_p, im_p, hg_s, re_s, im_s = [], [], [], [], [], []
    zeros_state = jnp.zeros((bp // SUBLANES, SUBLANES, nstate), F32)

    for l in range(depth):
        w = w_in[l]
        g_mix = norm_mix_g[l].reshape(1, d)
        u_col = 4 * kw
        gate_col = u_col + s5_width
        proj = _in_proj(hp, hs, g_mix, w, u_col)

        ar, ai, bb, cc = _s5_discretise(s5_lambda_re[l], s5_lambda_im[l], s5_log_dt[l], s5_b_re[l], s5_b_im[l],
                                        s5_c_re[l], s5_c_im[l])
        s5_args = (ar, ai, bb, cc, s5_d[l].reshape(1, s5_width), w_glu[l].astype(BF16), b_glu[l].reshape(1, -1))
        yb_p, fr_p, fi_p = _s5_branch(hp, bp, lp, g_mix, w, u_col, zeros_state, zeros_state, *s5_args)
        yb_s, fr_s, fi_s = _s5_branch(hs, bs, ls, g_mix, w, u_col,
                                      state_s5_re[l].reshape(bs // SUBLANES, SUBLANES, nstate),
                                      state_s5_im[l].reshape(bs // SUBLANES, SUBLANES, nstate), *s5_args)

        lb = lb_all[l].reshape(1, kw)
        gn = hgrn_onorm_g[l].reshape(1, kw)
        o_p, hgp = _hgrn_long(proj, lb, gn, bp, lp, heads, dk, 0)
        o_s, hgs = _hgrn_short(proj, lb, gn, state_hgrn[l].astype(F32), ls, rows_p)

        nr = -(-(moe_groups + n_exp) // SUBLANES) * SUBLANES
        wr = jnp.concatenate([w_router_group[l].T, w_router_expert[l].transpose(0, 2, 1).reshape(n_exp, d)], axis=0)
        wr = jnp.pad(wr, ((0, nr - wr.shape[0]), (0, 0))).astype(BF16)
        br = jnp.pad(jnp.concatenate([b_router_group[l], b_router_expert[l].reshape(n_exp)]),
                     (0, nr - moe_groups - n_exp)).reshape(nr, 1).astype(F32)
        h_all, xn_all, ids, wts, ranks, cnt = _merge(
            o_p, o_s, yb_p.reshape(rows_p, d), yb_s.reshape(rows_s, d), hp, hs, g_mix, w, gate_col,
            w_branch_a[l].astype(BF16), w_out[l].astype(BF16), norm_ffn_g[l].reshape(1, d), wr, br,
            moe_groups, experts)
        items, starts = _work_items(cnt[:, 0].astype(I32), n_sorted)
        pos_slots = [_lookup(starts, ids[k]) + ranks[k] for k in range(MOE_TOP_K)]
        xs = _dispatch(pos_slots, xn_all)
        ys = _experts(items, xs, w_exp_gate[l], w_exp_up[l], w_exp_down[l])

        last = l == depth - 1
        g_out = norm_final_g.reshape(1, d)
        hp = _combine(pos_slots, h_all, wts.T, g_out, ys, rows_p, 0, last)
        hs = _combine(pos_slots, h_all, wts.T, g_out, ys, rows_s, rows_p, last)

        hg_p.append(hgp)
        hg_s.append(hgs)
        re_p.append(fr_p.reshape(bp, s5_groups, s5_state))
        im_p.append(fi_p.reshape(bp, s5_groups, s5_state))
        re_s.append(fr_s.reshape(bs, s5_groups, s5_state))
        im_s.append(fi_s.reshape(bs, s5_groups, s5_state))

    y_prompt = hp.reshape(bp, lp, d).astype(x_prompt.dtype)
    y_sample = hs.reshape(bs, ls, d).astype(x_sample.dtype)
    return (y_prompt, y_sample, jnp.stack(hg_p), jnp.stack(re_p), jnp.stack(im_p),
            jnp.stack(hg_s), jnp.stack(re_s), jnp.stack(im_s))
```

```python
import functools
import math

import jax
import jax.numpy as jnp
from jax import lax
from jax.experimental import pallas as pl
from jax.experimental.pallas import tpu as pltpu

F32 = jnp.float32
BF16 = jnp.bfloat16
I32 = jnp.int32

RMS_EPS = 1e-6
HG_CHUNK = 64
MOE_TOP_K = 2

V7X_VMEM_BYTES = 64 * 1024 * 1024
VMEM_LIMIT_BYTES = V7X_VMEM_BYTES - 8 * 1024 * 1024
SUBLANES = 8
LANES = 128

TOKEN_TILE = 512
DISPATCH_TILE = 1024
COMBINE_TILE = 512
EXPERT_TILE = 512
EXPERT_WINDOWS = (EXPERT_TILE, EXPERT_TILE // 2, EXPERT_TILE // 4)
S5_TIME_TILE = 128
S5_ROW_BLOCK = 1024
HGRN_TIME_TILE = 1024
HGRN_SEQ_TILE = 16
HGRN_CHUNK_UNROLL = 16
HGRN_SEQ_UNROLL = 8
PROJ_COL_TILE = 512
S5_SCAN_LANES = 512
ROW_COPY_UNROLL = 8


def _cparams(sem):
    return pltpu.CompilerParams(dimension_semantics=sem, vmem_limit_bytes=VMEM_LIMIT_BYTES)


def _resident(shape):
    nd = len(shape)
    return pl.BlockSpec(shape, lambda *_: (0,) * nd, pipeline_mode=pl.Buffered(1))


def _rmsnorm(x, g):
    return x * lax.rsqrt(jnp.mean(x * x, axis=-1, keepdims=True) + RMS_EPS) * g


def _two_source_specs(tm, width, n_first):
    return [pl.BlockSpec((tm, width), lambda i: (jnp.minimum(i, n_first - 1), 0)),
            pl.BlockSpec((tm, width), lambda i: (jnp.maximum(i - n_first, 0), 0))]


def _pick(first_ref, second_ref, n_first):
    return jnp.where(pl.program_id(0) < n_first, first_ref[...], second_ref[...])


def _store_token_tiles(ref, x, lead=()):
    rows = x.shape[0]
    for c in range(SUBLANES):
        ref[lead + (pl.ds(c, rows, stride=SUBLANES), slice(None))] = x[:, c * LANES:(c + 1) * LANES]


def _load_token_tiles(ref, rows, lead=()):
    return jnp.concatenate([ref[lead + (pl.ds(c, rows, stride=SUBLANES), slice(None))] for c in range(SUBLANES)],
                           axis=-1)


def _inproj_body(xp_ref, xs_ref, g_ref, w_ref, o_ref, wb_scr, *, n_first):
    @pl.when(pl.program_id(0) == 0)
    def _():
        wb_scr[...] = w_ref[...].astype(BF16)

    xb = _rmsnorm(_pick(xp_ref, xs_ref, n_first), g_ref[...]).astype(BF16)
    for j in range(0, wb_scr.shape[1], PROJ_COL_TILE):
        o_ref[:, j:j + PROJ_COL_TILE] = jnp.dot(xb, wb_scr[:, j:j + PROJ_COL_TILE], preferred_element_type=F32)


def _in_proj(xp, xs, g, w, n):
    d = xp.shape[1]
    tm = TOKEN_TILE
    total = xp.shape[0] + xs.shape[0]
    n_first = xp.shape[0] // tm
    return pl.pallas_call(
        functools.partial(_inproj_body, n_first=n_first), grid=(total // tm,),
        in_specs=_two_source_specs(tm, d, n_first) + [
            _resident((1, d)), pl.BlockSpec((d, n), lambda i: (0, 0), pipeline_mode=pl.Buffered(1))],
        out_specs=pl.BlockSpec((tm, n), lambda i: (i, 0)),
        out_shape=jax.ShapeDtypeStruct((total, n), F32),
        scratch_shapes=[pltpu.VMEM((d, n), BF16)],
        compiler_params=_cparams(("arbitrary",)), name="in_proj")(xp, xs, g, w)


def _s5_body(*refs, tt, groups, nstate, column_inputs):
    n_x = len(refs) - 19
    x_refs = refs[:n_x]
    (gm_ref, wu_ref, h0r_ref, h0i_ref, ar_ref, ai_ref, bb_ref, cc_ref, d_ref, wg_ref, bg_ref,
     y_ref, hr_out, hi_out, hr_scr, hi_scr, bu_scr, x_scr, y_scr) = refs[n_x:]
    j = pl.program_id(1)
    half = nstate // 2
    d = x_scr.shape[0] * LANES
    kw = wu_ref.shape[-1] // 2

    @pl.when(j == 0)
    def _():
        hr_scr[...] = h0r_ref[...]
        hi_scr[...] = h0i_ref[...]

    if column_inputs:
        for s in range(d // LANES):
            for g in range(groups):
                for t in range(tt):
                    r0 = (g * tt + t) * SUBLANES
                    x_scr[s, r0:r0 + SUBLANES, :] = x_refs[s][pl.ds(g * SUBLANES * tt + t, SUBLANES, stride=tt), :]
    else:
        for b in range(SUBLANES):
            xb = x_refs[b][...]
            for s in range(d // LANES):
                x_scr[s, pl.ds(b, tt, stride=SUBLANES), :] = xb[:, s * LANES:(s + 1) * LANES]
    x = jnp.concatenate([x_scr[s] for s in range(d // LANES)], axis=-1)
    u = jnp.dot(_rmsnorm(x, gm_ref[...]).astype(BF16), wu_ref[...].astype(BF16), preferred_element_type=F32)
    ub16 = u.astype(BF16)
    for kt in range(2):
        ukt = ub16[:, kt * kw:(kt + 1) * kw]
        bu_scr[:, kt * half:(kt + 1) * half] = jnp.dot(ukt, bb_ref[kt, :, :half], preferred_element_type=F32)
        bu_scr[:, nstate + kt * half:nstate + (kt + 1) * half] = jnp.dot(
            ukt, bb_ref[kt, :, half:], preferred_element_type=F32)

    lane_chunk = S5_SCAN_LANES
    for lc in range(nstate // lane_chunk):
        lo = lc * lane_chunk
        re_sl = slice(lo, lo + lane_chunk)
        im_sl = slice(nstate + lo, nstate + lo + lane_chunk)
        ar = jnp.broadcast_to(ar_ref[:, re_sl], (SUBLANES, lane_chunk))
        ai = jnp.broadcast_to(ai_ref[:, re_sl], (SUBLANES, lane_chunk))

        for g in range(groups):
            hr, hi = hr_scr[g, :, re_sl], hi_scr[g, :, re_sl]
            for t in range(tt):
                r0 = (g * tt + t) * SUBLANES
                rs = slice(r0, r0 + SUBLANES)
                hr, hi = (ar * hr - ai * hi + bu_scr[rs, re_sl], ar * hi + ai * hr + bu_scr[rs, im_sl])
                bu_scr[rs, re_sl] = hr
                bu_scr[rs, im_sl] = hi
            hr_scr[g, :, re_sl] = hr
            hi_scr[g, :, re_sl] = hi

    dm = wg_ref.shape[-1] // 2
    ys = []
    for n in range(2):
        h_re = bu_scr[:, n * half:(n + 1) * half].astype(BF16)
        h_im = bu_scr[:, nstate + n * half:nstate + (n + 1) * half].astype(BF16)
        ys.append(jnp.dot(h_re, cc_ref[n, :half, :], preferred_element_type=F32)
                  + jnp.dot(h_im, cc_ref[n, half:, :], preferred_element_type=F32))
    y = jnp.concatenate(ys, axis=-1) + d_ref[...] * u
    z = jnp.dot(jax.nn.gelu(y).astype(BF16), wg_ref[...], preferred_element_type=F32) + bg_ref[...]
    yb = z[:, :dm] * jax.nn.sigmoid(z[:, dm:])
    for s in range(dm // LANES):
        y_scr[s] = yb[:, s * LANES:(s + 1) * LANES]
    for g in range(groups):
        for b in range(SUBLANES):
            for s in range(dm // LANES):
                y_ref[g * SUBLANES + b, :, s * LANES:(s + 1) * LANES] = y_scr[
                    s, pl.ds(g * tt * SUBLANES + b, tt, stride=SUBLANES), :]

    @pl.when(j == pl.num_programs(1) - 1)
    def _():
        hr_out[...] = hr_scr[...]
        hi_out[...] = hi_scr[...]


def _s5_branch(x2d, batch, seq, g_mix, w_in, u_col, h0r, h0i, ar, ai, bb, cc, d_skip, w_glu, b_glu):
    nstate = ar.shape[-1]
    d = x2d.shape[1]
    dm = w_glu.shape[1] // 2
    tt = min(S5_TIME_TILE, seq)
    nj = seq // tt
    ngroups = batch // SUBLANES
    column_inputs = nj == 1
    groups = min(ngroups, max(1, S5_ROW_BLOCK // (tt * SUBLANES))) if column_inputs else 1
    rows = groups * tt * SUBLANES
    body = functools.partial(_s5_body, tt=tt, groups=groups, nstate=nstate, column_inputs=column_inputs)
    if column_inputs:
        x_specs = [pl.BlockSpec((rows, LANES), lambda i, j, s=s: (i, s)) for s in range(d // LANES)]
    else:
        x_specs = [pl.BlockSpec((tt, d), lambda i, j, b=b: ((i * SUBLANES + b) * nj + j, 0))
                   for b in range(SUBLANES)]
    state_spec = pl.BlockSpec((groups, SUBLANES, nstate), lambda i, j: (i, 0, 0))
    width = d_skip.shape[-1]
    return pl.pallas_call(
        body, grid=(ngroups // groups, nj),
        in_specs=x_specs + [
            _resident(g_mix.shape),
            pl.BlockSpec((d, width), lambda i, j: (0, u_col // width), pipeline_mode=pl.Buffered(1)),
            state_spec, state_spec, _resident(ar.shape), _resident(ai.shape), _resident(bb.shape),
            _resident(cc.shape), _resident(d_skip.shape), _resident(w_glu.shape), _resident(b_glu.shape)],
        out_specs=[pl.BlockSpec((groups * SUBLANES, tt, dm), lambda i, j: (i, j, 0)), state_spec, state_spec],
        out_shape=[jax.ShapeDtypeStruct((batch, seq, dm), F32),
                   jax.ShapeDtypeStruct((ngroups, SUBLANES, nstate), F32),
                   jax.ShapeDtypeStruct((ngroups, SUBLANES, nstate), F32)],
        scratch_shapes=[pltpu.VMEM((groups, SUBLANES, nstate), F32), pltpu.VMEM((groups, SUBLANES, nstate), F32),
                        pltpu.VMEM((rows, 2 * nstate), F32),
                        pltpu.VMEM((d // LANES, rows, LANES), F32),
                        pltpu.VMEM((dm // LANES, rows, LANES), F32)],
        compiler_params=_cparams(("parallel", "arbitrary")), name="s5_branch")(
            *([x2d] * len(x_specs)), g_mix, w_in, h0r, h0i, ar, ai, bb, cc, d_skip, w_glu, b_glu)


def _cumsum_rows(x, c):
    row = lax.broadcasted_iota(I32, x.shape, 0) & (c - 1)
    s = 1
    while s < c:
        x = x + jnp.where(row >= s, pltpu.roll(x, s, axis=0), 0.0)
        s *= 2
    return x


def _hgrn_gates(q, fr, lb, scale, c):
    rows, n = q.shape
    f = lb + (1.0 - lb) * jax.nn.sigmoid(fr)
    k = 1.0 - f
    b = _cumsum_rows(jnp.log(f), c)
    b3 = b.reshape(rows // c, c, n)
    b_last = jnp.broadcast_to(b3[:, c - 1:c, :], b3.shape).reshape(rows, n)
    q_dec = (q * scale) * jnp.exp(b)
    k_dec = k * jnp.exp(-b)
    k_end = k * jnp.exp(b_last - b)
    return q_dec, k_dec, k_end, jnp.exp(b_last)


def _causal_scores(q_dec, k_dec):
    c = q_dec.shape[0]
    s = lax.dot_general(q_dec, k_dec, (((1,), (1,)), ((), ())), preferred_element_type=F32)
    keep = lax.broadcasted_iota(I32, (c, c), 0) >= lax.broadcasted_iota(I32, (c, c), 1)
    return jnp.where(keep, s, 0.0).astype(BF16)


def _gated_out(o, gn, og):
    o = o * lax.rsqrt(jnp.mean(o * o, axis=-1, keepdims=True) + RMS_EPS) * gn
    return (o * jax.nn.silu(og)).astype(BF16)


def _hgrn_long_body(q_ref, f_ref, v_ref, og_ref, lb_ref, gn_ref, o_ref, sfin_ref, st_scr, *, c, heads, dk, scale):
    j = pl.program_id(1)

    @pl.when(j == 0)
    def _():
        st_scr[...] = jnp.zeros_like(st_scr)

    def block_diag(a, b):
        z = jnp.zeros_like(a)
        return jnp.concatenate([jnp.concatenate([a, z], axis=1), jnp.concatenate([z, b], axis=1)], axis=0)

    nt = (((1,), (1,)), ((), ()))
    col = lax.broadcasted_iota(I32, (c, 2 * c), 1) & (c - 1)
    keep = lax.broadcasted_iota(I32, (c, 2 * c), 0) >= col

    def chunk(ci, carry):
        rs = pl.ds(pl.multiple_of(ci * c, c), c)
        for h0 in range(0, heads, 2):
            ps = slice(h0 * dk, (h0 + 2) * dk)
            q_dec, k_dec, k_end, decay = _hgrn_gates(q_ref[rs, ps], f_ref[rs, ps], lb_ref[:, ps], scale, c)
            q_dec = q_dec.astype(BF16)
            k_dec = k_dec.astype(BF16)
            k_end = k_end.astype(BF16)
            v = v_ref[rs, ps].astype(BF16)
            scores = lax.dot_general(q_dec, block_diag(k_dec[:, :dk], k_dec[:, dk:]), nt, preferred_element_type=F32)
            scores = jnp.where(keep, scores, 0.0).astype(BF16)
            st0, st1 = st_scr[h0], st_scr[h0 + 1]
            o = (lax.dot_general(q_dec, block_diag(st0.astype(BF16), st1.astype(BF16)), nt,
                                 preferred_element_type=F32)
                 + jnp.dot(scores, block_diag(v[:, :dk], v[:, dk:]), preferred_element_type=F32))
            upd = lax.dot_general(v, k_end, (((0,), (0,)), ((), ())), preferred_element_type=F32)
            for t, st in enumerate((st0, st1)):
                hs = slice(t * dk, (t + 1) * dk)
                gs = slice((h0 + t) * dk, (h0 + t + 1) * dk)
                st_scr[h0 + t] = decay[:1, hs] * st + upd[hs, hs]
                o_ref[rs, gs] = _gated_out(o[:, hs], gn_ref[:, gs], og_ref[rs, gs])
        return carry

    lax.fori_loop(0, q_ref.shape[0] // c, chunk, 0, unroll=HGRN_CHUNK_UNROLL)

    @pl.when(j == pl.num_programs(1) - 1)
    def _():
        for h in range(heads):
            sfin_ref[0, h] = st_scr[h].T


def _hgrn_long(proj, lb, gn, batch, seq, heads, dk, row_off):
    width = heads * dk
    tb = min(HGRN_TIME_TILE, seq)
    nj = seq // tb
    off = row_off // tb
    c = min(HG_CHUNK, seq)
    body = functools.partial(_hgrn_long_body, c=c, heads=heads, dk=dk, scale=dk ** -0.5)

    def col(k):
        return pl.BlockSpec((tb, width), lambda b, j, k=k: (off + b * nj + j, k))

    return pl.pallas_call(
        body, grid=(batch, nj),
        in_specs=[col(0), col(1), col(2), col(3), _resident(lb.shape), _resident(gn.shape)],
        out_specs=[pl.BlockSpec((tb, width), lambda b, j: (b * nj + j, 0)),
                   pl.BlockSpec((1, heads, dk, dk), lambda b, j: (b, 0, 0, 0))],
        out_shape=[jax.ShapeDtypeStruct((batch * seq, width), BF16),
                   jax.ShapeDtypeStruct((batch, heads, dk, dk), F32)],
        scratch_shapes=[pltpu.VMEM((heads, dk, dk), F32)],
        compiler_params=_cparams(("parallel", "arbitrary")), name="hgrn_long")(proj, proj, proj, proj, lb, gn)


def _hgrn_short_body(q_ref, f_ref, v_ref, og_ref, lb_ref, gn_ref, s0_ref, o_ref, snew_ref, *, c, heads, dk, scale):
    def one_seq(sq, carry):
        rs = pl.ds(pl.multiple_of(sq * c, c), c)
        for h in range(heads):
            hs = slice(h * dk, (h + 1) * dk)
            q_dec, k_dec, k_end, decay = _hgrn_gates(q_ref[rs, hs], f_ref[rs, hs], lb_ref[:, hs], scale, c)
            q_dec = q_dec.astype(BF16)
            v = v_ref[rs, hs].astype(BF16)
            scores = _causal_scores(q_dec, k_dec.astype(BF16))
            s0 = s0_ref[sq, h]
            o = (jnp.dot(q_dec, s0.astype(BF16), preferred_element_type=F32)
                 + jnp.dot(scores, v, preferred_element_type=F32))
            decay_col = jnp.broadcast_to(decay[:1], (dk, dk)).T
            snew_ref[sq, h] = decay_col * s0 + lax.dot_general(
                k_end.astype(BF16), v, (((0,), (0,)), ((), ())), preferred_element_type=F32)
            o_ref[rs, hs] = _gated_out(o, gn_ref[:, hs], og_ref[rs, hs])
        return carry

    lax.fori_loop(0, s0_ref.shape[0], one_seq, 0, unroll=HGRN_SEQ_UNROLL)


def _hgrn_short(proj, lb, gn, s0, seq, row_off):
    batch, heads, dk, _ = s0.shape
    width = heads * dk
    nb = HGRN_SEQ_TILE
    rows = nb * seq
    off = row_off // rows
    body = functools.partial(_hgrn_short_body, c=seq, heads=heads, dk=dk, scale=dk ** -0.5)

    def col(k):
        return pl.BlockSpec((rows, width), lambda i, k=k: (off + i, k))

    state_spec = pl.BlockSpec((nb, heads, dk, dk), lambda i: (i, 0, 0, 0))
    return pl.pallas_call(
        body, grid=(batch // nb,),
        in_specs=[col(0), col(1), col(2), col(3), _resident(lb.shape), _resident(gn.shape), state_spec],
        out_specs=[pl.BlockSpec((rows, width), lambda i: (i, 0)), state_spec],
        out_shape=[jax.ShapeDtypeStruct((batch * seq, width), BF16), jax.ShapeDtypeStruct(s0.shape, F32)],
        compiler_params=_cparams(("parallel",)), name="hgrn_short")(proj, proj, proj, proj, lb, gn, s0)


def _first_index_of_max(vals):
    m = vals[0]
    for v in vals[1:]:
        m = jnp.maximum(m, v)
    idx = jnp.full(m.shape, len(vals), I32)
    for e in range(len(vals) - 1, -1, -1):
        idx = jnp.where(vals[e] == m, e, idx)
    return m, idx


def _route_tile(lg, ids_ref, w_ref, rk_ref, cnt_ref, carry_scr, groups, experts):
    i = pl.program_id(0)
    tile = lg.shape[1]
    n_exp = groups * experts

    @pl.when(i == 0)
    def _():
        carry_scr[...] = jnp.zeros_like(carry_scr)

    gl = [lg[g:g + 1, :] for g in range(groups)]
    gmax, gidx = _first_index_of_max(gl)
    denom = jnp.exp(gl[0] - gmax)
    for g in range(1, groups):
        denom = denom + jnp.exp(gl[g] - gmax)
    g_w = 1.0 / denom

    el = []
    for e in range(experts):
        v = lg[groups + e:groups + e + 1, :]
        for g in range(1, groups):
            r = groups + g * experts + e
            v = jnp.where(gidx == g, lg[r:r + 1, :], v)
        el.append(v)
    v1, i1 = _first_index_of_max(el)
    rest = [jnp.where(i1 == e, -jnp.inf, el[e]) for e in range(experts)]
    v2, i2 = _first_index_of_max(rest)
    t = jnp.exp(v2 - v1)
    inv = 1.0 / (1.0 + t)
    e1 = gidx * experts + i1
    e2 = gidx * experts + i2

    erow = lax.broadcasted_iota(I32, (n_exp, tile), 0)
    oh1 = (erow == e1).astype(F32)
    oh2 = (erow == e2).astype(F32)
    oh = oh1 + oh2
    before = (lax.broadcasted_iota(I32, (tile, tile), 0) < lax.broadcasted_iota(I32, (tile, tile), 1))
    cnt = jnp.dot(oh.astype(BF16), before.astype(BF16), preferred_element_type=F32) + carry_scr[:, 0:1]
    ids_ref[0:1, :] = e1
    ids_ref[1:2, :] = e2
    w_ref[0:1, :] = inv * g_w
    w_ref[1:2, :] = (t * inv) * g_w
    rk_ref[0:1, :] = jnp.sum(oh1 * cnt, axis=0, keepdims=True).astype(I32)
    rk_ref[1:2, :] = jnp.sum(oh2 * cnt, axis=0, keepdims=True).astype(I32)
    carry_scr[...] = carry_scr[...] + jnp.sum(oh, axis=1, keepdims=True)

    @pl.when(i == pl.num_programs(0) - 1)
    def _():
        cnt_ref[...] = carry_scr[...]


def _merge_body(*refs, n_first, groups, experts, n_gate_blocks):
    op_ref, os_ref, ybp_ref, ybs_ref, xp_ref, xs_ref, gm_ref = refs[:7]
    gate_w_refs = refs[7:7 + n_gate_blocks]
    (wa_ref, wo_ref, gf_ref, wr_ref, br_ref, h_ref, xn_ref, ids_ref, w_ref, rk_ref, cnt_ref, carry_scr,
     wgt_ref) = refs[7 + n_gate_blocks:]
    d = h_ref.shape[1]

    @pl.when(pl.program_id(0) == 0)
    def _():
        gw = gate_w_refs[0].shape[1]
        for q, ref in enumerate(gate_w_refs):
            wgt_ref[:, q * gw:(q + 1) * gw] = ref[...].astype(BF16)

    x = _pick(xp_ref, xs_ref, n_first)
    gates = jnp.dot(_rmsnorm(x, gm_ref[...]).astype(BF16), wgt_ref[...], preferred_element_type=F32)
    y_a = jnp.dot(_pick(op_ref, os_ref, n_first), wa_ref[...], preferred_element_type=F32)
    merged = jax.nn.sigmoid(gates[:, :d]) * y_a + jax.nn.sigmoid(gates[:, d:]) * _pick(ybp_ref, ybs_ref, n_first)
    h = x + jnp.dot(merged.astype(BF16), wo_ref[...], preferred_element_type=F32)
    h_ref[...] = h
    xn = _rmsnorm(h, gf_ref[...])
    _store_token_tiles(xn_ref, xn)
    logits_t = lax.dot_general(wr_ref[...], xn.astype(BF16), (((1,), (1,)), ((), ())),
                               preferred_element_type=F32) + br_ref[...]
    _route_tile(logits_t, ids_ref, w_ref, rk_ref, cnt_ref, carry_scr, groups, experts)


def _merge(o_p, o_s, yb_p, yb_s, xp, xs, g_mix, w_in, gate_col, wa, wo, gf, wr, br, groups, experts):
    d = xp.shape[1]
    total = xp.shape[0] + xs.shape[0]
    tm = TOKEN_TILE
    n_first = xp.shape[0] // tm
    n_exp = groups * experts
    pair = _two_source_specs(tm, d, n_first)
    top = pl.BlockSpec((MOE_TOP_K, tm), lambda i: (0, i))
    gw = math.gcd(gate_col, 2 * d)
    n_gate_blocks = 2 * d // gw
    gate_specs = [pl.BlockSpec((d, gw), lambda i, q=q: (0, gate_col // gw + q), pipeline_mode=pl.Buffered(1))
                  for q in range(n_gate_blocks)]
    weights = [wa, wo, gf, wr, br]
    return pl.pallas_call(
        functools.partial(_merge_body, n_first=n_first, groups=groups, experts=experts,
                          n_gate_blocks=n_gate_blocks),
        grid=(total // tm,),
        in_specs=pair + pair + pair + [_resident(g_mix.shape)] + gate_specs + [_resident(w.shape) for w in weights],
        out_specs=[pl.BlockSpec((tm, d), lambda i: (i, 0)), pl.BlockSpec((tm * SUBLANES, LANES), lambda i: (i, 0)),
                   top, top, top, pl.BlockSpec((n_exp, LANES), lambda i: (0, 0))],
        out_shape=[jax.ShapeDtypeStruct((total, d), F32), jax.ShapeDtypeStruct((total * SUBLANES, LANES), F32),
                   jax.ShapeDtypeStruct((MOE_TOP_K, total), I32), jax.ShapeDtypeStruct((MOE_TOP_K, total), F32),
                   jax.ShapeDtypeStruct((MOE_TOP_K, total), I32), jax.ShapeDtypeStruct((n_exp, LANES), F32)],
        scratch_shapes=[pltpu.VMEM((n_exp, LANES), F32), pltpu.VMEM((d, 2 * d), BF16)],
        compiler_params=_cparams(("arbitrary",)), name="merge_route")(
            o_p, o_s, yb_p, yb_s, xp, xs, g_mix, *([w_in] * n_gate_blocks), *weights)


def _row_copy(src, dst, sem):
    return pltpu.make_async_copy(src, dst, sem)


def _token_rows(r):
    return pl.ds(pl.multiple_of(r * SUBLANES, SUBLANES), SUBLANES)


def _dispatch_body(*refs, n_sorted):
    pos_refs = refs[:MOE_TOP_K]
    x_ref, o_hbm, ring, zero_scr, sem, pad_sem = refs[MOE_TOP_K:]
    i = pl.program_id(0)
    tile = x_ref.shape[0] // SUBLANES
    par = lax.rem(i, 2)

    @pl.when(i == 0)
    def _():
        zero_scr[...] = jnp.zeros_like(zero_scr)
        pad = _row_copy(zero_scr, o_hbm.at[pl.ds(n_sorted * SUBLANES, zero_scr.shape[0])], pad_sem.at[0])
        pad.start()
        pad.wait()

    ring[par] = x_ref[...]

    def issue(r, carry):
        for k in range(MOE_TOP_K):
            p = pos_refs[k][r]
            _row_copy(ring.at[par, _token_rows(r)], o_hbm.at[_token_rows(p)], sem.at[par, k]).start(priority=k)
        return carry

    lax.fori_loop(0, tile, issue, 0, unroll=ROW_COPY_UNROLL)

    def drain(slot):
        for k in range(MOE_TOP_K):
            _row_copy(ring.at[slot], o_hbm.at[pl.ds(0, tile * SUBLANES)], sem.at[slot, k]).wait()

    @pl.when(i > 0)
    def _():
        drain(1 - par)

    @pl.when(i == pl.num_programs(0) - 1)
    def _():
        drain(par)


def _dispatch(pos_slots, xn_tiles):
    total = xn_tiles.shape[0] // SUBLANES
    tile = DISPATCH_TILE
    n_sorted = total * MOE_TOP_K
    pad = EXPERT_WINDOWS[-1]
    return pl.pallas_call(
        functools.partial(_dispatch_body, n_sorted=n_sorted), grid=(total // tile,),
        in_specs=[pl.BlockSpec((tile,), lambda i: (i,), memory_space=pltpu.SMEM)] * MOE_TOP_K + [
            pl.BlockSpec((tile * SUBLANES, LANES), lambda i: (i, 0))],
        out_specs=pl.BlockSpec(memory_space=pl.ANY),
        out_shape=jax.ShapeDtypeStruct(((n_sorted + pad) * SUBLANES, LANES), xn_tiles.dtype),
        scratch_shapes=[pltpu.VMEM((2, tile * SUBLANES, LANES), xn_tiles.dtype),
                        pltpu.VMEM((pad * SUBLANES, LANES), xn_tiles.dtype),
                        pltpu.SemaphoreType.DMA((2, MOE_TOP_K)), pltpu.SemaphoreType.DMA((1,))],
        compiler_params=_cparams(("arbitrary",)), name="dispatch")(*pos_slots, xn_tiles)


def _experts_body(it_exp, it_row, it_cls, it_first, it_next, n_items, xs_hbm, wg_hbm, wu_hbm, wd_hbm, ys_hbm,
                  xbuf, ybuf, wg_s, wu_s, wd_s, wg_b, wu_b, wd_b, sem_in, sem_out, sem_w):
    j = pl.program_id(0)
    n = n_items[0]
    pad = EXPERT_WINDOWS[-1]
    slot = lax.rem(j, 2)

    def weight_copies(e, s):
        return [pltpu.make_async_copy(hbm.at[e], stage.at[s], sem_w.at[s, t])
                for t, (hbm, stage) in enumerate(((wg_hbm, wg_s), (wu_hbm, wu_s), (wd_hbm, wd_s)))]

    def by_size(item, fn):
        for ci, m in enumerate(EXPERT_WINDOWS):
            pl.when(it_cls[item] == ci)(functools.partial(fn, m))

    def window(item, m):
        return pl.ds(pl.multiple_of(it_row[item] * SUBLANES, SUBLANES), m * SUBLANES)

    def in_copy(item, s, m):
        return pltpu.make_async_copy(xs_hbm.at[window(item, m)], xbuf.at[s, pl.ds(0, m * SUBLANES)], sem_in.at[s])

    def out_copy(item, s, m):
        return pltpu.make_async_copy(ybuf.at[s, pl.ds(0, m * SUBLANES)], ys_hbm.at[window(item, m)], sem_out.at[s])

    def compute(m):
        x = _load_token_tiles(xbuf, m, (slot,)).astype(BF16)
        hg = jnp.dot(x, wg_b[...], preferred_element_type=F32)
        hu = jnp.dot(x, wu_b[...], preferred_element_type=F32)
        hid = (jax.nn.silu(hg) * hu).astype(BF16)
        _store_token_tiles(ybuf, jnp.dot(hid, wd_b[...], preferred_element_type=F32), (slot,))

    @pl.when(j < n)
    def _():
        @pl.when(j == 0)
        def _():
            by_size(0, lambda m: in_copy(0, 0, m).start())
            for c in weight_copies(it_exp[0], it_first[0] - 1):
                c.start()
            tail_rows = pl.ds(0, pad * SUBLANES)
            ybuf[1, tail_rows, :] = jnp.zeros((pad * SUBLANES, LANES), F32)
            tail = pltpu.make_async_copy(
                ybuf.at[1, tail_rows], ys_hbm.at[pl.ds(ys_hbm.shape[0] - pad * SUBLANES, pad * SUBLANES)],
                sem_out.at[1])
            tail.start()
            tail.wait()

        @pl.when(j + 1 < n)
        def _():
            by_size(j + 1, lambda m: in_copy(j + 1, 1 - slot, m).start())

        @pl.when(it_first[j] > 0)
        def _():
            s = it_first[j] - 1
            for c in weight_copies(it_exp[j], s):
                c.wait()
            wg_b[...] = wg_s[s].astype(BF16)
            wu_b[...] = wu_s[s].astype(BF16)
            wd_b[...] = wd_s[s].astype(BF16)

            @pl.when(it_next[j] >= 0)
            def _():
                for c in weight_copies(it_next[j], 1 - s):
                    c.start()

        by_size(j, lambda m: in_copy(j, slot, m).wait())
        by_size(j, compute)

        @pl.when(j > 0)
        def _():
            by_size(j - 1, lambda m: out_copy(j - 1, 1 - slot, m).wait())

        by_size(j, lambda m: out_copy(j, slot, m).start())

        @pl.when(j == n - 1)
        def _():
            by_size(j, lambda m: out_copy(j, slot, m).wait())


def _experts(items, xs, wg, wu, wd):
    d, de = wg.shape[1], wg.shape[2]
    tm = EXPERT_TILE
    max_items = items[0].shape[0]
    grid_spec = pltpu.PrefetchScalarGridSpec(
        num_scalar_prefetch=6, grid=(max_items,),
        in_specs=[pl.BlockSpec(memory_space=pl.ANY)] * 4,
        out_specs=pl.BlockSpec(memory_space=pl.ANY),
        scratch_shapes=[pltpu.VMEM((2, tm * SUBLANES, LANES), F32), pltpu.VMEM((2, tm * SUBLANES, LANES), F32),
                        pltpu.VMEM((2, d, de), F32), pltpu.VMEM((2, d, de), F32), pltpu.VMEM((2, de, d), F32),
                        pltpu.VMEM((d, de), BF16), pltpu.VMEM((d, de), BF16), pltpu.VMEM((de, d), BF16),
                        pltpu.SemaphoreType.DMA((2,)), pltpu.SemaphoreType.DMA((2,)),
                        pltpu.SemaphoreType.DMA((2, 3))])
    return pl.pallas_call(
        _experts_body, grid_spec=grid_spec, out_shape=jax.ShapeDtypeStruct(xs.shape, F32),
        compiler_params=_cparams(("arbitrary",)), name="experts")(*items, xs, wg, wu, wd)


def _combine_body(*refs, final_norm):
    pos_refs, pos_next_refs = refs[:MOE_TOP_K], refs[MOE_TOP_K:2 * MOE_TOP_K]
    h_ref, w_ref, g_ref, ys_hbm, y_ref, buf, sem = refs[2 * MOE_TOP_K:]
    i = pl.program_id(0)
    tile = h_ref.shape[0]
    par = lax.rem(i, 2)

    def gather(tables, slot):
        def issue(r, carry):
            for k in range(MOE_TOP_K):
                p = tables[k][r]
                _row_copy(ys_hbm.at[_token_rows(p)], buf.at[slot, k, _token_rows(r)],
                          sem.at[slot, k]).start(priority=k)
            return carry

        lax.fori_loop(0, tile, issue, 0, unroll=ROW_COPY_UNROLL)

    @pl.when(i == 0)
    def _():
        gather(pos_refs, 0)

    for k in range(MOE_TOP_K):
        _row_copy(ys_hbm.at[pl.ds(0, tile * SUBLANES)], buf.at[par, k], sem.at[par, k]).wait()

    def combine_tile():
        h = h_ref[...] + (w_ref[:, 0:1] * _load_token_tiles(buf, tile, (par, 0))
                          + w_ref[:, 1:2] * _load_token_tiles(buf, tile, (par, 1)))
        y_ref[...] = _rmsnorm(h, g_ref[...]) if final_norm else h

    for slot in range(2):
        @pl.when(jnp.logical_and(i + 1 < pl.num_programs(0), par != slot))
        def _(slot=slot):
            for r in range(tile):
                for k in range(MOE_TOP_K):
                    p = pos_next_refs[k][r]
                    _row_copy(ys_hbm.at[_token_rows(p)], buf.at[slot, k, pl.ds(r * SUBLANES, SUBLANES)],
                              sem.at[slot, k]).start(priority=k)
            combine_tile()

    @pl.when(i + 1 == pl.num_programs(0))
    def _():
        combine_tile()


def _combine(pos_slots, h_all, w_t, g, ys, rows, row_off, final_norm):
    d = h_all.shape[1]
    tile = COMBINE_TILE
    off = row_off // tile
    last_block = h_all.shape[0] // tile - 1
    this_tile = pl.BlockSpec((tile,), lambda i: (off + i,), memory_space=pltpu.SMEM)
    next_tile = pl.BlockSpec((tile,), lambda i: (jnp.minimum(off + i + 1, last_block),), memory_space=pltpu.SMEM)
    return pl.pallas_call(
        functools.partial(_combine_body, final_norm=final_norm), grid=(rows // tile,),
        in_specs=[this_tile] * MOE_TOP_K + [next_tile] * MOE_TOP_K + [
                  pl.BlockSpec((tile, d), lambda i: (off + i, 0)),
                  pl.BlockSpec((tile, MOE_TOP_K), lambda i: (off + i, 0)),
                  _resident(g.shape),
                  pl.BlockSpec(memory_space=pl.ANY)],
        out_specs=pl.BlockSpec((tile, d), lambda i: (i, 0)),
        out_shape=jax.ShapeDtypeStruct((rows, d), F32),
        scratch_shapes=[pltpu.VMEM((2, MOE_TOP_K, tile * SUBLANES, LANES), F32),
                        pltpu.SemaphoreType.DMA((2, MOE_TOP_K))],
        compiler_params=_cparams(("arbitrary",)), name="combine")(*pos_slots, *pos_slots, h_all, w_t, g, ys)


def _lookup(table, idx):
    sel = idx[None] == jnp.arange(table.shape[0], dtype=I32).reshape((-1,) + (1,) * idx.ndim)
    return jnp.sum(jnp.where(sel, table.reshape(sel.shape[:1] + (1,) * idx.ndim), 0), axis=0)


def _work_items(counts, n_sorted):
    big, mid, small = EXPERT_WINDOWS
    n_exp = counts.shape[0]
    max_items = n_sorted // big + 2 * n_exp
    ends = jnp.cumsum(counts)
    starts = ends - counts
    units = (counts % big + small - 1) // small
    n_big = counts // big + (units == big // small)
    units = jnp.where(units == big // small, 0, units)
    n_mid = units // (mid // small)
    n_e = n_big + n_mid + units % (mid // small)
    item_end = jnp.cumsum(n_e)
    item_start = item_end - n_e
    n_items = item_end[-1]
    j = jnp.minimum(jnp.arange(max_items, dtype=I32), n_items - 1)
    e = jnp.sum((item_end[None, :] <= j[:, None]).astype(I32), axis=1)
    k = j - _lookup(item_start, e)
    nb, nm = _lookup(n_big, e), _lookup(n_mid, e)
    cls = jnp.where(k < nb, 0, jnp.where(k < nb + nm, 1, 2))
    row = _lookup(starts, e) + jnp.where(cls == 0, k * big, nb * big + jnp.where(cls == 1, 0, nm * mid))
    ordinal = jnp.cumsum((n_e > 0).astype(I32)) - 1
    first = jnp.where(k == 0, 1 + _lookup(ordinal, e) % 2, 0)
    nxt_item = _lookup(item_end, e)
    nxt = jnp.where(nxt_item < n_items, jnp.sum((item_end[None, :] <= nxt_item[:, None]).astype(I32), axis=1), -1)
    return (e, row.astype(I32), cls.astype(I32), first.astype(I32), nxt.astype(I32),
            n_items.reshape(1).astype(I32)), starts


def _s5_discretise(lam_re, lam_im, log_dt, b_re, b_im, c_re, c_im):
    g, p = lam_re.shape
    ch = b_re.shape[-1]
    lam_re = lam_re.astype(F32)
    lam_im = lam_im.astype(F32)
    dt = jnp.exp(log_dt.astype(F32))[:, None]
    mag = jnp.exp(lam_re * dt)
    ab_re = mag * jnp.cos(lam_im * dt)
    ab_im = mag * jnp.sin(lam_im * dt)
    den = lam_re * lam_re + lam_im * lam_im
    nr = ab_re - 1.0
    coef_re = (nr * lam_re + ab_im * lam_im) / den
    coef_im = (ab_im * lam_re - nr * lam_im) / den
    bb_re = coef_re[..., None] * b_re - coef_im[..., None] * b_im
    bb_im = coef_re[..., None] * b_im + coef_im[..., None] * b_re
    gh = g // 2
    eye = jnp.eye(gh, dtype=F32)

    def in_block(m):
        return jnp.einsum("gpc,gh->gchp", m, eye).reshape(gh * ch, gh * p)

    def out_block(m):
        return jnp.einsum("gcp,gh->gphc", m, eye).reshape(gh * p, gh * ch)

    bb = jnp.stack([jnp.concatenate([in_block(bb_re[k * gh:(k + 1) * gh]), in_block(bb_im[k * gh:(k + 1) * gh])],
                                    axis=1) for k in range(2)]).astype(BF16)
    cc = jnp.stack([jnp.concatenate([out_block(c_re[k * gh:(k + 1) * gh]), out_block(-c_im[k * gh:(k + 1) * gh])],
                                    axis=0) for k in range(2)]).astype(BF16)
    return ab_re.reshape(1, g * p), ab_im.reshape(1, g * p), bb, cc


def kernel(x_prompt, x_sample, state_hgrn, state_s5_re, state_s5_im, norm_mix_g, w_in, hgrn_lb_raw, hgrn_onorm_g, w_branch_a, s5_lambda_re, s5_lambda_im, s5_log_dt, s5_b_re, s5_b_im, s5_c_re, s5_c_im, s5_d, w_glu, b_glu, w_out, norm_ffn_g, w_router_group, b_router_group, w_router_expert, b_router_expert, w_exp_gate, w_exp_up, w_exp_down, norm_final_g):
    depth = norm_mix_g.shape[0]
    bp, lp, d = x_prompt.shape
    bs, ls, _ = x_sample.shape
    heads, dk = state_hgrn.shape[2], state_hgrn.shape[3]
    kw = heads * dk
    s5_groups, s5_state = state_s5_re.shape[2], state_s5_re.shape[3]
    s5_width = s5_d.shape[-1]
    nstate = s5_groups * s5_state
    moe_groups, _, experts = w_router_expert.shape[1:]
    n_exp = moe_groups * experts
    rows_p, rows_s = bp * lp, bs * ls
    total = rows_p + rows_s
    n_sorted = total * MOE_TOP_K
    assert kw == d and state_hgrn.shape[4] == dk, "column blocks assume key width == value width == model width"
    assert d == SUBLANES * LANES, "token-tile layout holds one token per (8, 128) tile"
    assert s5_groups % 2 == 0 and bp % SUBLANES == 0 and bs % HGRN_SEQ_TILE == 0
    assert (4 * kw) % s5_width == 0, "the S5 input columns must start on a multiple of their width"

    lb_all = jnp.cumsum(jax.nn.softmax(hgrn_lb_raw.astype(F32), axis=0), axis=0)

    hp = x_prompt.reshape(rows_p, d)
    hs = x_sample.reshape(rows_s, d)
    hg_p, re_p, im_p, hg_s, re_s, im_s = [], [], [], [], [], []
    zeros_state = jnp.zeros((bp // SUBLANES, SUBLANES, nstate), F32)

    for l in range(depth):
        w = w_in[l]
        g_mix = norm_mix_g[l].reshape(1, d)
        u_col = 4 * kw
        gate_col = u_col + s5_width
        proj = _in_proj(hp, hs, g_mix, w, u_col)

        ar, ai, bb, cc = _s5_discretise(s5_lambda_re[l], s5_lambda_im[l], s5_log_dt[l], s5_b_re[l], s5_b_im[l],
                                        s5_c_re[l], s5_c_im[l])
        s5_args = (ar, ai, bb, cc, s5_d[l].reshape(1, s5_width), w_glu[l].astype(BF16), b_glu[l].reshape(1, -1))
        yb_p, fr_p, fi_p = _s5_branch(hp, bp, lp, g_mix, w, u_col, zeros_state, zeros_state, *s5_args)
        yb_s, fr_s, fi_s = _s5_branch(hs, bs, ls, g_mix, w, u_col,
                                      state_s5_re[l].reshape(bs // SUBLANES, SUBLANES, nstate),
                                      state_s5_im[l].reshape(bs // SUBLANES, SUBLANES, nstate), *s5_args)

        lb = lb_all[l].reshape(1, kw)
        gn = hgrn_onorm_g[l].reshape(1, kw)
        o_p, hgp = _hgrn_long(proj, lb, gn, bp, lp, heads, dk, 0)
        o_s, hgs = _hgrn_short(proj, lb, gn, state_hgrn[l].astype(F32), ls, rows_p)

        nr = -(-(moe_groups + n_exp) // SUBLANES) * SUBLANES
        wr = jnp.concatenate([w_router_group[l].T, w_router_expert[l].transpose(0, 2, 1).reshape(n_exp, d)], axis=0)
        wr = jnp.pad(wr, ((0, nr - wr.shape[0]), (0, 0))).astype(BF16)
        br = jnp.pad(jnp.concatenate([b_router_group[l], b_router_expert[l].reshape(n_exp)]),
                     (0, nr - moe_groups - n_exp)).reshape(nr, 1).astype(F32)
        h_all, xn_all, ids, wts, ranks, cnt = _merge(
            o_p, o_s, yb_p.reshape(rows_p, d), yb_s.reshape(rows_s, d), hp, hs, g_mix, w, gate_col,
            w_branch_a[l].astype(BF16), w_out[l].astype(BF16), norm_ffn_g[l].reshape(1, d), wr, br,
            moe_groups, experts)
        items, starts = _work_items(cnt[:, 0].astype(I32), n_sorted)
        pos_slots = [_lookup(starts, ids[k]) + ranks[k] for k in range(MOE_TOP_K)]
        xs = _dispatch(pos_slots, xn_all)
        ys = _experts(items, xs, w_exp_gate[l], w_exp_up[l], w_exp_down[l])

        last = l == depth - 1
        g_out = norm_final_g.reshape(1, d)
        hp = _combine(pos_slots, h_all, wts.T, g_out, ys, rows_p, 0, last)
        hs = _combine(pos_slots, h_all, wts.T, g_out, ys, rows_s, rows_p, last)

        hg_p.append(hgp)
        hg_s.append(hgs)
        re_p.append(fr_p.reshape(bp, s5_groups, s5_state))
        im_p.append(fi_p.reshape(bp, s5_groups, s5_state))
        re_s.append(fr_s.reshape(bs, s5_groups, s5_state))
        im_s.append(fi_s.reshape(bs, s5_groups, s5_state))

    y_prompt = hp.reshape(bp, lp, d).astype(x_prompt.dtype)
    y_sample = hs.reshape(bs, ls, d).astype(x_sample.dtype)
    return (y_prompt, y_sample, jnp.stack(hg_p), jnp.stack(re_p), jnp.stack(im_p),
            jnp.stack(hg_s), jnp.stack(re_s), jnp.stack(im_s))
```

```python
import functools
import math

import jax
import jax.numpy as jnp
from jax import lax
from jax.experimental import pallas as pl
from jax.experimental.pallas import tpu as pltpu

F32 = jnp.float32
BF16 = jnp.bfloat16
I32 = jnp.int32

RMS_EPS = 1e-6
HG_CHUNK = 64
MOE_TOP_K = 2

V7X_VMEM_BYTES = 64 * 1024 * 1024
VMEM_LIMIT_BYTES = V7X_VMEM_BYTES - 8 * 1024 * 1024
SUBLANES = 8
LANES = 128

TOKEN_TILE = 512
DISPATCH_TILE = 1024
COMBINE_TILE = 512
EXPERT_TILE = 512
EXPERT_WINDOWS = (EXPERT_TILE, EXPERT_TILE // 2, EXPERT_TILE // 4)
S5_TIME_TILE = 128
S5_ROW_BLOCK = 1024
HGRN_TIME_TILE = 1024
HGRN_SEQ_TILE = 16
HGRN_CHUNK_UNROLL = 16
HGRN_SEQ_UNROLL = 16
PROJ_COL_TILE = 512
S5_SCAN_LANES = 512
ROW_COPY_UNROLL = 8


def _cparams(sem):
    return pltpu.CompilerParams(dimension_semantics=sem, vmem_limit_bytes=VMEM_LIMIT_BYTES)


def _resident(shape):
    nd = len(shape)
    return pl.BlockSpec(shape, lambda *_: (0,) * nd, pipeline_mode=pl.Buffered(1))


def _rmsnorm(x, g):
    return x * lax.rsqrt(jnp.mean(x * x, axis=-1, keepdims=True) + RMS_EPS) * g


def _two_source_specs(tm, width, n_first):
    return [pl.BlockSpec((tm, width), lambda i: (jnp.minimum(i, n_first - 1), 0)),
            pl.BlockSpec((tm, width), lambda i: (jnp.maximum(i - n_first, 0), 0))]


def _pick(first_ref, second_ref, n_first):
    return jnp.where(pl.program_id(0) < n_first, first_ref[...], second_ref[...])


def _store_token_tiles(ref, x, lead=()):
    rows = x.shape[0]
    for c in range(SUBLANES):
        ref[lead + (pl.ds(c, rows, stride=SUBLANES), slice(None))] = x[:, c * LANES:(c + 1) * LANES]


def _load_token_tiles(ref, rows, lead=()):
    return jnp.concatenate([ref[lead + (pl.ds(c, rows, stride=SUBLANES), slice(None))] for c in range(SUBLANES)],
                           axis=-1)


def _inproj_body(xp_ref, xs_ref, g_ref, w_ref, o_ref, wb_scr, *, n_first):
    @pl.when(pl.program_id(0) == 0)
    def _():
        wb_scr[...] = w_ref[...].astype(BF16)

    xb = _rmsnorm(_pick(xp_ref, xs_ref, n_first), g_ref[...]).astype(BF16)
    for j in range(0, wb_scr.shape[1], PROJ_COL_TILE):
        o_ref[:, j:j + PROJ_COL_TILE] = jnp.dot(xb, wb_scr[:, j:j + PROJ_COL_TILE], preferred_element_type=F32)


def _in_proj(xp, xs, g, w, n):
    d = xp.shape[1]
    tm = TOKEN_TILE
    total = xp.shape[0] + xs.shape[0]
    n_first = xp.shape[0] // tm
    return pl.pallas_call(
        functools.partial(_inproj_body, n_first=n_first), grid=(total // tm,),
        in_specs=_two_source_specs(tm, d, n_first) + [
            _resident((1, d)), pl.BlockSpec((d, n), lambda i: (0, 0), pipeline_mode=pl.Buffered(1))],
        out_specs=pl.BlockSpec((tm, n), lambda i: (i, 0)),
        out_shape=jax.ShapeDtypeStruct((total, n), F32),
        scratch_shapes=[pltpu.VMEM((d, n), BF16)],
        compiler_params=_cparams(("arbitrary",)), name="in_proj")(xp, xs, g, w)


def _s5_body(*refs, tt, groups, nstate, column_inputs):
    n_x = len(refs) - 19
    x_refs = refs[:n_x]
    (gm_ref, wu_ref, h0r_ref, h0i_ref, ar_ref, ai_ref, bb_ref, cc_ref, d_ref, wg_ref, bg_ref,
     y_ref, hr_out, hi_out, hr_scr, hi_scr, bu_scr, x_scr, y_scr) = refs[n_x:]
    j = pl.program_id(1)
    half = nstate // 2
    d = x_scr.shape[0] * LANES
    kw = wu_ref.shape[-1] // 2

    @pl.when(j == 0)
    def _():
        hr_scr[...] = h0r_ref[...]
        hi_scr[...] = h0i_ref[...]

    if column_inputs:
        for s in range(d // LANES):
            for g in range(groups):
                for t in range(tt):
                    r0 = (g * tt + t) * SUBLANES
                    x_scr[s, r0:r0 + SUBLANES, :] = x_refs[s][pl.ds(g * SUBLANES * tt + t, SUBLANES, stride=tt), :]
    else:
        for b in range(SUBLANES):
            xb = x_refs[b][...]
            for s in range(d // LANES):
                x_scr[s, pl.ds(b, tt, stride=SUBLANES), :] = xb[:, s * LANES:(s + 1) * LANES]
    x = jnp.concatenate([x_scr[s] for s in range(d // LANES)], axis=-1)
    u = jnp.dot(_rmsnorm(x, gm_ref[...]).astype(BF16), wu_ref[...].astype(BF16), preferred_element_type=F32)
    ub16 = u.astype(BF16)
    for kt in range(2):
        ukt = ub16[:, kt * kw:(kt + 1) * kw]
        bu_scr[:, kt * half:(kt + 1) * half] = jnp.dot(ukt, bb_ref[kt, :, :half], preferred_element_type=F32)
        bu_scr[:, nstate + kt * half:nstate + (kt + 1) * half] = jnp.dot(
            ukt, bb_ref[kt, :, half:], preferred_element_type=F32)

    lane_chunk = S5_SCAN_LANES
    for lc in range(nstate // lane_chunk):
        lo = lc * lane_chunk
        re_sl = slice(lo, lo + lane_chunk)
        im_sl = slice(nstate + lo, nstate + lo + lane_chunk)
        ar = jnp.broadcast_to(ar_ref[:, re_sl], (SUBLANES, lane_chunk))
        ai = jnp.broadcast_to(ai_ref[:, re_sl], (SUBLANES, lane_chunk))

        for g in range(groups):
            hr, hi = hr_scr[g, :, re_sl], hi_scr[g, :, re_sl]
            for t in range(tt):
                r0 = (g * tt + t) * SUBLANES
                rs = slice(r0, r0 + SUBLANES)
                hr, hi = (ar * hr - ai * hi + bu_scr[rs, re_sl], ar * hi + ai * hr + bu_scr[rs, im_sl])
                bu_scr[rs, re_sl] = hr
                bu_scr[rs, im_sl] = hi
            hr_scr[g, :, re_sl] = hr
            hi_scr[g, :, re_sl] = hi

    dm = wg_ref.shape[-1] // 2
    ys = []
    for n in range(2):
        h_re = bu_scr[:, n * half:(n + 1) * half].astype(BF16)
        h_im = bu_scr[:, nstate + n * half:nstate + (n + 1) * half].astype(BF16)
        ys.append(jnp.dot(h_re, cc_ref[n, :half, :], preferred_element_type=F32)
                  + jnp.dot(h_im, cc_ref[n, half:, :], preferred_element_type=F32))
    y = jnp.concatenate(ys, axis=-1) + d_ref[...] * u
    z = jnp.dot(jax.nn.gelu(y).astype(BF16), wg_ref[...], preferred_element_type=F32) + bg_ref[...]
    yb = z[:, :dm] * jax.nn.sigmoid(z[:, dm:])
    for s in range(dm // LANES):
        y_scr[s] = yb[:, s * LANES:(s + 1) * LANES]
    for g in range(groups):
        for b in range(SUBLANES):
            for s in range(dm // LANES):
                y_ref[g * SUBLANES + b, :, s * LANES:(s + 1) * LANES] = y_scr[
                    s, pl.ds(g * tt * SUBLANES + b, tt, stride=SUBLANES), :]

    @pl.when(j == pl.num_programs(1) - 1)
    def _():
        hr_out[...] = hr_scr[...]
        hi_out[...] = hi_scr[...]


def _s5_branch(x2d, batch, seq, g_mix, w_in, u_col, h0r, h0i, ar, ai, bb, cc, d_skip, w_glu, b_glu):
    nstate = ar.shape[-1]
    d = x2d.shape[1]
    dm = w_glu.shape[1] // 2
    tt = min(S5_TIME_TILE, seq)
    nj = seq // tt
    ngroups = batch // SUBLANES
    column_inputs = nj == 1
    groups = min(ngroups, max(1, S5_ROW_BLOCK // (tt * SUBLANES))) if column_inputs else 1
    rows = groups * tt * SUBLANES
    body = functools.partial(_s5_body, tt=tt, groups=groups, nstate=nstate, column_inputs=column_inputs)
    if column_inputs:
        x_specs = [pl.BlockSpec((rows, LANES), lambda i, j, s=s: (i, s)) for s in range(d // LANES)]
    else:
        x_specs = [pl.BlockSpec((tt, d), lambda i, j, b=b: ((i * SUBLANES + b) * nj + j, 0))
                   for b in range(SUBLANES)]
    state_spec = pl.BlockSpec((groups, SUBLANES, nstate), lambda i, j: (i, 0, 0))
    width = d_skip.shape[-1]
    return pl.pallas_call(
        body, grid=(ngroups // groups, nj),
        in_specs=x_specs + [
            _resident(g_mix.shape),
            pl.BlockSpec((d, width), lambda i, j: (0, u_col // width), pipeline_mode=pl.Buffered(1)),
            state_spec, state_spec, _resident(ar.shape), _resident(ai.shape), _resident(bb.shape),
            _resident(cc.shape), _resident(d_skip.shape), _resident(w_glu.shape), _resident(b_glu.shape)],
        out_specs=[pl.BlockSpec((groups * SUBLANES, tt, dm), lambda i, j: (i, j, 0)), state_spec, state_spec],
        out_shape=[jax.ShapeDtypeStruct((batch, seq, dm), F32),
                   jax.ShapeDtypeStruct((ngroups, SUBLANES, nstate), F32),
                   jax.ShapeDtypeStruct((ngroups, SUBLANES, nstate), F32)],
        scratch_shapes=[pltpu.VMEM((groups, SUBLANES, nstate), F32), pltpu.VMEM((groups, SUBLANES, nstate), F32),
                        pltpu.VMEM((rows, 2 * nstate), F32),
                        pltpu.VMEM((d // LANES, rows, LANES), F32),
                        pltpu.VMEM((dm // LANES, rows, LANES), F32)],
        compiler_params=_cparams(("parallel", "arbitrary")), name="s5_branch")(
            *([x2d] * len(x_specs)), g_mix, w_in, h0r, h0i, ar, ai, bb, cc, d_skip, w_glu, b_glu)


def _cumsum_rows(x, c):
    row = lax.broadcasted_iota(I32, x.shape, 0) & (c - 1)
    s = 1
    while s < c:
        x = x + jnp.where(row >= s, pltpu.roll(x, s, axis=0), 0.0)
        s *= 2
    return x


def _hgrn_gates(q, fr, lb, scale, c):
    rows, n = q.shape
    f = lb + (1.0 - lb) * jax.nn.sigmoid(fr)
    k = 1.0 - f
    b = _cumsum_rows(jnp.log(f), c)
    b3 = b.reshape(rows // c, c, n)
    b_last = jnp.broadcast_to(b3[:, c - 1:c, :], b3.shape).reshape(rows, n)
    q_dec = (q * scale) * jnp.exp(b)
    k_dec = k * jnp.exp(-b)
    k_end = k * jnp.exp(b_last - b)
    return q_dec, k_dec, k_end, jnp.exp(b_last)


def _causal_scores(q_dec, k_dec):
    c = q_dec.shape[0]
    s = lax.dot_general(q_dec, k_dec, (((1,), (1,)), ((), ())), preferred_element_type=F32)
    keep = lax.broadcasted_iota(I32, (c, c), 0) >= lax.broadcasted_iota(I32, (c, c), 1)
    return jnp.where(keep, s, 0.0).astype(BF16)


def _gated_out(o, gn, og):
    o = o * lax.rsqrt(jnp.mean(o * o, axis=-1, keepdims=True) + RMS_EPS) * gn
    return (o * jax.nn.silu(og)).astype(BF16)


def _hgrn_long_body(q_ref, f_ref, v_ref, og_ref, lb_ref, gn_ref, o_ref, sfin_ref, st_scr, *, c, heads, dk, scale):
    j = pl.program_id(1)

    @pl.when(j == 0)
    def _():
        st_scr[...] = jnp.zeros_like(st_scr)

    def block_diag(a, b):
        z = jnp.zeros_like(a)
        return jnp.concatenate([jnp.concatenate([a, z], axis=1), jnp.concatenate([z, b], axis=1)], axis=0)

    nt = (((1,), (1,)), ((), ()))
    col = lax.broadcasted_iota(I32, (c, 2 * c), 1) & (c - 1)
    keep = lax.broadcasted_iota(I32, (c, 2 * c), 0) >= col

    def chunk(ci, carry):
        rs = pl.ds(pl.multiple_of(ci * c, c), c)
        for h0 in range(0, heads, 2):
            ps = slice(h0 * dk, (h0 + 2) * dk)
            q_dec, k_dec, k_end, decay = _hgrn_gates(q_ref[rs, ps], f_ref[rs, ps], lb_ref[:, ps], scale, c)
            q_dec = q_dec.astype(BF16)
            k_dec = k_dec.astype(BF16)
            k_end = k_end.astype(BF16)
            v = v_ref[rs, ps].astype(BF16)
            scores = lax.dot_general(q_dec, block_diag(k_dec[:, :dk], k_dec[:, dk:]), nt, preferred_element_type=F32)
            scores = jnp.where(keep, scores, 0.0).astype(BF16)
            st0, st1 = st_scr[h0], st_scr[h0 + 1]
            o = (lax.dot_general(q_dec, block_diag(st0.astype(BF16), st1.astype(BF16)), nt,
                                 preferred_element_type=F32)
                 + jnp.dot(scores, block_diag(v[:, :dk], v[:, dk:]), preferred_element_type=F32))
            upd = lax.dot_general(v, k_end, (((0,), (0,)), ((), ())), preferred_element_type=F32)
            for t, st in enumerate((st0, st1)):
                hs = slice(t * dk, (t + 1) * dk)
                gs = slice((h0 + t) * dk, (h0 + t + 1) * dk)
                st_scr[h0 + t] = decay[:1, hs] * st + upd[hs, hs]
                o_ref[rs, gs] = _gated_out(o[:, hs], gn_ref[:, gs], og_ref[rs, gs])
        return carry

    lax.fori_loop(0, q_ref.shape[0] // c, chunk, 0, unroll=HGRN_CHUNK_UNROLL)

    @pl.when(j == pl.num_programs(1) - 1)
    def _():
        for h in range(heads):
            sfin_ref[0, h] = st_scr[h].T


def _hgrn_long(proj, lb, gn, batch, seq, heads, dk, row_off):
    width = heads * dk
    tb = min(HGRN_TIME_TILE, seq)
    nj = seq // tb
    off = row_off // tb
    c = min(HG_CHUNK, seq)
    body = functools.partial(_hgrn_long_body, c=c, heads=heads, dk=dk, scale=dk ** -0.5)

    def col(k):
        return pl.BlockSpec((tb, width), lambda b, j, k=k: (off + b * nj + j, k))

    return pl.pallas_call(
        body, grid=(batch, nj),
        in_specs=[col(0), col(1), col(2), col(3), _resident(lb.shape), _resident(gn.shape)],
        out_specs=[pl.BlockSpec((tb, width), lambda b, j: (b * nj + j, 0)),
                   pl.BlockSpec((1, heads, dk, dk), lambda b, j: (b, 0, 0, 0))],
        out_shape=[jax.ShapeDtypeStruct((batch * seq, width), BF16),
                   jax.ShapeDtypeStruct((batch, heads, dk, dk), F32)],
        scratch_shapes=[pltpu.VMEM((heads, dk, dk), F32)],
        compiler_params=_cparams(("parallel", "arbitrary")), name="hgrn_long")(proj, proj, proj, proj, lb, gn)


def _hgrn_short_body(q_ref, f_ref, v_ref, og_ref, lb_ref, gn_ref, s0_ref, o_ref, snew_ref, *, c, heads, dk, scale):
    def one_seq(sq, carry):
        rs = pl.ds(pl.multiple_of(sq * c, c), c)
        for h in range(heads):
            hs = slice(h * dk, (h + 1) * dk)
            q_dec, k_dec, k_end, decay = _hgrn_gates(q_ref[rs, hs], f_ref[rs, hs], lb_ref[:, hs], scale, c)
            q_dec = q_dec.astype(BF16)
            v = v_ref[rs, hs].astype(BF16)
            scores = _causal_scores(q_dec, k_dec.astype(BF16))
            s0 = s0_ref[sq, h]
            o = (jnp.dot(q_dec, s0.astype(BF16), preferred_element_type=F32)
                 + jnp.dot(scores, v, preferred_element_type=F32))
            decay_col = jnp.broadcast_to(decay[:1], (dk, dk)).T
            snew_ref[sq, h] = decay_col * s0 + lax.dot_general(
                k_end.astype(BF16), v, (((0,), (0,)), ((), ())), preferred_element_type=F32)
            o_ref[rs, hs] = _gated_out(o, gn_ref[:, hs], og_ref[rs, hs])
        return carry

    lax.fori_loop(0, s0_ref.shape[0], one_seq, 0, unroll=HGRN_SEQ_UNROLL)


def _hgrn_short(proj, lb, gn, s0, seq, row_off):
    batch, heads, dk, _ = s0.shape
    width = heads * dk
    nb = HGRN_SEQ_TILE
    rows = nb * seq
    off = row_off // rows
    body = functools.partial(_hgrn_short_body, c=seq, heads=heads, dk=dk, scale=dk ** -0.5)

    def col(k):
        return pl.BlockSpec((rows, width), lambda i, k=k: (off + i, k))

    state_spec = pl.BlockSpec((nb, heads, dk, dk), lambda i: (i, 0, 0, 0))
    return pl.pallas_call(
        body, grid=(batch // nb,),
        in_specs=[col(0), col(1), col(2), col(3), _resident(lb.shape), _resident(gn.shape), state_spec],
        out_specs=[pl.BlockSpec((rows, width), lambda i: (i, 0)), state_spec],
        out_shape=[jax.ShapeDtypeStruct((batch * seq, width), BF16), jax.ShapeDtypeStruct(s0.shape, F32)],
        compiler_params=_cparams(("parallel",)), name="hgrn_short")(proj, proj, proj, proj, lb, gn, s0)


def _first_index_of_max(vals):
    m = vals[0]
    for v in vals[1:]:
        m = jnp.maximum(m, v)
    idx = jnp.full(m.shape, len(vals), I32)
    for e in range(len(vals) - 1, -1, -1):
        idx = jnp.where(vals[e] == m, e, idx)
    return m, idx


def _route_tile(lg, ids_ref, w_ref, rk_ref, cnt_ref, carry_scr, groups, experts):
    i = pl.program_id(0)
    tile = lg.shape[1]
    n_exp = groups * experts

    @pl.when(i == 0)
    def _():
        carry_scr[...] = jnp.zeros_like(carry_scr)

    gl = [lg[g:g + 1, :] for g in range(groups)]
    gmax, gidx = _first_index_of_max(gl)
    denom = jnp.exp(gl[0] - gmax)
    for g in range(1, groups):
        denom = denom + jnp.exp(gl[g] - gmax)
    g_w = 1.0 / denom

    el = []
    for e in range(experts):
        v = lg[groups + e:groups + e + 1, :]
        for g in range(1, groups):
            r = groups + g * experts + e
            v = jnp.where(gidx == g, lg[r:r + 1, :], v)
        el.append(v)
    v1, i1 = _first_index_of_max(el)
    rest = [jnp.where(i1 == e, -jnp.inf, el[e]) for e in range(experts)]
    v2, i2 = _first_index_of_max(rest)
    t = jnp.exp(v2 - v1)
    inv = 1.0 / (1.0 + t)
    e1 = gidx * experts + i1
    e2 = gidx * experts + i2

    erow = lax.broadcasted_iota(I32, (n_exp, tile), 0)
    oh1 = (erow == e1).astype(F32)
    oh2 = (erow == e2).astype(F32)
    oh = oh1 + oh2
    before = (lax.broadcasted_iota(I32, (tile, tile), 0) < lax.broadcasted_iota(I32, (tile, tile), 1))
    cnt = jnp.dot(oh.astype(BF16), before.astype(BF16), preferred_element_type=F32) + carry_scr[:, 0:1]
    ids_ref[0:1, :] = e1
    ids_ref[1:2, :] = e2
    w_ref[0:1, :] = inv * g_w
    w_ref[1:2, :] = (t * inv) * g_w
    rk_ref[0:1, :] = jnp.sum(oh1 * cnt, axis=0, keepdims=True).astype(I32)
    rk_ref[1:2, :] = jnp.sum(oh2 * cnt, axis=0, keepdims=True).astype(I32)
    carry_scr[...] = carry_scr[...] + jnp.sum(oh, axis=1, keepdims=True)

    @pl.when(i == pl.num_programs(0) - 1)
    def _():
        cnt_ref[...] = carry_scr[...]


def _merge_body(*refs, n_first, groups, experts, n_gate_blocks):
    op_ref, os_ref, ybp_ref, ybs_ref, xp_ref, xs_ref, gm_ref = refs[:7]
    gate_w_refs = refs[7:7 + n_gate_blocks]
    (wa_ref, wo_ref, gf_ref, wr_ref, br_ref, h_ref, xn_ref, ids_ref, w_ref, rk_ref, cnt_ref, carry_scr,
     wgt_ref) = refs[7 + n_gate_blocks:]
    d = h_ref.shape[1]

    @pl.when(pl.program_id(0) == 0)
    def _():
        gw = gate_w_refs[0].shape[1]
        for q, ref in enumerate(gate_w_refs):
            wgt_ref[:, q * gw:(q + 1) * gw] = ref[...].astype(BF16)

    x = _pick(xp_ref, xs_ref, n_first)
    gates = jnp.dot(_rmsnorm(x, gm_ref[...]).astype(BF16), wgt_ref[...], preferred_element_type=F32)
    y_a = jnp.dot(_pick(op_ref, os_ref, n_first), wa_ref[...], preferred_element_type=F32)
    merged = jax.nn.sigmoid(gates[:, :d]) * y_a + jax.nn.sigmoid(gates[:, d:]) * _pick(ybp_ref, ybs_ref, n_first)
    h = x + jnp.dot(merged.astype(BF16), wo_ref[...], preferred_element_type=F32)
    h_ref[...] = h
    xn = _rmsnorm(h, gf_ref[...])
    _store_token_tiles(xn_ref, xn)
    logits_t = lax.dot_general(wr_ref[...], xn.astype(BF16), (((1,), (1,)), ((), ())),
                               preferred_element_type=F32) + br_ref[...]
    _route_tile(logits_t, ids_ref, w_ref, rk_ref, cnt_ref, carry_scr, groups, experts)


def _merge(o_p, o_s, yb_p, yb_s, xp, xs, g_mix, w_in, gate_col, wa, wo, gf, wr, br, groups, experts):
    d = xp.shape[1]
    total = xp.shape[0] + xs.shape[0]
    tm = TOKEN_TILE
    n_first = xp.shape[0] // tm
    n_exp = groups * experts
    pair = _two_source_specs(tm, d, n_first)
    top = pl.BlockSpec((MOE_TOP_K, tm), lambda i: (0, i))
    gw = math.gcd(gate_col, 2 * d)
    n_gate_blocks = 2 * d // gw
    gate_specs = [pl.BlockSpec((d, gw), lambda i, q=q: (0, gate_col // gw + q), pipeline_mode=pl.Buffered(1))
                  for q in range(n_gate_blocks)]
    weights = [wa, wo, gf, wr, br]
    return pl.pallas_call(
        functools.partial(_merge_body, n_first=n_first, groups=groups, experts=experts,
                          n_gate_blocks=n_gate_blocks),
        grid=(total // tm,),
        in_specs=pair + pair + pair + [_resident(g_mix.shape)] + gate_specs + [_resident(w.shape) for w in weights],
        out_specs=[pl.BlockSpec((tm, d), lambda i: (i, 0)), pl.BlockSpec((tm * SUBLANES, LANES), lambda i: (i, 0)),
                   top, top, top, pl.BlockSpec((n_exp, LANES), lambda i: (0, 0))],
        out_shape=[jax.ShapeDtypeStruct((total, d), F32), jax.ShapeDtypeStruct((total * SUBLANES, LANES), F32),
                   jax.ShapeDtypeStruct((MOE_TOP_K, total), I32), jax.ShapeDtypeStruct((MOE_TOP_K, total), F32),
                   jax.ShapeDtypeStruct((MOE_TOP_K, total), I32), jax.ShapeDtypeStruct((n_exp, LANES), F32)],
        scratch_shapes=[pltpu.VMEM((n_exp, LANES), F32), pltpu.VMEM((d, 2 * d), BF16)],
        compiler_params=_cparams(("arbitrary",)), name="merge_route")(
            o_p, o_s, yb_p, yb_s, xp, xs, g_mix, *([w_in] * n_gate_blocks), *weights)


def _row_copy(src, dst, sem):
    return pltpu.make_async_copy(src, dst, sem)


def _token_rows(r):
    return pl.ds(pl.multiple_of(r * SUBLANES, SUBLANES), SUBLANES)


def _dispatch_body(*refs, n_sorted):
    pos_refs = refs[:MOE_TOP_K]
    x_ref, o_hbm, ring, zero_scr, sem, pad_sem = refs[MOE_TOP_K:]
    i = pl.program_id(0)
    tile = x_ref.shape[0] // SUBLANES
    par = lax.rem(i, 2)

    @pl.when(i == 0)
    def _():
        zero_scr[...] = jnp.zeros_like(zero_scr)
        pad = _row_copy(zero_scr, o_hbm.at[pl.ds(n_sorted * SUBLANES, zero_scr.shape[0])], pad_sem.at[0])
        pad.start()
        pad.wait()

    ring[par] = x_ref[...]

    def issue(r, carry):
        for k in range(MOE_TOP_K):
            p = pos_refs[k][r]
            _row_copy(ring.at[par, _token_rows(r)], o_hbm.at[_token_rows(p)], sem.at[par, k]).start(priority=k)
        return carry

    lax.fori_loop(0, tile, issue, 0, unroll=ROW_COPY_UNROLL)

    def drain(slot):
        for k in range(MOE_TOP_K):
            _row_copy(ring.at[slot], o_hbm.at[pl.ds(0, tile * SUBLANES)], sem.at[slot, k]).wait()

    @pl.when(i > 0)
    def _():
        drain(1 - par)

    @pl.when(i == pl.num_programs(0) - 1)
    def _():
        drain(par)


def _dispatch(pos_slots, xn_tiles):
    total = xn_tiles.shape[0] // SUBLANES
    tile = DISPATCH_TILE
    n_sorted = total * MOE_TOP_K
    pad = EXPERT_WINDOWS[-1]
    return pl.pallas_call(
        functools.partial(_dispatch_body, n_sorted=n_sorted), grid=(total // tile,),
        in_specs=[pl.BlockSpec((tile,), lambda i: (i,), memory_space=pltpu.SMEM)] * MOE_TOP_K + [
            pl.BlockSpec((tile * SUBLANES, LANES), lambda i: (i, 0))],
        out_specs=pl.BlockSpec(memory_space=pl.ANY),
        out_shape=jax.ShapeDtypeStruct(((n_sorted + pad) * SUBLANES, LANES), xn_tiles.dtype),
        scratch_shapes=[pltpu.VMEM((2, tile * SUBLANES, LANES), xn_tiles.dtype),
                        pltpu.VMEM((pad * SUBLANES, LANES), xn_tiles.dtype),
                        pltpu.SemaphoreType.DMA((2, MOE_TOP_K)), pltpu.SemaphoreType.DMA((1,))],
        compiler_params=_cparams(("arbitrary",)), name="dispatch")(*pos_slots, xn_tiles)


def _experts_body(it_exp, it_row, it_cls, it_first, it_next, n_items, xs_hbm, wg_hbm, wu_hbm, wd_hbm, ys_hbm,
                  xbuf, ybuf, wg_s, wu_s, wd_s, wg_b, wu_b, wd_b, sem_in, sem_out, sem_w):
    j = pl.program_id(0)
    n = n_items[0]
    pad = EXPERT_WINDOWS[-1]
    slot = lax.rem(j, 2)

    def weight_copies(e, s):
        return [pltpu.make_async_copy(hbm.at[e], stage.at[s], sem_w.at[s, t])
                for t, (hbm, stage) in enumerate(((wg_hbm, wg_s), (wu_hbm, wu_s), (wd_hbm, wd_s)))]

    def by_size(item, fn):
        for ci, m in enumerate(EXPERT_WINDOWS):
            pl.when(it_cls[item] == ci)(functools.partial(fn, m))

    def window(item, m):
        return pl.ds(pl.multiple_of(it_row[item] * SUBLANES, SUBLANES), m * SUBLANES)

    def in_copy(item, s, m):
        return pltpu.make_async_copy(xs_hbm.at[window(item, m)], xbuf.at[s, pl.ds(0, m * SUBLANES)], sem_in.at[s])

    def out_copy(item, s, m):
        return pltpu.make_async_copy(ybuf.at[s, pl.ds(0, m * SUBLANES)], ys_hbm.at[window(item, m)], sem_out.at[s])

    def compute(m):
        x = _load_token_tiles(xbuf, m, (slot,)).astype(BF16)
        hg = jnp.dot(x, wg_b[...], preferred_element_type=F32)
        hu = jnp.dot(x, wu_b[...], preferred_element_type=F32)
        hid = (jax.nn.silu(hg) * hu).astype(BF16)
        _store_token_tiles(ybuf, jnp.dot(hid, wd_b[...], preferred_element_type=F32), (slot,))

    @pl.when(j < n)
    def _():
        @pl.when(j == 0)
        def _():
            by_size(0, lambda m: in_copy(0, 0, m).start())
            for c in weight_copies(it_exp[0], it_first[0] - 1):
                c.start()
            tail_rows = pl.ds(0, pad * SUBLANES)
            ybuf[1, tail_rows, :] = jnp.zeros((pad * SUBLANES, LANES), F32)
            tail = pltpu.make_async_copy(
                ybuf.at[1, tail_rows], ys_hbm.at[pl.ds(ys_hbm.shape[0] - pad * SUBLANES, pad * SUBLANES)],
                sem_out.at[1])
            tail.start()
            tail.wait()

        @pl.when(j + 1 < n)
        def _():
            by_size(j + 1, lambda m: in_copy(j + 1, 1 - slot, m).start())

        @pl.when(it_first[j] > 0)
        def _():
            s = it_first[j] - 1
            for c in weight_copies(it_exp[j], s):
                c.wait()
            wg_b[...] = wg_s[s].astype(BF16)
            wu_b[...] = wu_s[s].astype(BF16)
            wd_b[...] = wd_s[s].astype(BF16)

            @pl.when(it_next[j] >= 0)
            def _():
                for c in weight_copies(it_next[j], 1 - s):
                    c.start()

        by_size(j, lambda m: in_copy(j, slot, m).wait())
        by_size(j, compute)

        @pl.when(j > 0)
        def _():
            by_size(j - 1, lambda m: out_copy(j - 1, 1 - slot, m).wait())

        by_size(j, lambda m: out_copy(j, slot, m).start())

        @pl.when(j == n - 1)
        def _():
            by_size(j, lambda m: out_copy(j, slot, m).wait())


def _experts(items, xs, wg, wu, wd):
    d, de = wg.shape[1], wg.shape[2]
    tm = EXPERT_TILE
    max_items = items[0].shape[0]
    grid_spec = pltpu.PrefetchScalarGridSpec(
        num_scalar_prefetch=6, grid=(max_items,),
        in_specs=[pl.BlockSpec(memory_space=pl.ANY)] * 4,
        out_specs=pl.BlockSpec(memory_space=pl.ANY),
        scratch_shapes=[pltpu.VMEM((2, tm * SUBLANES, LANES), F32), pltpu.VMEM((2, tm * SUBLANES, LANES), F32),
                        pltpu.VMEM((2, d, de), F32), pltpu.VMEM((2, d, de), F32), pltpu.VMEM((2, de, d), F32),
                        pltpu.VMEM((d, de), BF16), pltpu.VMEM((d, de), BF16), pltpu.VMEM((de, d), BF16),
                        pltpu.SemaphoreType.DMA((2,)), pltpu.SemaphoreType.DMA((2,)),
                        pltpu.SemaphoreType.DMA((2, 3))])
    return pl.pallas_call(
        _experts_body, grid_spec=grid_spec, out_shape=jax.ShapeDtypeStruct(xs.shape, F32),
        compiler_params=_cparams(("arbitrary",)), name="experts")(*items, xs, wg, wu, wd)


def _combine_body(*refs, final_norm):
    pos_refs, pos_next_refs = refs[:MOE_TOP_K], refs[MOE_TOP_K:2 * MOE_TOP_K]
    h_ref, w_ref, g_ref, ys_hbm, y_ref, buf, sem = refs[2 * MOE_TOP_K:]
    i = pl.program_id(0)
    tile = h_ref.shape[0]
    par = lax.rem(i, 2)

    def gather(tables, slot):
        def issue(r, carry):
            for k in range(MOE_TOP_K):
                p = tables[k][r]
                _row_copy(ys_hbm.at[_token_rows(p)], buf.at[slot, k, _token_rows(r)],
                          sem.at[slot, k]).start(priority=k)
            return carry

        lax.fori_loop(0, tile, issue, 0, unroll=ROW_COPY_UNROLL)

    @pl.when(i == 0)
    def _():
        gather(pos_refs, 0)

    for k in range(MOE_TOP_K):
        _row_copy(ys_hbm.at[pl.ds(0, tile * SUBLANES)], buf.at[par, k], sem.at[par, k]).wait()

    def combine_tile():
        h = h_ref[...] + (w_ref[:, 0:1] * _load_token_tiles(buf, tile, (par, 0))
                          + w_ref[:, 1:2] * _load_token_tiles(buf, tile, (par, 1)))
        y_ref[...] = _rmsnorm(h, g_ref[...]) if final_norm else h

    for slot in range(2):
        @pl.when(jnp.logical_and(i + 1 < pl.num_programs(0), par != slot))
        def _(slot=slot):
            for r in range(tile):
                for k in range(MOE_TOP_K):
                    p = pos_next_refs[k][r]
                    _row_copy(ys_hbm.at[_token_rows(p)], buf.at[slot, k, pl.ds(r * SUBLANES, SUBLANES)],
                              sem.at[slot, k]).start(priority=k)
            combine_tile()

    @pl.when(i + 1 == pl.num_programs(0))
    def _():
        combine_tile()


def _combine(pos_slots, h_all, w_t, g, ys, rows, row_off, final_norm):
    d = h_all.shape[1]
    tile = COMBINE_TILE
    off = row_off // tile
    last_block = h_all.shape[0] // tile - 1
    this_tile = pl.BlockSpec((tile,), lambda i: (off + i,), memory_space=pltpu.SMEM)
    next_tile = pl.BlockSpec((tile,), lambda i: (jnp.minimum(off + i + 1, last_block),), memory_space=pltpu.SMEM)
    return pl.pallas_call(
        functools.partial(_combine_body, final_norm=final_norm), grid=(rows // tile,),
        in_specs=[this_tile] * MOE_TOP_K + [next_tile] * MOE_TOP_K + [
                  pl.BlockSpec((tile, d), lambda i: (off + i, 0)),
                  pl.BlockSpec((tile, MOE_TOP_K), lambda i: (off + i, 0)),
                  _resident(g.shape),
                  pl.BlockSpec(memory_space=pl.ANY)],
        out_specs=pl.BlockSpec((tile, d), lambda i: (i, 0)),
        out_shape=jax.ShapeDtypeStruct((rows, d), F32),
        scratch_shapes=[pltpu.VMEM((2, MOE_TOP_K, tile * SUBLANES, LANES), F32),
                        pltpu.SemaphoreType.DMA((2, MOE_TOP_K))],
        compiler_params=_cparams(("arbitrary",)), name="combine")(*pos_slots, *pos_slots, h_all, w_t, g, ys)


def _lookup(table, idx):
    sel = idx[None] == jnp.arange(table.shape[0], dtype=I32).reshape((-1,) + (1,) * idx.ndim)
    return jnp.sum(jnp.where(sel, table.reshape(sel.shape[:1] + (1,) * idx.ndim), 0), axis=0)


def _work_items(counts, n_sorted):
    big, mid, small = EXPERT_WINDOWS
    n_exp = counts.shape[0]
    max_items = n_sorted // big + 2 * n_exp
    ends = jnp.cumsum(counts)
    starts = ends - counts
    units = (counts % big + small - 1) // small
    n_big = counts // big + (units == big // small)
    units = jnp.where(units == big // small, 0, units)
    n_mid = units // (mid // small)
    n_e = n_big + n_mid + units % (mid // small)
    item_end = jnp.cumsum(n_e)
    item_start = item_end - n_e
    n_items = item_end[-1]
    j = jnp.minimum(jnp.arange(max_items, dtype=I32), n_items - 1)
    e = jnp.sum((item_end[None, :] <= j[:, None]).astype(I32), axis=1)
    k = j - _lookup(item_start, e)
    nb, nm = _lookup(n_big, e), _lookup(n_mid, e)
    cls = jnp.where(k < nb, 0, jnp.where(k < nb + nm, 1, 2))
    row = _lookup(starts, e) + jnp.where(cls == 0, k * big, nb * big + jnp.where(cls == 1, 0, nm * mid))
    ordinal = jnp.cumsum((n_e > 0).astype(I32)) - 1
    first = jnp.where(k == 0, 1 + _lookup(ordinal, e) % 2, 0)
    nxt_item = _lookup(item_end, e)
    nxt = jnp.where(nxt_item < n_items, jnp.sum((item_end[None, :] <= nxt_item[:, None]).astype(I32), axis=1), -1)
    return (e, row.astype(I32), cls.astype(I32), first.astype(I32), nxt.astype(I32),
            n_items.reshape(1).astype(I32)), starts


def _s5_discretise(lam_re, lam_im, log_dt, b_re, b_im, c_re, c_im):
    g, p = lam_re.shape
    ch = b_re.shape[-1]
    lam_re = lam_re.astype(F32)
    lam_im = lam_im.astype(F32)
    dt = jnp.exp(log_dt.astype(F32))[:, None]
    mag = jnp.exp(lam_re * dt)
    ab_re = mag * jnp.cos(lam_im * dt)
    ab_im = mag * jnp.sin(lam_im * dt)
    den = lam_re * lam_re + lam_im * lam_im
    nr = ab_re - 1.0
    coef_re = (nr * lam_re + ab_im * lam_im) / den
    coef_im = (ab_im * lam_re - nr * lam_im) / den
    bb_re = coef_re[..., None] * b_re - coef_im[..., None] * b_im
    bb_im = coef_re[..., None] * b_im + coef_im[..., None] * b_re
    gh = g // 2
    eye = jnp.eye(gh, dtype=F32)

    def in_block(m):
        return jnp.einsum("gpc,gh->gchp", m, eye).reshape(gh * ch, gh * p)

    def out_block(m):
        return jnp.einsum("gcp,gh->gphc", m, eye).reshape(gh * p, gh * ch)

    bb = jnp.stack([jnp.concatenate([in_block(bb_re[k * gh:(k + 1) * gh]), in_block(bb_im[k * gh:(k + 1) * gh])],
                                    axis=1) for k in range(2)]).astype(BF16)
    cc = jnp.stack([jnp.concatenate([out_block(c_re[k * gh:(k + 1) * gh]), out_block(-c_im[k * gh:(k + 1) * gh])],
                                    axis=0) for k in range(2)]).astype(BF16)
    return ab_re.reshape(1, g * p), ab_im.reshape(1, g * p), bb, cc


def kernel(x_prompt, x_sample, state_hgrn, state_s5_re, state_s5_im, norm_mix_g, w_in, hgrn_lb_raw, hgrn_onorm_g, w_branch_a, s5_lambda_re, s5_lambda_im, s5_log_dt, s5_b_re, s5_b_im, s5_c_re, s5_c_im, s5_d, w_glu, b_glu, w_out, norm_ffn_g, w_router_group, b_router_group, w_router_expert, b_router_expert, w_exp_gate, w_exp_up, w_exp_down, norm_final_g):
    depth = norm_mix_g.shape[0]
    bp, lp, d = x_prompt.shape
    bs, ls, _ = x_sample.shape
    heads, dk = state_hgrn.shape[2], state_hgrn.shape[3]
    kw = heads * dk
    s5_groups, s5_state = state_s5_re.shape[2], state_s5_re.shape[3]
    s5_width = s5_d.shape[-1]
    nstate = s5_groups * s5_state
    moe_groups, _, experts = w_router_expert.shape[1:]
    n_exp = moe_groups * experts
    rows_p, rows_s = bp * lp, bs * ls
    total = rows_p + rows_s
    n_sorted = total * MOE_TOP_K
    assert kw == d and state_hgrn.shape[4] == dk, "column blocks assume key width == value width == model width"
    assert d == SUBLANES * LANES, "token-tile layout holds one token per (8, 128) tile"
    assert s5_groups % 2 == 0 and bp % SUBLANES == 0 and bs % HGRN_SEQ_TILE == 0
    assert (4 * kw) % s5_width == 0, "the S5 input columns must start on a multiple of their width"

    lb_all = jnp.cumsum(jax.nn.softmax(hgrn_lb_raw.astype(F32), axis=0), axis=0)

    hp = x_prompt.reshape(rows_p, d)
    hs = x_sample.reshape(rows_s, d)
    hg_p, re_p, im_p, hg_s, re_s, im_s = [], [], [], [], [], []
    zeros_state = jnp.zeros((bp // SUBLANES, SUBLANES, nstate), F32)

    for l in range(depth):
        w = w_in[l]
        g_mix = norm_mix_g[l].reshape(1, d)
        u_col = 4 * kw
        gate_col = u_col + s5_width
        proj = _in_proj(hp, hs, g_mix, w, u_col)

        ar, ai, bb, cc = _s5_discretise(s5_lambda_re[l], s5_lambda_im[l], s5_log_dt[l], s5_b_re[l], s5_b_im[l],
                                        s5_c_re[l], s5_c_im[l])
        s5_args = (ar, ai, bb, cc, s5_d[l].reshape(1, s5_width), w_glu[l].astype(BF16), b_glu[l].reshape(1, -1))
        yb_p, fr_p, fi_p = _s5_branch(hp, bp, lp, g_mix, w, u_col, zeros_state, zeros_state, *s5_args)
        yb_s, fr_s, fi_s = _s5_branch(hs, bs, ls, g_mix, w, u_col,
                                      state_s5_re[l].reshape(bs // SUBLANES, SUBLANES, nstate),
                                      state_s5_im[l].reshape(bs // SUBLANES, SUBLANES, nstate), *s5_args)

        lb = lb_all[l].reshape(1, kw)
        gn = hgrn_onorm_g[l].reshape(1, kw)
        o_p, hgp = _hgrn_long(proj, lb, gn, bp, lp, heads, dk, 0)
        o_s, hgs = _hgrn_short(proj, lb, gn, state_hgrn[l].astype(F32), ls, rows_p)

        nr = -(-(moe_groups + n_exp) // SUBLANES) * SUBLANES
        wr = jnp.concatenate([w_router_group[l].T, w_router_expert[l].transpose(0, 2, 1).reshape(n_exp, d)], axis=0)
        wr = jnp.pad(wr, ((0, nr - wr.shape[0]), (0, 0))).astype(BF16)
        br = jnp.pad(jnp.concatenate([b_router_group[l], b_router_expert[l].reshape(n_exp)]),
                     (0, nr - moe_groups - n_exp)).reshape(nr, 1).astype(F32)
        h_all, xn_all, ids, wts, ranks, cnt = _merge(
            o_p, o_s, yb_p.reshape(rows_p, d), yb_s.reshape(rows_s, d), hp, hs, g_mix, w, gate_col,
            w_branch_a[l].astype(BF16), w_out[l].astype(BF16), norm_ffn_g[l].reshape(1, d), wr, br,
            moe_groups, experts)
        items, starts = _work_items(cnt[:, 0].astype(I32), n_sorted)
        pos_slots = [_lookup(starts, ids[k]) + ranks[k] for k in range(MOE_TOP_K)]
        xs = _dispatch(pos_slots, xn_all)
        ys = _experts(items, xs, w_exp_gate[l], w_exp_up[l], w_exp_down[l])

        last = l == depth - 1
        g_out = norm_final_g.reshape(1, d)
        hp = _combine(pos_slots, h_all, wts.T, g_out, ys, rows_p, 0, last)
        hs = _combine(pos_slots, h_all, wts.T, g_out, ys, rows_s, rows_p, last)

        hg_p.append(hgp)
        hg_s.append(hgs)
        re_p.append(fr_p.reshape(bp, s5_groups, s5_state))
        im_p.append(fi_p.reshape(bp, s5_groups, s5_state))
        re_s.append(fr_s.reshape(bs, s5_groups, s5_state))
        im_s.append(fi_s.reshape(bs, s5_groups, s5_state))

    y_prompt = hp.reshape(bp, lp, d).astype(x_prompt.dtype)
    y_sample = hs.reshape(bs, ls, d).astype(x_sample.dtype)
    return (y_prompt, y_sample, jnp.stack(hg_p), jnp.stack(re_p), jnp.stack(im_p),
            jnp.stack(hg_s), jnp.stack(re_s), jnp.stack(im_s))
```
